```python
import jax, jax.numpy as jnp
from jax import lax
import numpy as np

D_MODEL = 1024
BATCH = 16
SEQ = 4096
DEPTH = 4

N_META = 16
NORM_EPS = 1e-6
N_BRANCHES = 4
MASK_VALUE = -1e30

A_HEADS = 4
A_HEAD_DIM = 64
A_WIDTH = A_HEADS * A_HEAD_DIM
A_DECAY_LORA = 64
A_ICL_LORA = 64
A_VRES_LORA = 32
A_GATE_LORA = 128
A_GN_EPS = 64e-5
A_COLS = 3 * A_WIDTH + A_DECAY_LORA + A_ICL_LORA + A_GATE_LORA

B_GROUPS = 4
B_GROUP_DIM = 64
B_WIDTH = B_GROUPS * B_GROUP_DIM
B_WINDOWS = (2, 4, 8, 16)

C_HEADS = 8
C_KV_HEADS = 2
C_GROUP = C_HEADS // C_KV_HEADS
C_HEAD_DIM = 64
C_WIDTH = C_HEADS * C_HEAD_DIM
C_KV_WIDTH = C_KV_HEADS * C_HEAD_DIM
C_COLS = C_WIDTH + 2 * C_KV_WIDTH
WINDOW = 128
C_BLOCK = 128

D_HEADS = 4
D_KEY_DIM = 64
D_VAL_DIM = 64
D_WIDTH = D_HEADS * D_KEY_DIM
D_OUT = D_HEADS * D_VAL_DIM
D_COLS = 2 * D_WIDTH + 2 * D_OUT
D_CHUNK = 64

D_FF = ((8 * D_MODEL + 3 * 256 - 1) // (3 * 256)) * 256

GATE_COLS = N_BRANCHES * D_MODEL
OFF_A = GATE_COLS
OFF_B = OFF_A + A_COLS
OFF_C = OFF_B + B_WIDTH
OFF_D = OFF_C + C_COLS
IN_COLS = OFF_D + D_COLS

ROW_B = A_WIDTH
ROW_C = ROW_B + B_WIDTH
ROW_D = ROW_C + C_WIDTH
MIX_WIDTH = ROW_D + D_OUT

kernel_name = "hybrid_rwkv7_pool_swa_hgrn2_gated"

F32 = jnp.float32


def rms_norm(x, g):
    xf = x.astype(F32)
    y = xf * lax.rsqrt(jnp.mean(xf * xf, axis=-1, keepdims=True) + NORM_EPS)
    return (y * g.astype(F32)).astype(x.dtype)


def token_shift(u):
    return jnp.pad(u, ((0, 0), (1, 0), (0, 0)))[:, :-1]


def split_heads(t, n):
    return t.reshape(t.shape[:-1] + (-1, n))


def alibi_slopes(n):
    return jnp.asarray([2.0 ** (-8.0 * (i + 1) / n) for i in range(n)], dtype=F32)


def rwkv7_branch(u, mu, w_up, w0, a_up, a0, g_up, k_k, k_a, r_k, ln_w, ln_b,
                 v_first, vres_down, vres_up, vres0):
    dt = u.dtype
    bsz, L, _ = u.shape
    u = u + (token_shift(u) - u) * mu
    r = u[..., 0:A_WIDTH]
    k = u[..., A_WIDTH:2 * A_WIDTH]
    v = u[..., 2 * A_WIDTH:3 * A_WIDTH]
    o1 = 3 * A_WIDTH
    o2 = o1 + A_DECAY_LORA
    o3 = o2 + A_ICL_LORA
    wd = u[..., o1:o2]
    ad = u[..., o2:o3]
    gd = u[..., o3:o3 + A_GATE_LORA]
    if vres_down is not None:
        v = v + (v_first - v) * jax.nn.sigmoid(vres0 + (v @ vres_down) @ vres_up)
    w_log = -jax.nn.softplus(-(w0 + jnp.tanh(wd) @ w_up)) - 0.5
    decay = jnp.exp(-jnp.exp(w_log.astype(F32)))
    a = jax.nn.sigmoid(a0 + ad @ a_up)
    g = jax.nn.sigmoid(gd) @ g_up
    kk = split_heads((k * k_k).astype(F32), A_HEAD_DIM)
    kk = kk / jnp.maximum(jnp.sqrt(jnp.sum(kk * kk, axis=-1, keepdims=True)), 1e-12)
    k = k * (1 + (a - 1) * k_a)

    def tm(t):
        return jnp.moveaxis(split_heads(t.astype(F32), A_HEAD_DIM), 1, 0)

    xs = (tm(r), tm(decay), tm(k), tm(v), jnp.moveaxis(kk, 1, 0), tm(a))

    def step(S, inp):
        r_t, w_t, k_t, v_t, kk_t, a_t = inp
        sa = jnp.einsum('bhvk,bhk->bhv', S, -kk_t)
        S = (S * w_t[:, :, None, :] + sa[..., None] * (kk_t * a_t)[:, :, None, :]
             + v_t[..., None] * k_t[:, :, None, :])
        return S, jnp.einsum('bhvk,bhk->bhv', S, r_t)

    S0 = jnp.zeros((bsz, A_HEADS, A_HEAD_DIM, A_HEAD_DIM), F32)
    _, o = lax.scan(step, S0, xs)
    o = jnp.moveaxis(o, 0, 1)
    mean = jnp.mean(o, axis=-1, keepdims=True)
    var = jnp.mean(jnp.square(o - mean), axis=-1, keepdims=True)
    o = ((o - mean) * lax.rsqrt(var + A_GN_EPS)).reshape(bsz, L, A_WIDTH)
    o = o * ln_w.astype(F32) + ln_b.astype(F32)
    bonus = (jnp.sum(split_heads((r * k * r_k).astype(F32), A_HEAD_DIM), axis=-1, keepdims=True)
             * split_heads(v.astype(F32), A_HEAD_DIM)).reshape(bsz, L, A_WIDTH)
    return ((o + bonus) * g.astype(F32)).astype(dt), v


def pool_branch(u, mix, scale):
    dt = u.dtype
    bsz, L, _ = u.shape
    ug = u.reshape(bsz, L, B_GROUPS, B_GROUP_DIM).astype(F32)
    cs = jnp.cumsum(ug, axis=1)
    wmax = max(B_WINDOWS)
    cs_pad = jnp.pad(cs, ((0, 0), (wmax, 0), (0, 0), (0, 0)))
    t = jnp.arange(L)
    outs = []
    for gi, w in enumerate(B_WINDOWS):
        prev = cs_pad[:, wmax - w:wmax - w + L, gi]
        cnt = jnp.minimum(t + 1, w).astype(F32)[None, :, None]
        outs.append((cs[:, :, gi] - prev) / cnt)
    pooled = jnp.stack(outs, axis=2) - ug
    y = jnp.einsum('blgc,gcd->blgd', pooled, mix.astype(F32)).reshape(bsz, L, B_WIDTH)
    return (y * scale.astype(F32)).astype(dt)


def swa_branch(u, sinks, slopes):
    dt = u.dtype
    bsz, L, _ = u.shape
    pad = (-L) % C_BLOCK
    Lp = L + pad
    nb = Lp // C_BLOCK

    def blocks(t, nh):
        t = jnp.pad(t.astype(F32), ((0, 0), (pad, 0), (0, 0)))
        return t.reshape(bsz, nb, C_BLOCK, nh, C_HEAD_DIM)

    q = blocks(u[..., :C_WIDTH], C_HEADS).reshape(bsz, nb, C_BLOCK, C_KV_HEADS, C_GROUP, C_HEAD_DIM)
    k = blocks(u[..., C_WIDTH:C_WIDTH + C_KV_WIDTH], C_KV_HEADS)
    v = blocks(u[..., C_WIDTH + C_KV_WIDTH:], C_KV_HEADS)

    def with_prev(t):
        prev = jnp.pad(t, ((0, 0), (1, 0), (0, 0), (0, 0), (0, 0)))[:, :-1]
        return jnp.concatenate([prev, t], axis=2)

    kw, vw = with_prev(k), with_prev(v)
    scores = jnp.einsum('bnqhgd,bnshd->bhgnqs', q, kw) * (C_HEAD_DIM ** -0.5)
    qi = np.arange(C_BLOCK)[:, None]
    si = np.arange(2 * C_BLOCK)[None, :]
    dist = C_BLOCK + qi - si
    band = (dist >= 0) & (dist < WINDOW)
    key_pos = np.arange(nb)[:, None] * C_BLOCK + np.arange(2 * C_BLOCK)[None, :] - C_BLOCK
    mask = band[None] & (key_pos >= pad)[:, None, :]
    sl = slopes.reshape(C_KV_HEADS, C_GROUP)
    logits = scores - sl[:, :, None, None, None] * jnp.asarray(dist, F32)
    logits = jnp.where(mask, logits, MASK_VALUE)
    sink = sinks.astype(F32).reshape(C_KV_HEADS, C_GROUP)[None, :, :, None, None, None]
    m = jnp.maximum(jnp.max(logits, axis=-1, keepdims=True), sink)
    p = jnp.exp(logits - m)
    denom = jnp.sum(p, axis=-1, keepdims=True) + jnp.exp(sink - m)
    out = jnp.einsum('bhgnqs,bnshd->bnqhgd', p / denom, vw)
    return out.reshape(bsz, Lp, C_WIDTH)[:, pad:].astype(dt)


def hgrn2_branch(u, lb, norm_g):
    dt = u.dtype
    bsz, L, _ = u.shape
    q = jax.nn.silu(u[..., :D_WIDTH].astype(F32))
    fpre = u[..., D_WIDTH:2 * D_WIDTH].astype(F32)
    i_in = u[..., 2 * D_WIDTH:2 * D_WIDTH + D_OUT].astype(F32)
    g = u[..., 2 * D_WIDTH + D_OUT:].astype(F32)
    f = lb + (1 - lb) * jax.nn.sigmoid(fpre)
    logf = jnp.log(jnp.maximum(f, 1e-30))
    k = (1 - lb) * jax.nn.sigmoid(-fpre)
    pad = (-L) % D_CHUNK
    n = (L + pad) // D_CHUNK

    def chunks(t, d):
        t = jnp.pad(t, ((0, 0), (pad, 0), (0, 0)))
        return t.reshape(bsz, n, D_CHUNK, D_HEADS, d).transpose(1, 0, 3, 2, 4)

    qc, kc, vc = chunks(q, D_KEY_DIM), chunks(k, D_KEY_DIM), chunks(i_in, D_VAL_DIM)
    bc = jnp.cumsum(chunks(logf, D_KEY_DIM), axis=3)
    causal = jnp.asarray(np.tril(np.ones((D_CHUNK, D_CHUNK), dtype=bool)))

    def chunk_step(S, inp):
        q_, k_, v_, b_ = inp
        o_inter = jnp.einsum('bhtc,bhcv->bhtv', q_ * jnp.exp(b_), S)
        diff = b_[:, :, :, None, :] - b_[:, :, None, :, :]
        dec = jnp.where(causal[:, :, None], jnp.exp(jnp.minimum(diff, 0.0)), 0.0)
        A = jnp.einsum('bhtc,bhsc,bhtsc->bhts', q_, k_, dec)
        o = o_inter + jnp.einsum('bhts,bhsv->bhtv', A, v_)
        b_last = b_[:, :, -1:, :]
        S = S * jnp.exp(b_last)[:, :, 0, :, None] + jnp.einsum(
            'bhsc,bhsv->bhcv', k_ * jnp.exp(b_last - b_), v_)
        return S, o

    S0 = jnp.zeros((bsz, D_HEADS, D_KEY_DIM, D_VAL_DIM), F32)
    _, o = lax.scan(chunk_step, S0, (qc, kc, vc, bc))
    o = o.transpose(1, 0, 3, 2, 4).reshape(bsz, n * D_CHUNK, D_HEADS, D_VAL_DIM)[:, pad:]
    o = o * lax.rsqrt(jnp.mean(o * o, axis=-1, keepdims=True) + NORM_EPS)
    o = o.reshape(bsz, L, D_OUT) * norm_g.astype(F32) * jax.nn.silu(g)
    return o.astype(dt)


def setup_inputs(seed: int = 0) -> dict:
    key = jax.random.key(seed)
    ks = jax.random.split(key, 32)

    def nrm(k, shape, scale):
        return jax.random.normal(k, shape, F32) * scale

    row_scale = jnp.concatenate([
        jnp.full((A_WIDTH,), A_WIDTH ** -0.5, F32), jnp.full((B_WIDTH,), B_WIDTH ** -0.5, F32),
        jnp.full((C_WIDTH,), C_WIDTH ** -0.5, F32), jnp.full((D_OUT,), D_OUT ** -0.5, F32)])
    return {
        "x": nrm(ks[0], (BATCH, SEQ, D_MODEL), 1.0),
        "meta": nrm(ks[1], (N_META, D_MODEL), 1.0),
        "norm_mix": 1.0 + nrm(ks[2], (DEPTH, D_MODEL), 0.1),
        "norm_ffn": 1.0 + nrm(ks[3], (DEPTH, D_MODEL), 0.1),
        "norm_final": 1.0 + nrm(ks[4], (D_MODEL,), 0.1),
        "w_in": nrm(ks[5], (DEPTH, D_MODEL, IN_COLS), D_MODEL ** -0.5),
        "w_branch": nrm(ks[6], (DEPTH, MIX_WIDTH, D_MODEL), 1.0) * row_scale[None, :, None],
        "w_out": nrm(ks[7], (DEPTH, D_MODEL, D_MODEL), D_MODEL ** -0.5),
        "a_mu": jax.random.uniform(ks[8], (DEPTH, A_COLS), F32),
        "a_w_up": nrm(ks[9], (DEPTH, A_DECAY_LORA, A_WIDTH), A_DECAY_LORA ** -0.5),
        "a_w0": -1.0 + nrm(ks[10], (DEPTH, A_WIDTH), 0.5),
        "a_a_up": nrm(ks[11], (DEPTH, A_ICL_LORA, A_WIDTH), A_ICL_LORA ** -0.5),
        "a_a0": nrm(ks[12], (DEPTH, A_WIDTH), 0.1),
        "a_g_up": nrm(ks[13], (DEPTH, A_GATE_LORA, A_WIDTH), A_GATE_LORA ** -0.5),
        "a_kk": 0.85 + nrm(ks[14], (DEPTH, A_WIDTH), 0.1),
        "a_ka": 1.0 + nrm(ks[15], (DEPTH, A_WIDTH), 0.1),
        "a_rk": nrm(ks[16], (DEPTH, A_WIDTH), 0.1),
        "a_ln_w": 1.0 + nrm(ks[17], (DEPTH, A_WIDTH), 0.1),
        "a_ln_b": nrm(ks[18], (DEPTH, A_WIDTH), 0.01),
        "a_vres_down": nrm(ks[19], (DEPTH - 1, A_WIDTH, A_VRES_LORA), A_WIDTH ** -0.5),
        "a_vres_up": nrm(ks[20], (DEPTH - 1, A_VRES_LORA, A_WIDTH), A_VRES_LORA ** -0.5),
        "a_vres0": nrm(ks[21], (DEPTH - 1, A_WIDTH), 0.1),
        "b_mix": nrm(ks[22], (DEPTH, B_GROUPS, B_GROUP_DIM, B_GROUP_DIM), B_GROUP_DIM ** -0.5),
        "b_scale": 1.0 + nrm(ks[23], (DEPTH, B_WIDTH), 0.1),
        "c_sinks": nrm(ks[24], (DEPTH, C_HEADS), 0.5),
        "d_lower_bounds": nrm(ks[25], (DEPTH, D_WIDTH), 0.1),
        "d_norm": 1.0 + nrm(ks[26], (DEPTH, D_OUT), 0.1),
        "w_ffn_up": nrm(ks[27], (DEPTH, D_MODEL, 2 * D_FF), D_MODEL ** -0.5),
        "w_ffn_down": nrm(ks[28], (DEPTH, D_FF, D_MODEL), D_FF ** -0.5),
    }


def reference(x, meta, norm_mix, norm_ffn, norm_final, w_in, w_branch, w_out,
              a_mu, a_w_up, a_w0, a_a_up, a_a0, a_g_up, a_kk, a_ka, a_rk, a_ln_w, a_ln_b,
              a_vres_down, a_vres_up, a_vres0, b_mix, b_scale, c_sinks,
              d_lower_bounds, d_norm, w_ffn_up, w_ffn_down):
    bsz = x.shape[0]
    h = jnp.concatenate(
        [jnp.broadcast_to(meta.astype(x.dtype)[None], (bsz, N_META, D_MODEL)), x], axis=1)
    L = h.shape[1]
    lb_w = jax.nn.softmax(d_lower_bounds.astype(F32), axis=0)
    lb_table = jnp.cumsum(lb_w, axis=0) - lb_w[0]
    slopes = alibi_slopes(C_HEADS)
    v_first = None
    for l in range(DEPTH):
        z = rms_norm(h, norm_mix[l])
        proj = z @ w_in[l]
        gates = jax.nn.sigmoid(proj[..., :GATE_COLS]).reshape(bsz, L, N_BRANCHES, D_MODEL)
        u_a = proj[..., OFF_A:OFF_B]
        if l == 0:
            y_a, v_first = rwkv7_branch(u_a, a_mu[l], a_w_up[l], a_w0[l], a_a_up[l], a_a0[l],
                                        a_g_up[l], a_kk[l], a_ka[l], a_rk[l], a_ln_w[l], a_ln_b[l],
                                        None, None, None, None)
        else:
            y_a, _ = rwkv7_branch(u_a, a_mu[l], a_w_up[l], a_w0[l], a_a_up[l], a_a0[l],
                                  a_g_up[l], a_kk[l], a_ka[l], a_rk[l], a_ln_w[l], a_ln_b[l],
                                  v_first, a_vres_down[l - 1], a_vres_up[l - 1], a_vres0[l - 1])
        y_b = pool_branch(proj[..., OFF_B:OFF_C], b_mix[l], b_scale[l])
        y_c = swa_branch(proj[..., OFF_C:OFF_D], c_sinks[l], slopes)
        y_d = hgrn2_branch(proj[..., OFF_D:IN_COLS], lb_table[l], d_norm[l])
        wb = w_branch[l]
        merged = (gates[:, :, 0] * (y_a @ wb[:ROW_B])
                  + gates[:, :, 1] * (y_b @ wb[ROW_B:ROW_C])
                  + gates[:, :, 2] * (y_c @ wb[ROW_C:ROW_D])
                  + gates[:, :, 3] * (y_d @ wb[ROW_D:]))
        h = h + merged @ w_out[l]
        z = rms_norm(h, norm_ffn[l])
        gu = z @ w_ffn_up[l]
        h = h + (jax.nn.silu(gu[..., :D_FF]) * gu[..., D_FF:]) @ w_ffn_down[l]
    return rms_norm(h, norm_final)[:, N_META:]
```

```python
import functools

import jax
import jax.numpy as jnp
from jax import lax
from jax.experimental import pallas as pl
from jax.experimental.pallas import tpu as pltpu

F32 = jnp.float32
BF16 = jnp.bfloat16

D_MODEL = 1024
N_META = 16
NORM_EPS = 1e-6
MASK_VALUE = -1e30

A_HEADS = 4
A_HEAD_DIM = 64
A_WIDTH = 256
A_GN_EPS = 64e-5
A_COLS = 1024
A_CHUNK = 64

B_WIDTH = 256
B_GROUP_DIM = 64
B_WINDOWS = (2, 4, 8, 16)
B_HALO = 16

C_HEADS = 8
C_KV_HEADS = 2
C_GROUP = 4
C_HEAD_DIM = 64
C_WIDTH = 512
C_KV_WIDTH = 128
C_COLS = 768

D_HEADS = 4
D_KEY_DIM = 64
D_WIDTH = 256
D_COLS = 1024
D_SUB = 16

D_FF = 2816
GATE_COLS = 4 * D_MODEL
OFF_A = GATE_COLS
OFF_B = OFF_A + A_COLS
OFF_C = OFF_B + B_WIDTH
OFF_D = OFF_C + C_COLS
IN_COLS = OFF_D + D_COLS
MIX_WIDTH = 1280

SEQ_TILE = 128
ROW_TILE = 384
VMEM_LIMIT = 56 * 1024 * 1024


def _dot(a, b):
    return jnp.dot(a.astype(BF16), b.astype(BF16), preferred_element_type=F32)


def _dot_nt(a, b):
    return lax.dot_general(a.astype(BF16), b.astype(BF16), (((1,), (1,)), ((), ())),
                           preferred_element_type=F32)


def _dot_tn(a, b):
    return lax.dot_general(a.astype(BF16), b.astype(BF16), (((0,), (0,)), ((), ())),
                           preferred_element_type=F32)


def _split2(x):
    hi = x.astype(BF16)
    lo = (x - hi.astype(F32)).astype(BF16)
    return hi, lo


def _split3(x):
    hi = x.astype(BF16)
    r = x - hi.astype(F32)
    mid = r.astype(BF16)
    lo = (r - mid.astype(F32)).astype(BF16)
    return hi, mid, lo


def _const_dot(c, x):
    hi, mid, lo = _split3(x)
    d = lambda p: jnp.dot(c, p, preferred_element_type=F32)
    return d(hi) + d(mid) + d(lo)


def _dot3(a, b):
    ah, al = _split2(a)
    bh, bl = _split2(b)
    d = lambda p, q: jnp.dot(p, q, preferred_element_type=F32)
    return d(ah, bh) + d(ah, bl) + d(al, bh)


def _head_sum(x, ones_bd):
    hi, lo = _split2(x)
    d = lambda p: jnp.dot(p, ones_bd, preferred_element_type=F32)
    return d(hi) + d(lo)


def _head_ones(width, seg):
    r = lax.broadcasted_iota(jnp.int32, (width, width), 0) // seg
    c = lax.broadcasted_iota(jnp.int32, (width, width), 1) // seg
    return jnp.where(r == c, 1.0, 0.0).astype(BF16)


def _sigmoid(x):
    return 1.0 / (1.0 + jnp.exp(-x))


def _silu(x):
    return x * _sigmoid(x)


def _rms(x, g):
    return x * lax.rsqrt(jnp.mean(x * x, axis=-1, keepdims=True) + NORM_EPS) * g


def _in_proj_kernel(x_ref, g_ref, w_ref, o_ref, *, pad, tiles_per_seq, tm, tn):
    z = _rms(x_ref[...], g_ref[...])
    row = (pl.program_id(0) % tiles_per_seq) * tm + lax.broadcasted_iota(jnp.int32, (tm, 1), 0)
    zb = jnp.where(row >= pad, z, 0.0).astype(BF16)
    for j in range(IN_COLS // tn):
        o_ref[:, j * tn:(j + 1) * tn] = jnp.dot(zb, w_ref[:, j * tn:(j + 1) * tn],
                                                preferred_element_type=F32)


def _in_proj(h2, g, w, *, pad, lp):
    m = h2.shape[0]
    tm = ROW_TILE
    assert lp % tm == 0
    return pl.pallas_call(
        functools.partial(_in_proj_kernel, pad=pad, tiles_per_seq=lp // tm, tm=tm, tn=1024),
        grid=(m // tm,),
        in_specs=[pl.BlockSpec((tm, D_MODEL), lambda i: (i, 0)),
                  pl.BlockSpec((1, D_MODEL), lambda i: (0, 0)),
                  pl.BlockSpec((D_MODEL, IN_COLS), lambda i: (0, 0), pipeline_mode=pl.Buffered(1))],
        out_specs=pl.BlockSpec((tm, IN_COLS), lambda i: (i, 0)),
        out_shape=jax.ShapeDtypeStruct((m, IN_COLS), F32),
        compiler_params=pltpu.CompilerParams(dimension_semantics=("arbitrary",),
                                             vmem_limit_bytes=VMEM_LIMIT),
        name="in_proj",
    )(h2, g, w)


def _rwkv_kernel(*refs, has_vres):
    if has_vres:
        (u_ref, vf_ref, mu_ref, wwa_ref, w0_ref, a0_ref, gup_ref, kk_ref, ka_ref, rk_ref,
         lnw_ref, lnb_ref, vd_ref, vu_ref, v0_ref, y_ref, state_sc, carry_sc, o_sc) = refs
    else:
        (u_ref, mu_ref, wwa_ref, w0_ref, a0_ref, gup_ref, kk_ref, ka_ref, rk_ref,
         lnw_ref, lnb_ref, y_ref, vout_ref, state_sc, carry_sc, o_sc) = refs
    T = SEQ_TILE
    C = A_CHUNK
    N = A_HEAD_DIM

    @pl.when(pl.program_id(1) == 0)
    def _():
        state_sc[...] = jnp.zeros_like(state_sc)
        carry_sc[...] = jnp.zeros_like(carry_sc)

    u = u_ref[0]
    row = lax.broadcasted_iota(jnp.int32, (T, 1), 0)
    prev = jnp.where(row == 0, carry_sc[0:1, :], pltpu.roll(u, 1, axis=0))
    carry_sc[0:1, :] = u[T - 1:T, :]
    x = u + (prev - u) * mu_ref[...]
    r = x[:, 0:A_WIDTH]
    k = x[:, A_WIDTH:2 * A_WIDTH]
    v = x[:, 2 * A_WIDTH:3 * A_WIDTH]
    slab = x[:, 3 * A_WIDTH:3 * A_WIDTH + 128]
    gd = x[:, 3 * A_WIDTH + 128:]
    lane = lax.broadcasted_iota(jnp.int32, (1, 128), 1)
    slab = jnp.where(lane < 64, jnp.tanh(slab), slab)
    wa = _dot(slab, wwa_ref[...])
    zw = -(w0_ref[...] + wa[:, :A_WIDTH])
    softplus = jnp.maximum(zw, 0.0) + jnp.log(1.0 + jnp.exp(-jnp.abs(zw)))
    logw = -jnp.exp(-softplus - 0.5)
    a = _sigmoid(a0_ref[...] + wa[:, A_WIDTH:])
    g = _dot(_sigmoid(gd), gup_ref[...])
    if has_vres:
        mix = _sigmoid(v0_ref[...] + _dot(_dot(v, vd_ref[...]), vu_ref[...]))
        v = v + (vf_ref[0] - v) * mix
    else:
        vout_ref[0] = v

    ones_bd = _head_ones(A_WIDTH, N)
    kkr = k * kk_ref[...]
    kk = kkr / jnp.maximum(jnp.sqrt(_head_sum(kkr * kkr, ones_bd)), 1e-12)
    k2 = k * (1.0 + (a - 1.0) * ka_ref[...])
    bhat = kk * a

    ti = lax.broadcasted_iota(jnp.int32, (C, C), 0)
    si = lax.broadcasted_iota(jnp.int32, (C, C), 1)
    tri = jnp.where(si <= ti, 1.0, 0.0).astype(BF16)
    lower = si <= ti
    strict = si < ti
    eye = jnp.where(si == ti, 1.0, 0.0)

    for c in range(T // C):
        sl = slice(c * C, (c + 1) * C)
        lw = logw[sl]
        cl = _const_dot(tri, lw)
        cl_last = cl[C - 1:C, :]
        e_neg = jnp.exp(-cl)
        e_last = jnp.exp(cl_last - cl)
        at = -kk[sl] * jnp.exp(cl - lw)
        rt = r[sl] * jnp.exp(cl)
        bbar = bhat[sl] * e_neg
        kbar = k2[sl] * e_neg
        btil = bhat[sl] * e_last
        ktil = k2[sl] * e_last
        gam = jnp.exp(cl_last)
        vc = v[sl]
        for hd in range(A_HEADS):
            hs = slice(hd * N, (hd + 1) * N)
            at_h, rt_h, v_h = at[:, hs], rt[:, hs], vc[:, hs]
            aa = _dot_nt(jnp.concatenate([at_h, rt_h], axis=0),
                         jnp.concatenate([bbar[:, hs], kbar[:, hs]], axis=0))
            a_ab = jnp.where(strict, aa[:C, :C], 0.0)
            a_ak = jnp.where(strict, aa[:C, C:], 0.0)
            a_rb = jnp.where(lower, aa[C:, :C], 0.0)
            a_rk = jnp.where(lower, aa[C:, C:], 0.0)
            xp = a_ab
            tinv = eye + xp
            for _ in range(5):
                xp = _dot3(xp, xp)
                tinv = tinv + _dot3(tinv, xp)
            s_h = state_sc[:, hs]
            uu = _dot3(tinv, _dot_nt(at_h, s_h) + _dot(a_ak, v_h))
            o_sc[sl, hs] = _dot_nt(rt_h, s_h) + _dot(a_rb, uu) + _dot(a_rk, v_h)
            state_sc[:, hs] = s_h * gam[:, hs] + _dot_tn(
                jnp.concatenate([uu, v_h], axis=0),
                jnp.concatenate([btil[:, hs], ktil[:, hs]], axis=0))

    o = o_sc[...]
    mean = _head_sum(o, ones_bd) * (1.0 / N)
    d = o - mean
    var = _head_sum(d * d, ones_bd) * (1.0 / N)
    o = d * lax.rsqrt(var + A_GN_EPS) * lnw_ref[...] + lnb_ref[...]
    bonus = _head_sum(r * k2 * rk_ref[...], ones_bd) * v
    y_ref[0] = (o + bonus) * g


def _rwkv(proj3, v_first, p, *, has_vres):
    bsz, lp, _ = proj3.shape
    T = SEQ_TILE
    seq_spec = lambda w, col: pl.BlockSpec((1, T, w), lambda b, j: (b, j, col))
    full = lambda arr: pl.BlockSpec(arr.shape, lambda b, j: (0,) * arr.ndim)
    args = [proj3]
    specs = [seq_spec(A_COLS, OFF_A // A_COLS)]
    if has_vres:
        args.append(v_first)
        specs.append(seq_spec(A_WIDTH, 0))
    names = ["mu", "wwa", "w0", "a0", "gup", "kk", "ka", "rk", "lnw", "lnb"]
    if has_vres:
        names += ["vd", "vu", "v0"]
    for n in names:
        args.append(p[n])
        specs.append(full(p[n]))
    y_shape = jax.ShapeDtypeStruct((bsz, lp, A_WIDTH), F32)
    out_shape = y_shape if has_vres else (y_shape, y_shape)
    out_specs = seq_spec(A_WIDTH, 0) if has_vres else (seq_spec(A_WIDTH, 0), seq_spec(A_WIDTH, 0))
    return pl.pallas_call(
        functools.partial(_rwkv_kernel, has_vres=has_vres),
        grid=(bsz, lp // T),
        in_specs=specs,
        out_specs=out_specs,
        out_shape=out_shape,
        scratch_shapes=[pltpu.VMEM((A_HEAD_DIM, A_WIDTH), F32),
                        pltpu.VMEM((8, A_COLS), F32),
                        pltpu.VMEM((T, A_WIDTH), F32)],
        compiler_params=pltpu.CompilerParams(dimension_semantics=("arbitrary", "arbitrary"),
                                             vmem_limit_bytes=VMEM_LIMIT),
        name="rwkv7",
    )(*args)


def _pool_kernel(u_ref, mix_ref, scale_ref, y_ref, buf, *, pad):
    T = SEQ_TILE
    j = pl.program_id(1)

    @pl.when(j == 0)
    def _():
        buf[0:B_HALO, :] = jnp.zeros((B_HALO, B_WIDTH), F32)

    u = u_ref[0]
    buf[B_HALO:B_HALO + T, :] = u
    lane_group = lax.broadcasted_iota(jnp.int32, (1, B_WIDTH), 1) // B_GROUP_DIM
    acc = u
    win = jnp.zeros_like(u)
    for s in range(1, max(B_WINDOWS)):
        acc = acc + buf[B_HALO - s:B_HALO - s + T, :]
        if s + 1 in B_WINDOWS:
            win = jnp.where(lane_group == B_WINDOWS.index(s + 1), acc, win)
    wlane = jnp.zeros((1, B_WIDTH), jnp.int32)
    for gi, w in enumerate(B_WINDOWS):
        wlane = jnp.where(lane_group == gi, w, wlane)
    t_real = j * T + lax.broadcasted_iota(jnp.int32, (T, 1), 0) - pad
    cnt = jnp.minimum(jnp.maximum(t_real + 1, 1), wlane).astype(F32)
    pooled = win / cnt - u
    y_ref[0] = _dot(pooled, mix_ref[...]) * scale_ref[...]
    buf[0:B_HALO, :] = u[T - B_HALO:, :]


def _pool(proj3, mix_bd, scale, *, pad):
    bsz, lp, _ = proj3.shape
    T = SEQ_TILE
    return pl.pallas_call(
        functools.partial(_pool_kernel, pad=pad),
        grid=(bsz, lp // T),
        in_specs=[pl.BlockSpec((1, T, B_WIDTH), lambda b, j: (b, j, OFF_B // B_WIDTH)),
                  pl.BlockSpec((B_WIDTH, B_WIDTH), lambda b, j: (0, 0)),
                  pl.BlockSpec((1, B_WIDTH), lambda b, j: (0, 0))],
        out_specs=pl.BlockSpec((1, T, B_WIDTH), lambda b, j: (b, j, 0)),
        out_shape=jax.ShapeDtypeStruct((bsz, lp, B_WIDTH), F32),
        scratch_shapes=[pltpu.VMEM((B_HALO + T, B_WIDTH), F32)],
        compiler_params=pltpu.CompilerParams(dimension_semantics=("arbitrary", "arbitrary")),
        name="pool",
    )(proj3, mix_bd, scale)


def _swa_kernel(sinks_ref, u_ref, y_ref, kv_sc, *, pad):
    T = SEQ_TILE
    j = pl.program_id(1)

    @pl.when(j == 0)
    def _():
        kv_sc[...] = jnp.zeros_like(kv_sc)

    blk = u_ref[0]
    kv = blk[:, C_WIDTH:]
    kw = jnp.concatenate([kv_sc[:, :C_KV_WIDTH], kv[:, :C_KV_WIDTH]], axis=0).astype(BF16)
    vw = jnp.concatenate([kv_sc[:, C_KV_WIDTH:], kv[:, C_KV_WIDTH:]], axis=0).astype(BF16)
    kv_sc[...] = kv
    qi = lax.broadcasted_iota(jnp.int32, (T, 2 * T), 0)
    si = lax.broadcasted_iota(jnp.int32, (T, 2 * T), 1)
    dist = T + qi - si
    mask = (dist >= 0) & (dist < T) & ((j - 1) * T + si >= pad)
    distf = dist.astype(F32)
    for hk in range(C_KV_HEADS):
        ks = slice(hk * C_HEAD_DIM, (hk + 1) * C_HEAD_DIM)
        k_h, v_h = kw[:, ks], vw[:, ks]
        for gq in range(C_GROUP):
            hq = hk * C_GROUP + gq
            qs = slice(hq * C_HEAD_DIM, (hq + 1) * C_HEAD_DIM)
            slope = 2.0 ** (-8.0 * (hq + 1) / C_HEADS)
            sink = sinks_ref[hq]
            s = _dot_nt(blk[:, qs] * (C_HEAD_DIM ** -0.5), k_h)
            logits = jnp.where(mask, s - slope * distf, MASK_VALUE)
            m = jnp.maximum(jnp.max(logits, axis=-1, keepdims=True), sink)
            p = jnp.exp(logits - m)
            denom = jnp.sum(p, axis=-1, keepdims=True) + jnp.exp(sink - m)
            y_ref[0, :, qs] = _dot(p / denom, v_h)


def _swa(proj3, sinks, *, pad):
    bsz, lp, _ = proj3.shape
    T = SEQ_TILE
    return pl.pallas_call(
        functools.partial(_swa_kernel, pad=pad),
        grid=(bsz, lp // T),
        in_specs=[pl.BlockSpec(memory_space=pltpu.SMEM),
                  pl.BlockSpec((1, T, C_COLS), lambda b, j: (b, j, OFF_C // C_COLS))],
        out_specs=pl.BlockSpec((1, T, C_WIDTH), lambda b, j: (b, j, 0)),
        out_shape=jax.ShapeDtypeStruct((bsz, lp, C_WIDTH), F32),
        scratch_shapes=[pltpu.VMEM((T, 2 * C_KV_WIDTH), F32)],
        compiler_params=pltpu.CompilerParams(dimension_semantics=("arbitrary", "arbitrary")),
        name="swa",
    )(sinks, proj3)


def _hgrn_kernel(u_ref, lb_ref, ng_ref, y_ref, state_sc, o_sc, *, pad):
    T = SEQ_TILE
    S = D_SUB
    N = D_KEY_DIM
    j = pl.program_id(1)

    @pl.when(j == 0)
    def _():
        state_sc[...] = jnp.zeros_like(state_sc)

    u = u_ref[0]
    lb = lb_ref[...]
    q = _silu(u[:, :D_WIDTH])
    fpre = u[:, D_WIDTH:2 * D_WIDTH]
    vin = u[:, 2 * D_WIDTH:3 * D_WIDTH]
    gate = u[:, 3 * D_WIDTH:]
    f = lb + (1.0 - lb) * _sigmoid(fpre)
    t_real = j * T + lax.broadcasted_iota(jnp.int32, (T, 1), 0) - pad
    logf = jnp.where(t_real >= 0, jnp.log(jnp.maximum(f, 1e-30)), 0.0)
    kx = (1.0 - lb) * _sigmoid(-fpre)

    ti = lax.broadcasted_iota(jnp.int32, (T, T), 0)
    si = lax.broadcasted_iota(jnp.int32, (T, T), 1)
    same = (ti // S) == (si // S)
    tri_bd = jnp.where(same & (si <= ti), 1.0, 0.0).astype(BF16)
    ones_sub = jnp.where(same, 1.0, 0.0).astype(BF16)
    b = _const_dot(tri_bd, logf)
    b_last = _const_dot(ones_sub, logf)
    qe = q * jnp.exp(b)
    ke = kx * jnp.exp(b_last - b)
    dec = jnp.exp(b_last)

    ones_bd = _head_ones(D_WIDTH, N)
    causal = (lax.broadcasted_iota(jnp.int32, (S, S, 1), 1)
              <= lax.broadcasted_iota(jnp.int32, (S, S, 1), 0))

    for i in range(T // S):
        sl = slice(i * S, (i + 1) * S)
        q_s, k_s, v_s, b_s = q[sl], kx[sl], vin[sl], b[sl]
        diff = jnp.minimum(b_s[:, None, :] - b_s[None, :, :], 0.0)
        p3 = q_s[:, None, :] * (k_s[None, :, :] * jnp.exp(diff))
        w3 = _head_sum(p3.reshape(S * S, D_WIDTH), ones_bd).reshape(S, S, D_WIDTH)
        o_intra = jnp.sum(jnp.where(causal, w3 * v_s[None, :, :], 0.0), axis=1)
        st = state_sc[...]
        for hd in range(D_HEADS):
            hs = slice(hd * N, (hd + 1) * N)
            o_sc[sl, hs] = o_intra[:, hs] + _dot_nt(qe[sl, hs], st[:, hs])
            state_sc[:, hs] = st[:, hs] * dec[i * S:i * S + 1, hs] + _dot_tn(v_s[:, hs], ke[sl, hs])

    o = o_sc[...]
    ms = _head_sum(o * o, ones_bd) * (1.0 / N)
    y_ref[0] = o * lax.rsqrt(ms + NORM_EPS) * ng_ref[...] * _silu(gate)


def _hgrn(proj3, lb, norm_g, *, pad):
    bsz, lp, _ = proj3.shape
    T = SEQ_TILE
    return pl.pallas_call(
        functools.partial(_hgrn_kernel, pad=pad),
        grid=(bsz, lp // T),
        in_specs=[pl.BlockSpec((1, T, D_COLS), lambda b, j: (b, j, OFF_D // D_COLS)),
                  pl.BlockSpec((1, D_WIDTH), lambda b, j: (0, 0)),
                  pl.BlockSpec((1, D_WIDTH), lambda b, j: (0, 0))],
        out_specs=pl.BlockSpec((1, T, D_WIDTH), lambda b, j: (b, j, 0)),
        out_shape=jax.ShapeDtypeStruct((bsz, lp, D_WIDTH), F32),
        scratch_shapes=[pltpu.VMEM((D_KEY_DIM, D_WIDTH), F32),
                        pltpu.VMEM((T, D_WIDTH), F32)],
        compiler_params=pltpu.CompilerParams(dimension_semantics=("arbitrary", "arbitrary"),
                                             vmem_limit_bytes=VMEM_LIMIT),
        name="hgrn2",
    )(proj3, lb, norm_g)


def _merge_ffn_kernel(h_ref, gates_ref, ya_ref, yb_ref, yc_ref, yd_ref, wb_ref, wo_ref,
                      gf_ref, wu_ref, wd_ref, o_ref, *, ff_chunk):
    merged = None
    row = 0
    for bi, y_ref in enumerate((ya_ref, yb_ref, yc_ref, yd_ref)):
        w = y_ref.shape[-1]
        part = _sigmoid(gates_ref[:, bi * D_MODEL:(bi + 1) * D_MODEL]) * jnp.dot(
            y_ref[...].astype(BF16), wb_ref[row:row + w, :], preferred_element_type=F32)
        merged = part if merged is None else merged + part
        row += w
    h = h_ref[...] + jnp.dot(merged.astype(BF16), wo_ref[...], preferred_element_type=F32)
    zb = _rms(h, gf_ref[...]).astype(BF16)
    acc = h
    for c in range(D_FF // ff_chunk):
        cs = slice(c * ff_chunk, (c + 1) * ff_chunk)
        gu = jnp.dot(zb, wu_ref[:, cs], preferred_element_type=F32)
        up = jnp.dot(zb, wu_ref[:, D_FF + c * ff_chunk:D_FF + (c + 1) * ff_chunk],
                     preferred_element_type=F32)
        acc = acc + jnp.dot((_silu(gu) * up).astype(BF16), wd_ref[cs, :],
                            preferred_element_type=F32)
    o_ref[...] = acc


def _merge_ffn(h2, proj2, ya, yb, yc, yd, wb, wo, gf, wu, wd):
    m = h2.shape[0]
    tm = ROW_TILE
    rows = lambda w, col=0: pl.BlockSpec((tm, w), lambda i: (i, col))
    const = lambda arr: pl.BlockSpec(arr.shape, lambda i: (0, 0), pipeline_mode=pl.Buffered(1))
    return pl.pallas_call(
        functools.partial(_merge_ffn_kernel, ff_chunk=256),
        grid=(m // tm,),
        in_specs=[rows(D_MODEL), rows(GATE_COLS), rows(A_WIDTH), rows(B_WIDTH), rows(C_WIDTH),
                  rows(D_WIDTH), const(wb), const(wo), const(gf), const(wu), const(wd)],
        out_specs=rows(D_MODEL),
        out_shape=jax.ShapeDtypeStruct((m, D_MODEL), F32),
        compiler_params=pltpu.CompilerParams(dimension_semantics=("arbitrary",),
                                             vmem_limit_bytes=VMEM_LIMIT),
        name="merge_ffn",
    )(h2, proj2, ya, yb, yc, yd, wb, wo, gf, wu, wd)


def _final_norm_kernel(h_ref, g_ref, o_ref):
    o_ref[0] = _rms(h_ref[0], g_ref[...])


def _final_norm(h3, g, *, first_tile, seq):
    bsz = h3.shape[0]
    T = SEQ_TILE
    return pl.pallas_call(
        _final_norm_kernel,
        grid=(bsz, seq // T),
        in_specs=[pl.BlockSpec((1, T, D_MODEL), lambda b, j: (b, j + first_tile, 0)),
                  pl.BlockSpec((1, D_MODEL), lambda b, j: (0, 0))],
        out_specs=pl.BlockSpec((1, T, D_MODEL), lambda b, j: (b, j, 0)),
        out_shape=jax.ShapeDtypeStruct((bsz, seq, D_MODEL), F32),
        name="final_norm",
    )(h3, g)


def _block_diag(blocks):
    n, r, c = blocks.shape
    out = jnp.zeros((n * r, n * c), blocks.dtype)
    for i in range(n):
        out = out.at[i * r:(i + 1) * r, i * c:(i + 1) * c].set(blocks[i])
    return out


def kernel(x, meta, norm_mix, norm_ffn, norm_final, w_in, w_branch, w_out, a_mu, a_w_up, a_w0, a_a_up, a_a0, a_g_up, a_kk, a_ka, a_rk, a_ln_w, a_ln_b, a_vres_down, a_vres_up, a_vres0, b_mix, b_scale, c_sinks, d_lower_bounds, d_norm, w_ffn_up, w_ffn_down):
    bsz, seq, _ = x.shape
    depth = w_in.shape[0]
    T = SEQ_TILE
    L = N_META + seq
    pad = (-L) % T
    lp = L + pad
    assert (pad + N_META) % T == 0 and seq % T == 0
    h = jnp.concatenate([jnp.zeros((bsz, pad, D_MODEL), F32),
                         jnp.broadcast_to(meta.astype(F32)[None], (bsz, N_META, D_MODEL)),
                         x.astype(F32)], axis=1).reshape(bsz * lp, D_MODEL)
    lb_w = jax.nn.softmax(d_lower_bounds.astype(F32), axis=0)
    lb_table = jnp.cumsum(lb_w, axis=0) - lb_w[0]
    row2 = lambda t: t.reshape(1, -1).astype(F32)
    v_first = None
    for l in range(depth):
        proj2 = _in_proj(h, row2(norm_mix[l]), w_in[l].astype(BF16), pad=pad, lp=lp)
        proj3 = proj2.reshape(bsz, lp, IN_COLS)
        wwa = jnp.zeros((128, 2 * A_WIDTH), F32)
        wwa = wwa.at[:64, :A_WIDTH].set(a_w_up[l]).at[64:, A_WIDTH:].set(a_a_up[l])
        pa = dict(mu=row2(a_mu[l]), wwa=wwa.astype(BF16), w0=row2(a_w0[l]), a0=row2(a_a0[l]),
                  gup=a_g_up[l].astype(BF16), kk=row2(a_kk[l]), ka=row2(a_ka[l]), rk=row2(a_rk[l]),
                  lnw=row2(a_ln_w[l]), lnb=row2(a_ln_b[l]))
        if l == 0:
            y_a, v_first = _rwkv(proj3, None, pa, has_vres=False)
        else:
            pa.update(vd=a_vres_down[l - 1].astype(BF16), vu=a_vres_up[l - 1].astype(BF16),
                      v0=row2(a_vres0[l - 1]))
            y_a = _rwkv(proj3, v_first, pa, has_vres=True)
        y_b = _pool(proj3, _block_diag(b_mix[l]).astype(BF16), row2(b_scale[l]), pad=pad)
        y_c = _swa(proj3, c_sinks[l].astype(F32), pad=pad)
        y_d = _hgrn(proj3, row2(lb_table[l]), row2(d_norm[l]), pad=pad)
        flat = lambda t: t.reshape(bsz * lp, t.shape[-1])
        h = _merge_ffn(h, proj2, flat(y_a), flat(y_b), flat(y_c), flat(y_d),
                       w_branch[l].astype(BF16), w_out[l].astype(BF16), row2(norm_ffn[l]),
                       w_ffn_up[l].astype(BF16), w_ffn_down[l].astype(BF16))
    return _final_norm(h.reshape(bsz, lp, D_MODEL), row2(norm_final),
                       first_tile=(pad + N_META) // T, seq=seq)
```

```python
import functools

import jax
import jax.numpy as jnp
from jax import lax
from jax.experimental import pallas as pl
from jax.experimental.pallas import tpu as pltpu

F32 = jnp.float32
BF16 = jnp.bfloat16

D_MODEL = 1024
N_META = 16
NORM_EPS = 1e-6
MASK_VALUE = -1e30

A_HEADS = 4
A_HEAD_DIM = 64
A_WIDTH = 256
A_GN_EPS = 64e-5
A_COLS = 1024
A_CHUNK = 64
A_BATCH = 2

B_WIDTH = 256
B_GROUP_DIM = 64
B_WINDOWS = (2, 4, 8, 16)
B_HALO = 16

C_HEADS = 8
C_KV_HEADS = 2
C_GROUP = 4
C_HEAD_DIM = 64
C_WIDTH = 512
C_KV_WIDTH = 128
C_COLS = 768

D_HEADS = 4
D_KEY_DIM = 64
D_WIDTH = 256
D_COLS = 1024
D_SUB = 16

D_FF = 2816
GATE_COLS = 4 * D_MODEL
OFF_A = GATE_COLS
OFF_B = OFF_A + A_COLS
OFF_C = OFF_B + B_WIDTH
OFF_D = OFF_C + C_COLS
IN_COLS = OFF_D + D_COLS
MIX_WIDTH = 1280

SEQ_TILE = 128
ROW_TILE = 384
VMEM_LIMIT = 56 * 1024 * 1024


def _dot(a, b):
    return jnp.dot(a.astype(BF16), b.astype(BF16), preferred_element_type=F32)


def _dot_nt(a, b):
    return lax.dot_general(a.astype(BF16), b.astype(BF16), (((1,), (1,)), ((), ())),
                           preferred_element_type=F32)


def _dot_tn(a, b):
    return lax.dot_general(a.astype(BF16), b.astype(BF16), (((0,), (0,)), ((), ())),
                           preferred_element_type=F32)


def _split2(x):
    hi = x.astype(BF16)
    lo = (x - hi.astype(F32)).astype(BF16)
    return hi, lo


def _split3(x):
    hi = x.astype(BF16)
    r = x - hi.astype(F32)
    mid = r.astype(BF16)
    lo = (r - mid.astype(F32)).astype(BF16)
    return hi, mid, lo


def _const_dot(c, x):
    hi, mid, lo = _split3(x)
    d = lambda p: jnp.dot(c, p, preferred_element_type=F32)
    return d(hi) + d(mid) + d(lo)


def _mm(a, b):
    return jnp.dot(a, b, preferred_element_type=F32)


def _mm_tn(a, b):
    return lax.dot_general(a, b, (((0,), (0,)), ((), ())), preferred_element_type=F32)


def _d3(a, b):
    return _mm(a[0], b[0]) + _mm(a[0], b[1]) + _mm(a[1], b[0])


def _unit_lower_inverses(mats, ri, ci):
    in16 = (ri // 16) == (ci // 16)
    in32 = (ri // 32) == (ci // 32)
    eye = jnp.where(ri == ci, 1.0, 0.0)
    bf = lambda m: m.astype(BF16)
    x1 = [jnp.where(in16, a, 0.0) for a in mats]
    xb = [bf(x) for x in x1]
    t = [eye + x for x in x1]
    for _ in range(3):
        xb = [bf(_mm(x, x)) for x in xb]
        t = [ti + _mm(bf(ti), x) for ti, x in zip(t, xb)]
    for sel in (in32 & ~in16, ~in32):
        tb = [bf(ti) for ti in t]
        lt = [bf(_mm(bf(jnp.where(sel, a, 0.0)), tbi)) for a, tbi in zip(mats, tb)]
        t = [ti + _mm(tbi, lti) for ti, tbi, lti in zip(t, tb, lt)]
    return t


def _head_sum(x, ones_bd):
    hi, lo = _split2(x)
    d = lambda p: jnp.dot(p, ones_bd, preferred_element_type=F32)
    return d(hi) + d(lo)


def _head_ones(width, seg):
    r = lax.broadcasted_iota(jnp.int32, (width, width), 0) // seg
    c = lax.broadcasted_iota(jnp.int32, (width, width), 1) // seg
    return jnp.where(r == c, 1.0, 0.0).astype(BF16)


def _sigmoid(x):
    return 1.0 / (1.0 + jnp.exp(-x))


def _silu(x):
    return x * _sigmoid(x)


def _rms(x, g):
    return x * lax.rsqrt(jnp.mean(x * x, axis=-1, keepdims=True) + NORM_EPS) * g


def _in_proj_kernel(x_ref, g_ref, w_ref, o_ref, *, pad, tiles_per_seq, tm, tn):
    z = _rms(x_ref[...], g_ref[...])
    row = (pl.program_id(0) % tiles_per_seq) * tm + lax.broadcasted_iota(jnp.int32, (tm, 1), 0)
    zb = jnp.where(row >= pad, z, 0.0).astype(BF16)
    for j in range(IN_COLS // tn):
        o_ref[:, j * tn:(j + 1) * tn] = jnp.dot(zb, w_ref[:, j * tn:(j + 1) * tn],
                                                preferred_element_type=F32)


def _in_proj(h2, g, w, *, pad, lp):
    m = h2.shape[0]
    tm = ROW_TILE
    assert lp % tm == 0
    return pl.pallas_call(
        functools.partial(_in_proj_kernel, pad=pad, tiles_per_seq=lp // tm, tm=tm, tn=1024),
        grid=(m // tm,),
        in_specs=[pl.BlockSpec((tm, D_MODEL), lambda i: (i, 0)),
                  pl.BlockSpec((1, D_MODEL), lambda i: (0, 0)),
                  pl.BlockSpec((D_MODEL, IN_COLS), lambda i: (0, 0), pipeline_mode=pl.Buffered(1))],
        out_specs=pl.BlockSpec((tm, IN_COLS), lambda i: (i, 0)),
        out_shape=jax.ShapeDtypeStruct((m, IN_COLS), F32),
        compiler_params=pltpu.CompilerParams(dimension_semantics=("arbitrary",),
                                             vmem_limit_bytes=VMEM_LIMIT),
        name="in_proj",
    )(h2, g, w)


def _rwkv_kernel(*refs, has_vres):
    if has_vres:
        (u_ref, vf_ref, mu_ref, wwa_ref, w0_ref, a0_ref, gup_ref, kk_ref, ka_ref, rk_ref,
         lnw_ref, lnb_ref, vd_ref, vu_ref, v0_ref, y_ref, state_sc, carry_sc) = refs
    else:
        (u_ref, mu_ref, wwa_ref, w0_ref, a0_ref, gup_ref, kk_ref, ka_ref, rk_ref,
         lnw_ref, lnb_ref, y_ref, vout_ref, state_sc, carry_sc) = refs
    T = SEQ_TILE
    C = A_CHUNK
    N = A_HEAD_DIM
    W = A_WIDTH
    NB = A_BATCH
    R = NB * T

    @pl.when(pl.program_id(1) == 0)
    def _():
        state_sc[...] = jnp.zeros_like(state_sc)
        carry_sc[...] = jnp.zeros_like(carry_sc)

    u = u_ref[...].reshape(R, A_COLS)
    row = lax.broadcasted_iota(jnp.int32, (R, 1), 0)
    prev = pltpu.roll(u, 1, axis=0)
    for b in range(NB):
        prev = jnp.where(row == b * T, carry_sc[b:b + 1, :], prev)
        carry_sc[b:b + 1, :] = u[(b + 1) * T - 1:(b + 1) * T, :]
    x = u + (prev - u) * mu_ref[...]
    r = x[:, 0:W]
    k = x[:, W:2 * W]
    v = x[:, 2 * W:3 * W]
    slab = x[:, 3 * W:3 * W + 128]
    gd = x[:, 3 * W + 128:]
    lane = lax.broadcasted_iota(jnp.int32, (1, 128), 1)
    slab = jnp.where(lane < 64, jnp.tanh(slab), slab)
    wa = _dot(slab, wwa_ref[...])
    zw = -(w0_ref[...] + wa[:, :W])
    softplus = jnp.maximum(zw, 0.0) + jnp.log(1.0 + jnp.exp(-jnp.abs(zw)))
    logw = -jnp.exp(-softplus - 0.5)
    a = _sigmoid(a0_ref[...] + wa[:, W:])
    g = _dot(_sigmoid(gd), gup_ref[...])
    if has_vres:
        mix = _sigmoid(v0_ref[...] + _dot(_dot(v, vd_ref[...]), vu_ref[...]))
        v = v + (vf_ref[...].reshape(R, W) - v) * mix
    else:
        vout_ref[...] = v.reshape(NB, T, W)

    ones_bd = _head_ones(W, N)
    kkr = k * kk_ref[...]
    kk = kkr / jnp.maximum(jnp.sqrt(_head_sum(kkr * kkr, ones_bd)), 1e-12)
    k2 = k * (1.0 + (a - 1.0) * ka_ref[...])
    bhat = kk * a

    ri = lax.broadcasted_iota(jnp.int32, (W, W), 0)
    ci = lax.broadcasted_iota(jnp.int32, (W, W), 1)
    same_head = (ri // C) == (ci // C)
    tw = lax.broadcasted_iota(jnp.int32, (C, W), 0)
    sw = lax.broadcasted_iota(jnp.int32, (C, W), 1) % C
    tri = jnp.where(lax.broadcasted_iota(jnp.int32, (C, C), 1)
                    <= lax.broadcasted_iota(jnp.int32, (C, C), 0), 1.0, 0.0).astype(BF16)

    def stack(t):
        return jnp.where(same_head, jnp.concatenate([t] * A_HEADS, axis=0), 0.0)

    pre = []
    for i in range(R // C):
        sl = slice(i * C, (i + 1) * C)
        lw = logw[sl]
        cl = _const_dot(tri, lw)
        cl_last = cl[C - 1:C, :]
        e_neg = jnp.exp(-cl)
        e_last = jnp.exp(cl_last - cl)
        at = -kk[sl] * jnp.exp(cl - lw)
        rt = r[sl] * jnp.exp(cl)
        v_s = stack(v[sl]).astype(BF16)
        aa = _dot_nt(jnp.concatenate([at, rt], axis=0),
                     jnp.concatenate([stack(bhat[sl] * e_neg), stack(k2[sl] * e_neg)], axis=0))
        a_ab = stack(jnp.where(sw < tw, aa[:C, :W], 0.0))
        a_ak = stack(jnp.where(sw < tw, aa[:C, W:], 0.0)).astype(BF16)
        a_rb = jnp.where(sw <= tw, aa[C:, :W], 0.0).astype(BF16)
        a_rk = jnp.where(sw <= tw, aa[C:, W:], 0.0).astype(BF16)
        rhs = jnp.concatenate([stack(at), _mm(a_ak, v_s)], axis=1).astype(BF16)
        btil = stack(bhat[sl] * e_last).astype(BF16)
        ktil = stack(k2[sl] * e_last).astype(BF16)
        pre.append((a_ab, rhs, a_rb, a_rk, v_s, btil, ktil, rt, jnp.exp(cl_last)))

    tinvs = _unit_lower_inverses([p[0] for p in pre], ri, ci)

    terms = []
    for (_, rhs, a_rb, a_rk, v_s, btil, ktil, rt, gam), tinv in zip(pre, tinvs):
        wu = _mm(tinv.astype(BF16), rhs)
        w_t = wu[:, :W].astype(BF16)
        u_t = wu[:, W:].astype(BF16)
        m_mat = jnp.where(ri == ci, gam, 0.0) + _mm_tn(btil, w_t)
        n_mat = _mm_tn(btil, u_t) + _mm_tn(ktil, v_s)
        q_mat = rt + _mm(a_rb, w_t)
        p_mat = _mm(a_rb, u_t) + _mm(a_rk, v_s)
        terms.append((m_mat, n_mat, q_mat, p_mat))

    outs = []
    for b in range(NB):
        st = state_sc[b]
        for m_mat, n_mat, q_mat, p_mat in terms[b * (T // C):(b + 1) * (T // C)]:
            outs.append(_dot(q_mat, st) + p_mat)
            st = _d3(_split2(m_mat), _split2(st)) + n_mat
        state_sc[b] = st

    o = jnp.concatenate(outs, axis=0)
    mean = _head_sum(o, ones_bd) * (1.0 / N)
    d = o - mean
    var = _head_sum(d * d, ones_bd) * (1.0 / N)
    o = d * lax.rsqrt(var + A_GN_EPS) * lnw_ref[...] + lnb_ref[...]
    bonus = _head_sum(r * k2 * rk_ref[...], ones_bd) * v
    y_ref[...] = ((o + bonus) * g).reshape(NB, T, W)


def _rwkv(proj3, v_first, p, *, has_vres):
    bsz, lp, _ = proj3.shape
    T = SEQ_TILE
    NB = A_BATCH
    assert bsz % NB == 0
    seq_spec = lambda w, col: pl.BlockSpec((NB, T, w), lambda b, j: (b, j, col))
    full = lambda arr: pl.BlockSpec(arr.shape, lambda b, j: (0,) * arr.ndim)
    args = [proj3]
    specs = [seq_spec(A_COLS, OFF_A // A_COLS)]
    if has_vres:
        args.append(v_first)
        specs.append(seq_spec(A_WIDTH, 0))
    names = ["mu", "wwa", "w0", "a0", "gup", "kk", "ka", "rk", "lnw", "lnb"]
    if has_vres:
        names += ["vd", "vu", "v0"]
    for n in names:
        args.append(p[n])
        specs.append(full(p[n]))
    y_shape = jax.ShapeDtypeStruct((bsz, lp, A_WIDTH), F32)
    out_shape = y_shape if has_vres else (y_shape, y_shape)
    out_specs = seq_spec(A_WIDTH, 0) if has_vres else (seq_spec(A_WIDTH, 0), seq_spec(A_WIDTH, 0))
    return pl.pallas_call(
        functools.partial(_rwkv_kernel, has_vres=has_vres),
        grid=(bsz // NB, lp // T),
        in_specs=specs,
        out_specs=out_specs,
        out_shape=out_shape,
        scratch_shapes=[pltpu.VMEM((NB, A_WIDTH, A_WIDTH), F32),
                        pltpu.VMEM((8, A_COLS), F32)],
        compiler_params=pltpu.CompilerParams(dimension_semantics=("arbitrary", "arbitrary"),
                                             vmem_limit_bytes=VMEM_LIMIT),
        name="rwkv7",
    )(*args)


def _pool_kernel(u_ref, mix_ref, scale_ref, y_ref, buf, *, pad):
    T = SEQ_TILE
    j = pl.program_id(1)

    @pl.when(j == 0)
    def _():
        buf[0:B_HALO, :] = jnp.zeros((B_HALO, B_WIDTH), F32)

    u = u_ref[0]
    buf[B_HALO:B_HALO + T, :] = u
    lane_group = lax.broadcasted_iota(jnp.int32, (1, B_WIDTH), 1) // B_GROUP_DIM
    acc = u
    win = jnp.zeros_like(u)
    for s in range(1, max(B_WINDOWS)):
        acc = acc + buf[B_HALO - s:B_HALO - s + T, :]
        if s + 1 in B_WINDOWS:
            win = jnp.where(lane_group == B_WINDOWS.index(s + 1), acc, win)
    wlane = jnp.zeros((1, B_WIDTH), jnp.int32)
    for gi, w in enumerate(B_WINDOWS):
        wlane = jnp.where(lane_group == gi, w, wlane)
    t_real = j * T + lax.broadcasted_iota(jnp.int32, (T, 1), 0) - pad
    cnt = jnp.minimum(jnp.maximum(t_real + 1, 1), wlane).astype(F32)
    pooled = win / cnt - u
    y_ref[0] = _dot(pooled, mix_ref[...]) * scale_ref[...]
    buf[0:B_HALO, :] = u[T - B_HALO:, :]


def _pool(proj3, mix_bd, scale, *, pad):
    bsz, lp, _ = proj3.shape
    T = SEQ_TILE
    return pl.pallas_call(
        functools.partial(_pool_kernel, pad=pad),
        grid=(bsz, lp // T),
        in_specs=[pl.BlockSpec((1, T, B_WIDTH), lambda b, j: (b, j, OFF_B // B_WIDTH)),
                  pl.BlockSpec((B_WIDTH, B_WIDTH), lambda b, j: (0, 0)),
                  pl.BlockSpec((1, B_WIDTH), lambda b, j: (0, 0))],
        out_specs=pl.BlockSpec((1, T, B_WIDTH), lambda b, j: (b, j, 0)),
        out_shape=jax.ShapeDtypeStruct((bsz, lp, B_WIDTH), F32),
        scratch_shapes=[pltpu.VMEM((B_HALO + T, B_WIDTH), F32)],
        compiler_params=pltpu.CompilerParams(dimension_semantics=("arbitrary", "arbitrary")),
        name="pool",
    )(proj3, mix_bd, scale)


def _swa_kernel(sinks_ref, u_ref, y_ref, kv_sc, *, pad):
    T = SEQ_TILE
    j = pl.program_id(1)

    @pl.when(j == 0)
    def _():
        kv_sc[...] = jnp.zeros_like(kv_sc)

    blk = u_ref[0]
    kv = blk[:, C_WIDTH:]
    kw = jnp.concatenate([kv_sc[:, :C_KV_WIDTH], kv[:, :C_KV_WIDTH]], axis=0).astype(BF16)
    vw = jnp.concatenate([kv_sc[:, C_KV_WIDTH:], kv[:, C_KV_WIDTH:]], axis=0).astype(BF16)
    kv_sc[...] = kv
    qi = lax.broadcasted_iota(jnp.int32, (T, 2 * T), 0)
    si = lax.broadcasted_iota(jnp.int32, (T, 2 * T), 1)
    dist = T + qi - si
    mask = (dist >= 0) & (dist < T) & ((j - 1) * T + si >= pad)
    distf = dist.astype(F32)
    for hk in range(C_KV_HEADS):
        ks = slice(hk * C_HEAD_DIM, (hk + 1) * C_HEAD_DIM)
        k_h, v_h = kw[:, ks], vw[:, ks]
        for gq in range(C_GROUP):
            hq = hk * C_GROUP + gq
            qs = slice(hq * C_HEAD_DIM, (hq + 1) * C_HEAD_DIM)
            slope = 2.0 ** (-8.0 * (hq + 1) / C_HEADS)
            sink = sinks_ref[hq]
            s = _dot_nt(blk[:, qs] * (C_HEAD_DIM ** -0.5), k_h)
            logits = jnp.where(mask, s - slope * distf, MASK_VALUE)
            m = jnp.maximum(jnp.max(logits, axis=-1, keepdims=True), sink)
            p = jnp.exp(logits - m)
            denom = jnp.sum(p, axis=-1, keepdims=True) + jnp.exp(sink - m)
            y_ref[0, :, qs] = _dot(p / denom, v_h)


def _swa(proj3, sinks, *, pad):
    bsz, lp, _ = proj3.shape
    T = SEQ_TILE
    return pl.pallas_call(
        functools.partial(_swa_kernel, pad=pad),
        grid=(bsz, lp // T),
        in_specs=[pl.BlockSpec(memory_space=pltpu.SMEM),
                  pl.BlockSpec((1, T, C_COLS), lambda b, j: (b, j, OFF_C // C_COLS))],
        out_specs=pl.BlockSpec((1, T, C_WIDTH), lambda b, j: (b, j, 0)),
        out_shape=jax.ShapeDtypeStruct((bsz, lp, C_WIDTH), F32),
        scratch_shapes=[pltpu.VMEM((T, 2 * C_KV_WIDTH), F32)],
        compiler_params=pltpu.CompilerParams(dimension_semantics=("arbitrary", "arbitrary")),
        name="swa",
    )(sinks, proj3)


def _hgrn_kernel(u_ref, lb_ref, ng_ref, y_ref, state_sc, o_sc, *, pad):
    T = SEQ_TILE
    S = D_SUB
    N = D_KEY_DIM
    j = pl.program_id(1)

    @pl.when(j == 0)
    def _():
        state_sc[...] = jnp.zeros_like(state_sc)

    u = u_ref[0]
    lb = lb_ref[...]
    q = _silu(u[:, :D_WIDTH])
    fpre = u[:, D_WIDTH:2 * D_WIDTH]
    vin = u[:, 2 * D_WIDTH:3 * D_WIDTH]
    gate = u[:, 3 * D_WIDTH:]
    f = lb + (1.0 - lb) * _sigmoid(fpre)
    t_real = j * T + lax.broadcasted_iota(jnp.int32, (T, 1), 0) - pad
    logf = jnp.where(t_real >= 0, jnp.log(jnp.maximum(f, 1e-30)), 0.0)
    kx = (1.0 - lb) * _sigmoid(-fpre)

    ti = lax.broadcasted_iota(jnp.int32, (T, T), 0)
    si = lax.broadcasted_iota(jnp.int32, (T, T), 1)
    same = (ti // S) == (si // S)
    tri_bd = jnp.where(same & (si <= ti), 1.0, 0.0).astype(BF16)
    ones_sub = jnp.where(same, 1.0, 0.0).astype(BF16)
    b = _const_dot(tri_bd, logf)
    b_last = _const_dot(ones_sub, logf)
    qe = q * jnp.exp(b)
    ke = kx * jnp.exp(b_last - b)
    dec = jnp.exp(b_last)

    ones_bd = _head_ones(D_WIDTH, N)
    causal = (lax.broadcasted_iota(jnp.int32, (S, S, 1), 1)
              <= lax.broadcasted_iota(jnp.int32, (S, S, 1), 0))

    for i in range(T // S):
        sl = slice(i * S, (i + 1) * S)
        q_s, k_s, v_s, b_s = q[sl], kx[sl], vin[sl], b[sl]
        diff = jnp.minimum(b_s[:, None, :] - b_s[None, :, :], 0.0)
        p3 = q_s[:, None, :] * (k_s[None, :, :] * jnp.exp(diff))
        w3 = _head_sum(p3.reshape(S * S, D_WIDTH), ones_bd).reshape(S, S, D_WIDTH)
        o_intra = jnp.sum(jnp.where(causal, w3 * v_s[None, :, :], 0.0), axis=1)
        st = state_sc[...]
        for hd in range(D_HEADS):
            hs = slice(hd * N, (hd + 1) * N)
            o_sc[sl, hs] = o_intra[:, hs] + _dot_nt(qe[sl, hs], st[:, hs])
            state_sc[:, hs] = st[:, hs] * dec[i * S:i * S + 1, hs] + _dot_tn(v_s[:, hs], ke[sl, hs])

    o = o_sc[...]
    ms = _head_sum(o * o, ones_bd) * (1.0 / N)
    y_ref[0] = o * lax.rsqrt(ms + NORM_EPS) * ng_ref[...] * _silu(gate)


def _hgrn(proj3, lb, norm_g, *, pad):
    bsz, lp, _ = proj3.shape
    T = SEQ_TILE
    return pl.pallas_call(
        functools.partial(_hgrn_kernel, pad=pad),
        grid=(bsz, lp // T),
        in_specs=[pl.BlockSpec((1, T, D_COLS), lambda b, j: (b, j, OFF_D // D_COLS)),
                  pl.BlockSpec((1, D_WIDTH), lambda b, j: (0, 0)),
                  pl.BlockSpec((1, D_WIDTH), lambda b, j: (0, 0))],
        out_specs=pl.BlockSpec((1, T, D_WIDTH), lambda b, j: (b, j, 0)),
        out_shape=jax.ShapeDtypeStruct((bsz, lp, D_WIDTH), F32),
        scratch_shapes=[pltpu.VMEM((D_KEY_DIM, D_WIDTH), F32),
                        pltpu.VMEM((T, D_WIDTH), F32)],
        compiler_params=pltpu.CompilerParams(dimension_semantics=("arbitrary", "arbitrary"),
                                             vmem_limit_bytes=VMEM_LIMIT),
        name="hgrn2",
    )(proj3, lb, norm_g)


def _merge_ffn_kernel(h_ref, gates_ref, ya_ref, yb_ref, yc_ref, yd_ref, wb_ref, wo_ref,
                      gf_ref, wu_ref, wd_ref, o_ref, *, ff_chunk):
    merged = None
    row = 0
    for bi, y_ref in enumerate((ya_ref, yb_ref, yc_ref, yd_ref)):
        w = y_ref.shape[-1]
        part = _sigmoid(gates_ref[:, bi * D_MODEL:(bi + 1) * D_MODEL]) * jnp.dot(
            y_ref[...].astype(BF16), wb_ref[row:row + w, :], preferred_element_type=F32)
        merged = part if merged is None else merged + part
        row += w
    h = h_ref[...] + jnp.dot(merged.astype(BF16), wo_ref[...], preferred_element_type=F32)
    zb = _rms(h, gf_ref[...]).astype(BF16)
    acc = h
    for c in range(D_FF // ff_chunk):
        cs = slice(c * ff_chunk, (c + 1) * ff_chunk)
        gu = jnp.dot(zb, wu_ref[:, cs], preferred_element_type=F32)
        up = jnp.dot(zb, wu_ref[:, D_FF + c * ff_chunk:D_FF + (c + 1) * ff_chunk],
                     preferred_element_type=F32)
        acc = acc + jnp.dot((_silu(gu) * up).astype(BF16), wd_ref[cs, :],
                            preferred_element_type=F32)
    o_ref[...] = acc


def _merge_ffn(h2, proj2, ya, yb, yc, yd, wb, wo, gf, wu, wd):
    m = h2.shape[0]
    tm = ROW_TILE
    rows = lambda w, col=0: pl.BlockSpec((tm, w), lambda i: (i, col))
    const = lambda arr: pl.BlockSpec(arr.shape, lambda i: (0, 0), pipeline_mode=pl.Buffered(1))
    return pl.pallas_call(
        functools.partial(_merge_ffn_kernel, ff_chunk=256),
        grid=(m // tm,),
        in_specs=[rows(D_MODEL), rows(GATE_COLS), rows(A_WIDTH), rows(B_WIDTH), rows(C_WIDTH),
                  rows(D_WIDTH), const(wb), const(wo), const(gf), const(wu), const(wd)],
        out_specs=rows(D_MODEL),
        out_shape=jax.ShapeDtypeStruct((m, D_MODEL), F32),
        compiler_params=pltpu.CompilerParams(dimension_semantics=("arbitrary",),
                                             vmem_limit_bytes=VMEM_LIMIT),
        name="merge_ffn",
    )(h2, proj2, ya, yb, yc, yd, wb, wo, gf, wu, wd)


def _final_norm_kernel(h_ref, g_ref, o_ref):
    o_ref[0] = _rms(h_ref[0], g_ref[...])


def _final_norm(h3, g, *, first_tile, seq):
    bsz = h3.shape[0]
    T = SEQ_TILE
    return pl.pallas_call(
        _final_norm_kernel,
        grid=(bsz, seq // T),
        in_specs=[pl.BlockSpec((1, T, D_MODEL), lambda b, j: (b, j + first_tile, 0)),
                  pl.BlockSpec((1, D_MODEL), lambda b, j: (0, 0))],
        out_specs=pl.BlockSpec((1, T, D_MODEL), lambda b, j: (b, j, 0)),
        out_shape=jax.ShapeDtypeStruct((bsz, seq, D_MODEL), F32),
        name="final_norm",
    )(h3, g)


def _block_diag(blocks):
    n, r, c = blocks.shape
    out = jnp.zeros((n * r, n * c), blocks.dtype)
    for i in range(n):
        out = out.at[i * r:(i + 1) * r, i * c:(i + 1) * c].set(blocks[i])
    return out


def kernel(x, meta, norm_mix, norm_ffn, norm_final, w_in, w_branch, w_out, a_mu, a_w_up, a_w0, a_a_up, a_a0, a_g_up, a_kk, a_ka, a_rk, a_ln_w, a_ln_b, a_vres_down, a_vres_up, a_vres0, b_mix, b_scale, c_sinks, d_lower_bounds, d_norm, w_ffn_up, w_ffn_down):
    bsz, seq, _ = x.shape
    depth = w_in.shape[0]
    T = SEQ_TILE
    L = N_META + seq
    pad = (-L) % T
    lp = L + pad
    assert (pad + N_META) % T == 0 and seq % T == 0
    h = jnp.concatenate([jnp.zeros((bsz, pad, D_MODEL), F32),
                         jnp.broadcast_to(meta.astype(F32)[None], (bsz, N_META, D_MODEL)),
                         x.astype(F32)], axis=1).reshape(bsz * lp, D_MODEL)
    lb_w = jax.nn.softmax(d_lower_bounds.astype(F32), axis=0)
    lb_table = jnp.cumsum(lb_w, axis=0) - lb_w[0]
    row2 = lambda t: t.reshape(1, -1).astype(F32)
    v_first = None
    for l in range(depth):
        proj2 = _in_proj(h, row2(norm_mix[l]), w_in[l].astype(BF16), pad=pad, lp=lp)
        proj3 = proj2.reshape(bsz, lp, IN_COLS)
        wwa = jnp.zeros((128, 2 * A_WIDTH), F32)
        wwa = wwa.at[:64, :A_WIDTH].set(a_w_up[l]).at[64:, A_WIDTH:].set(a_a_up[l])
        pa = dict(mu=row2(a_mu[l]), wwa=wwa.astype(BF16), w0=row2(a_w0[l]), a0=row2(a_a0[l]),
                  gup=a_g_up[l].astype(BF16), kk=row2(a_kk[l]), ka=row2(a_ka[l]), rk=row2(a_rk[l]),
                  lnw=row2(a_ln_w[l]), lnb=row2(a_ln_b[l]))
        if l == 0:
            y_a, v_first = _rwkv(proj3, None, pa, has_vres=False)
        else:
            pa.update(vd=a_vres_down[l - 1].astype(BF16), vu=a_vres_up[l - 1].astype(BF16),
                      v0=row2(a_vres0[l - 1]))
            y_a = _rwkv(proj3, v_first, pa, has_vres=True)
        y_b = _pool(proj3, _block_diag(b_mix[l]).astype(BF16), row2(b_scale[l]), pad=pad)
        y_c = _swa(proj3, c_sinks[l].astype(F32), pad=pad)
        y_d = _hgrn(proj3, row2(lb_table[l]), row2(d_norm[l]), pad=pad)
        flat = lambda t: t.reshape(bsz * lp, t.shape[-1])
        h = _merge_ffn(h, proj2, flat(y_a), flat(y_b), flat(y_c), flat(y_d),
                       w_branch[l].astype(BF16), w_out[l].astype(BF16), row2(norm_ffn[l]),
                       w_ffn_up[l].astype(BF16), w_ffn_down[l].astype(BF16))
    return _final_norm(h.reshape(bsz, lp, D_MODEL), row2(norm_final),
                       first_tile=(pad + N_META) // T, seq=seq)
```

```python
import functools

import jax
import jax.numpy as jnp
import numpy as np
from jax import lax
from jax.experimental import pallas as pl
from jax.experimental.pallas import tpu as pltpu

F32 = jnp.float32
BF16 = jnp.bfloat16

D_MODEL = 1024
N_META = 16
NORM_EPS = 1e-6
MASK_VALUE = -1e30

A_HEADS = 4
A_HEAD_DIM = 64
A_WIDTH = 256
A_GN_EPS = 64e-5
A_COLS = 1024
A_CHUNK = 64
A_BATCH = 2

B_WIDTH = 256
B_GROUP_DIM = 64
B_WINDOWS = (2, 4, 8, 16)
B_HALO = 16

C_HEADS = 8
C_KV_HEADS = 2
C_GROUP = 4
C_HEAD_DIM = 64
C_WIDTH = 512
C_KV_WIDTH = 128
C_COLS = 768

D_HEADS = 4
D_KEY_DIM = 64
D_WIDTH = 256
D_COLS = 1024
D_LEVELS = 7

D_FF = 2816
GATE_COLS = 4 * D_MODEL
OFF_A = GATE_COLS
OFF_B = OFF_A + A_COLS
OFF_C = OFF_B + B_WIDTH
OFF_D = OFF_C + C_COLS
IN_COLS = OFF_D + D_COLS
MIX_WIDTH = 1280

SEQ_TILE = 128
ROW_TILE = 384
VMEM_LIMIT = 56 * 1024 * 1024


def _dot(a, b):
    return jnp.dot(a.astype(BF16), b.astype(BF16), preferred_element_type=F32)


def _dot_nt(a, b):
    return lax.dot_general(a.astype(BF16), b.astype(BF16), (((1,), (1,)), ((), ())),
                           preferred_element_type=F32)


def _dot_tn(a, b):
    return lax.dot_general(a.astype(BF16), b.astype(BF16), (((0,), (0,)), ((), ())),
                           preferred_element_type=F32)


def _split2(x):
    hi = x.astype(BF16)
    lo = (x - hi.astype(F32)).astype(BF16)
    return hi, lo


def _split3(x):
    hi = x.astype(BF16)
    r = x - hi.astype(F32)
    mid = r.astype(BF16)
    lo = (r - mid.astype(F32)).astype(BF16)
    return hi, mid, lo


def _const_dot(c, x):
    hi, mid, lo = _split3(x)
    d = lambda p: jnp.dot(c, p, preferred_element_type=F32)
    return d(hi) + d(mid) + d(lo)


def _mm(a, b):
    return jnp.dot(a, b, preferred_element_type=F32)


def _mm_tn(a, b):
    return lax.dot_general(a, b, (((0,), (0,)), ((), ())), preferred_element_type=F32)


def _d3(a, b):
    return _mm(a[0], b[0]) + _mm(a[0], b[1]) + _mm(a[1], b[0])


def _unit_lower_inverses(mats, ri, ci):
    in16 = (ri // 16) == (ci // 16)
    in32 = (ri // 32) == (ci // 32)
    eye = jnp.where(ri == ci, 1.0, 0.0)
    bf = lambda m: m.astype(BF16)
    x1 = [jnp.where(in16, a, 0.0) for a in mats]
    xb = [bf(x) for x in x1]
    t = [eye + x for x in x1]
    for _ in range(3):
        xb = [bf(_mm(x, x)) for x in xb]
        t = [ti + _mm(bf(ti), x) for ti, x in zip(t, xb)]
    for sel in (in32 & ~in16, ~in32):
        tb = [bf(ti) for ti in t]
        lt = [bf(_mm(bf(jnp.where(sel, a, 0.0)), tbi)) for a, tbi in zip(mats, tb)]
        t = [ti + _mm(tbi, lti) for ti, tbi, lti in zip(t, tb, lt)]
    return t


def _head_sum(x, ones_bd):
    hi, lo = _split2(x)
    d = lambda p: jnp.dot(p, ones_bd, preferred_element_type=F32)
    return d(hi) + d(lo)


def _head_ones(width, seg):
    r = lax.broadcasted_iota(jnp.int32, (width, width), 0) // seg
    c = lax.broadcasted_iota(jnp.int32, (width, width), 1) // seg
    return jnp.where(r == c, 1.0, 0.0).astype(BF16)


def _sigmoid(x):
    return 1.0 / (1.0 + jnp.exp(-x))


def _silu(x):
    return x * _sigmoid(x)


def _rms(x, g):
    return x * lax.rsqrt(jnp.mean(x * x, axis=-1, keepdims=True) + NORM_EPS) * g


def _in_proj_kernel(x_ref, g_ref, w_ref, o_ref, *, pad, tiles_per_seq, tm, tn):
    z = _rms(x_ref[...], g_ref[...])
    row = (pl.program_id(0) % tiles_per_seq) * tm + lax.broadcasted_iota(jnp.int32, (tm, 1), 0)
    zb = jnp.where(row >= pad, z, 0.0).astype(BF16)
    for j in range(IN_COLS // tn):
        o_ref[:, j * tn:(j + 1) * tn] = jnp.dot(zb, w_ref[:, j * tn:(j + 1) * tn],
                                                preferred_element_type=F32)


def _in_proj(h2, g, w, *, pad, lp):
    m = h2.shape[0]
    tm = ROW_TILE
    assert lp % tm == 0
    return pl.pallas_call(
        functools.partial(_in_proj_kernel, pad=pad, tiles_per_seq=lp // tm, tm=tm, tn=1024),
        grid=(m // tm,),
        in_specs=[pl.BlockSpec((tm, D_MODEL), lambda i: (i, 0)),
                  pl.BlockSpec((1, D_MODEL), lambda i: (0, 0)),
                  pl.BlockSpec((D_MODEL, IN_COLS), lambda i: (0, 0), pipeline_mode=pl.Buffered(1))],
        out_specs=pl.BlockSpec((tm, IN_COLS), lambda i: (i, 0)),
        out_shape=jax.ShapeDtypeStruct((m, IN_COLS), F32),
        compiler_params=pltpu.CompilerParams(dimension_semantics=("arbitrary",),
                                             vmem_limit_bytes=VMEM_LIMIT),
        name="in_proj",
    )(h2, g, w)


def _rwkv_kernel(*refs, has_vres):
    if has_vres:
        (u_ref, vf_ref, mu_ref, wwa_ref, w0_ref, a0_ref, gup_ref, kk_ref, ka_ref, rk_ref,
         lnw_ref, lnb_ref, vd_ref, vu_ref, v0_ref, y_ref, state_sc, carry_sc) = refs
    else:
        (u_ref, mu_ref, wwa_ref, w0_ref, a0_ref, gup_ref, kk_ref, ka_ref, rk_ref,
         lnw_ref, lnb_ref, y_ref, vout_ref, state_sc, carry_sc) = refs
    T = SEQ_TILE
    C = A_CHUNK
    N = A_HEAD_DIM
    W = A_WIDTH
    NB = A_BATCH
    R = NB * T

    @pl.when(pl.program_id(1) == 0)
    def _():
        state_sc[...] = jnp.zeros_like(state_sc)
        carry_sc[...] = jnp.zeros_like(carry_sc)

    u = u_ref[...].reshape(R, A_COLS)
    row = lax.broadcasted_iota(jnp.int32, (R, 1), 0)
    prev = pltpu.roll(u, 1, axis=0)
    for b in range(NB):
        prev = jnp.where(row == b * T, carry_sc[b:b + 1, :], prev)
        carry_sc[b:b + 1, :] = u[(b + 1) * T - 1:(b + 1) * T, :]
    x = u + (prev - u) * mu_ref[...]
    r = x[:, 0:W]
    k = x[:, W:2 * W]
    v = x[:, 2 * W:3 * W]
    slab = x[:, 3 * W:3 * W + 128]
    gd = x[:, 3 * W + 128:]
    lane = lax.broadcasted_iota(jnp.int32, (1, 128), 1)
    slab = jnp.where(lane < 64, jnp.tanh(slab), slab)
    wa = _dot(slab, wwa_ref[...])
    zw = -(w0_ref[...] + wa[:, :W])
    softplus = jnp.maximum(zw, 0.0) + jnp.log(1.0 + jnp.exp(-jnp.abs(zw)))
    logw = -jnp.exp(-softplus - 0.5)
    a = _sigmoid(a0_ref[...] + wa[:, W:])
    g = _dot(_sigmoid(gd), gup_ref[...])
    if has_vres:
        mix = _sigmoid(v0_ref[...] + _dot(_dot(v, vd_ref[...]), vu_ref[...]))
        v = v + (vf_ref[...].reshape(R, W) - v) * mix
    else:
        vout_ref[...] = v.reshape(NB, T, W)

    ones_bd = _head_ones(W, N)
    kkr = k * kk_ref[...]
    kk = kkr / jnp.maximum(jnp.sqrt(_head_sum(kkr * kkr, ones_bd)), 1e-12)
    k2 = k * (1.0 + (a - 1.0) * ka_ref[...])
    bhat = kk * a

    ri = lax.broadcasted_iota(jnp.int32, (W, W), 0)
    ci = lax.broadcasted_iota(jnp.int32, (W, W), 1)
    same_head = (ri // C) == (ci // C)
    tw = lax.broadcasted_iota(jnp.int32, (C, W), 0)
    sw = lax.broadcasted_iota(jnp.int32, (C, W), 1) % C
    tri = jnp.where(lax.broadcasted_iota(jnp.int32, (C, C), 1)
                    <= lax.broadcasted_iota(jnp.int32, (C, C), 0), 1.0, 0.0).astype(BF16)

    def stack(t):
        return jnp.where(same_head, jnp.concatenate([t] * A_HEADS, axis=0), 0.0)

    pre = []
    for i in range(R // C):
        sl = slice(i * C, (i + 1) * C)
        lw = logw[sl]
        cl = _const_dot(tri, lw)
        cl_last = cl[C - 1:C, :]
        e_neg = jnp.exp(-cl)
        e_last = jnp.exp(cl_last - cl)
        at = -kk[sl] * jnp.exp(cl - lw)
        rt = r[sl] * jnp.exp(cl)
        v_s = stack(v[sl]).astype(BF16)
        aa = _dot_nt(jnp.concatenate([at, rt], axis=0),
                     jnp.concatenate([stack(bhat[sl] * e_neg), stack(k2[sl] * e_neg)], axis=0))
        a_ab = stack(jnp.where(sw < tw, aa[:C, :W], 0.0))
        a_ak = stack(jnp.where(sw < tw, aa[:C, W:], 0.0)).astype(BF16)
        a_rb = jnp.where(sw <= tw, aa[C:, :W], 0.0).astype(BF16)
        a_rk = jnp.where(sw <= tw, aa[C:, W:], 0.0).astype(BF16)
        rhs = jnp.concatenate([stack(at), _mm(a_ak, v_s)], axis=1).astype(BF16)
        btil = stack(bhat[sl] * e_last).astype(BF16)
        ktil = stack(k2[sl] * e_last).astype(BF16)
        pre.append((a_ab, rhs, a_rb, a_rk, v_s, btil, ktil, rt, jnp.exp(cl_last)))

    tinvs = _unit_lower_inverses([p[0] for p in pre], ri, ci)

    terms = []
    for (_, rhs, a_rb, a_rk, v_s, btil, ktil, rt, gam), tinv in zip(pre, tinvs):
        wu = _mm(tinv.astype(BF16), rhs)
        w_t = wu[:, :W].astype(BF16)
        u_t = wu[:, W:].astype(BF16)
        m_mat = jnp.where(ri == ci, gam, 0.0) + _mm_tn(btil, w_t)
        n_mat = _mm_tn(btil, u_t) + _mm_tn(ktil, v_s)
        q_mat = rt + _mm(a_rb, w_t)
        p_mat = _mm(a_rb, u_t) + _mm(a_rk, v_s)
        terms.append((m_mat, n_mat, q_mat, p_mat))

    outs = []
    for b in range(NB):
        st = state_sc[b]
        for m_mat, n_mat, q_mat, p_mat in terms[b * (T // C):(b + 1) * (T // C)]:
            outs.append(_dot(q_mat, st) + p_mat)
            st = _d3(_split2(m_mat), _split2(st)) + n_mat
        state_sc[b] = st

    o = jnp.concatenate(outs, axis=0)
    mean = _head_sum(o, ones_bd) * (1.0 / N)
    d = o - mean
    var = _head_sum(d * d, ones_bd) * (1.0 / N)
    o = d * lax.rsqrt(var + A_GN_EPS) * lnw_ref[...] + lnb_ref[...]
    bonus = _head_sum(r * k2 * rk_ref[...], ones_bd) * v
    y_ref[...] = ((o + bonus) * g).reshape(NB, T, W)


def _rwkv(proj3, v_first, p, *, has_vres):
    bsz, lp, _ = proj3.shape
    T = SEQ_TILE
    NB = A_BATCH
    assert bsz % NB == 0
    seq_spec = lambda w, col: pl.BlockSpec((NB, T, w), lambda b, j: (b, j, col))
    full = lambda arr: pl.BlockSpec(arr.shape, lambda b, j: (0,) * arr.ndim)
    args = [proj3]
    specs = [seq_spec(A_COLS, OFF_A // A_COLS)]
    if has_vres:
        args.append(v_first)
        specs.append(seq_spec(A_WIDTH, 0))
    names = ["mu", "wwa", "w0", "a0", "gup", "kk", "ka", "rk", "lnw", "lnb"]
    if has_vres:
        names += ["vd", "vu", "v0"]
    for n in names:
        args.append(p[n])
        specs.append(full(p[n]))
    y_shape = jax.ShapeDtypeStruct((bsz, lp, A_WIDTH), F32)
    out_shape = y_shape if has_vres else (y_shape, y_shape)
    out_specs = seq_spec(A_WIDTH, 0) if has_vres else (seq_spec(A_WIDTH, 0), seq_spec(A_WIDTH, 0))
    return pl.pallas_call(
        functools.partial(_rwkv_kernel, has_vres=has_vres),
        grid=(bsz // NB, lp // T),
        in_specs=specs,
        out_specs=out_specs,
        out_shape=out_shape,
        scratch_shapes=[pltpu.VMEM((NB, A_WIDTH, A_WIDTH), F32),
                        pltpu.VMEM((8, A_COLS), F32)],
        compiler_params=pltpu.CompilerParams(dimension_semantics=("arbitrary", "arbitrary"),
                                             vmem_limit_bytes=VMEM_LIMIT),
        name="rwkv7",
    )(*args)


def _pool_kernel(u_ref, mix_ref, scale_ref, y_ref, buf, *, pad):
    T = SEQ_TILE
    j = pl.program_id(1)

    @pl.when(j == 0)
    def _():
        buf[0:B_HALO, :] = jnp.zeros((B_HALO, B_WIDTH), F32)

    u = u_ref[0]
    buf[B_HALO:B_HALO + T, :] = u
    lane_group = lax.broadcasted_iota(jnp.int32, (1, B_WIDTH), 1) // B_GROUP_DIM
    acc = u
    win = jnp.zeros_like(u)
    for s in range(1, max(B_WINDOWS)):
        acc = acc + buf[B_HALO - s:B_HALO - s + T, :]
        if s + 1 in B_WINDOWS:
            win = jnp.where(lane_group == B_WINDOWS.index(s + 1), acc, win)
    wlane = jnp.zeros((1, B_WIDTH), jnp.int32)
    for gi, w in enumerate(B_WINDOWS):
        wlane = jnp.where(lane_group == gi, w, wlane)
    t_real = j * T + lax.broadcasted_iota(jnp.int32, (T, 1), 0) - pad
    cnt = jnp.minimum(jnp.maximum(t_real + 1, 1), wlane).astype(F32)
    pooled = win / cnt - u
    y_ref[0] = _dot(pooled, mix_ref[...]) * scale_ref[...]
    buf[0:B_HALO, :] = u[T - B_HALO:, :]


def _pool(proj3, mix_bd, scale, *, pad):
    bsz, lp, _ = proj3.shape
    T = SEQ_TILE
    return pl.pallas_call(
        functools.partial(_pool_kernel, pad=pad),
        grid=(bsz, lp // T),
        in_specs=[pl.BlockSpec((1, T, B_WIDTH), lambda b, j: (b, j, OFF_B // B_WIDTH)),
                  pl.BlockSpec((B_WIDTH, B_WIDTH), lambda b, j: (0, 0)),
                  pl.BlockSpec((1, B_WIDTH), lambda b, j: (0, 0))],
        out_specs=pl.BlockSpec((1, T, B_WIDTH), lambda b, j: (b, j, 0)),
        out_shape=jax.ShapeDtypeStruct((bsz, lp, B_WIDTH), F32),
        scratch_shapes=[pltpu.VMEM((B_HALO + T, B_WIDTH), F32)],
        compiler_params=pltpu.CompilerParams(dimension_semantics=("arbitrary", "arbitrary")),
        name="pool",
    )(proj3, mix_bd, scale)


def _swa_kernel(sinks_ref, u_ref, y_ref, kv_sc, *, pad):
    T = SEQ_TILE
    j = pl.program_id(1)

    @pl.when(j == 0)
    def _():
        kv_sc[...] = jnp.zeros_like(kv_sc)

    blk = u_ref[0]
    kv = blk[:, C_WIDTH:]
    kw = jnp.concatenate([kv_sc[:, :C_KV_WIDTH], kv[:, :C_KV_WIDTH]], axis=0).astype(BF16)
    vw = jnp.concatenate([kv_sc[:, C_KV_WIDTH:], kv[:, C_KV_WIDTH:]], axis=0).astype(BF16)
    kv_sc[...] = kv
    qi = lax.broadcasted_iota(jnp.int32, (T, 2 * T), 0)
    si = lax.broadcasted_iota(jnp.int32, (T, 2 * T), 1)
    dist = T + qi - si
    mask = (dist >= 0) & (dist < T) & ((j - 1) * T + si >= pad)
    distf = dist.astype(F32)
    for hk in range(C_KV_HEADS):
        ks = slice(hk * C_HEAD_DIM, (hk + 1) * C_HEAD_DIM)
        k_h, v_h = kw[:, ks], vw[:, ks]
        for gq in range(C_GROUP):
            hq = hk * C_GROUP + gq
            qs = slice(hq * C_HEAD_DIM, (hq + 1) * C_HEAD_DIM)
            slope = 2.0 ** (-8.0 * (hq + 1) / C_HEADS)
            sink = sinks_ref[hq]
            s = _dot_nt(blk[:, qs] * (C_HEAD_DIM ** -0.5), k_h)
            logits = jnp.where(mask, s - slope * distf, MASK_VALUE)
            m = jnp.maximum(jnp.max(logits, axis=-1, keepdims=True), sink)
            p = jnp.exp(logits - m)
            denom = jnp.sum(p, axis=-1, keepdims=True) + jnp.exp(sink - m)
            y_ref[0, :, qs] = _dot(p / denom, v_h)


def _swa(proj3, sinks, *, pad):
    bsz, lp, _ = proj3.shape
    T = SEQ_TILE
    return pl.pallas_call(
        functools.partial(_swa_kernel, pad=pad),
        grid=(bsz, lp // T),
        in_specs=[pl.BlockSpec(memory_space=pltpu.SMEM),
                  pl.BlockSpec((1, T, C_COLS), lambda b, j: (b, j, OFF_C // C_COLS))],
        out_specs=pl.BlockSpec((1, T, C_WIDTH), lambda b, j: (b, j, 0)),
        out_shape=jax.ShapeDtypeStruct((bsz, lp, C_WIDTH), F32),
        scratch_shapes=[pltpu.VMEM((T, 2 * C_KV_WIDTH), F32)],
        compiler_params=pltpu.CompilerParams(dimension_semantics=("arbitrary", "arbitrary")),
        name="swa",
    )(sinks, proj3)


def _hgrn_kernel(u_ref, lb_ref, ng_ref, sums_ref, level_ref, y_ref, state_sc, *, pad):
    T = SEQ_TILE
    N = D_KEY_DIM
    j = pl.program_id(1)

    @pl.when(j == 0)
    def _():
        state_sc[...] = jnp.zeros_like(state_sc)

    u = u_ref[0]
    lb = lb_ref[...]
    q = _silu(u[:, :D_WIDTH])
    fpre = u[:, D_WIDTH:2 * D_WIDTH]
    vin = u[:, 2 * D_WIDTH:3 * D_WIDTH]
    gate = u[:, 3 * D_WIDTH:]
    f = lb + (1.0 - lb) * _sigmoid(fpre)
    t_real = j * T + lax.broadcasted_iota(jnp.int32, (T, 1), 0) - pad
    logf = jnp.where(t_real >= 0, jnp.log(jnp.maximum(f, 1e-30)), 0.0)
    kx = (1.0 - lb) * _sigmoid(-fpre)

    hi, lo = _split2(logf)
    sums = _mm(sums_ref[...], hi) + _mm(sums_ref[...], lo)
    same_head = (lax.broadcasted_iota(jnp.int32, (D_WIDTH, D_WIDTH), 0) // N
                 == lax.broadcasted_iota(jnp.int32, (D_WIDTH, D_WIDTH), 1) // N)
    head_rows = (lax.broadcasted_iota(jnp.int32, (D_HEADS * T, D_WIDTH), 0) // T
                 == lax.broadcasted_iota(jnp.int32, (D_HEADS * T, D_WIDTH), 1) // N)
    stack = lambda t: jnp.where(head_rows, jnp.concatenate([t] * D_HEADS, axis=0), 0.0)
    k_st = stack(kx)
    v_st = stack(vin).astype(BF16)
    lv = level_ref[...]
    att = jnp.where(lv == 0, _dot_nt(q, k_st), 0.0)
    for l in range(D_LEVELS):
        if l == 0:
            q_l, k_l = q * jnp.exp(logf), k_st
        else:
            p_h = sums[(2 * l - 2) * T:(2 * l - 1) * T]
            x_h = sums[(2 * l - 1) * T:2 * l * T]
            q_l = q * jnp.exp(p_h)
            k_l = k_st * jnp.concatenate([jnp.exp(x_h)] * D_HEADS, axis=0)
        att = jnp.where(lv == l + 1, _dot_nt(q_l, k_l), att)
    b = sums[(2 * D_LEVELS - 2) * T:(2 * D_LEVELS - 1) * T]
    b_rev = sums[(2 * D_LEVELS - 1) * T:]
    st = state_sc[...]
    o = _dot(att, v_st) + _dot_nt(q * jnp.exp(b), st)
    upd = jnp.where(same_head, _dot_tn(vin, kx * jnp.exp(b_rev)), 0.0)
    state_sc[...] = st * jnp.exp(b[T - 1:T, :]) + upd

    ones_bd = _head_ones(D_WIDTH, N)
    ms = _head_sum(o * o, ones_bd) * (1.0 / N)
    y_ref[0] = o * lax.rsqrt(ms + NORM_EPS) * ng_ref[...] * _silu(gate)


def _hgrn_constants():
    T = SEQ_TILE
    t = np.arange(T)[:, None]
    i = np.arange(T)[None, :]
    mats = []
    for l in list(range(1, D_LEVELS)) + [D_LEVELS]:
        h = 2 ** l
        same = (t // h) == (i // h)
        mats += [same & (i <= t), same & (i > t)]
    sums = np.concatenate(mats, axis=0).astype(np.float32)
    x = t ^ i
    level = np.where(i == t, 0, np.where(i < t, np.floor(np.log2(np.maximum(x, 1))).astype(np.int64) + 1, -1))
    return jnp.asarray(sums, BF16), jnp.asarray(np.tile(level, (1, D_HEADS)), jnp.int32)


def _hgrn(proj3, lb, norm_g, *, pad):
    bsz, lp, _ = proj3.shape
    T = SEQ_TILE
    sums, level = _hgrn_constants()
    const = lambda arr: pl.BlockSpec(arr.shape, lambda b, j: (0, 0))
    return pl.pallas_call(
        functools.partial(_hgrn_kernel, pad=pad),
        grid=(bsz, lp // T),
        in_specs=[pl.BlockSpec((1, T, D_COLS), lambda b, j: (b, j, OFF_D // D_COLS)),
                  const(lb), const(norm_g), const(sums), const(level)],
        out_specs=pl.BlockSpec((1, T, D_WIDTH), lambda b, j: (b, j, 0)),
        out_shape=jax.ShapeDtypeStruct((bsz, lp, D_WIDTH), F32),
        scratch_shapes=[pltpu.VMEM((D_WIDTH, D_WIDTH), F32)],
        compiler_params=pltpu.CompilerParams(dimension_semantics=("arbitrary", "arbitrary"),
                                             vmem_limit_bytes=VMEM_LIMIT),
        name="hgrn2",
    )(proj3, lb, norm_g, sums, level)


def _merge_ffn_kernel(h_ref, gates_ref, ya_ref, yb_ref, yc_ref, yd_ref, wb_ref, wo_ref,
                      gf_ref, wu_ref, wd_ref, o_ref, *, ff_chunk):
    merged = None
    row = 0
    for bi, y_ref in enumerate((ya_ref, yb_ref, yc_ref, yd_ref)):
        w = y_ref.shape[-1]
        part = _sigmoid(gates_ref[:, bi * D_MODEL:(bi + 1) * D_MODEL]) * jnp.dot(
            y_ref[...].astype(BF16), wb_ref[row:row + w, :], preferred_element_type=F32)
        merged = part if merged is None else merged + part
        row += w
    h = h_ref[...] + jnp.dot(merged.astype(BF16), wo_ref[...], preferred_element_type=F32)
    zb = _rms(h, gf_ref[...]).astype(BF16)
    acc = h
    for c in range(D_FF // ff_chunk):
        cs = slice(c * ff_chunk, (c + 1) * ff_chunk)
        gu = jnp.dot(zb, wu_ref[:, cs], preferred_element_type=F32)
        up = jnp.dot(zb, wu_ref[:, D_FF + c * ff_chunk:D_FF + (c + 1) * ff_chunk],
                     preferred_element_type=F32)
        acc = acc + jnp.dot((_silu(gu) * up).astype(BF16), wd_ref[cs, :],
                            preferred_element_type=F32)
    o_ref[...] = acc


def _merge_ffn(h2, proj2, ya, yb, yc, yd, wb, wo, gf, wu, wd):
    m = h2.shape[0]
    tm = ROW_TILE
    rows = lambda w, col=0: pl.BlockSpec((tm, w), lambda i: (i, col))
    const = lambda arr: pl.BlockSpec(arr.shape, lambda i: (0, 0), pipeline_mode=pl.Buffered(1))
    return pl.pallas_call(
        functools.partial(_merge_ffn_kernel, ff_chunk=256),
        grid=(m // tm,),
        in_specs=[rows(D_MODEL), rows(GATE_COLS), rows(A_WIDTH), rows(B_WIDTH), rows(C_WIDTH),
                  rows(D_WIDTH), const(wb), const(wo), const(gf), const(wu), const(wd)],
        out_specs=rows(D_MODEL),
        out_shape=jax.ShapeDtypeStruct((m, D_MODEL), F32),
        compiler_params=pltpu.CompilerParams(dimension_semantics=("arbitrary",),
                                             vmem_limit_bytes=VMEM_LIMIT),
        name="merge_ffn",
    )(h2, proj2, ya, yb, yc, yd, wb, wo, gf, wu, wd)


def _final_norm_kernel(h_ref, g_ref, o_ref):
    o_ref[0] = _rms(h_ref[0], g_ref[...])


def _final_norm(h3, g, *, first_tile, seq):
    bsz = h3.shape[0]
    T = SEQ_TILE
    return pl.pallas_call(
        _final_norm_kernel,
        grid=(bsz, seq // T),
        in_specs=[pl.BlockSpec((1, T, D_MODEL), lambda b, j: (b, j + first_tile, 0)),
                  pl.BlockSpec((1, D_MODEL), lambda b, j: (0, 0))],
        out_specs=pl.BlockSpec((1, T, D_MODEL), lambda b, j: (b, j, 0)),
        out_shape=jax.ShapeDtypeStruct((bsz, seq, D_MODEL), F32),
        name="final_norm",
    )(h3, g)


def _block_diag(blocks):
    n, r, c = blocks.shape
    out = jnp.zeros((n * r, n * c), blocks.dtype)
    for i in range(n):
        out = out.at[i * r:(i + 1) * r, i * c:(i + 1) * c].set(blocks[i])
    return out


def kernel(x, meta, norm_mix, norm_ffn, norm_final, w_in, w_branch, w_out, a_mu, a_w_up, a_w0, a_a_up, a_a0, a_g_up, a_kk, a_ka, a_rk, a_ln_w, a_ln_b, a_vres_down, a_vres_up, a_vres0, b_mix, b_scale, c_sinks, d_lower_bounds, d_norm, w_ffn_up, w_ffn_down):
    bsz, seq, _ = x.shape
    depth = w_in.shape[0]
    T = SEQ_TILE
    L = N_META + seq
    pad = (-L) % T
    lp = L + pad
    assert (pad + N_META) % T == 0 and seq % T == 0
    h = jnp.concatenate([jnp.zeros((bsz, pad, D_MODEL), F32),
                         jnp.broadcast_to(meta.astype(F32)[None], (bsz, N_META, D_MODEL)),
                         x.astype(F32)], axis=1).reshape(bsz * lp, D_MODEL)
    lb_w = jax.nn.softmax(d_lower_bounds.astype(F32), axis=0)
    lb_table = jnp.cumsum(lb_w, axis=0) - lb_w[0]
    row2 = lambda t: t.reshape(1, -1).astype(F32)
    v_first = None
    for l in range(depth):
        proj2 = _in_proj(h, row2(norm_mix[l]), w_in[l].astype(BF16), pad=pad, lp=lp)
        proj3 = proj2.reshape(bsz, lp, IN_COLS)
        wwa = jnp.zeros((128, 2 * A_WIDTH), F32)
        wwa = wwa.at[:64, :A_WIDTH].set(a_w_up[l]).at[64:, A_WIDTH:].set(a_a_up[l])
        pa = dict(mu=row2(a_mu[l]), wwa=wwa.astype(BF16), w0=row2(a_w0[l]), a0=row2(a_a0[l]),
                  gup=a_g_up[l].astype(BF16), kk=row2(a_kk[l]), ka=row2(a_ka[l]), rk=row2(a_rk[l]),
                  lnw=row2(a_ln_w[l]), lnb=row2(a_ln_b[l]))
        if l == 0:
            y_a, v_first = _rwkv(proj3, None, pa, has_vres=False)
        else:
            pa.update(vd=a_vres_down[l - 1].astype(BF16), vu=a_vres_up[l - 1].astype(BF16),
                      v0=row2(a_vres0[l - 1]))
            y_a = _rwkv(proj3, v_first, pa, has_vres=True)
        y_b = _pool(proj3, _block_diag(b_mix[l]).astype(BF16), row2(b_scale[l]), pad=pad)
        y_c = _swa(proj3, c_sinks[l].astype(F32), pad=pad)
        y_d = _hgrn(proj3, row2(lb_table[l]), row2(d_norm[l]), pad=pad)
        flat = lambda t: t.reshape(bsz * lp, t.shape[-1])
        h = _merge_ffn(h, proj2, flat(y_a), flat(y_b), flat(y_c), flat(y_d),
                       w_branch[l].astype(BF16), w_out[l].astype(BF16), row2(norm_ffn[l]),
                       w_ffn_up[l].astype(BF16), w_ffn_down[l].astype(BF16))
    return _final_norm(h.reshape(bsz, lp, D_MODEL), row2(norm_final),
                       first_tile=(pad + N_META) // T, seq=seq)
```

```python
import functools

import jax
import jax.numpy as jnp
import numpy as np
from jax import lax
from jax.experimental import pallas as pl
from jax.experimental.pallas import tpu as pltpu

F32 = jnp.float32
BF16 = jnp.bfloat16

D_MODEL = 1024
N_META = 16
NORM_EPS = 1e-6
MASK_VALUE = -1e30

A_HEADS = 4
A_HEAD_DIM = 64
A_WIDTH = 256
A_GN_EPS = 64e-5
A_COLS = 1024
A_CHUNK = 64
A_BATCH = 4
A_GROUP = 2

B_WIDTH = 256
B_GROUP_DIM = 64
B_WINDOWS = (2, 4, 8, 16)
B_HALO = 16

C_HEADS = 8
C_KV_HEADS = 2
C_GROUP = 4
C_HEAD_DIM = 64
C_WIDTH = 512
C_KV_WIDTH = 128
C_COLS = 768

D_HEADS = 4
D_KEY_DIM = 64
D_WIDTH = 256
D_COLS = 1024
D_LEVELS = 7

D_FF = 2816
GATE_COLS = 4 * D_MODEL
OFF_A = GATE_COLS
OFF_B = OFF_A + A_COLS
OFF_C = OFF_B + B_WIDTH
OFF_D = OFF_C + C_COLS
IN_COLS = OFF_D + D_COLS
MIX_WIDTH = 1280

SEQ_TILE = 128
ROW_TILE = 384
VMEM_LIMIT = 56 * 1024 * 1024


def _dot(a, b):
    return jnp.dot(a.astype(BF16), b.astype(BF16), preferred_element_type=F32)


def _dot_nt(a, b):
    return lax.dot_general(a.astype(BF16), b.astype(BF16), (((1,), (1,)), ((), ())),
                           preferred_element_type=F32)


def _dot_tn(a, b):
    return lax.dot_general(a.astype(BF16), b.astype(BF16), (((0,), (0,)), ((), ())),
                           preferred_element_type=F32)


def _split2(x):
    hi = x.astype(BF16)
    lo = (x - hi.astype(F32)).astype(BF16)
    return hi, lo


def _split3(x):
    hi = x.astype(BF16)
    r = x - hi.astype(F32)
    mid = r.astype(BF16)
    lo = (r - mid.astype(F32)).astype(BF16)
    return hi, mid, lo


def _const_dot(c, x):
    hi, mid, lo = _split3(x)
    d = lambda p: jnp.dot(c, p, preferred_element_type=F32)
    return d(hi) + d(mid) + d(lo)


def _mm(a, b):
    return jnp.dot(a, b, preferred_element_type=F32)


def _mm_tn(a, b):
    return lax.dot_general(a, b, (((0,), (0,)), ((), ())), preferred_element_type=F32)


def _d3(a, b):
    return _mm(a[0], b[0]) + _mm(a[0], b[1]) + _mm(a[1], b[0])


def _unit_lower_inverses(mats, ri, ci, box):
    in16 = (ri // 16) == (ci // 16)
    in32 = (ri // 32) == (ci // 32)
    diag = ri == ci
    plus_eye = lambda m: jnp.where(diag, 1.0, m).astype(BF16)
    x = [jnp.where(in16, a, 0.0) for a in mats]
    t = [plus_eye(xi) for xi in x]
    xb = [xi.astype(BF16) for xi in x]
    yield
    for _ in range(3):
        x = [_mm(xi, xi) for xi in xb]
        yield
        xb = [xi.astype(BF16) for xi in x]
        t = [_mm(ti, plus_eye(xi)).astype(BF16) for ti, xi in zip(t, x)]
        yield
    for sel in (in32 & ~in16, ~in32):
        lt = [_mm(jnp.where(sel, a, 0.0).astype(BF16), ti) for a, ti in zip(mats, t)]
        yield
        t = [_mm(ti, plus_eye(li)).astype(BF16) for ti, li in zip(t, lt)]
        yield
    box["tinv"] = t


def _head_sum(x, ones_bd):
    hi, lo = _split2(x)
    d = lambda p: jnp.dot(p, ones_bd, preferred_element_type=F32)
    return d(hi) + d(lo)


def _head_ones(width, seg):
    r = lax.broadcasted_iota(jnp.int32, (width, width), 0) // seg
    c = lax.broadcasted_iota(jnp.int32, (width, width), 1) // seg
    return jnp.where(r == c, 1.0, 0.0).astype(BF16)


def _sigmoid(x):
    return 1.0 / (1.0 + jnp.exp(-x))


def _silu(x):
    return x * _sigmoid(x)


def _rms(x, g):
    return x * lax.rsqrt(jnp.mean(x * x, axis=-1, keepdims=True) + NORM_EPS) * g


def _in_proj_kernel(x_ref, g_ref, w_ref, o_ref, *, pad, tiles_per_seq, tm, tn):
    z = _rms(x_ref[...], g_ref[...])
    row = (pl.program_id(0) % tiles_per_seq) * tm + lax.broadcasted_iota(jnp.int32, (tm, 1), 0)
    zb = jnp.where(row >= pad, z, 0.0).astype(BF16)
    for j in range(IN_COLS // tn):
        o_ref[:, j * tn:(j + 1) * tn] = jnp.dot(zb, w_ref[:, j * tn:(j + 1) * tn],
                                                preferred_element_type=F32)


def _in_proj(h2, g, w, *, pad, lp):
    m = h2.shape[0]
    tm = ROW_TILE
    assert lp % tm == 0
    return pl.pallas_call(
        functools.partial(_in_proj_kernel, pad=pad, tiles_per_seq=lp // tm, tm=tm, tn=1024),
        grid=(m // tm,),
        in_specs=[pl.BlockSpec((tm, D_MODEL), lambda i: (i, 0)),
                  pl.BlockSpec((1, D_MODEL), lambda i: (0, 0)),
                  pl.BlockSpec((D_MODEL, IN_COLS), lambda i: (0, 0), pipeline_mode=pl.Buffered(1))],
        out_specs=pl.BlockSpec((tm, IN_COLS), lambda i: (i, 0)),
        out_shape=jax.ShapeDtypeStruct((m, IN_COLS), F32),
        compiler_params=pltpu.CompilerParams(dimension_semantics=("arbitrary",),
                                             vmem_limit_bytes=VMEM_LIMIT),
        name="in_proj",
    )(h2, g, w)


def _rwkv_kernel(*refs, has_vres):
    for parity in (0, 1):
        pl.when(pl.program_id(1) % 2 == parity)(
            functools.partial(_rwkv_step, refs, has_vres=has_vres, wslot=parity))


def _rwkv_step(refs, *, has_vres, wslot):
    handoff = refs[-7:]
    tok_sc, aab_sc, rhs_sc, bkv_sc, arb_sc, rt_sc, gam_sc = handoff
    refs = refs[:-7]
    if has_vres:
        (u_ref, vf_ref, mu_ref, wwa_ref, w0_ref, a0_ref, gup_ref, kk_ref, ka_ref, rk_ref,
         lnw_ref, lnb_ref, vd_ref, vu_ref, v0_ref, y_ref, state_sc, carry_sc) = refs
    else:
        (u_ref, mu_ref, wwa_ref, w0_ref, a0_ref, gup_ref, kk_ref, ka_ref, rk_ref,
         lnw_ref, lnb_ref, y_ref, vout_ref, state_sc, carry_sc) = refs
    T = SEQ_TILE
    C = A_CHUNK
    N = A_HEAD_DIM
    W = A_WIDTH
    NB = A_BATCH
    R = NB * T
    NCH = R // C
    j = pl.program_id(1)

    @pl.when((pl.program_id(0) == 0) & (j == 0))
    def _():
        state_sc[...] = jnp.zeros_like(state_sc)
        for ref in handoff:
            ref[...] = jnp.zeros_like(ref)

    @pl.when(j == 0)
    def _():
        carry_sc[...] = jnp.zeros_like(carry_sc)

    rslot = 1 - wslot

    u = u_ref[...].reshape(R, A_COLS)
    row = lax.broadcasted_iota(jnp.int32, (R, 1), 0)
    prev = pltpu.roll(u, 1, axis=0)
    for b in range(NB):
        prev = jnp.where(row == b * T, carry_sc[b:b + 1, :], prev)
        carry_sc[b:b + 1, :] = u[(b + 1) * T - 1:(b + 1) * T, :]
    ones_bd = _head_ones(W, N)
    lane = lax.broadcasted_iota(jnp.int32, (1, 128), 1)

    def token_stage(b):
        rows = slice(b * T, (b + 1) * T)
        x = u[rows] + (prev[rows] - u[rows]) * mu_ref[...]
        r = x[:, 0:W]
        k = x[:, W:2 * W]
        v = x[:, 2 * W:3 * W]
        slab = x[:, 3 * W:3 * W + 128]
        gd = x[:, 3 * W + 128:]
        slab = jnp.where(lane < 64, jnp.tanh(slab), slab)
        wa = _dot(slab, wwa_ref[...])
        zw = -(w0_ref[...] + wa[:, :W])
        softplus = jnp.maximum(zw, 0.0) + jnp.log(1.0 + jnp.exp(-jnp.abs(zw)))
        logw = -jnp.exp(-softplus - 0.5)
        a = _sigmoid(a0_ref[...] + wa[:, W:])
        g = _dot(_sigmoid(gd), gup_ref[...])
        if has_vres:
            mix = _sigmoid(v0_ref[...] + _dot(_dot(v, vd_ref[...]), vu_ref[...]))
            v = v + (vf_ref[b] - v) * mix
        else:
            vout_ref[b] = v
        kkr = k * kk_ref[...]
        kk = kkr / jnp.maximum(jnp.sqrt(_head_sum(kkr * kkr, ones_bd)), 1e-12)
        k2 = k * (1.0 + (a - 1.0) * ka_ref[...])
        return dict(r=r, v=v, kk=kk, k2=k2, bhat=kk * a, logw=logw, g=g)

    ri = lax.broadcasted_iota(jnp.int32, (W, W), 0)
    ci = lax.broadcasted_iota(jnp.int32, (W, W), 1)
    same_head = (ri // C) == (ci // C)
    tw = lax.broadcasted_iota(jnp.int32, (C, W), 0)
    sw = lax.broadcasted_iota(jnp.int32, (C, W), 1) % C
    tri = jnp.where(lax.broadcasted_iota(jnp.int32, (C, C), 1)
                    <= lax.broadcasted_iota(jnp.int32, (C, C), 0), 1.0, 0.0).astype(BF16)

    def stack(t):
        return jnp.where(same_head, jnp.concatenate([t] * A_HEADS, axis=0), 0.0)

    def chunk_inputs(tok, i):
        sl = slice(i * C, (i + 1) * C)
        lw = tok["logw"][sl]
        bhat, k2 = tok["bhat"][sl], tok["k2"][sl]
        cl = _const_dot(tri, lw)
        cl_last = cl[C - 1:C, :]
        e_neg = jnp.exp(-cl)
        e_last = jnp.exp(cl_last - cl)
        at = -tok["kk"][sl] * jnp.exp(cl - lw)
        rt = tok["r"][sl] * jnp.exp(cl)
        v_s = stack(tok["v"][sl]).astype(BF16)
        aa = _dot_nt(jnp.concatenate([at, rt], axis=0),
                     jnp.concatenate([stack(bhat * e_neg), stack(k2 * e_neg)], axis=0))
        a_ab = stack(jnp.where(sw < tw, aa[:C, :W], 0.0))
        a_ak = stack(jnp.where(sw < tw, aa[:C, W:], 0.0)).astype(BF16)
        a_rb = jnp.where(sw <= tw, aa[C:, :W], 0.0).astype(BF16)
        a_rk = jnp.where(sw <= tw, aa[C:, W:], 0.0).astype(BF16)
        rhs = jnp.concatenate([stack(at), _mm(a_ak, v_s)], axis=1).astype(BF16)
        btil = stack(bhat * e_last).astype(BF16)
        ktil = stack(k2 * e_last).astype(BF16)
        return a_ab, rhs, a_rb, a_rk, v_s, btil, ktil, rt, jnp.exp(cl_last)

    def chunk_terms(inputs, tinv):
        _, rhs, a_rb, a_rk, v_s, btil, ktil, rt, gam = inputs
        wu = _mm(tinv, rhs)
        w_t = wu[:, :W].astype(BF16)
        u_t = wu[:, W:].astype(BF16)
        g_col = jnp.sum(jnp.where(ri == ci, gam, 0.0), axis=1, keepdims=True)
        m_mat = (g_col, _mm_tn(btil, w_t).astype(BF16))
        n_mat = _mm_tn(btil, u_t) + _mm_tn(ktil, v_s)
        q_mat = rt + _mm(a_rb, w_t)
        p_mat = _mm(a_rb, u_t) + _mm(a_rk, v_s)
        return m_mat, n_mat, q_mat, p_mat

    def finish(tok, outs):
        o = jnp.concatenate(outs, axis=0)
        mean = _head_sum(o, ones_bd) * (1.0 / N)
        d = o - mean
        var = _head_sum(d * d, ones_bd) * (1.0 / N)
        o = d * lax.rsqrt(var + A_GN_EPS) * lnw_ref[...] + lnb_ref[...]
        bonus = _head_sum(tok["r"] * tok["k2"] * rk_ref[...], ones_bd) * tok["v"]
        y_ref[...] = ((o + bonus) * tok["g"]).reshape(NB, T, W)

    per = T // C

    def finish_previous_tile():
        box = {}
        yield from _unit_lower_inverses([aab_sc[rslot, i] for i in range(NCH)], ri, ci, box)
        outs = []
        for b in range(NB):
            st = state_sc[b]
            for i in range(b * per, (b + 1) * per):
                held = (None, rhs_sc[rslot, i], arb_sc[rslot, i, 0], arb_sc[rslot, i, 1],
                        bkv_sc[rslot, i, 2], bkv_sc[rslot, i, 0], bkv_sc[rslot, i, 1],
                        rt_sc[rslot, i], gam_sc[rslot, i, 0:1])
                m_mat, n_mat, q_mat, p_mat = chunk_terms(held, box["tinv"][i])
                sb = st.astype(BF16)
                outs.append(_mm(q_mat.astype(BF16), sb) + p_mat)
                st = m_mat[0] * st + _mm(m_mat[1], sb) + n_mat
                yield
            state_sc[b] = st
        finish(dict(r=tok_sc[rslot, 0], k2=tok_sc[rslot, 1], v=tok_sc[rslot, 2],
                    g=tok_sc[rslot, 3]), outs)

    def prepare_this_tile():
        for b in range(NB):
            tok = token_stage(b)
            for n, name in enumerate(("r", "k2", "v", "g")):
                tok_sc[wslot, n, b * T:(b + 1) * T] = tok[name]
            yield
            for c in range(per):
                a_ab, rhs, a_rb, a_rk, v_s, btil, ktil, rt, gam = chunk_inputs(tok, c)
                i = b * per + c
                aab_sc[wslot, i] = a_ab
                rhs_sc[wslot, i] = rhs
                arb_sc[wslot, i, 0] = a_rb
                arb_sc[wslot, i, 1] = a_rk
                bkv_sc[wslot, i, 0] = btil
                bkv_sc[wslot, i, 1] = ktil
                bkv_sc[wslot, i, 2] = v_s
                rt_sc[wslot, i] = rt
                gam_sc[wslot, i, 0:1] = gam
                yield

    gens = [finish_previous_tile(), prepare_this_tile()]
    while gens:
        for gen in list(gens):
            if next(gen, "done") == "done":
                gens.remove(gen)

    @pl.when(j == 0)
    def _():
        state_sc[...] = jnp.zeros_like(state_sc)


def _rwkv(proj3, v_first, p, *, has_vres):
    bsz, lp, _ = proj3.shape
    T = SEQ_TILE
    NB = A_BATCH
    assert bsz % NB == 0
    nt = lp // T
    NCH = NB * T // A_CHUNK
    W = A_WIDTH
    in_spec = lambda w, col: pl.BlockSpec((NB, T, w), lambda b, j: (b, jnp.minimum(j, nt - 1), col))
    y_spec = pl.BlockSpec((NB, T, W), lambda b, j: (b, jnp.maximum(j - 1, 0), 0))
    v_spec = pl.BlockSpec((NB, T, W), lambda b, j: (b, j, 0))
    full = lambda arr: pl.BlockSpec(arr.shape, lambda b, j: (0,) * arr.ndim)
    args = [proj3]
    specs = [in_spec(A_COLS, OFF_A // A_COLS)]
    if has_vres:
        args.append(v_first)
        specs.append(in_spec(W, 0))
    names = ["mu", "wwa", "w0", "a0", "gup", "kk", "ka", "rk", "lnw", "lnb"]
    if has_vres:
        names += ["vd", "vu", "v0"]
    for n in names:
        args.append(p[n])
        specs.append(full(p[n]))
    y_shape = jax.ShapeDtypeStruct((bsz, lp, W), F32)
    v_shape = jax.ShapeDtypeStruct((bsz, lp + T, W), F32)
    out_shape = y_shape if has_vres else (y_shape, v_shape)
    out_specs = y_spec if has_vres else (y_spec, v_spec)
    return pl.pallas_call(
        functools.partial(_rwkv_kernel, has_vres=has_vres),
        grid=(bsz // NB, nt + 1),
        in_specs=specs,
        out_specs=out_specs,
        out_shape=out_shape,
        scratch_shapes=[pltpu.VMEM((NB, W, W), F32),
                        pltpu.VMEM((8, A_COLS), F32),
                        pltpu.VMEM((2, 4, NB * T, W), F32),
                        pltpu.VMEM((2, NCH, W, W), F32),
                        pltpu.VMEM((2, NCH, W, 2 * W), BF16),
                        pltpu.VMEM((2, NCH, 3, W, W), BF16),
                        pltpu.VMEM((2, NCH, 2, A_CHUNK, W), BF16),
                        pltpu.VMEM((2, NCH, A_CHUNK, W), F32),
                        pltpu.VMEM((2, NCH, 8, W), F32)],
        compiler_params=pltpu.CompilerParams(dimension_semantics=("arbitrary", "arbitrary"),
                                             vmem_limit_bytes=VMEM_LIMIT),
        name="rwkv7",
    )(*args)


def _pool_kernel(u_ref, mix_ref, scale_ref, y_ref, buf, *, pad):
    T = SEQ_TILE
    j = pl.program_id(1)

    @pl.when(j == 0)
    def _():
        buf[0:B_HALO, :] = jnp.zeros((B_HALO, B_WIDTH), F32)

    u = u_ref[0]
    buf[B_HALO:B_HALO + T, :] = u
    lane_group = lax.broadcasted_iota(jnp.int32, (1, B_WIDTH), 1) // B_GROUP_DIM
    acc = u
    win = jnp.zeros_like(u)
    for s in range(1, max(B_WINDOWS)):
        acc = acc + buf[B_HALO - s:B_HALO - s + T, :]
        if s + 1 in B_WINDOWS:
            win = jnp.where(lane_group == B_WINDOWS.index(s + 1), acc, win)
    wlane = jnp.zeros((1, B_WIDTH), jnp.int32)
    for gi, w in enumerate(B_WINDOWS):
        wlane = jnp.where(lane_group == gi, w, wlane)
    t_real = j * T + lax.broadcasted_iota(jnp.int32, (T, 1), 0) - pad
    cnt = jnp.minimum(jnp.maximum(t_real + 1, 1), wlane).astype(F32)
    pooled = win / cnt - u
    y_ref[0] = _dot(pooled, mix_ref[...]) * scale_ref[...]
    buf[0:B_HALO, :] = u[T - B_HALO:, :]


def _pool(proj3, mix_bd, scale, *, pad):
    bsz, lp, _ = proj3.shape
    T = SEQ_TILE
    return pl.pallas_call(
        functools.partial(_pool_kernel, pad=pad),
        grid=(bsz, lp // T),
        in_specs=[pl.BlockSpec((1, T, B_WIDTH), lambda b, j: (b, j, OFF_B // B_WIDTH)),
                  pl.BlockSpec((B_WIDTH, B_WIDTH), lambda b, j: (0, 0)),
                  pl.BlockSpec((1, B_WIDTH), lambda b, j: (0, 0))],
        out_specs=pl.BlockSpec((1, T, B_WIDTH), lambda b, j: (b, j, 0)),
        out_shape=jax.ShapeDtypeStruct((bsz, lp, B_WIDTH), F32),
        scratch_shapes=[pltpu.VMEM((B_HALO + T, B_WIDTH), F32)],
        compiler_params=pltpu.CompilerParams(dimension_semantics=("arbitrary", "arbitrary")),
        name="pool",
    )(proj3, mix_bd, scale)


def _swa_kernel(sinks_ref, u_ref, y_ref, kv_sc, *, pad):
    T = SEQ_TILE
    j = pl.program_id(1)

    @pl.when(j == 0)
    def _():
        kv_sc[...] = jnp.zeros_like(kv_sc)

    blk = u_ref[0]
    kv = blk[:, C_WIDTH:]
    kw = jnp.concatenate([kv_sc[:, :C_KV_WIDTH], kv[:, :C_KV_WIDTH]], axis=0).astype(BF16)
    vw = jnp.concatenate([kv_sc[:, C_KV_WIDTH:], kv[:, C_KV_WIDTH:]], axis=0).astype(BF16)
    kv_sc[...] = kv
    qi = lax.broadcasted_iota(jnp.int32, (T, 2 * T), 0)
    si = lax.broadcasted_iota(jnp.int32, (T, 2 * T), 1)
    dist = T + qi - si
    mask = (dist >= 0) & (dist < T) & ((j - 1) * T + si >= pad)
    distf = dist.astype(F32)
    for hk in range(C_KV_HEADS):
        ks = slice(hk * C_HEAD_DIM, (hk + 1) * C_HEAD_DIM)
        k_h, v_h = kw[:, ks], vw[:, ks]
        for gq in range(C_GROUP):
            hq = hk * C_GROUP + gq
            qs = slice(hq * C_HEAD_DIM, (hq + 1) * C_HEAD_DIM)
            slope = 2.0 ** (-8.0 * (hq + 1) / C_HEADS)
            sink = sinks_ref[hq]
            s = _dot_nt(blk[:, qs] * (C_HEAD_DIM ** -0.5), k_h)
            logits = jnp.where(mask, s - slope * distf, MASK_VALUE)
            m = jnp.maximum(jnp.max(logits, axis=-1, keepdims=True), sink)
            p = jnp.exp(logits - m)
            denom = jnp.sum(p, axis=-1, keepdims=True) + jnp.exp(sink - m)
            y_ref[0, :, qs] = _dot(p / denom, v_h)


def _swa(proj3, sinks, *, pad):
    bsz, lp, _ = proj3.shape
    T = SEQ_TILE
    return pl.pallas_call(
        functools.partial(_swa_kernel, pad=pad),
        grid=(bsz, lp // T),
        in_specs=[pl.BlockSpec(memory_space=pltpu.SMEM),
                  pl.BlockSpec((1, T, C_COLS), lambda b, j: (b, j, OFF_C // C_COLS))],
        out_specs=pl.BlockSpec((1, T, C_WIDTH), lambda b, j: (b, j, 0)),
        out_shape=jax.ShapeDtypeStruct((bsz, lp, C_WIDTH), F32),
        scratch_shapes=[pltpu.VMEM((T, 2 * C_KV_WIDTH), F32)],
        compiler_params=pltpu.CompilerParams(dimension_semantics=("arbitrary", "arbitrary")),
        name="swa",
    )(sinks, proj3)


def _hgrn_kernel(u_ref, lb_ref, ng_ref, sums_ref, level_ref, y_ref, state_sc, *, pad):
    T = SEQ_TILE
    N = D_KEY_DIM
    j = pl.program_id(1)

    @pl.when(j == 0)
    def _():
        state_sc[...] = jnp.zeros_like(state_sc)

    u = u_ref[0]
    lb = lb_ref[...]
    q = _silu(u[:, :D_WIDTH])
    fpre = u[:, D_WIDTH:2 * D_WIDTH]
    vin = u[:, 2 * D_WIDTH:3 * D_WIDTH]
    gate = u[:, 3 * D_WIDTH:]
    f = lb + (1.0 - lb) * _sigmoid(fpre)
    t_real = j * T + lax.broadcasted_iota(jnp.int32, (T, 1), 0) - pad
    logf = jnp.where(t_real >= 0, jnp.log(jnp.maximum(f, 1e-30)), 0.0)
    kx = (1.0 - lb) * _sigmoid(-fpre)

    hi, lo = _split2(logf)
    sums = _mm(sums_ref[...], hi) + _mm(sums_ref[...], lo)
    same_head = (lax.broadcasted_iota(jnp.int32, (D_WIDTH, D_WIDTH), 0) // N
                 == lax.broadcasted_iota(jnp.int32, (D_WIDTH, D_WIDTH), 1) // N)
    head_rows = (lax.broadcasted_iota(jnp.int32, (D_HEADS * T, D_WIDTH), 0) // T
                 == lax.broadcasted_iota(jnp.int32, (D_HEADS * T, D_WIDTH), 1) // N)
    stack = lambda t: jnp.where(head_rows, jnp.concatenate([t] * D_HEADS, axis=0), 0.0)
    k_st = stack(kx)
    v_st = stack(vin).astype(BF16)
    lv = level_ref[...]
    att = jnp.where(lv == 0, _dot_nt(q, k_st), 0.0)
    for l in range(D_LEVELS):
        if l == 0:
            q_l, k_l = q * jnp.exp(logf), k_st
        else:
            p_h = sums[(2 * l - 2) * T:(2 * l - 1) * T]
            x_h = sums[(2 * l - 1) * T:2 * l * T]
            q_l = q * jnp.exp(p_h)
            k_l = k_st * jnp.concatenate([jnp.exp(x_h)] * D_HEADS, axis=0)
        att = jnp.where(lv == l + 1, _dot_nt(q_l, k_l), att)
    b = sums[(2 * D_LEVELS - 2) * T:(2 * D_LEVELS - 1) * T]
    b_rev = sums[(2 * D_LEVELS - 1) * T:]
    st = state_sc[...]
    o = _dot(att, v_st) + _dot_nt(q * jnp.exp(b), st)
    upd = jnp.where(same_head, _dot_tn(vin, kx * jnp.exp(b_rev)), 0.0)
    state_sc[...] = st * jnp.exp(b[T - 1:T, :]) + upd

    ones_bd = _head_ones(D_WIDTH, N)
    ms = _head_sum(o * o, ones_bd) * (1.0 / N)
    y_ref[0] = o * lax.rsqrt(ms + NORM_EPS) * ng_ref[...] * _silu(gate)


def _hgrn_constants():
    T = SEQ_TILE
    t = np.arange(T)[:, None]
    i = np.arange(T)[None, :]
    mats = []
    for l in list(range(1, D_LEVELS)) + [D_LEVELS]:
        h = 2 ** l
        same = (t // h) == (i // h)
        mats += [same & (i <= t), same & (i > t)]
    sums = np.concatenate(mats, axis=0).astype(np.float32)
    x = t ^ i
    level = np.where(i == t, 0, np.where(i < t, np.floor(np.log2(np.maximum(x, 1))).astype(np.int64) + 1, -1))
    return jnp.asarray(sums, BF16), jnp.asarray(np.tile(level, (1, D_HEADS)), jnp.int32)


def _hgrn(proj3, lb, norm_g, *, pad):
    bsz, lp, _ = proj3.shape
    T = SEQ_TILE
    sums, level = _hgrn_constants()
    const = lambda arr: pl.BlockSpec(arr.shape, lambda b, j: (0, 0))
    return pl.pallas_call(
        functools.partial(_hgrn_kernel, pad=pad),
        grid=(bsz, lp // T),
        in_specs=[pl.BlockSpec((1, T, D_COLS), lambda b, j: (b, j, OFF_D // D_COLS)),
                  const(lb), const(norm_g), const(sums), const(level)],
        out_specs=pl.BlockSpec((1, T, D_WIDTH), lambda b, j: (b, j, 0)),
        out_shape=jax.ShapeDtypeStruct((bsz, lp, D_WIDTH), F32),
        scratch_shapes=[pltpu.VMEM((D_WIDTH, D_WIDTH), F32)],
        compiler_params=pltpu.CompilerParams(dimension_semantics=("arbitrary", "arbitrary"),
                                             vmem_limit_bytes=VMEM_LIMIT),
        name="hgrn2",
    )(proj3, lb, norm_g, sums, level)


def _merge_ffn_kernel(h_ref, gates_ref, ya_ref, yb_ref, yc_ref, yd_ref, wb_ref, wo_ref,
                      gf_ref, wu_ref, wd_ref, o_ref, *, ff_chunk):
    merged = None
    row = 0
    for bi, y_ref in enumerate((ya_ref, yb_ref, yc_ref, yd_ref)):
        w = y_ref.shape[-1]
        part = _sigmoid(gates_ref[:, bi * D_MODEL:(bi + 1) * D_MODEL]) * jnp.dot(
            y_ref[...].astype(BF16), wb_ref[row:row + w, :], preferred_element_type=F32)
        merged = part if merged is None else merged + part
        row += w
    h = h_ref[...] + jnp.dot(merged.astype(BF16), wo_ref[...], preferred_element_type=F32)
    zb = _rms(h, gf_ref[...]).astype(BF16)
    acc = h
    for c in range(D_FF // ff_chunk):
        cs = slice(c * ff_chunk, (c + 1) * ff_chunk)
        gu = jnp.dot(zb, wu_ref[:, cs], preferred_element_type=F32)
        up = jnp.dot(zb, wu_ref[:, D_FF + c * ff_chunk:D_FF + (c + 1) * ff_chunk],
                     preferred_element_type=F32)
        acc = acc + jnp.dot((_silu(gu) * up).astype(BF16), wd_ref[cs, :],
                            preferred_element_type=F32)
    o_ref[...] = acc


def _merge_ffn(h2, proj2, ya, yb, yc, yd, wb, wo, gf, wu, wd):
    m = h2.shape[0]
    tm = ROW_TILE
    rows = lambda w, col=0: pl.BlockSpec((tm, w), lambda i: (i, col))
    const = lambda arr: pl.BlockSpec(arr.shape, lambda i: (0, 0), pipeline_mode=pl.Buffered(1))
    return pl.pallas_call(
        functools.partial(_merge_ffn_kernel, ff_chunk=256),
        grid=(m // tm,),
        in_specs=[rows(D_MODEL), rows(GATE_COLS), rows(A_WIDTH), rows(B_WIDTH), rows(C_WIDTH),
                  rows(D_WIDTH), const(wb), const(wo), const(gf), const(wu), const(wd)],
        out_specs=rows(D_MODEL),
        out_shape=jax.ShapeDtypeStruct((m, D_MODEL), F32),
        compiler_params=pltpu.CompilerParams(dimension_semantics=("arbitrary",),
                                             vmem_limit_bytes=VMEM_LIMIT),
        name="merge_ffn",
    )(h2, proj2, ya, yb, yc, yd, wb, wo, gf, wu, wd)


def _final_norm_kernel(h_ref, g_ref, o_ref):
    o_ref[0] = _rms(h_ref[0], g_ref[...])


def _final_norm(h3, g, *, first_tile, seq):
    bsz = h3.shape[0]
    T = SEQ_TILE
    return pl.pallas_call(
        _final_norm_kernel,
        grid=(bsz, seq // T),
        in_specs=[pl.BlockSpec((1, T, D_MODEL), lambda b, j: (b, j + first_tile, 0)),
                  pl.BlockSpec((1, D_MODEL), lambda b, j: (0, 0))],
        out_specs=pl.BlockSpec((1, T, D_MODEL), lambda b, j: (b, j, 0)),
        out_shape=jax.ShapeDtypeStruct((bsz, seq, D_MODEL), F32),
        name="final_norm",
    )(h3, g)


def _block_diag(blocks):
    n, r, c = blocks.shape
    out = jnp.zeros((n * r, n * c), blocks.dtype)
    for i in range(n):
        out = out.at[i * r:(i + 1) * r, i * c:(i + 1) * c].set(blocks[i])
    return out


def kernel(x, meta, norm_mix, norm_ffn, norm_final, w_in, w_branch, w_out, a_mu, a_w_up, a_w0, a_a_up, a_a0, a_g_up, a_kk, a_ka, a_rk, a_ln_w, a_ln_b, a_vres_down, a_vres_up, a_vres0, b_mix, b_scale, c_sinks, d_lower_bounds, d_norm, w_ffn_up, w_ffn_down):
    bsz, seq, _ = x.shape
    depth = w_in.shape[0]
    T = SEQ_TILE
    L = N_META + seq
    pad = (-L) % T
    lp = L + pad
    assert (pad + N_META) % T == 0 and seq % T == 0
    h = jnp.concatenate([jnp.zeros((bsz, pad, D_MODEL), F32),
                         jnp.broadcast_to(meta.astype(F32)[None], (bsz, N_META, D_MODEL)),
                         x.astype(F32)], axis=1).reshape(bsz * lp, D_MODEL)
    lb_w = jax.nn.softmax(d_lower_bounds.astype(F32), axis=0)
    lb_table = jnp.cumsum(lb_w, axis=0) - lb_w[0]
    row2 = lambda t: t.reshape(1, -1).astype(F32)
    v_first = None
    for l in range(depth):
        proj2 = _in_proj(h, row2(norm_mix[l]), w_in[l].astype(BF16), pad=pad, lp=lp)
        proj3 = proj2.reshape(bsz, lp, IN_COLS)
        wwa = jnp.zeros((128, 2 * A_WIDTH), F32)
        wwa = wwa.at[:64, :A_WIDTH].set(a_w_up[l]).at[64:, A_WIDTH:].set(a_a_up[l])
        pa = dict(mu=row2(a_mu[l]), wwa=wwa.astype(BF16), w0=row2(a_w0[l]), a0=row2(a_a0[l]),
                  gup=a_g_up[l].astype(BF16), kk=row2(a_kk[l]), ka=row2(a_ka[l]), rk=row2(a_rk[l]),
                  lnw=row2(a_ln_w[l]), lnb=row2(a_ln_b[l]))
        if l == 0:
            y_a, v_first = _rwkv(proj3, None, pa, has_vres=False)
        else:
            pa.update(vd=a_vres_down[l - 1].astype(BF16), vu=a_vres_up[l - 1].astype(BF16),
                      v0=row2(a_vres0[l - 1]))
            y_a = _rwkv(proj3, v_first, pa, has_vres=True)
        y_b = _pool(proj3, _block_diag(b_mix[l]).astype(BF16), row2(b_scale[l]), pad=pad)
        y_c = _swa(proj3, c_sinks[l].astype(F32), pad=pad)
        y_d = _hgrn(proj3, row2(lb_table[l]), row2(d_norm[l]), pad=pad)
        flat = lambda t: t.reshape(bsz * lp, t.shape[-1])
        h = _merge_ffn(h, proj2, flat(y_a), flat(y_b), flat(y_c), flat(y_d),
                       w_branch[l].astype(BF16), w_out[l].astype(BF16), row2(norm_ffn[l]),
                       w_ffn_up[l].astype(BF16), w_ffn_down[l].astype(BF16))
    return _final_norm(h.reshape(bsz, lp, D_MODEL), row2(norm_final),
                       first_tile=(pad + N_META) // T, seq=seq)
```

```python
import functools

import jax
import jax.numpy as jnp
import numpy as np
from jax import lax
from jax.experimental import pallas as pl
from jax.experimental.pallas import tpu as pltpu

F32 = jnp.float32
BF16 = jnp.bfloat16

D_MODEL = 1024
N_META = 16
NORM_EPS = 1e-6
MASK_VALUE = -1e30

A_HEADS = 4
A_HEAD_DIM = 64
A_WIDTH = 256
A_GN_EPS = 64e-5
A_COLS = 1024
A_CHUNK = 64
A_BATCH = 4
A_GROUP = 2

B_WIDTH = 256
B_GROUP_DIM = 64
B_WINDOWS = (2, 4, 8, 16)
B_HALO = 16

C_HEADS = 8
C_KV_HEADS = 2
C_GROUP = 4
C_HEAD_DIM = 64
C_WIDTH = 512
C_KV_WIDTH = 128
C_COLS = 768

D_HEADS = 4
D_KEY_DIM = 64
D_WIDTH = 256
D_COLS = 1024
CD_BATCH = 2
D_LEVELS = 7

D_FF = 2816
GATE_COLS = 4 * D_MODEL
OFF_A = GATE_COLS
OFF_B = OFF_A + A_COLS
OFF_C = OFF_B + B_WIDTH
OFF_D = OFF_C + C_COLS
IN_COLS = OFF_D + D_COLS
MIX_WIDTH = 1280

SEQ_TILE = 128
ROW_TILE = 384
VMEM_LIMIT = 56 * 1024 * 1024


def _dot(a, b):
    return jnp.dot(a.astype(BF16), b.astype(BF16), preferred_element_type=F32)


def _dot_nt(a, b):
    return lax.dot_general(a.astype(BF16), b.astype(BF16), (((1,), (1,)), ((), ())),
                           preferred_element_type=F32)


def _dot_tn(a, b):
    return lax.dot_general(a.astype(BF16), b.astype(BF16), (((0,), (0,)), ((), ())),
                           preferred_element_type=F32)


def _split2(x):
    hi = x.astype(BF16)
    lo = (x - hi.astype(F32)).astype(BF16)
    return hi, lo


def _split3(x):
    hi = x.astype(BF16)
    r = x - hi.astype(F32)
    mid = r.astype(BF16)
    lo = (r - mid.astype(F32)).astype(BF16)
    return hi, mid, lo


def _const_dot(c, x):
    hi, mid, lo = _split3(x)
    d = lambda p: jnp.dot(c, p, preferred_element_type=F32)
    return d(hi) + d(mid) + d(lo)


def _mm(a, b):
    return jnp.dot(a, b, preferred_element_type=F32)


def _mm_tn(a, b):
    return lax.dot_general(a, b, (((0,), (0,)), ((), ())), preferred_element_type=F32)


def _d3(a, b):
    return _mm(a[0], b[0]) + _mm(a[0], b[1]) + _mm(a[1], b[0])


def _unit_lower_inverses(mats, ri, ci, box):
    in16 = (ri // 16) == (ci // 16)
    in32 = (ri // 32) == (ci // 32)
    diag = ri == ci
    plus_eye = lambda m: jnp.where(diag, 1.0, m).astype(BF16)
    x = [jnp.where(in16, a, 0.0) for a in mats]
    t = [plus_eye(xi) for xi in x]
    xb = [xi.astype(BF16) for xi in x]
    yield
    for _ in range(3):
        x = [_mm(xi, xi) for xi in xb]
        yield
        xb = [xi.astype(BF16) for xi in x]
        t = [_mm(ti, plus_eye(xi)).astype(BF16) for ti, xi in zip(t, x)]
        yield
    for sel in (in32 & ~in16, ~in32):
        lt = [_mm(jnp.where(sel, a, 0.0).astype(BF16), ti) for a, ti in zip(mats, t)]
        yield
        t = [_mm(ti, plus_eye(li)).astype(BF16) for ti, li in zip(t, lt)]
        yield
    box["tinv"] = t


def _head_sum(x, ones_bd):
    hi, lo = _split2(x)
    d = lambda p: jnp.dot(p, ones_bd, preferred_element_type=F32)
    return d(hi) + d(lo)


def _head_ones(width, seg):
    r = lax.broadcasted_iota(jnp.int32, (width, width), 0) // seg
    c = lax.broadcasted_iota(jnp.int32, (width, width), 1) // seg
    return jnp.where(r == c, 1.0, 0.0).astype(BF16)


def _sigmoid(x):
    return 1.0 / (1.0 + jnp.exp(-x))


def _silu(x):
    return x * _sigmoid(x)


def _rms(x, g):
    return x * lax.rsqrt(jnp.mean(x * x, axis=-1, keepdims=True) + NORM_EPS) * g


def _in_proj_kernel(x_ref, g_ref, w_ref, o_ref, *, pad, tiles_per_seq, tm, tn):
    z = _rms(x_ref[...], g_ref[...])
    row = (pl.program_id(0) % tiles_per_seq) * tm + lax.broadcasted_iota(jnp.int32, (tm, 1), 0)
    zb = jnp.where(row >= pad, z, 0.0).astype(BF16)
    for j in range(IN_COLS // tn):
        o_ref[:, j * tn:(j + 1) * tn] = jnp.dot(zb, w_ref[:, j * tn:(j + 1) * tn],
                                                preferred_element_type=F32)


def _in_proj(h2, g, w, *, pad, lp):
    m = h2.shape[0]
    tm = ROW_TILE
    assert lp % tm == 0
    return pl.pallas_call(
        functools.partial(_in_proj_kernel, pad=pad, tiles_per_seq=lp // tm, tm=tm, tn=1024),
        grid=(m // tm,),
        in_specs=[pl.BlockSpec((tm, D_MODEL), lambda i: (i, 0)),
                  pl.BlockSpec((1, D_MODEL), lambda i: (0, 0)),
                  pl.BlockSpec((D_MODEL, IN_COLS), lambda i: (0, 0), pipeline_mode=pl.Buffered(1))],
        out_specs=pl.BlockSpec((tm, IN_COLS), lambda i: (i, 0)),
        out_shape=jax.ShapeDtypeStruct((m, IN_COLS), F32),
        compiler_params=pltpu.CompilerParams(dimension_semantics=("arbitrary",),
                                             vmem_limit_bytes=VMEM_LIMIT),
        name="in_proj",
    )(h2, g, w)


def _rwkv_kernel(*refs, has_vres):
    for parity in (0, 1):
        pl.when(pl.program_id(1) % 2 == parity)(
            functools.partial(_rwkv_step, refs, has_vres=has_vres, wslot=parity))


def _rwkv_step(refs, *, has_vres, wslot):
    handoff = refs[-7:]
    tok_sc, aab_sc, rhs_sc, bkv_sc, arb_sc, rt_sc, gam_sc = handoff
    refs = refs[:-7]
    if has_vres:
        (u_ref, vf_ref, mu_ref, wwa_ref, w0_ref, a0_ref, gup_ref, kk_ref, ka_ref, rk_ref,
         lnw_ref, lnb_ref, vd_ref, vu_ref, v0_ref, y_ref, state_sc, carry_sc) = refs
    else:
        (u_ref, mu_ref, wwa_ref, w0_ref, a0_ref, gup_ref, kk_ref, ka_ref, rk_ref,
         lnw_ref, lnb_ref, y_ref, vout_ref, state_sc, carry_sc) = refs
    T = SEQ_TILE
    C = A_CHUNK
    N = A_HEAD_DIM
    W = A_WIDTH
    NB = A_BATCH
    R = NB * T
    NCH = R // C
    j = pl.program_id(1)

    @pl.when((pl.program_id(0) == 0) & (j == 0))
    def _():
        state_sc[...] = jnp.zeros_like(state_sc)
        for ref in handoff:
            ref[...] = jnp.zeros_like(ref)

    @pl.when(j == 0)
    def _():
        carry_sc[...] = jnp.zeros_like(carry_sc)

    rslot = 1 - wslot

    u = u_ref[...].reshape(R, A_COLS)
    row = lax.broadcasted_iota(jnp.int32, (R, 1), 0)
    prev = pltpu.roll(u, 1, axis=0)
    for b in range(NB):
        prev = jnp.where(row == b * T, carry_sc[b:b + 1, :], prev)
        carry_sc[b:b + 1, :] = u[(b + 1) * T - 1:(b + 1) * T, :]
    ones_bd = _head_ones(W, N)
    lane = lax.broadcasted_iota(jnp.int32, (1, 128), 1)

    def token_stage(b):
        rows = slice(b * T, (b + 1) * T)
        x = u[rows] + (prev[rows] - u[rows]) * mu_ref[...]
        r = x[:, 0:W]
        k = x[:, W:2 * W]
        v = x[:, 2 * W:3 * W]
        slab = x[:, 3 * W:3 * W + 128]
        gd = x[:, 3 * W + 128:]
        slab = jnp.where(lane < 64, jnp.tanh(slab), slab)
        wa = _dot(slab, wwa_ref[...])
        zw = -(w0_ref[...] + wa[:, :W])
        softplus = jnp.maximum(zw, 0.0) + jnp.log(1.0 + jnp.exp(-jnp.abs(zw)))
        logw = -jnp.exp(-softplus - 0.5)
        a = _sigmoid(a0_ref[...] + wa[:, W:])
        g = _dot(_sigmoid(gd), gup_ref[...])
        if has_vres:
            mix = _sigmoid(v0_ref[...] + _dot(_dot(v, vd_ref[...]), vu_ref[...]))
            v = v + (vf_ref[b] - v) * mix
        else:
            vout_ref[b] = v
        kkr = k * kk_ref[...]
        kk = kkr / jnp.maximum(jnp.sqrt(_head_sum(kkr * kkr, ones_bd)), 1e-12)
        k2 = k * (1.0 + (a - 1.0) * ka_ref[...])
        return dict(r=r, v=v, kk=kk, k2=k2, bhat=kk * a, logw=logw, g=g)

    ri = lax.broadcasted_iota(jnp.int32, (W, W), 0)
    ci = lax.broadcasted_iota(jnp.int32, (W, W), 1)
    same_head = (ri // C) == (ci // C)
    tw = lax.broadcasted_iota(jnp.int32, (C, W), 0)
    sw = lax.broadcasted_iota(jnp.int32, (C, W), 1) % C
    tri = jnp.where(lax.broadcasted_iota(jnp.int32, (C, C), 1)
                    <= lax.broadcasted_iota(jnp.int32, (C, C), 0), 1.0, 0.0).astype(BF16)

    def stack(t):
        return jnp.where(same_head, jnp.concatenate([t] * A_HEADS, axis=0), 0.0)

    def chunk_inputs(tok, i):
        sl = slice(i * C, (i + 1) * C)
        lw = tok["logw"][sl]
        bhat, k2 = tok["bhat"][sl], tok["k2"][sl]
        cl = _const_dot(tri, lw)
        cl_last = cl[C - 1:C, :]
        e_neg = jnp.exp(-cl)
        e_last = jnp.exp(cl_last - cl)
        at = -tok["kk"][sl] * jnp.exp(cl - lw)
        rt = tok["r"][sl] * jnp.exp(cl)
        v_s = stack(tok["v"][sl]).astype(BF16)
        aa = _dot_nt(jnp.concatenate([at, rt], axis=0),
                     jnp.concatenate([stack(bhat * e_neg), stack(k2 * e_neg)], axis=0))
        a_ab = stack(jnp.where(sw < tw, aa[:C, :W], 0.0))
        a_ak = stack(jnp.where(sw < tw, aa[:C, W:], 0.0)).astype(BF16)
        a_rb = jnp.where(sw <= tw, aa[C:, :W], 0.0).astype(BF16)
        a_rk = jnp.where(sw <= tw, aa[C:, W:], 0.0).astype(BF16)
        rhs = jnp.concatenate([stack(at), _mm(a_ak, v_s)], axis=1).astype(BF16)
        btil = stack(bhat * e_last).astype(BF16)
        ktil = stack(k2 * e_last).astype(BF16)
        return a_ab, rhs, a_rb, a_rk, v_s, btil, ktil, rt, jnp.exp(cl_last)

    def chunk_terms(inputs, tinv):
        _, rhs, a_rb, a_rk, v_s, btil, ktil, rt, gam = inputs
        wu = _mm(tinv, rhs)
        w_t = wu[:, :W].astype(BF16)
        u_t = wu[:, W:].astype(BF16)
        g_col = jnp.sum(jnp.where(ri == ci, gam, 0.0), axis=1, keepdims=True)
        m_mat = (g_col, _mm_tn(btil, w_t).astype(BF16))
        n_mat = _mm_tn(btil, u_t) + _mm_tn(ktil, v_s)
        q_mat = rt + _mm(a_rb, w_t)
        p_mat = _mm(a_rb, u_t) + _mm(a_rk, v_s)
        return m_mat, n_mat, q_mat, p_mat

    def finish(tok, outs):
        o = jnp.concatenate(outs, axis=0)
        mean = _head_sum(o, ones_bd) * (1.0 / N)
        d = o - mean
        var = _head_sum(d * d, ones_bd) * (1.0 / N)
        o = d * lax.rsqrt(var + A_GN_EPS) * lnw_ref[...] + lnb_ref[...]
        bonus = _head_sum(tok["r"] * tok["k2"] * rk_ref[...], ones_bd) * tok["v"]
        y_ref[...] = ((o + bonus) * tok["g"]).reshape(NB, T, W)

    per = T // C

    def finish_previous_tile():
        box = {}
        yield from _unit_lower_inverses([aab_sc[rslot, i] for i in range(NCH)], ri, ci, box)
        outs = []
        for b in range(NB):
            st = state_sc[b]
            for i in range(b * per, (b + 1) * per):
                held = (None, rhs_sc[rslot, i], arb_sc[rslot, i, 0], arb_sc[rslot, i, 1],
                        bkv_sc[rslot, i, 2], bkv_sc[rslot, i, 0], bkv_sc[rslot, i, 1],
                        rt_sc[rslot, i], gam_sc[rslot, i, 0:1])
                m_mat, n_mat, q_mat, p_mat = chunk_terms(held, box["tinv"][i])
                sb = st.astype(BF16)
                outs.append(_mm(q_mat.astype(BF16), sb) + p_mat)
                st = m_mat[0] * st + _mm(m_mat[1], sb) + n_mat
                yield
            state_sc[b] = st
        finish(dict(r=tok_sc[rslot, 0], k2=tok_sc[rslot, 1], v=tok_sc[rslot, 2],
                    g=tok_sc[rslot, 3]), outs)

    def prepare_this_tile():
        for b in range(NB):
            tok = token_stage(b)
            for n, name in enumerate(("r", "k2", "v", "g")):
                tok_sc[wslot, n, b * T:(b + 1) * T] = tok[name]
            yield
            for c in range(per):
                a_ab, rhs, a_rb, a_rk, v_s, btil, ktil, rt, gam = chunk_inputs(tok, c)
                i = b * per + c
                aab_sc[wslot, i] = a_ab
                rhs_sc[wslot, i] = rhs
                arb_sc[wslot, i, 0] = a_rb
                arb_sc[wslot, i, 1] = a_rk
                bkv_sc[wslot, i, 0] = btil
                bkv_sc[wslot, i, 1] = ktil
                bkv_sc[wslot, i, 2] = v_s
                rt_sc[wslot, i] = rt
                gam_sc[wslot, i, 0:1] = gam
                yield

    gens = [finish_previous_tile(), prepare_this_tile()]
    while gens:
        for gen in list(gens):
            if next(gen, "done") == "done":
                gens.remove(gen)

    @pl.when(j == 0)
    def _():
        state_sc[...] = jnp.zeros_like(state_sc)


def _rwkv(proj3, v_first, p, *, has_vres):
    bsz, lp, _ = proj3.shape
    T = SEQ_TILE
    NB = A_BATCH
    assert bsz % NB == 0
    nt = lp // T
    NCH = NB * T // A_CHUNK
    W = A_WIDTH
    in_spec = lambda w, col: pl.BlockSpec((NB, T, w), lambda b, j: (b, jnp.minimum(j, nt - 1), col))
    y_spec = pl.BlockSpec((NB, T, W), lambda b, j: (b, jnp.maximum(j - 1, 0), 0))
    v_spec = pl.BlockSpec((NB, T, W), lambda b, j: (b, j, 0))
    full = lambda arr: pl.BlockSpec(arr.shape, lambda b, j: (0,) * arr.ndim)
    args = [proj3]
    specs = [in_spec(A_COLS, OFF_A // A_COLS)]
    if has_vres:
        args.append(v_first)
        specs.append(in_spec(W, 0))
    names = ["mu", "wwa", "w0", "a0", "gup", "kk", "ka", "rk", "lnw", "lnb"]
    if has_vres:
        names += ["vd", "vu", "v0"]
    for n in names:
        args.append(p[n])
        specs.append(full(p[n]))
    y_shape = jax.ShapeDtypeStruct((bsz, lp, W), F32)
    v_shape = jax.ShapeDtypeStruct((bsz, lp + T, W), F32)
    out_shape = y_shape if has_vres else (y_shape, v_shape)
    out_specs = y_spec if has_vres else (y_spec, v_spec)
    return pl.pallas_call(
        functools.partial(_rwkv_kernel, has_vres=has_vres),
        grid=(bsz // NB, nt + 1),
        in_specs=specs,
        out_specs=out_specs,
        out_shape=out_shape,
        scratch_shapes=[pltpu.VMEM((NB, W, W), F32),
                        pltpu.VMEM((8, A_COLS), F32),
                        pltpu.VMEM((2, 4, NB * T, W), F32),
                        pltpu.VMEM((2, NCH, W, W), F32),
                        pltpu.VMEM((2, NCH, W, 2 * W), BF16),
                        pltpu.VMEM((2, NCH, 3, W, W), BF16),
                        pltpu.VMEM((2, NCH, 2, A_CHUNK, W), BF16),
                        pltpu.VMEM((2, NCH, A_CHUNK, W), F32),
                        pltpu.VMEM((2, NCH, 8, W), F32)],
        compiler_params=pltpu.CompilerParams(dimension_semantics=("arbitrary", "arbitrary"),
                                             vmem_limit_bytes=VMEM_LIMIT),
        name="rwkv7",
    )(*args)


def _pool_tile(u, buf, tile_in_seq, mix, scale, *, pad):
    rows = u.shape[0]

    @pl.when(tile_in_seq == 0)
    def _():
        buf[0:B_HALO, :] = jnp.zeros((B_HALO, B_WIDTH), F32)

    buf[B_HALO:B_HALO + rows, :] = u
    lane_group = lax.broadcasted_iota(jnp.int32, (1, B_WIDTH), 1) // B_GROUP_DIM
    acc = u
    win = jnp.zeros_like(u)
    for s in range(1, max(B_WINDOWS)):
        acc = acc + buf[B_HALO - s:B_HALO - s + rows, :]
        if s + 1 in B_WINDOWS:
            win = jnp.where(lane_group == B_WINDOWS.index(s + 1), acc, win)
    wlane = jnp.zeros((1, B_WIDTH), jnp.int32)
    for gi, w in enumerate(B_WINDOWS):
        wlane = jnp.where(lane_group == gi, w, wlane)
    t_real = tile_in_seq * rows + lax.broadcasted_iota(jnp.int32, (rows, 1), 0) - pad
    cnt = jnp.minimum(jnp.maximum(t_real + 1, 1), wlane).astype(F32)
    pooled = win / cnt - u
    buf[0:B_HALO, :] = u[rows - B_HALO:, :]
    return _dot(pooled, mix) * scale


def _swa_tile(blk, kv_prev, sinks_ref, j, *, pad):
    T = SEQ_TILE
    G = C_GROUP
    kv = blk[:, C_WIDTH:]
    kw = jnp.concatenate([kv_prev[:, :C_KV_WIDTH], kv[:, :C_KV_WIDTH]], axis=0)
    vw = jnp.concatenate([kv_prev[:, C_KV_WIDTH:], kv[:, C_KV_WIDTH:]], axis=0)
    lane = lax.broadcasted_iota(jnp.int32, (1, 2 * C_HEAD_DIM), 1)
    low = lane < C_HEAD_DIM
    kw_sw = pltpu.roll(kw, C_HEAD_DIM, axis=1)
    vw_sw = pltpu.roll(vw, C_HEAD_DIM, axis=1)
    qi = lax.broadcasted_iota(jnp.int32, (G * T, 2 * T), 0) % T
    si = lax.broadcasted_iota(jnp.int32, (G * T, 2 * T), 1)
    head = lax.broadcasted_iota(jnp.int32, (G * T, 1), 0) // T
    dist = T + qi - si
    mask = (dist >= 0) & (dist < T) & ((j - 1) * T + si >= pad)
    distf = dist.astype(F32)
    tiles = []
    for hk in range(C_KV_HEADS):
        k2 = jnp.where(low == (hk == 0), kw, kw_sw).astype(BF16)
        v2 = jnp.where(low == (hk == 0), vw, vw_sw).astype(BF16)
        q_rows = []
        slope = jnp.zeros((G * T, 1), F32)
        sink = jnp.zeros((G * T, 1), F32)
        for gq in range(G):
            hq = hk * G + gq
            pair = blk[:, (hq // 2) * 128:(hq // 2 + 1) * 128]
            q_rows.append(jnp.where(low == (hq % 2 == 0), pair * (C_HEAD_DIM ** -0.5), 0.0))
            slope = jnp.where(head == gq, 2.0 ** (-8.0 * (hq + 1) / C_HEADS), slope)
            sink = jnp.where(head == gq, sinks_ref[hq], sink)
        s = _dot_nt(jnp.concatenate(q_rows, axis=0), k2)
        logits = jnp.where(mask, s - slope * distf, MASK_VALUE)
        m = jnp.maximum(jnp.max(logits, axis=-1, keepdims=True), sink)
        p = jnp.exp(logits - m)
        denom = jnp.sum(p, axis=-1, keepdims=True) + jnp.exp(sink - m)
        o = _dot(p, v2) / denom
        for pr in range(G // 2):
            tiles.append(jnp.where(low, o[2 * pr * T:(2 * pr + 1) * T], o[(2 * pr + 1) * T:(2 * pr + 2) * T]))
    return jnp.concatenate(tiles, axis=1)


def _hgrn_tile(u, state_sc, lb_ref, ng_ref, sums_ref, level_ref, j, *, pad):
    T = SEQ_TILE
    N = D_KEY_DIM
    lb = lb_ref[...]
    q = _silu(u[:, :D_WIDTH])
    fpre = u[:, D_WIDTH:2 * D_WIDTH]
    vin = u[:, 2 * D_WIDTH:3 * D_WIDTH]
    gate = u[:, 3 * D_WIDTH:]
    f = lb + (1.0 - lb) * _sigmoid(fpre)
    t_real = j * T + lax.broadcasted_iota(jnp.int32, (T, 1), 0) - pad
    logf = jnp.where(t_real >= 0, jnp.log(jnp.maximum(f, 1e-30)), 0.0)
    kx = (1.0 - lb) * _sigmoid(-fpre)

    hi, lo = _split2(logf)
    sums = _mm(sums_ref[...], hi) + _mm(sums_ref[...], lo)
    same_head = (lax.broadcasted_iota(jnp.int32, (D_WIDTH, D_WIDTH), 0) // N
                 == lax.broadcasted_iota(jnp.int32, (D_WIDTH, D_WIDTH), 1) // N)
    head_rows = (lax.broadcasted_iota(jnp.int32, (D_HEADS * T, D_WIDTH), 0) // T
                 == lax.broadcasted_iota(jnp.int32, (D_HEADS * T, D_WIDTH), 1) // N)
    stack = lambda t: jnp.where(head_rows, jnp.concatenate([t] * D_HEADS, axis=0), 0.0)
    k_st = stack(kx)
    v_st = stack(vin).astype(BF16)
    lv = level_ref[...]
    att = jnp.where(lv == 0, _dot_nt(q, k_st), 0.0)
    for l in range(D_LEVELS):
        if l == 0:
            q_l, k_l = q * jnp.exp(logf), k_st
        else:
            p_h = sums[(2 * l - 2) * T:(2 * l - 1) * T]
            x_h = sums[(2 * l - 1) * T:2 * l * T]
            q_l = q * jnp.exp(p_h)
            k_l = k_st * jnp.concatenate([jnp.exp(x_h)] * D_HEADS, axis=0)
        att = jnp.where(lv == l + 1, _dot_nt(q_l, k_l), att)
    b = sums[(2 * D_LEVELS - 2) * T:(2 * D_LEVELS - 1) * T]
    b_rev = sums[(2 * D_LEVELS - 1) * T:]
    st = state_sc[...]
    o = _dot(att, v_st) + _dot_nt(q * jnp.exp(b), st)
    upd = jnp.where(same_head, _dot_tn(vin, kx * jnp.exp(b_rev)), 0.0)
    state_sc[...] = st * jnp.exp(b[T - 1:T, :]) + upd

    ones_bd = _head_ones(D_WIDTH, N)
    ms = _head_sum(o * o, ones_bd) * (1.0 / N)
    return o * lax.rsqrt(ms + NORM_EPS) * ng_ref[...] * _silu(gate)


def _swa_hgrn_kernel(sinks_ref, uc_ref, ud_ref, lb_ref, ng_ref, sums_ref, level_ref, yc_ref, yd_ref,
                     kv_sc, state_sc, *, pad):
    j = pl.program_id(1)

    @pl.when(j == 0)
    def _():
        kv_sc[...] = jnp.zeros_like(kv_sc)
        state_sc[...] = jnp.zeros_like(state_sc)

    for b in range(CD_BATCH):
        blk = uc_ref[b]
        yc_ref[b] = _swa_tile(blk, kv_sc[b], sinks_ref, j, pad=pad)
        kv_sc[b] = blk[:, C_WIDTH:]
        yd_ref[b] = _hgrn_tile(ud_ref[b], state_sc.at[b], lb_ref, ng_ref, sums_ref, level_ref, j,
                               pad=pad)


def _hgrn_constants():
    T = SEQ_TILE
    t = np.arange(T)[:, None]
    i = np.arange(T)[None, :]
    mats = []
    for l in list(range(1, D_LEVELS)) + [D_LEVELS]:
        h = 2 ** l
        same = (t // h) == (i // h)
        mats += [same & (i <= t), same & (i > t)]
    sums = np.concatenate(mats, axis=0).astype(np.float32)
    x = t ^ i
    level = np.where(i == t, 0, np.where(i < t, np.floor(np.log2(np.maximum(x, 1))).astype(np.int64) + 1, -1))
    return jnp.asarray(sums, BF16), jnp.asarray(np.tile(level, (1, D_HEADS)), jnp.int32)


def _swa_hgrn(proj3, sinks, lb, norm_g, *, pad):
    bsz, lp, _ = proj3.shape
    T = SEQ_TILE
    NB = CD_BATCH
    assert bsz % NB == 0
    sums, level = _hgrn_constants()
    const = lambda arr: pl.BlockSpec(arr.shape, lambda b, j: (0, 0))
    seq = lambda w, col: pl.BlockSpec((NB, T, w), lambda b, j: (b, j, col))
    return pl.pallas_call(
        functools.partial(_swa_hgrn_kernel, pad=pad),
        grid=(bsz // NB, lp // T),
        in_specs=[pl.BlockSpec(memory_space=pltpu.SMEM),
                  seq(C_COLS, OFF_C // C_COLS), seq(D_COLS, OFF_D // D_COLS),
                  const(lb), const(norm_g), const(sums), const(level)],
        out_specs=(seq(C_WIDTH, 0), seq(D_WIDTH, 0)),
        out_shape=(jax.ShapeDtypeStruct((bsz, lp, C_WIDTH), F32),
                   jax.ShapeDtypeStruct((bsz, lp, D_WIDTH), F32)),
        scratch_shapes=[pltpu.VMEM((NB, T, 2 * C_KV_WIDTH), F32),
                        pltpu.VMEM((NB, D_WIDTH, D_WIDTH), F32)],
        compiler_params=pltpu.CompilerParams(dimension_semantics=("arbitrary", "arbitrary"),
                                             vmem_limit_bytes=VMEM_LIMIT),
        name="swa_hgrn2",
    )(sinks, proj3, proj3, lb, norm_g, sums, level)


def _merge_ffn_kernel(h_ref, gates_ref, ya_ref, ub_ref, yc_ref, yd_ref, mix_ref, scale_ref, wb_ref,
                      wo_ref, gf_ref, wu_ref, wd_ref, o_ref, pool_buf, *, ff_chunk, pad, tiles_per_seq):
    y_b = _pool_tile(ub_ref[...], pool_buf, pl.program_id(0) % tiles_per_seq, mix_ref[...],
                     scale_ref[...], pad=pad)
    merged = None
    row = 0
    for bi, y in enumerate((ya_ref[...], y_b, yc_ref[...], yd_ref[...])):
        w = y.shape[-1]
        part = _sigmoid(gates_ref[:, bi * D_MODEL:(bi + 1) * D_MODEL]) * jnp.dot(
            y.astype(BF16), wb_ref[row:row + w, :], preferred_element_type=F32)
        merged = part if merged is None else merged + part
        row += w
    h = h_ref[...] + jnp.dot(merged.astype(BF16), wo_ref[...], preferred_element_type=F32)
    zb = _rms(h, gf_ref[...]).astype(BF16)
    acc = h
    for c in range(D_FF // ff_chunk):
        cs = slice(c * ff_chunk, (c + 1) * ff_chunk)
        gu = jnp.dot(zb, wu_ref[:, cs], preferred_element_type=F32)
        up = jnp.dot(zb, wu_ref[:, D_FF + c * ff_chunk:D_FF + (c + 1) * ff_chunk],
                     preferred_element_type=F32)
        acc = acc + jnp.dot((_silu(gu) * up).astype(BF16), wd_ref[cs, :],
                            preferred_element_type=F32)
    o_ref[...] = acc


def _merge_ffn(h2, proj2, ya, yc, yd, mix_bd, scale, wb, wo, gf, wu, wd, *, pad, lp):
    m = h2.shape[0]
    tm = ROW_TILE
    assert lp % tm == 0
    rows = lambda w, col=0: pl.BlockSpec((tm, w), lambda i: (i, col))
    const = lambda arr: pl.BlockSpec(arr.shape, lambda i: (0, 0), pipeline_mode=pl.Buffered(1))
    return pl.pallas_call(
        functools.partial(_merge_ffn_kernel, ff_chunk=256, pad=pad, tiles_per_seq=lp // tm),
        grid=(m // tm,),
        in_specs=[rows(D_MODEL), rows(GATE_COLS), rows(A_WIDTH), rows(B_WIDTH, OFF_B // B_WIDTH),
                  rows(C_WIDTH), rows(D_WIDTH), const(mix_bd), const(scale), const(wb), const(wo),
                  const(gf), const(wu), const(wd)],
        out_specs=rows(D_MODEL),
        out_shape=jax.ShapeDtypeStruct((m, D_MODEL), F32),
        scratch_shapes=[pltpu.VMEM((B_HALO + tm, B_WIDTH), F32)],
        compiler_params=pltpu.CompilerParams(dimension_semantics=("arbitrary",),
                                             vmem_limit_bytes=VMEM_LIMIT),
        name="merge_ffn",
    )(h2, proj2, ya, proj2, yc, yd, mix_bd, scale, wb, wo, gf, wu, wd)


def _final_norm_kernel(h_ref, g_ref, o_ref):
    o_ref[0] = _rms(h_ref[0], g_ref[...])


def _final_norm(h3, g, *, first_tile, seq):
    bsz = h3.shape[0]
    T = SEQ_TILE
    return pl.pallas_call(
        _final_norm_kernel,
        grid=(bsz, seq // T),
        in_specs=[pl.BlockSpec((1, T, D_MODEL), lambda b, j: (b, j + first_tile, 0)),
                  pl.BlockSpec((1, D_MODEL), lambda b, j: (0, 0))],
        out_specs=pl.BlockSpec((1, T, D_MODEL), lambda b, j: (b, j, 0)),
        out_shape=jax.ShapeDtypeStruct((bsz, seq, D_MODEL), F32),
        name="final_norm",
    )(h3, g)


def _block_diag(blocks):
    n, r, c = blocks.shape
    out = jnp.zeros((n * r, n * c), blocks.dtype)
    for i in range(n):
        out = out.at[i * r:(i + 1) * r, i * c:(i + 1) * c].set(blocks[i])
    return out


def kernel(x, meta, norm_mix, norm_ffn, norm_final, w_in, w_branch, w_out, a_mu, a_w_up, a_w0, a_a_up, a_a0, a_g_up, a_kk, a_ka, a_rk, a_ln_w, a_ln_b, a_vres_down, a_vres_up, a_vres0, b_mix, b_scale, c_sinks, d_lower_bounds, d_norm, w_ffn_up, w_ffn_down):
    bsz, seq, _ = x.shape
    depth = w_in.shape[0]
    T = SEQ_TILE
    L = N_META + seq
    pad = (-L) % T
    lp = L + pad
    assert (pad + N_META) % T == 0 and seq % T == 0
    h = jnp.concatenate([jnp.zeros((bsz, pad, D_MODEL), F32),
                         jnp.broadcast_to(meta.astype(F32)[None], (bsz, N_META, D_MODEL)),
                         x.astype(F32)], axis=1).reshape(bsz * lp, D_MODEL)
    lb_w = jax.nn.softmax(d_lower_bounds.astype(F32), axis=0)
    lb_table = jnp.cumsum(lb_w, axis=0) - lb_w[0]
    row2 = lambda t: t.reshape(1, -1).astype(F32)
    v_first = None
    for l in range(depth):
        proj2 = _in_proj(h, row2(norm_mix[l]), w_in[l].astype(BF16), pad=pad, lp=lp)
        proj3 = proj2.reshape(bsz, lp, IN_COLS)
        wwa = jnp.zeros((128, 2 * A_WIDTH), F32)
        wwa = wwa.at[:64, :A_WIDTH].set(a_w_up[l]).at[64:, A_WIDTH:].set(a_a_up[l])
        pa = dict(mu=row2(a_mu[l]), wwa=wwa.astype(BF16), w0=row2(a_w0[l]), a0=row2(a_a0[l]),
                  gup=a_g_up[l].astype(BF16), kk=row2(a_kk[l]), ka=row2(a_ka[l]), rk=row2(a_rk[l]),
                  lnw=row2(a_ln_w[l]), lnb=row2(a_ln_b[l]))
        if l == 0:
            y_a, v_first = _rwkv(proj3, None, pa, has_vres=False)
        else:
            pa.update(vd=a_vres_down[l - 1].astype(BF16), vu=a_vres_up[l - 1].astype(BF16),
                      v0=row2(a_vres0[l - 1]))
            y_a = _rwkv(proj3, v_first, pa, has_vres=True)
        y_c, y_d = _swa_hgrn(proj3, c_sinks[l].astype(F32), row2(lb_table[l]), row2(d_norm[l]),
                             pad=pad)
        flat = lambda t: t.reshape(bsz * lp, t.shape[-1])
        h = _merge_ffn(h, proj2, flat(y_a), flat(y_c), flat(y_d),
                       _block_diag(b_mix[l]).astype(BF16), row2(b_scale[l]),
                       w_branch[l].astype(BF16), w_out[l].astype(BF16), row2(norm_ffn[l]),
                       w_ffn_up[l].astype(BF16), w_ffn_down[l].astype(BF16), pad=pad, lp=lp)
    return _final_norm(h.reshape(bsz, lp, D_MODEL), row2(norm_final),
                       first_tile=(pad + N_META) // T, seq=seq)
```

```python
import functools

import jax
import jax.numpy as jnp
import numpy as np
from jax import lax
from jax.experimental import pallas as pl
from jax.experimental.pallas import tpu as pltpu

F32 = jnp.float32
BF16 = jnp.bfloat16

D_MODEL = 1024
N_META = 16
NORM_EPS = 1e-6
MASK_VALUE = -1e30

A_HEADS = 4
A_HEAD_DIM = 64
A_WIDTH = 256
A_GN_EPS = 64e-5
A_COLS = 1024
A_CHUNK = 64
A_BATCH = 4
A_GROUP = 2

B_WIDTH = 256
B_GROUP_DIM = 64
B_WINDOWS = (2, 4, 8, 16)
B_HALO = 16

C_HEADS = 8
C_KV_HEADS = 2
C_GROUP = 4
C_HEAD_DIM = 64
C_WIDTH = 512
C_KV_WIDTH = 128
C_COLS = 768

D_HEADS = 4
D_KEY_DIM = 64
D_WIDTH = 256
D_COLS = 1024
CD_BATCH = 2
D_LEVELS = 7

D_FF = 2816
GATE_COLS = 4 * D_MODEL
OFF_A = GATE_COLS
OFF_B = OFF_A + A_COLS
OFF_C = OFF_B + B_WIDTH
OFF_D = OFF_C + C_COLS
IN_COLS = OFF_D + D_COLS
MIX_WIDTH = 1280

SEQ_TILE = 128
ROW_TILE = 384
VMEM_LIMIT = 56 * 1024 * 1024


def _dot(a, b):
    return jnp.dot(a.astype(BF16), b.astype(BF16), preferred_element_type=F32)


def _dot_nt(a, b):
    return lax.dot_general(a.astype(BF16), b.astype(BF16), (((1,), (1,)), ((), ())),
                           preferred_element_type=F32)


def _dot_tn(a, b):
    return lax.dot_general(a.astype(BF16), b.astype(BF16), (((0,), (0,)), ((), ())),
                           preferred_element_type=F32)


def _split2(x):
    hi = x.astype(BF16)
    lo = (x - hi.astype(F32)).astype(BF16)
    return hi, lo


def _split3(x):
    hi = x.astype(BF16)
    r = x - hi.astype(F32)
    mid = r.astype(BF16)
    lo = (r - mid.astype(F32)).astype(BF16)
    return hi, mid, lo


def _const_dot(c, x):
    hi, mid, lo = _split3(x)
    d = lambda p: jnp.dot(c, p, preferred_element_type=F32)
    return d(hi) + d(mid) + d(lo)


def _mm(a, b):
    return jnp.dot(a, b, preferred_element_type=F32)


def _mm_tn(a, b):
    return lax.dot_general(a, b, (((0,), (0,)), ((), ())), preferred_element_type=F32)


def _d3(a, b):
    return _mm(a[0], b[0]) + _mm(a[0], b[1]) + _mm(a[1], b[0])


def _unit_lower_inverses(mats, ri, ci, box):
    in16 = (ri // 16) == (ci // 16)
    in32 = (ri // 32) == (ci // 32)
    diag = ri == ci
    plus_eye = lambda m: jnp.where(diag, 1.0, m).astype(BF16)
    x = [jnp.where(in16, a, 0.0) for a in mats]
    t = [plus_eye(xi) for xi in x]
    xb = [xi.astype(BF16) for xi in x]
    yield
    for _ in range(3):
        x = [_mm(xi, xi) for xi in xb]
        yield
        xb = [xi.astype(BF16) for xi in x]
        t = [_mm(ti, plus_eye(xi)).astype(BF16) for ti, xi in zip(t, x)]
        yield
    for sel in (in32 & ~in16, ~in32):
        lt = [_mm(jnp.where(sel, a, 0.0).astype(BF16), ti) for a, ti in zip(mats, t)]
        yield
        t = [_mm(ti, plus_eye(li)).astype(BF16) for ti, li in zip(t, lt)]
        yield
    box["tinv"] = t


def _head_sum(x, ones_bd):
    hi, lo = _split2(x)
    d = lambda p: jnp.dot(p, ones_bd, preferred_element_type=F32)
    return d(hi) + d(lo)


def _head_ones(width, seg):
    r = lax.broadcasted_iota(jnp.int32, (width, width), 0) // seg
    c = lax.broadcasted_iota(jnp.int32, (width, width), 1) // seg
    return jnp.where(r == c, 1.0, 0.0).astype(BF16)


def _sigmoid(x):
    return 1.0 / (1.0 + jnp.exp(-x))


def _silu(x):
    return x * _sigmoid(x)


def _rms(x, g):
    return x * lax.rsqrt(jnp.mean(x * x, axis=-1, keepdims=True) + NORM_EPS) * g


def _in_proj_kernel(x_ref, g_ref, w_ref, o_ref, *, pad, tiles_per_seq, tm, tn):
    z = _rms(x_ref[...], g_ref[...])
    row = (pl.program_id(0) % tiles_per_seq) * tm + lax.broadcasted_iota(jnp.int32, (tm, 1), 0)
    zb = jnp.where(row >= pad, z, 0.0).astype(BF16)
    for j in range(IN_COLS // tn):
        o_ref[:, j * tn:(j + 1) * tn] = jnp.dot(zb, w_ref[:, j * tn:(j + 1) * tn],
                                                preferred_element_type=F32)


def _in_proj(h2, g, w, *, pad, lp):
    m = h2.shape[0]
    tm = ROW_TILE
    assert lp % tm == 0
    return pl.pallas_call(
        functools.partial(_in_proj_kernel, pad=pad, tiles_per_seq=lp // tm, tm=tm, tn=1024),
        grid=(m // tm,),
        in_specs=[pl.BlockSpec((tm, D_MODEL), lambda i: (i, 0)),
                  pl.BlockSpec((1, D_MODEL), lambda i: (0, 0)),
                  pl.BlockSpec((D_MODEL, IN_COLS), lambda i: (0, 0), pipeline_mode=pl.Buffered(1))],
        out_specs=pl.BlockSpec((tm, IN_COLS), lambda i: (i, 0)),
        out_shape=jax.ShapeDtypeStruct((m, IN_COLS), F32),
        compiler_params=pltpu.CompilerParams(dimension_semantics=("arbitrary",),
                                             vmem_limit_bytes=VMEM_LIMIT),
        name="in_proj",
    )(h2, g, w)


def _rwkv_kernel(*refs, has_vres):
    for parity in (0, 1):
        pl.when(pl.program_id(1) % 2 == parity)(
            functools.partial(_rwkv_step, refs, has_vres=has_vres, wslot=parity))


def _rwkv_step(refs, *, has_vres, wslot):
    handoff = refs[-7:]
    tok_sc, aab_sc, rhs_sc, bkv_sc, arb_sc, rt_sc, gam_sc = handoff
    refs = refs[:-7]
    if has_vres:
        (u_ref, vf_ref, mu_ref, wwa_ref, w0_ref, a0_ref, gup_ref, kk_ref, ka_ref, rk_ref,
         lnw_ref, lnb_ref, vd_ref, vu_ref, v0_ref, y_ref, state_sc, carry_sc) = refs
    else:
        (u_ref, mu_ref, wwa_ref, w0_ref, a0_ref, gup_ref, kk_ref, ka_ref, rk_ref,
         lnw_ref, lnb_ref, y_ref, vout_ref, state_sc, carry_sc) = refs
    T = SEQ_TILE
    C = A_CHUNK
    N = A_HEAD_DIM
    W = A_WIDTH
    NB = A_BATCH
    R = NB * T
    NCH = R // C
    j = pl.program_id(1)

    @pl.when((pl.program_id(0) == 0) & (j == 0))
    def _():
        state_sc[...] = jnp.zeros_like(state_sc)
        for ref in handoff:
            ref[...] = jnp.zeros_like(ref)

    @pl.when(j == 0)
    def _():
        carry_sc[...] = jnp.zeros_like(carry_sc)

    rslot = 1 - wslot

    u = u_ref[...].reshape(R, A_COLS)
    row = lax.broadcasted_iota(jnp.int32, (R, 1), 0)
    prev = pltpu.roll(u, 1, axis=0)
    for b in range(NB):
        prev = jnp.where(row == b * T, carry_sc[b:b + 1, :], prev)
        carry_sc[b:b + 1, :] = u[(b + 1) * T - 1:(b + 1) * T, :]
    ones_bd = _head_ones(W, N)
    lane = lax.broadcasted_iota(jnp.int32, (1, 128), 1)

    def token_stage(b):
        rows = slice(b * T, (b + 1) * T)
        x = u[rows] + (prev[rows] - u[rows]) * mu_ref[...]
        r = x[:, 0:W]
        k = x[:, W:2 * W]
        v = x[:, 2 * W:3 * W]
        slab = x[:, 3 * W:3 * W + 128]
        gd = x[:, 3 * W + 128:]
        slab = jnp.where(lane < 64, jnp.tanh(slab), slab)
        wa = _dot(slab, wwa_ref[...])
        zw = -(w0_ref[...] + wa[:, :W])
        softplus = jnp.maximum(zw, 0.0) + jnp.log(1.0 + jnp.exp(-jnp.abs(zw)))
        logw = -jnp.exp(-softplus - 0.5)
        a = _sigmoid(a0_ref[...] + wa[:, W:])
        g = _dot(_sigmoid(gd), gup_ref[...])
        if has_vres:
            mix = _sigmoid(v0_ref[...] + _dot(_dot(v, vd_ref[...]), vu_ref[...]))
            v = v + (vf_ref[b] - v) * mix
        else:
            vout_ref[b] = v
        kkr = k * kk_ref[...]
        kk = kkr / jnp.maximum(jnp.sqrt(_head_sum(kkr * kkr, ones_bd)), 1e-12)
        k2 = k * (1.0 + (a - 1.0) * ka_ref[...])
        return dict(r=r, v=v, kk=kk, k2=k2, bhat=kk * a, logw=logw, g=g)

    ri = lax.broadcasted_iota(jnp.int32, (W, W), 0)
    ci = lax.broadcasted_iota(jnp.int32, (W, W), 1)
    same_head = (ri // C) == (ci // C)
    tw = lax.broadcasted_iota(jnp.int32, (C, W), 0)
    sw = lax.broadcasted_iota(jnp.int32, (C, W), 1) % C
    tri = jnp.where(lax.broadcasted_iota(jnp.int32, (C, C), 1)
                    <= lax.broadcasted_iota(jnp.int32, (C, C), 0), 1.0, 0.0).astype(BF16)

    def stack(t):
        return jnp.where(same_head, jnp.concatenate([t] * A_HEADS, axis=0), 0.0)

    def chunk_inputs(tok, i):
        sl = slice(i * C, (i + 1) * C)
        lw = tok["logw"][sl]
        bhat, k2 = tok["bhat"][sl], tok["k2"][sl]
        cl = _const_dot(tri, lw)
        cl_last = cl[C - 1:C, :]
        e_neg = jnp.exp(-cl)
        e_last = jnp.exp(cl_last - cl)
        at = -tok["kk"][sl] * jnp.exp(cl - lw)
        rt = tok["r"][sl] * jnp.exp(cl)
        v_s = stack(tok["v"][sl]).astype(BF16)
        aa = _dot_nt(jnp.concatenate([at, rt], axis=0),
                     jnp.concatenate([stack(bhat * e_neg), stack(k2 * e_neg)], axis=0))
        a_ab = stack(jnp.where(sw < tw, aa[:C, :W], 0.0))
        a_ak = stack(jnp.where(sw < tw, aa[:C, W:], 0.0)).astype(BF16)
        a_rb = jnp.where(sw <= tw, aa[C:, :W], 0.0).astype(BF16)
        a_rk = jnp.where(sw <= tw, aa[C:, W:], 0.0).astype(BF16)
        rhs = jnp.concatenate([stack(at), _mm(a_ak, v_s)], axis=1).astype(BF16)
        btil = stack(bhat * e_last).astype(BF16)
        ktil = stack(k2 * e_last).astype(BF16)
        return a_ab, rhs, a_rb, a_rk, v_s, btil, ktil, rt, jnp.exp(cl_last)

    def chunk_terms(inputs, tinv):
        _, rhs, a_rb, a_rk, v_s, btil, ktil, rt, gam = inputs
        wu = _mm(tinv, rhs)
        w_t = wu[:, :W].astype(BF16)
        u_t = wu[:, W:].astype(BF16)
        g_col = jnp.sum(jnp.where(ri == ci, gam, 0.0), axis=1, keepdims=True)
        m_mat = (g_col, _mm_tn(btil, w_t).astype(BF16))
        n_mat = _mm_tn(btil, u_t) + _mm_tn(ktil, v_s)
        q_mat = rt + _mm(a_rb, w_t)
        p_mat = _mm(a_rb, u_t) + _mm(a_rk, v_s)
        return m_mat, n_mat, q_mat, p_mat

    def finish(tok, outs):
        o = jnp.concatenate(outs, axis=0)
        mean = _head_sum(o, ones_bd) * (1.0 / N)
        d = o - mean
        var = _head_sum(d * d, ones_bd) * (1.0 / N)
        o = d * lax.rsqrt(var + A_GN_EPS) * lnw_ref[...] + lnb_ref[...]
        bonus = _head_sum(tok["r"] * tok["k2"] * rk_ref[...], ones_bd) * tok["v"]
        y_ref[...] = ((o + bonus) * tok["g"]).reshape(NB, T, W)

    per = T // C

    def finish_previous_tile():
        box = {}
        yield from _unit_lower_inverses([aab_sc[rslot, i] for i in range(NCH)], ri, ci, box)
        terms = []
        for i in range(NCH):
            held = (None, rhs_sc[rslot, i], arb_sc[rslot, i, 0], arb_sc[rslot, i, 1],
                    bkv_sc[rslot, i, 2], bkv_sc[rslot, i, 0], bkv_sc[rslot, i, 1],
                    rt_sc[rslot, i], gam_sc[rslot, i, 0:1])
            terms.append(chunk_terms(held, box["tinv"][i]))
            yield
        sts = [state_sc[b] for b in range(NB)]
        outs = [None] * NCH
        for c in range(per):
            for b in range(NB):
                m_mat, n_mat, q_mat, p_mat = terms[b * per + c]
                sb = sts[b].astype(BF16)
                outs[b * per + c] = _mm(q_mat.astype(BF16), sb) + p_mat
                sts[b] = m_mat[0] * sts[b] + _mm(m_mat[1], sb) + n_mat
            yield
        for b in range(NB):
            state_sc[b] = sts[b]
        finish(dict(r=tok_sc[rslot, 0], k2=tok_sc[rslot, 1], v=tok_sc[rslot, 2],
                    g=tok_sc[rslot, 3]), outs)

    def prepare_this_tile():
        for b in range(NB):
            tok = token_stage(b)
            for n, name in enumerate(("r", "k2", "v", "g")):
                tok_sc[wslot, n, b * T:(b + 1) * T] = tok[name]
            yield
            for c in range(per):
                a_ab, rhs, a_rb, a_rk, v_s, btil, ktil, rt, gam = chunk_inputs(tok, c)
                i = b * per + c
                aab_sc[wslot, i] = a_ab
                rhs_sc[wslot, i] = rhs
                arb_sc[wslot, i, 0] = a_rb
                arb_sc[wslot, i, 1] = a_rk
                bkv_sc[wslot, i, 0] = btil
                bkv_sc[wslot, i, 1] = ktil
                bkv_sc[wslot, i, 2] = v_s
                rt_sc[wslot, i] = rt
                gam_sc[wslot, i, 0:1] = gam
                yield

    gens = [finish_previous_tile(), prepare_this_tile()]
    while gens:
        for gen in list(gens):
            if next(gen, "done") == "done":
                gens.remove(gen)

    @pl.when(j == 0)
    def _():
        state_sc[...] = jnp.zeros_like(state_sc)


def _rwkv(proj3, v_first, p, *, has_vres):
    bsz, lp, _ = proj3.shape
    T = SEQ_TILE
    NB = A_BATCH
    assert bsz % NB == 0
    nt = lp // T
    NCH = NB * T // A_CHUNK
    W = A_WIDTH
    in_spec = lambda w, col: pl.BlockSpec((NB, T, w), lambda b, j: (b, jnp.minimum(j, nt - 1), col))
    y_spec = pl.BlockSpec((NB, T, W), lambda b, j: (b, jnp.maximum(j - 1, 0), 0))
    v_spec = pl.BlockSpec((NB, T, W), lambda b, j: (b, j, 0))
    full = lambda arr: pl.BlockSpec(arr.shape, lambda b, j: (0,) * arr.ndim)
    args = [proj3]
    specs = [in_spec(A_COLS, OFF_A // A_COLS)]
    if has_vres:
        args.append(v_first)
        specs.append(in_spec(W, 0))
    names = ["mu", "wwa", "w0", "a0", "gup", "kk", "ka", "rk", "lnw", "lnb"]
    if has_vres:
        names += ["vd", "vu", "v0"]
    for n in names:
        args.append(p[n])
        specs.append(full(p[n]))
    y_shape = jax.ShapeDtypeStruct((bsz, lp, W), F32)
    v_shape = jax.ShapeDtypeStruct((bsz, lp + T, W), F32)
    out_shape = y_shape if has_vres else (y_shape, v_shape)
    out_specs = y_spec if has_vres else (y_spec, v_spec)
    return pl.pallas_call(
        functools.partial(_rwkv_kernel, has_vres=has_vres),
        grid=(bsz // NB, nt + 1),
        in_specs=specs,
        out_specs=out_specs,
        out_shape=out_shape,
        scratch_shapes=[pltpu.VMEM((NB, W, W), F32),
                        pltpu.VMEM((8, A_COLS), F32),
                        pltpu.VMEM((2, 4, NB * T, W), F32),
                        pltpu.VMEM((2, NCH, W, W), F32),
                        pltpu.VMEM((2, NCH, W, 2 * W), BF16),
                        pltpu.VMEM((2, NCH, 3, W, W), BF16),
                        pltpu.VMEM((2, NCH, 2, A_CHUNK, W), BF16),
                        pltpu.VMEM((2, NCH, A_CHUNK, W), F32),
                        pltpu.VMEM((2, NCH, 8, W), F32)],
        compiler_params=pltpu.CompilerParams(dimension_semantics=("arbitrary", "arbitrary"),
                                             vmem_limit_bytes=VMEM_LIMIT),
        name="rwkv7",
    )(*args)


def _pool_tile(u, buf, tile_in_seq, mix, scale, *, pad):
    rows = u.shape[0]

    @pl.when(tile_in_seq == 0)
    def _():
        buf[0:B_HALO, :] = jnp.zeros((B_HALO, B_WIDTH), F32)

    buf[B_HALO:B_HALO + rows, :] = u
    lane_group = lax.broadcasted_iota(jnp.int32, (1, B_WIDTH), 1) // B_GROUP_DIM
    acc = u
    win = jnp.zeros_like(u)
    for s in range(1, max(B_WINDOWS)):
        acc = acc + buf[B_HALO - s:B_HALO - s + rows, :]
        if s + 1 in B_WINDOWS:
            win = jnp.where(lane_group == B_WINDOWS.index(s + 1), acc, win)
    wlane = jnp.zeros((1, B_WIDTH), jnp.int32)
    for gi, w in enumerate(B_WINDOWS):
        wlane = jnp.where(lane_group == gi, w, wlane)
    t_real = tile_in_seq * rows + lax.broadcasted_iota(jnp.int32, (rows, 1), 0) - pad
    cnt = jnp.minimum(jnp.maximum(t_real + 1, 1), wlane).astype(F32)
    pooled = win / cnt - u
    buf[0:B_HALO, :] = u[rows - B_HALO:, :]
    return _dot(pooled, mix) * scale


def _swa_constants():
    T = SEQ_TILE
    dist = T + np.arange(T)[:, None] - np.arange(2 * T)[None, :]
    band = (dist >= 0) & (dist < T)
    bias = np.empty((C_KV_HEADS, C_GROUP * T, 2 * T), np.float32)
    for hq in range(C_HEADS):
        slope = np.float32(2.0 ** (-8.0 * (hq + 1) / C_HEADS))
        rows = slice((hq % C_GROUP) * T, (hq % C_GROUP + 1) * T)
        bias[hq // C_GROUP, rows] = np.where(band, -slope * dist.astype(np.float32), MASK_VALUE)
    return jnp.asarray(bias)


def _swa_tile(blk, kv_prev, sinks_ref, bias_ref, j, *, pad):
    T = SEQ_TILE
    G = C_GROUP
    kv = blk[:, C_WIDTH:]
    kw = jnp.concatenate([kv_prev[:, :C_KV_WIDTH], kv[:, :C_KV_WIDTH]], axis=0)
    vw = jnp.concatenate([kv_prev[:, C_KV_WIDTH:], kv[:, C_KV_WIDTH:]], axis=0)
    lane = lax.broadcasted_iota(jnp.int32, (1, 2 * C_HEAD_DIM), 1)
    low = lane < C_HEAD_DIM
    kw_sw = pltpu.roll(kw, C_HEAD_DIM, axis=1)
    vw_sw = pltpu.roll(vw, C_HEAD_DIM, axis=1)
    head = lax.broadcasted_iota(jnp.int32, (G * T, 1), 0) // T
    key_real = (j - 1) * T + lax.broadcasted_iota(jnp.int32, (1, 2 * T), 1) >= pad
    tiles = []
    for hk in range(C_KV_HEADS):
        k2 = jnp.where(low == (hk == 0), kw, kw_sw).astype(BF16)
        v2 = jnp.where(low == (hk == 0), vw, vw_sw).astype(BF16)
        q_rows = []
        sink = jnp.zeros((G * T, 1), F32)
        for gq in range(G):
            hq = hk * G + gq
            pair = blk[:, (hq // 2) * 128:(hq // 2 + 1) * 128]
            q_rows.append(jnp.where(low == (hq % 2 == 0), pair * (C_HEAD_DIM ** -0.5), 0.0))
            sink = jnp.where(head == gq, sinks_ref[hq], sink)
        s = _dot_nt(jnp.concatenate(q_rows, axis=0), k2)
        logits = jnp.where(key_real, s + bias_ref[hk], MASK_VALUE)
        m = jnp.maximum(jnp.max(logits, axis=-1, keepdims=True), sink)
        p = jnp.exp(logits - m)
        denom = jnp.sum(p, axis=-1, keepdims=True) + jnp.exp(sink - m)
        o = _dot(p, v2) / denom
        for pr in range(G // 2):
            tiles.append(jnp.where(low, o[2 * pr * T:(2 * pr + 1) * T], o[(2 * pr + 1) * T:(2 * pr + 2) * T]))
    return jnp.concatenate(tiles, axis=1)


def _hgrn_tile(u, state_sc, lb_ref, ng_ref, sums_ref, level_ref, j, *, pad):
    T = SEQ_TILE
    N = D_KEY_DIM
    lb = lb_ref[...]
    q = _silu(u[:, :D_WIDTH])
    fpre = u[:, D_WIDTH:2 * D_WIDTH]
    vin = u[:, 2 * D_WIDTH:3 * D_WIDTH]
    gate = u[:, 3 * D_WIDTH:]
    sig = _sigmoid(fpre)
    f = lb + (1.0 - lb) * sig
    t_real = j * T + lax.broadcasted_iota(jnp.int32, (T, 1), 0) - pad
    logf = jnp.where(t_real >= 0, jnp.log(jnp.maximum(f, 1e-30)), 0.0)
    kx = (1.0 - lb) * (1.0 - sig)

    sums = _mm(sums_ref[...], jnp.concatenate(_split2(logf), axis=0))
    same_head = (lax.broadcasted_iota(jnp.int32, (D_WIDTH, D_WIDTH), 0) // N
                 == lax.broadcasted_iota(jnp.int32, (D_WIDTH, D_WIDTH), 1) // N)
    head_rows = (lax.broadcasted_iota(jnp.int32, (D_HEADS * T, D_WIDTH), 0) // T
                 == lax.broadcasted_iota(jnp.int32, (D_HEADS * T, D_WIDTH), 1) // N)
    stack = lambda t: jnp.where(head_rows, jnp.concatenate([t] * D_HEADS, axis=0), 0.0)
    k_st = stack(kx)
    v_st = stack(vin).astype(BF16)
    lv = level_ref[...]
    att = jnp.where(lv == 0, _dot_nt(q, k_st), 0.0)
    for l in range(D_LEVELS):
        if l == 0:
            q_l, k_l = q * jnp.exp(logf), k_st
        else:
            p_h = sums[(2 * l - 2) * T:(2 * l - 1) * T]
            x_h = sums[(2 * l - 1) * T:2 * l * T]
            q_l = q * jnp.exp(p_h)
            k_l = k_st * jnp.concatenate([jnp.exp(x_h)] * D_HEADS, axis=0)
        att = jnp.where(lv == l + 1, _dot_nt(q_l, k_l), att)
    b = sums[(2 * D_LEVELS - 2) * T:(2 * D_LEVELS - 1) * T]
    b_rev = sums[(2 * D_LEVELS - 1) * T:]
    st = state_sc[...]
    o = _dot(att, v_st) + _dot_nt(q * jnp.exp(b), st)
    upd = jnp.where(same_head, _dot_tn(vin, kx * jnp.exp(b_rev)), 0.0)
    state_sc[...] = st * jnp.exp(b[T - 1:T, :]) + upd

    ones_bd = _head_ones(D_WIDTH, N)
    ms = _head_sum(o * o, ones_bd) * (1.0 / N)
    return o * lax.rsqrt(ms + NORM_EPS) * ng_ref[...] * _silu(gate)


def _swa_hgrn_kernel(sinks_ref, uc_ref, ud_ref, bias_ref, lb_ref, ng_ref, sums_ref, level_ref, yc_ref, yd_ref,
                     kv_sc, state_sc, *, pad):
    j = pl.program_id(1)

    @pl.when(j == 0)
    def _():
        kv_sc[...] = jnp.zeros_like(kv_sc)
        state_sc[...] = jnp.zeros_like(state_sc)

    for b in range(CD_BATCH):
        blk = uc_ref[b]
        yc_ref[b] = _swa_tile(blk, kv_sc[b], sinks_ref, bias_ref, j, pad=pad)
        kv_sc[b] = blk[:, C_WIDTH:]
        yd_ref[b] = _hgrn_tile(ud_ref[b], state_sc.at[b], lb_ref, ng_ref, sums_ref, level_ref, j,
                               pad=pad)


def _hgrn_constants():
    T = SEQ_TILE
    t = np.arange(T)[:, None]
    i = np.arange(T)[None, :]
    mats = []
    for l in list(range(1, D_LEVELS)) + [D_LEVELS]:
        h = 2 ** l
        same = (t // h) == (i // h)
        mats += [same & (i <= t), same & (i > t)]
    sums = np.tile(np.concatenate(mats, axis=0).astype(np.float32), (1, 2))
    x = t ^ i
    level = np.where(i == t, 0, np.where(i < t, np.floor(np.log2(np.maximum(x, 1))).astype(np.int64) + 1, -1))
    return jnp.asarray(sums, BF16), jnp.asarray(np.tile(level, (1, D_HEADS)), jnp.int32)


def _swa_hgrn(proj3, sinks, lb, norm_g, *, pad):
    bsz, lp, _ = proj3.shape
    T = SEQ_TILE
    NB = CD_BATCH
    assert bsz % NB == 0
    sums, level = _hgrn_constants()
    bias = _swa_constants()
    const = lambda arr: pl.BlockSpec(arr.shape, lambda b, j: (0, 0))
    seq = lambda w, col: pl.BlockSpec((NB, T, w), lambda b, j: (b, j, col))
    return pl.pallas_call(
        functools.partial(_swa_hgrn_kernel, pad=pad),
        grid=(bsz // NB, lp // T),
        in_specs=[pl.BlockSpec(memory_space=pltpu.SMEM),
                  seq(C_COLS, OFF_C // C_COLS), seq(D_COLS, OFF_D // D_COLS),
                  pl.BlockSpec(bias.shape, lambda b, j: (0, 0, 0)),
                  const(lb), const(norm_g), const(sums), const(level)],
        out_specs=(seq(C_WIDTH, 0), seq(D_WIDTH, 0)),
        out_shape=(jax.ShapeDtypeStruct((bsz, lp, C_WIDTH), F32),
                   jax.ShapeDtypeStruct((bsz, lp, D_WIDTH), F32)),
        scratch_shapes=[pltpu.VMEM((NB, T, 2 * C_KV_WIDTH), F32),
                        pltpu.VMEM((NB, D_WIDTH, D_WIDTH), F32)],
        compiler_params=pltpu.CompilerParams(dimension_semantics=("arbitrary", "arbitrary"),
                                             vmem_limit_bytes=VMEM_LIMIT),
        name="swa_hgrn2",
    )(sinks, proj3, proj3, bias, lb, norm_g, sums, level)


def _merge_ffn_kernel(h_ref, gates_ref, ya_ref, ub_ref, yc_ref, yd_ref, mix_ref, scale_ref, wb_ref,
                      wo_ref, gf_ref, wu_ref, wd_ref, *rest, ff_chunk, pad, tiles_per_seq):
    fg_ref, o_ref, pool_buf = rest if len(rest) == 3 else (None,) + rest
    y_b = _pool_tile(ub_ref[...], pool_buf, pl.program_id(0) % tiles_per_seq, mix_ref[...],
                     scale_ref[...], pad=pad)
    merged = None
    row = 0
    for bi, y in enumerate((ya_ref[...], y_b, yc_ref[...], yd_ref[...])):
        w = y.shape[-1]
        part = _sigmoid(gates_ref[:, bi * D_MODEL:(bi + 1) * D_MODEL]) * jnp.dot(
            y.astype(BF16), wb_ref[row:row + w, :], preferred_element_type=F32)
        merged = part if merged is None else merged + part
        row += w
    h = h_ref[...] + jnp.dot(merged.astype(BF16), wo_ref[...], preferred_element_type=F32)
    zb = _rms(h, gf_ref[...]).astype(BF16)
    acc = h
    for c in range(D_FF // ff_chunk):
        cs = slice(c * ff_chunk, (c + 1) * ff_chunk)
        gu = jnp.dot(zb, wu_ref[:, cs], preferred_element_type=F32)
        up = jnp.dot(zb, wu_ref[:, D_FF + c * ff_chunk:D_FF + (c + 1) * ff_chunk],
                     preferred_element_type=F32)
        acc = acc + jnp.dot((_silu(gu) * up).astype(BF16), wd_ref[cs, :],
                            preferred_element_type=F32)
    o_ref[...] = acc if fg_ref is None else _rms(acc, fg_ref[...])


def _merge_ffn(h2, proj2, ya, yc, yd, mix_bd, scale, wb, wo, gf, wu, wd, final_g=None, *, pad, lp):
    m = h2.shape[0]
    tm = ROW_TILE
    assert lp % tm == 0
    rows = lambda w, col=0: pl.BlockSpec((tm, w), lambda i: (i, col))
    const = lambda arr: pl.BlockSpec(arr.shape, lambda i: (0, 0), pipeline_mode=pl.Buffered(1))
    last = () if final_g is None else (final_g,)
    return pl.pallas_call(
        functools.partial(_merge_ffn_kernel, ff_chunk=256, pad=pad, tiles_per_seq=lp // tm),
        grid=(m // tm,),
        in_specs=[rows(D_MODEL), rows(GATE_COLS), rows(A_WIDTH), rows(B_WIDTH, OFF_B // B_WIDTH),
                  rows(C_WIDTH), rows(D_WIDTH), const(mix_bd), const(scale), const(wb), const(wo),
                  const(gf), const(wu), const(wd)] + [const(g) for g in last],
        out_specs=rows(D_MODEL),
        out_shape=jax.ShapeDtypeStruct((m, D_MODEL), F32),
        scratch_shapes=[pltpu.VMEM((B_HALO + tm, B_WIDTH), F32)],
        compiler_params=pltpu.CompilerParams(dimension_semantics=("arbitrary",),
                                             vmem_limit_bytes=VMEM_LIMIT),
        name="merge_ffn",
    )(h2, proj2, ya, proj2, yc, yd, mix_bd, scale, wb, wo, gf, wu, wd, *last)


def _block_diag(blocks):
    n, r, c = blocks.shape
    out = jnp.zeros((n * r, n * c), blocks.dtype)
    for i in range(n):
        out = out.at[i * r:(i + 1) * r, i * c:(i + 1) * c].set(blocks[i])
    return out


def kernel(x, meta, norm_mix, norm_ffn, norm_final, w_in, w_branch, w_out, a_mu, a_w_up, a_w0, a_a_up, a_a0, a_g_up, a_kk, a_ka, a_rk, a_ln_w, a_ln_b, a_vres_down, a_vres_up, a_vres0, b_mix, b_scale, c_sinks, d_lower_bounds, d_norm, w_ffn_up, w_ffn_down):
    bsz, seq, _ = x.shape
    depth = w_in.shape[0]
    T = SEQ_TILE
    L = N_META + seq
    pad = (-L) % T
    lp = L + pad
    assert (pad + N_META) % T == 0 and seq % T == 0
    h = jnp.concatenate([jnp.zeros((bsz, pad, D_MODEL), F32),
                         jnp.broadcast_to(meta.astype(F32)[None], (bsz, N_META, D_MODEL)),
                         x.astype(F32)], axis=1).reshape(bsz * lp, D_MODEL)
    lb_w = jax.nn.softmax(d_lower_bounds.astype(F32), axis=0)
    lb_table = jnp.cumsum(lb_w, axis=0) - lb_w[0]
    row2 = lambda t: t.reshape(1, -1).astype(F32)
    v_first = None
    for l in range(depth):
        proj2 = _in_proj(h, row2(norm_mix[l]), w_in[l].astype(BF16), pad=pad, lp=lp)
        proj3 = proj2.reshape(bsz, lp, IN_COLS)
        wwa = jnp.zeros((128, 2 * A_WIDTH), F32)
        wwa = wwa.at[:64, :A_WIDTH].set(a_w_up[l]).at[64:, A_WIDTH:].set(a_a_up[l])
        pa = dict(mu=row2(a_mu[l]), wwa=wwa.astype(BF16), w0=row2(a_w0[l]), a0=row2(a_a0[l]),
                  gup=a_g_up[l].astype(BF16), kk=row2(a_kk[l]), ka=row2(a_ka[l]), rk=row2(a_rk[l]),
                  lnw=row2(a_ln_w[l]), lnb=row2(a_ln_b[l]))
        if l == 0:
            y_a, v_first = _rwkv(proj3, None, pa, has_vres=False)
        else:
            pa.update(vd=a_vres_down[l - 1].astype(BF16), vu=a_vres_up[l - 1].astype(BF16),
                      v0=row2(a_vres0[l - 1]))
            y_a = _rwkv(proj3, v_first, pa, has_vres=True)
        y_c, y_d = _swa_hgrn(proj3, c_sinks[l].astype(F32), row2(lb_table[l]), row2(d_norm[l]),
                             pad=pad)
        flat = lambda t: t.reshape(bsz * lp, t.shape[-1])
        h = _merge_ffn(h, proj2, flat(y_a), flat(y_c), flat(y_d),
                       _block_diag(b_mix[l]).astype(BF16), row2(b_scale[l]),
                       w_branch[l].astype(BF16), w_out[l].astype(BF16), row2(norm_ffn[l]),
                       w_ffn_up[l].astype(BF16), w_ffn_down[l].astype(BF16),
                       row2(norm_final) if l == depth - 1 else None, pad=pad, lp=lp)
    return h.reshape(bsz, lp, D_MODEL)[:, pad + N_META:]
```

```python
import functools

import jax
import jax.numpy as jnp
import numpy as np
from jax import lax
from jax.experimental import pallas as pl
from jax.experimental.pallas import tpu as pltpu

F32 = jnp.float32
BF16 = jnp.bfloat16

D_MODEL = 1024
N_META = 16
NORM_EPS = 1e-6
MASK_VALUE = -1e30

A_HEADS = 4
A_HEAD_DIM = 64
A_WIDTH = 256
A_GN_EPS = 64e-5
A_COLS = 1024
A_CHUNK = 64
A_BATCH = 4
A_GROUP = 2

B_WIDTH = 256
B_GROUP_DIM = 64
B_WINDOWS = (2, 4, 8, 16)
B_HALO = 16

C_HEADS = 8
C_KV_HEADS = 2
C_GROUP = 4
C_HEAD_DIM = 64
C_WIDTH = 512
C_KV_WIDTH = 128
C_COLS = 768

D_HEADS = 4
D_KEY_DIM = 64
D_WIDTH = 256
D_COLS = 1024
CD_BATCH = 4
D_LEVELS = 7

D_FF = 2816
GATE_COLS = 4 * D_MODEL
OFF_A = GATE_COLS
OFF_B = OFF_A + A_COLS
OFF_C = OFF_B + B_WIDTH
OFF_D = OFF_C + C_COLS
IN_COLS = OFF_D + D_COLS
MIX_WIDTH = 1280

SEQ_TILE = 128
ROW_TILE = 384
VMEM_LIMIT = 56 * 1024 * 1024


def _dot(a, b):
    return jnp.dot(a.astype(BF16), b.astype(BF16), preferred_element_type=F32)


def _dot_nt(a, b):
    return lax.dot_general(a.astype(BF16), b.astype(BF16), (((1,), (1,)), ((), ())),
                           preferred_element_type=F32)


def _dot_tn(a, b):
    return lax.dot_general(a.astype(BF16), b.astype(BF16), (((0,), (0,)), ((), ())),
                           preferred_element_type=F32)


def _interleave(*gens):
    gens = list(gens)
    while gens:
        for gen in list(gens):
            if next(gen, "done") == "done":
                gens.remove(gen)


def _split2(x):
    hi = x.astype(BF16)
    lo = (x - hi.astype(F32)).astype(BF16)
    return hi, lo


def _split3(x):
    hi = x.astype(BF16)
    r = x - hi.astype(F32)
    mid = r.astype(BF16)
    lo = (r - mid.astype(F32)).astype(BF16)
    return hi, mid, lo


def _const_dot(c, x):
    hi, mid, lo = _split3(x)
    d = lambda p: jnp.dot(c, p, preferred_element_type=F32)
    return d(hi) + d(mid) + d(lo)


def _mm(a, b):
    return jnp.dot(a, b, preferred_element_type=F32)


def _mm_tn(a, b):
    return lax.dot_general(a, b, (((0,), (0,)), ((), ())), preferred_element_type=F32)


def _d3(a, b):
    return _mm(a[0], b[0]) + _mm(a[0], b[1]) + _mm(a[1], b[0])


def _unit_lower_inverses(mats, ri, ci, box):
    in16 = (ri // 16) == (ci // 16)
    in32 = (ri // 32) == (ci // 32)
    diag = ri == ci
    as_mask = lambda cond: jnp.where(cond, 1.0, 0.0).astype(BF16)
    plus_eye = lambda m: jnp.where(diag, 1.0, m).astype(BF16)
    m16, eye = as_mask(in16), as_mask(diag)
    xb = [a * m16 for a in mats]
    t = [xi + eye for xi in xb]
    yield
    for _ in range(3):
        x = [_mm(xi, xi) for xi in xb]
        yield
        xb = [xi.astype(BF16) for xi in x]
        t = [_mm(ti, plus_eye(xi)).astype(BF16) for ti, xi in zip(t, x)]
        yield
    for sel in (as_mask(in32 & ~in16), as_mask(~in32)):
        lt = [_mm(a * sel, ti) for a, ti in zip(mats, t)]
        yield
        t = [_mm(ti, plus_eye(li)).astype(BF16) for ti, li in zip(t, lt)]
        yield
    box["tinv"] = t


def _head_sum(x, ones_bd):
    hi, lo = _split2(x)
    d = lambda p: jnp.dot(p, ones_bd, preferred_element_type=F32)
    return d(hi) + d(lo)


def _head_ones(width, seg):
    r = lax.broadcasted_iota(jnp.int32, (width, width), 0) // seg
    c = lax.broadcasted_iota(jnp.int32, (width, width), 1) // seg
    return jnp.where(r == c, 1.0, 0.0).astype(BF16)


def _sigmoid(x):
    return 1.0 / (1.0 + jnp.exp(-x))


def _silu(x):
    return x * _sigmoid(x)


def _rms(x, g):
    return x * lax.rsqrt(jnp.mean(x * x, axis=-1, keepdims=True) + NORM_EPS) * g


def _in_proj_kernel(x_ref, g_ref, w_ref, o_ref, *, pad, tiles_per_seq, tm, tn):
    z = _rms(x_ref[...], g_ref[...])
    row = (pl.program_id(0) % tiles_per_seq) * tm + lax.broadcasted_iota(jnp.int32, (tm, 1), 0)
    zb = jnp.where(row >= pad, z, 0.0).astype(BF16)
    for j in range(IN_COLS // tn):
        o_ref[:, j * tn:(j + 1) * tn] = jnp.dot(zb, w_ref[:, j * tn:(j + 1) * tn],
                                                preferred_element_type=F32)


def _in_proj(h2, g, w, *, pad, lp):
    m = h2.shape[0]
    tm = ROW_TILE
    assert lp % tm == 0
    return pl.pallas_call(
        functools.partial(_in_proj_kernel, pad=pad, tiles_per_seq=lp // tm, tm=tm, tn=1024),
        grid=(m // tm,),
        in_specs=[pl.BlockSpec((tm, D_MODEL), lambda i: (i, 0)),
                  pl.BlockSpec((1, D_MODEL), lambda i: (0, 0)),
                  pl.BlockSpec((D_MODEL, IN_COLS), lambda i: (0, 0), pipeline_mode=pl.Buffered(1))],
        out_specs=pl.BlockSpec((tm, IN_COLS), lambda i: (i, 0)),
        out_shape=jax.ShapeDtypeStruct((m, IN_COLS), F32),
        compiler_params=pltpu.CompilerParams(dimension_semantics=("arbitrary",),
                                             vmem_limit_bytes=VMEM_LIMIT),
        name="in_proj",
    )(h2, g, w)


def _rwkv_kernel(*refs, has_vres):
    for parity in (0, 1):
        pl.when(pl.program_id(1) % 2 == parity)(
            functools.partial(_rwkv_step, refs, has_vres=has_vres, wslot=parity))


def _rwkv_step(refs, *, has_vres, wslot):
    handoff = refs[-7:]
    tok_sc, aab_sc, rhs_sc, bkv_sc, arb_sc, rt_sc, gam_sc = handoff
    refs = refs[:-7]
    if has_vres:
        (u_ref, vf_ref, mu_ref, wwa_ref, w0_ref, a0_ref, gup_ref, kk_ref, ka_ref, rk_ref,
         lnw_ref, lnb_ref, vd_ref, vu_ref, v0_ref, y_ref, state_sc, carry_sc) = refs
    else:
        (u_ref, mu_ref, wwa_ref, w0_ref, a0_ref, gup_ref, kk_ref, ka_ref, rk_ref,
         lnw_ref, lnb_ref, y_ref, vout_ref, state_sc, carry_sc) = refs
    T = SEQ_TILE
    C = A_CHUNK
    N = A_HEAD_DIM
    W = A_WIDTH
    NB = A_BATCH
    R = NB * T
    NCH = R // C
    j = pl.program_id(1)

    @pl.when((pl.program_id(0) == 0) & (j == 0))
    def _():
        state_sc[...] = jnp.zeros_like(state_sc)
        for ref in handoff:
            ref[...] = jnp.zeros_like(ref)

    @pl.when(j == 0)
    def _():
        carry_sc[...] = jnp.zeros_like(carry_sc)

    rslot = 1 - wslot

    u = u_ref[...].reshape(R, A_COLS)
    row = lax.broadcasted_iota(jnp.int32, (R, 1), 0)
    prev = pltpu.roll(u, 1, axis=0)
    for b in range(NB):
        prev = jnp.where(row == b * T, carry_sc[b:b + 1, :], prev)
        carry_sc[b:b + 1, :] = u[(b + 1) * T - 1:(b + 1) * T, :]
    ones_bd = _head_ones(W, N)
    lane = lax.broadcasted_iota(jnp.int32, (1, 128), 1)

    def token_stage(b):
        rows = slice(b * T, (b + 1) * T)
        x = u[rows] + (prev[rows] - u[rows]) * mu_ref[...]
        r = x[:, 0:W]
        k = x[:, W:2 * W]
        v = x[:, 2 * W:3 * W]
        slab = x[:, 3 * W:3 * W + 128]
        gd = x[:, 3 * W + 128:]
        slab = jnp.where(lane < 64, jnp.tanh(slab), slab)
        wa = _dot(slab, wwa_ref[...])
        zw = -(w0_ref[...] + wa[:, :W])
        softplus = jnp.maximum(zw, 0.0) + jnp.log(1.0 + jnp.exp(-jnp.abs(zw)))
        logw = -jnp.exp(-softplus - 0.5)
        a = _sigmoid(a0_ref[...] + wa[:, W:])
        g = _dot(_sigmoid(gd), gup_ref[...])
        if has_vres:
            mix = _sigmoid(v0_ref[...] + _dot(_dot(v, vd_ref[...]), vu_ref[...]))
            v = v + (vf_ref[b] - v) * mix
        else:
            vout_ref[b] = v
        kkr = k * kk_ref[...]
        kk = kkr / jnp.maximum(jnp.sqrt(_head_sum(kkr * kkr, ones_bd)), 1e-12)
        k2 = k * (1.0 + (a - 1.0) * ka_ref[...])
        return dict(r=r, v=v, kk=kk, k2=k2, bhat=kk * a, logw=logw, g=g)

    ri = lax.broadcasted_iota(jnp.int32, (W, W), 0)
    ci = lax.broadcasted_iota(jnp.int32, (W, W), 1)
    same_head = (ri // C) == (ci // C)
    tw = lax.broadcasted_iota(jnp.int32, (C, W), 0)
    sw = lax.broadcasted_iota(jnp.int32, (C, W), 1) % C
    tri = jnp.where(lax.broadcasted_iota(jnp.int32, (C, C), 1)
                    <= lax.broadcasted_iota(jnp.int32, (C, C), 0), 1.0, 0.0).astype(BF16)

    def stack(t):
        return jnp.concatenate([t.astype(BF16)] * A_HEADS, axis=0) * ones_bd

    def chunk_inputs(tok, i):
        sl = slice(i * C, (i + 1) * C)
        lw = tok["logw"][sl]
        bhat, k2 = tok["bhat"][sl], tok["k2"][sl]
        cl = _const_dot(tri, lw)
        cl_last = cl[C - 1:C, :]
        e_neg = jnp.exp(-cl)
        e_last = jnp.exp(cl_last - cl)
        at = -tok["kk"][sl] * jnp.exp(cl - lw)
        rt = tok["r"][sl] * jnp.exp(cl)
        v_s = stack(tok["v"][sl])
        aa = _dot_nt(jnp.concatenate([at, rt], axis=0),
                     jnp.concatenate([stack(bhat * e_neg), stack(k2 * e_neg)], axis=0))
        a_ab = stack(jnp.where(sw < tw, aa[:C, :W], 0.0))
        a_ak = stack(jnp.where(sw < tw, aa[:C, W:], 0.0))
        a_rb = jnp.where(sw <= tw, aa[C:, :W], 0.0).astype(BF16)
        a_rk = jnp.where(sw <= tw, aa[C:, W:], 0.0).astype(BF16)
        rhs = jnp.concatenate([stack(at), _mm(a_ak, v_s).astype(BF16)], axis=1)
        btil = stack(bhat * e_last)
        ktil = stack(k2 * e_last)
        return a_ab, rhs, a_rb, a_rk, v_s, btil, ktil, rt, jnp.exp(cl_last)

    def chunk_terms(inputs, tinv):
        _, rhs, a_rb, a_rk, v_s, btil, ktil, rt, gam = inputs
        wu = _mm(tinv, rhs)
        w_t = wu[:, :W].astype(BF16)
        u_t = wu[:, W:].astype(BF16)
        g_col = jnp.sum(jnp.where(ri == ci, gam, 0.0), axis=1, keepdims=True)
        m_mat = (g_col, _mm_tn(btil, w_t).astype(BF16))
        n_mat = _mm_tn(btil, u_t) + _mm_tn(ktil, v_s)
        q_mat = rt + _mm(a_rb, w_t)
        p_mat = _mm(a_rb, u_t) + _mm(a_rk, v_s)
        return m_mat, n_mat, q_mat, p_mat

    def finish(tok, outs):
        o = jnp.concatenate(outs, axis=0)
        mean = _head_sum(o, ones_bd) * (1.0 / N)
        d = o - mean
        var = _head_sum(d * d, ones_bd) * (1.0 / N)
        o = d * lax.rsqrt(var + A_GN_EPS) * lnw_ref[...] + lnb_ref[...]
        bonus = _head_sum(tok["r"] * tok["k2"] * rk_ref[...], ones_bd) * tok["v"]
        y_ref[...] = ((o + bonus) * tok["g"]).reshape(NB, T, W)

    per = T // C

    def finish_previous_tile():
        box = {}
        yield from _unit_lower_inverses([aab_sc[rslot, i] for i in range(NCH)], ri, ci, box)
        terms = []
        for i in range(NCH):
            held = (None, rhs_sc[rslot, i], arb_sc[rslot, i, 0], arb_sc[rslot, i, 1],
                    bkv_sc[rslot, i, 2], bkv_sc[rslot, i, 0], bkv_sc[rslot, i, 1],
                    rt_sc[rslot, i], gam_sc[rslot, i, 0:1])
            terms.append(chunk_terms(held, box["tinv"][i]))
            yield
        sts = [state_sc[b] for b in range(NB)]
        outs = [None] * NCH
        for c in range(per):
            for b in range(NB):
                m_mat, n_mat, q_mat, p_mat = terms[b * per + c]
                sb = sts[b].astype(BF16)
                outs[b * per + c] = _mm(q_mat.astype(BF16), sb) + p_mat
                sts[b] = m_mat[0] * sts[b] + _mm(m_mat[1], sb) + n_mat
            yield
        for b in range(NB):
            state_sc[b] = sts[b]
        finish(dict(r=tok_sc[rslot, 0], k2=tok_sc[rslot, 1], v=tok_sc[rslot, 2],
                    g=tok_sc[rslot, 3]), outs)

    def prepare_this_tile():
        for b in range(NB):
            tok = token_stage(b)
            for n, name in enumerate(("r", "k2", "v", "g")):
                tok_sc[wslot, n, b * T:(b + 1) * T] = tok[name]
            yield
            for c in range(per):
                a_ab, rhs, a_rb, a_rk, v_s, btil, ktil, rt, gam = chunk_inputs(tok, c)
                i = b * per + c
                aab_sc[wslot, i] = a_ab
                rhs_sc[wslot, i] = rhs
                arb_sc[wslot, i, 0] = a_rb
                arb_sc[wslot, i, 1] = a_rk
                bkv_sc[wslot, i, 0] = btil
                bkv_sc[wslot, i, 1] = ktil
                bkv_sc[wslot, i, 2] = v_s
                rt_sc[wslot, i] = rt
                gam_sc[wslot, i, 0:1] = gam
                yield

    _interleave(finish_previous_tile(), prepare_this_tile())

    @pl.when(j == 0)
    def _():
        state_sc[...] = jnp.zeros_like(state_sc)


def _rwkv(proj3, v_first, p, *, has_vres):
    bsz, lp, _ = proj3.shape
    T = SEQ_TILE
    NB = A_BATCH
    assert bsz % NB == 0
    nt = lp // T
    NCH = NB * T // A_CHUNK
    W = A_WIDTH
    in_spec = lambda w, col: pl.BlockSpec((NB, T, w), lambda b, j: (b, jnp.minimum(j, nt - 1), col))
    y_spec = pl.BlockSpec((NB, T, W), lambda b, j: (b, jnp.maximum(j - 1, 0), 0))
    v_spec = pl.BlockSpec((NB, T, W), lambda b, j: (b, j, 0))
    full = lambda arr: pl.BlockSpec(arr.shape, lambda b, j: (0,) * arr.ndim)
    args = [proj3]
    specs = [in_spec(A_COLS, OFF_A // A_COLS)]
    if has_vres:
        args.append(v_first)
        specs.append(in_spec(W, 0))
    names = ["mu", "wwa", "w0", "a0", "gup", "kk", "ka", "rk", "lnw", "lnb"]
    if has_vres:
        names += ["vd", "vu", "v0"]
    for n in names:
        args.append(p[n])
        specs.append(full(p[n]))
    y_shape = jax.ShapeDtypeStruct((bsz, lp, W), F32)
    v_shape = jax.ShapeDtypeStruct((bsz, lp + T, W), F32)
    out_shape = y_shape if has_vres else (y_shape, v_shape)
    out_specs = y_spec if has_vres else (y_spec, v_spec)
    return pl.pallas_call(
        functools.partial(_rwkv_kernel, has_vres=has_vres),
        grid=(bsz // NB, nt + 1),
        in_specs=specs,
        out_specs=out_specs,
        out_shape=out_shape,
        scratch_shapes=[pltpu.VMEM((NB, W, W), F32),
                        pltpu.VMEM((8, A_COLS), F32),
                        pltpu.VMEM((2, 4, NB * T, W), F32),
                        pltpu.VMEM((2, NCH, W, W), BF16),
                        pltpu.VMEM((2, NCH, W, 2 * W), BF16),
                        pltpu.VMEM((2, NCH, 3, W, W), BF16),
                        pltpu.VMEM((2, NCH, 2, A_CHUNK, W), BF16),
                        pltpu.VMEM((2, NCH, A_CHUNK, W), F32),
                        pltpu.VMEM((2, NCH, 8, W), F32)],
        compiler_params=pltpu.CompilerParams(dimension_semantics=("arbitrary", "arbitrary"),
                                             vmem_limit_bytes=VMEM_LIMIT),
        name="rwkv7",
    )(*args)


def _pool_tile(u, buf, tile_in_seq, mix, scale, *, pad):
    rows = u.shape[0]

    @pl.when(tile_in_seq == 0)
    def _():
        buf[0:B_HALO, :] = jnp.zeros((B_HALO, B_WIDTH), F32)

    buf[B_HALO:B_HALO + rows, :] = u
    lane_group = lax.broadcasted_iota(jnp.int32, (1, B_WIDTH), 1) // B_GROUP_DIM
    acc = u
    win = jnp.zeros_like(u)
    for s in range(1, max(B_WINDOWS)):
        acc = acc + buf[B_HALO - s:B_HALO - s + rows, :]
        if s + 1 in B_WINDOWS:
            win = jnp.where(lane_group == B_WINDOWS.index(s + 1), acc, win)
    wlane = jnp.zeros((1, B_WIDTH), jnp.int32)
    for gi, w in enumerate(B_WINDOWS):
        wlane = jnp.where(lane_group == gi, w, wlane)
    t_real = tile_in_seq * rows + lax.broadcasted_iota(jnp.int32, (rows, 1), 0) - pad
    cnt = jnp.minimum(jnp.maximum(t_real + 1, 1), wlane).astype(F32)
    pooled = win / cnt - u
    buf[0:B_HALO, :] = u[rows - B_HALO:, :]
    return _dot(pooled, mix) * scale


def _swa_constants():
    T = SEQ_TILE
    dist = T + np.arange(T)[:, None] - np.arange(2 * T)[None, :]
    band = (dist >= 0) & (dist < T)
    bias = np.empty((C_KV_HEADS, C_GROUP * T, 2 * T), np.float32)
    for hq in range(C_HEADS):
        slope = np.float32(2.0 ** (-8.0 * (hq + 1) / C_HEADS))
        rows = slice((hq % C_GROUP) * T, (hq % C_GROUP + 1) * T)
        bias[hq // C_GROUP, rows] = np.where(band, -slope * dist.astype(np.float32), MASK_VALUE)
    return jnp.asarray(bias)


def _swa_tile(y_ref, blk, kv_prev, sinks_ref, bias_ref, j, *, pad):
    T = SEQ_TILE
    G = C_GROUP
    kv = blk[:, C_WIDTH:]
    kw = jnp.concatenate([kv_prev[:, :C_KV_WIDTH], kv[:, :C_KV_WIDTH]], axis=0)
    vw = jnp.concatenate([kv_prev[:, C_KV_WIDTH:], kv[:, C_KV_WIDTH:]], axis=0)
    lane = lax.broadcasted_iota(jnp.int32, (1, 2 * C_HEAD_DIM), 1)
    low = lane < C_HEAD_DIM
    kw_sw = pltpu.roll(kw, C_HEAD_DIM, axis=1)
    vw_sw = pltpu.roll(vw, C_HEAD_DIM, axis=1)
    head = lax.broadcasted_iota(jnp.int32, (G * T, 1), 0) // T
    key_real = (j - 1) * T + lax.broadcasted_iota(jnp.int32, (1, 2 * T), 1) >= pad
    tiles = []
    for hk in range(C_KV_HEADS):
        k2 = jnp.where(low == (hk == 0), kw, kw_sw).astype(BF16)
        v2 = jnp.where(low == (hk == 0), vw, vw_sw).astype(BF16)
        q_rows = []
        sink = jnp.zeros((G * T, 1), F32)
        for gq in range(G):
            hq = hk * G + gq
            pair = blk[:, (hq // 2) * 128:(hq // 2 + 1) * 128]
            q_rows.append(jnp.where(low == (hq % 2 == 0), pair * (C_HEAD_DIM ** -0.5), 0.0))
            sink = jnp.where(head == gq, sinks_ref[hq], sink)
        s = _dot_nt(jnp.concatenate(q_rows, axis=0), k2)
        logits = jnp.where(key_real, s + bias_ref[hk], MASK_VALUE)
        m = jnp.maximum(jnp.max(logits, axis=-1, keepdims=True), sink)
        p = jnp.exp(logits - m)
        denom = jnp.sum(p, axis=-1, keepdims=True) + jnp.exp(sink - m)
        yield
        o = _dot(p, v2) / denom
        for pr in range(G // 2):
            tiles.append(jnp.where(low, o[2 * pr * T:(2 * pr + 1) * T], o[(2 * pr + 1) * T:(2 * pr + 2) * T]))
        yield
    y_ref[...] = jnp.concatenate(tiles, axis=1)


def _hgrn_tile(y_ref, u, state_sc, lb_ref, ng_ref, sums_ref, level_ref, j, *, pad):
    T = SEQ_TILE
    N = D_KEY_DIM
    lb = lb_ref[...]
    q = _silu(u[:, :D_WIDTH])
    fpre = u[:, D_WIDTH:2 * D_WIDTH]
    vin = u[:, 2 * D_WIDTH:3 * D_WIDTH]
    gate = u[:, 3 * D_WIDTH:]
    sig = _sigmoid(fpre)
    f = lb + (1.0 - lb) * sig
    t_real = j * T + lax.broadcasted_iota(jnp.int32, (T, 1), 0) - pad
    logf = jnp.where(t_real >= 0, jnp.log(jnp.maximum(f, 1e-30)), 0.0)
    kx = (1.0 - lb) * (1.0 - sig)

    sums = _mm(sums_ref[...], jnp.concatenate(_split2(logf), axis=0))
    same_head = (lax.broadcasted_iota(jnp.int32, (D_WIDTH, D_WIDTH), 0) // N
                 == lax.broadcasted_iota(jnp.int32, (D_WIDTH, D_WIDTH), 1) // N)
    head_rows = (lax.broadcasted_iota(jnp.int32, (D_HEADS * T, D_WIDTH), 0) // T
                 == lax.broadcasted_iota(jnp.int32, (D_HEADS * T, D_WIDTH), 1) // N)
    stack = lambda t: jnp.where(head_rows, jnp.concatenate([t] * D_HEADS, axis=0), 0.0)
    k_st = stack(kx).astype(BF16)
    v_st = stack(vin).astype(BF16)
    lv = level_ref[...]
    att = jnp.where(lv == 0, _dot_nt(q, k_st), 0.0)
    for l in range(D_LEVELS):
        if l == 0:
            q_l, k_l = q * jnp.exp(logf), k_st
        else:
            p_h = sums[(2 * l - 2) * T:(2 * l - 1) * T]
            x_h = sums[(2 * l - 1) * T:2 * l * T]
            q_l = q * jnp.exp(p_h)
            k_l = k_st * jnp.concatenate([jnp.exp(x_h).astype(BF16)] * D_HEADS, axis=0)
        att = jnp.where(lv == l + 1, _dot_nt(q_l, k_l), att)
        yield
    b = sums[(2 * D_LEVELS - 2) * T:(2 * D_LEVELS - 1) * T]
    b_rev = sums[(2 * D_LEVELS - 1) * T:]
    st = state_sc[...]
    o = _dot(att, v_st) + _dot_nt(q * jnp.exp(b), st)
    upd = jnp.where(same_head, _dot_tn(vin, kx * jnp.exp(b_rev)), 0.0)
    state_sc[...] = st * jnp.exp(b[T - 1:T, :]) + upd

    ones_bd = _head_ones(D_WIDTH, N)
    ms = _head_sum(o * o, ones_bd) * (1.0 / N)
    y_ref[...] = o * lax.rsqrt(ms + NORM_EPS) * ng_ref[...] * _silu(gate)


def _swa_hgrn_kernel(sinks_ref, uc_ref, ud_ref, bias_ref, lb_ref, ng_ref, sums_ref, level_ref, yc_ref, yd_ref,
                     kv_sc, state_sc, *, pad):
    j = pl.program_id(1)

    @pl.when(j == 0)
    def _():
        kv_sc[...] = jnp.zeros_like(kv_sc)
        state_sc[...] = jnp.zeros_like(state_sc)

    swa, hgrn = [], []
    for b in range(CD_BATCH):
        blk = uc_ref[b]
        swa.append(_swa_tile(yc_ref.at[b], blk, kv_sc[b], sinks_ref, bias_ref, j, pad=pad))
        kv_sc[b] = blk[:, C_WIDTH:]
        hgrn.append(_hgrn_tile(yd_ref.at[b], ud_ref[b], state_sc.at[b], lb_ref, ng_ref, sums_ref,
                               level_ref, j, pad=pad))
    for b in range(CD_BATCH):
        _interleave(swa[b], hgrn[(b + 1) % CD_BATCH])


def _hgrn_constants():
    T = SEQ_TILE
    t = np.arange(T)[:, None]
    i = np.arange(T)[None, :]
    mats = []
    for l in list(range(1, D_LEVELS)) + [D_LEVELS]:
        h = 2 ** l
        same = (t // h) == (i // h)
        mats += [same & (i <= t), same & (i > t)]
    sums = np.tile(np.concatenate(mats, axis=0).astype(np.float32), (1, 2))
    x = t ^ i
    level = np.where(i == t, 0, np.where(i < t, np.floor(np.log2(np.maximum(x, 1))).astype(np.int64) + 1, -1))
    return jnp.asarray(sums, BF16), jnp.asarray(np.tile(level, (1, D_HEADS)), jnp.int32)


def _swa_hgrn(proj3, sinks, lb, norm_g, *, pad):
    bsz, lp, _ = proj3.shape
    T = SEQ_TILE
    NB = CD_BATCH
    assert bsz % NB == 0
    sums, level = _hgrn_constants()
    bias = _swa_constants()
    const = lambda arr: pl.BlockSpec(arr.shape, lambda b, j: (0, 0))
    seq = lambda w, col: pl.BlockSpec((NB, T, w), lambda b, j: (b, j, col))
    return pl.pallas_call(
        functools.partial(_swa_hgrn_kernel, pad=pad),
        grid=(bsz // NB, lp // T),
        in_specs=[pl.BlockSpec(memory_space=pltpu.SMEM),
                  seq(C_COLS, OFF_C // C_COLS), seq(D_COLS, OFF_D // D_COLS),
                  pl.BlockSpec(bias.shape, lambda b, j: (0, 0, 0)),
                  const(lb), const(norm_g), const(sums), const(level)],
        out_specs=(seq(C_WIDTH, 0), seq(D_WIDTH, 0)),
        out_shape=(jax.ShapeDtypeStruct((bsz, lp, C_WIDTH), F32),
                   jax.ShapeDtypeStruct((bsz, lp, D_WIDTH), F32)),
        scratch_shapes=[pltpu.VMEM((NB, T, 2 * C_KV_WIDTH), F32),
                        pltpu.VMEM((NB, D_WIDTH, D_WIDTH), F32)],
        compiler_params=pltpu.CompilerParams(dimension_semantics=("arbitrary", "arbitrary"),
                                             vmem_limit_bytes=VMEM_LIMIT),
        name="swa_hgrn2",
    )(sinks, proj3, proj3, bias, lb, norm_g, sums, level)


def _merge_ffn_kernel(h_ref, gates_ref, ya_ref, ub_ref, yc_ref, yd_ref, mix_ref, scale_ref, wb_ref,
                      wo_ref, gf_ref, wu_ref, wd_ref, *rest, ff_chunk, pad, tiles_per_seq):
    fg_ref, o_ref, pool_buf = rest if len(rest) == 3 else (None,) + rest
    y_b = _pool_tile(ub_ref[...], pool_buf, pl.program_id(0) % tiles_per_seq, mix_ref[...],
                     scale_ref[...], pad=pad)
    merged = None
    row = 0
    for bi, y in enumerate((ya_ref[...], y_b, yc_ref[...], yd_ref[...])):
        w = y.shape[-1]
        part = _sigmoid(gates_ref[:, bi * D_MODEL:(bi + 1) * D_MODEL]) * jnp.dot(
            y.astype(BF16), wb_ref[row:row + w, :], preferred_element_type=F32)
        merged = part if merged is None else merged + part
        row += w
    h = h_ref[...] + jnp.dot(merged.astype(BF16), wo_ref[...], preferred_element_type=F32)
    zb = _rms(h, gf_ref[...]).astype(BF16)
    acc = h
    for c in range(D_FF // ff_chunk):
        cs = slice(c * ff_chunk, (c + 1) * ff_chunk)
        gu = jnp.dot(zb, wu_ref[:, cs], preferred_element_type=F32)
        up = jnp.dot(zb, wu_ref[:, D_FF + c * ff_chunk:D_FF + (c + 1) * ff_chunk],
                     preferred_element_type=F32)
        acc = acc + jnp.dot((_silu(gu) * up).astype(BF16), wd_ref[cs, :],
                            preferred_element_type=F32)
    o_ref[...] = acc if fg_ref is None else _rms(acc, fg_ref[...])


def _merge_ffn(h2, proj2, ya, yc, yd, mix_bd, scale, wb, wo, gf, wu, wd, final_g=None, *, pad, lp):
    m = h2.shape[0]
    tm = ROW_TILE
    assert lp % tm == 0
    rows = lambda w, col=0: pl.BlockSpec((tm, w), lambda i: (i, col))
    const = lambda arr: pl.BlockSpec(arr.shape, lambda i: (0, 0), pipeline_mode=pl.Buffered(1))
    last = () if final_g is None else (final_g,)
    return pl.pallas_call(
        functools.partial(_merge_ffn_kernel, ff_chunk=256, pad=pad, tiles_per_seq=lp // tm),
        grid=(m // tm,),
        in_specs=[rows(D_MODEL), rows(GATE_COLS), rows(A_WIDTH), rows(B_WIDTH, OFF_B // B_WIDTH),
                  rows(C_WIDTH), rows(D_WIDTH), const(mix_bd), const(scale), const(wb), const(wo),
                  const(gf), const(wu), const(wd)] + [const(g) for g in last],
        out_specs=rows(D_MODEL),
        out_shape=jax.ShapeDtypeStruct((m, D_MODEL), F32),
        scratch_shapes=[pltpu.VMEM((B_HALO + tm, B_WIDTH), F32)],
        compiler_params=pltpu.CompilerParams(dimension_semantics=("arbitrary",),
                                             vmem_limit_bytes=VMEM_LIMIT),
        name="merge_ffn",
    )(h2, proj2, ya, proj2, yc, yd, mix_bd, scale, wb, wo, gf, wu, wd, *last)


def _block_diag(blocks):
    n, r, c = blocks.shape
    out = jnp.zeros((n * r, n * c), blocks.dtype)
    for i in range(n):
        out = out.at[i * r:(i + 1) * r, i * c:(i + 1) * c].set(blocks[i])
    return out


def kernel(x, meta, norm_mix, norm_ffn, norm_final, w_in, w_branch, w_out, a_mu, a_w_up, a_w0, a_a_up, a_a0, a_g_up, a_kk, a_ka, a_rk, a_ln_w, a_ln_b, a_vres_down, a_vres_up, a_vres0, b_mix, b_scale, c_sinks, d_lower_bounds, d_norm, w_ffn_up, w_ffn_down):
    bsz, seq, _ = x.shape
    depth = w_in.shape[0]
    T = SEQ_TILE
    L = N_META + seq
    pad = (-L) % T
    lp = L + pad
    assert (pad + N_META) % T == 0 and seq % T == 0
    h = jnp.concatenate([jnp.zeros((bsz, pad, D_MODEL), F32),
                         jnp.broadcast_to(meta.astype(F32)[None], (bsz, N_META, D_MODEL)),
                         x.astype(F32)], axis=1).reshape(bsz * lp, D_MODEL)
    lb_w = jax.nn.softmax(d_lower_bounds.astype(F32), axis=0)
    lb_table = jnp.cumsum(lb_w, axis=0) - lb_w[0]
    row2 = lambda t: t.reshape(1, -1).astype(F32)
    v_first = None
    for l in range(depth):
        proj2 = _in_proj(h, row2(norm_mix[l]), w_in[l].astype(BF16), pad=pad, lp=lp)
        proj3 = proj2.reshape(bsz, lp, IN_COLS)
        wwa = jnp.zeros((128, 2 * A_WIDTH), F32)
        wwa = wwa.at[:64, :A_WIDTH].set(a_w_up[l]).at[64:, A_WIDTH:].set(a_a_up[l])
        pa = dict(mu=row2(a_mu[l]), wwa=wwa.astype(BF16), w0=row2(a_w0[l]), a0=row2(a_a0[l]),
                  gup=a_g_up[l].astype(BF16), kk=row2(a_kk[l]), ka=row2(a_ka[l]), rk=row2(a_rk[l]),
                  lnw=row2(a_ln_w[l]), lnb=row2(a_ln_b[l]))
        if l == 0:
            y_a, v_first = _rwkv(proj3, None, pa, has_vres=False)
        else:
            pa.update(vd=a_vres_down[l - 1].astype(BF16), vu=a_vres_up[l - 1].astype(BF16),
                      v0=row2(a_vres0[l - 1]))
            y_a = _rwkv(proj3, v_first, pa, has_vres=True)
        y_c, y_d = _swa_hgrn(proj3, c_sinks[l].astype(F32), row2(lb_table[l]), row2(d_norm[l]),
                             pad=pad)
        flat = lambda t: t.reshape(bsz * lp, t.shape[-1])
        h = _merge_ffn(h, proj2, flat(y_a), flat(y_c), flat(y_d),
                       _block_diag(b_mix[l]).astype(BF16), row2(b_scale[l]),
                       w_branch[l].astype(BF16), w_out[l].astype(BF16), row2(norm_ffn[l]),
                       w_ffn_up[l].astype(BF16), w_ffn_down[l].astype(BF16),
                       row2(norm_final) if l == depth - 1 else None, pad=pad, lp=lp)
    return h.reshape(bsz, lp, D_MODEL)[:, pad + N_META:]
```

```python
import functools
import math

import jax
import jax.numpy as jnp
import numpy as np
from jax import lax
from jax.experimental import pallas as pl
from jax.experimental.pallas import tpu as pltpu

F32 = jnp.float32
BF16 = jnp.bfloat16

D_MODEL = 1024
N_META = 16
NORM_EPS = 1e-6
MASK_VALUE = -1e30

A_HEADS = 4
A_HEAD_DIM = 64
A_WIDTH = 256
A_GN_EPS = 64e-5
A_COLS = 1024
A_CHUNK = 64
A_BATCH = 4
A_GROUP = 2

B_WIDTH = 256
B_GROUP_DIM = 64
B_WINDOWS = (2, 4, 8, 16)
B_HALO = 16

C_HEADS = 8
C_KV_HEADS = 2
C_GROUP = 4
C_HEAD_DIM = 64
C_WIDTH = 512
C_KV_WIDTH = 128
C_COLS = 768

D_HEADS = 4
D_KEY_DIM = 64
D_WIDTH = 256
D_COLS = 1024
CD_BATCH = 4
D_LEVELS = 7

D_FF = 2816
GATE_COLS = 4 * D_MODEL
OFF_A = GATE_COLS
OFF_B = OFF_A + A_COLS
OFF_C = OFF_B + B_WIDTH
OFF_D = OFF_C + C_COLS
IN_COLS = OFF_D + D_COLS
MIX_WIDTH = 1280

SEQ_TILE = 128
ROW_TILE = 384
VMEM_LIMIT = 56 * 1024 * 1024


def _dot(a, b):
    return jnp.dot(a.astype(BF16), b.astype(BF16), preferred_element_type=F32)


def _dot_nt(a, b):
    return lax.dot_general(a.astype(BF16), b.astype(BF16), (((1,), (1,)), ((), ())),
                           preferred_element_type=F32)


def _dot_tn(a, b):
    return lax.dot_general(a.astype(BF16), b.astype(BF16), (((0,), (0,)), ((), ())),
                           preferred_element_type=F32)


def _interleave(*gens):
    gens = list(gens)
    while gens:
        for gen in list(gens):
            if next(gen, "done") == "done":
                gens.remove(gen)


def _split2(x):
    hi = x.astype(BF16)
    lo = (x - hi.astype(F32)).astype(BF16)
    return hi, lo


def _split3(x):
    hi = x.astype(BF16)
    r = x - hi.astype(F32)
    mid = r.astype(BF16)
    lo = (r - mid.astype(F32)).astype(BF16)
    return hi, mid, lo


def _const_dot(c4, x):
    hi, mid, lo = _split3(x)
    return jnp.dot(c4, jnp.concatenate([hi, mid, lo, jnp.zeros_like(hi)], axis=0),
                   preferred_element_type=F32)


def _mm(a, b):
    return jnp.dot(a, b, preferred_element_type=F32)


def _mm_tn(a, b):
    return lax.dot_general(a, b, (((0,), (0,)), ((), ())), preferred_element_type=F32)


def _d3(a, b):
    return _mm(a[0], b[0]) + _mm(a[0], b[1]) + _mm(a[1], b[0])


def _unit_lower_inverses(mats, ri, ci, box):
    in16 = (ri // 16) == (ci // 16)
    in32 = (ri // 32) == (ci // 32)
    diag = ri == ci
    as_mask = lambda cond: jnp.where(cond, 1.0, 0.0).astype(BF16)
    plus_eye = lambda m: jnp.where(diag, 1.0, m).astype(BF16)
    m16, eye = as_mask(in16), as_mask(diag)
    xb = [a * m16 for a in mats]
    t = [xi + eye for xi in xb]
    yield
    for _ in range(3):
        x = [_mm(xi, xi) for xi in xb]
        yield
        xb = [xi.astype(BF16) for xi in x]
        t = [_mm(ti, plus_eye(xi)).astype(BF16) for ti, xi in zip(t, x)]
        yield
    n = mats[0].shape[0]
    for half, sel in ((16, as_mask(in32 & ~in16)), (32, as_mask(~in32))):
        starts = range(0, n, 2 * half)
        low = lambda m: jnp.concatenate([m[r + half:r + 2 * half] for r in starts], axis=0)
        put = lambda rows, base: jnp.concatenate(
            [p for k, r in enumerate(starts) for p in (base[r:r + half], rows[k * half:(k + 1) * half])],
            axis=0)
        lt = [_mm(low(a * sel), ti).astype(BF16) for a, ti in zip(mats, t)]
        yield
        zero = jnp.zeros_like(t[0])
        t = [put(_mm(low(ti), put(li, zero) + eye).astype(BF16), ti) for ti, li in zip(t, lt)]
        yield
    box["tinv"] = t


def _head_sum(x, ones_bd):
    return jnp.dot(x.astype(BF16), ones_bd, preferred_element_type=F32)


def _head_ones(width, seg):
    r = lax.broadcasted_iota(jnp.int32, (width, width), 0) // seg
    c = lax.broadcasted_iota(jnp.int32, (width, width), 1) // seg
    return jnp.where(r == c, 1.0, 0.0).astype(BF16)


def _sigmoid(x):
    return 1.0 / (1.0 + jnp.exp(-x))


def _silu(x):
    return x * _sigmoid(x)


def _rms(x, g):
    return x * lax.rsqrt(jnp.mean(x * x, axis=-1, keepdims=True) + NORM_EPS) * g


def _in_proj_kernel(x_ref, g_ref, w_ref, o_ref, *, pad, tiles_per_seq, tm, tn):
    z = _rms(x_ref[...], g_ref[...])
    row = (pl.program_id(0) % tiles_per_seq) * tm + lax.broadcasted_iota(jnp.int32, (tm, 1), 0)
    zb = jnp.where(row >= pad, z, 0.0).astype(BF16)
    for j in range(IN_COLS // tn):
        o_ref[:, j * tn:(j + 1) * tn] = jnp.dot(zb, w_ref[:, j * tn:(j + 1) * tn],
                                                preferred_element_type=F32)


def _in_proj(h2, g, w, *, pad, lp):
    m = h2.shape[0]
    tm = ROW_TILE
    assert lp % tm == 0
    return pl.pallas_call(
        functools.partial(_in_proj_kernel, pad=pad, tiles_per_seq=lp // tm, tm=tm, tn=1024),
        grid=(m // tm,),
        in_specs=[pl.BlockSpec((tm, D_MODEL), lambda i: (i, 0)),
                  pl.BlockSpec((1, D_MODEL), lambda i: (0, 0)),
                  pl.BlockSpec((D_MODEL, IN_COLS), lambda i: (0, 0), pipeline_mode=pl.Buffered(1))],
        out_specs=pl.BlockSpec((tm, IN_COLS), lambda i: (i, 0)),
        out_shape=jax.ShapeDtypeStruct((m, IN_COLS), F32),
        compiler_params=pltpu.CompilerParams(dimension_semantics=("arbitrary",),
                                             vmem_limit_bytes=VMEM_LIMIT),
        name="in_proj",
    )(h2, g, w)


def _rwkv_kernel(*refs, has_vres):
    for parity in (0, 1):
        pl.when(pl.program_id(1) % 2 == parity)(
            functools.partial(_rwkv_step, refs, has_vres=has_vres, wslot=parity))


def _rwkv_step(refs, *, has_vres, wslot):
    handoff = refs[-7:]
    tok_sc, aab_sc, rhs_sc, bkv_sc, arb_sc, rt_sc, gam_sc = handoff
    refs = refs[:-7]
    if has_vres:
        (u_ref, vf_ref, mu_ref, wwa_ref, w0_ref, a0_ref, gup_ref, kk_ref, ka_ref, rk_ref,
         lnw_ref, lnb_ref, vd_ref, vu_ref, v0_ref, y_ref, state_sc, carry_sc) = refs
    else:
        (u_ref, mu_ref, wwa_ref, w0_ref, a0_ref, gup_ref, kk_ref, ka_ref, rk_ref,
         lnw_ref, lnb_ref, y_ref, vout_ref, state_sc, carry_sc) = refs
    T = SEQ_TILE
    C = A_CHUNK
    N = A_HEAD_DIM
    W = A_WIDTH
    NB = A_BATCH
    R = NB * T
    NCH = R // C
    j = pl.program_id(1)

    @pl.when((pl.program_id(0) == 0) & (j == 0))
    def _():
        state_sc[...] = jnp.zeros_like(state_sc)
        for ref in handoff:
            ref[...] = jnp.zeros_like(ref)

    @pl.when(j == 0)
    def _():
        carry_sc[...] = jnp.zeros_like(carry_sc)

    rslot = 1 - wslot

    u = u_ref[...].reshape(R, A_COLS)
    row = lax.broadcasted_iota(jnp.int32, (R, 1), 0)
    prev = pltpu.roll(u, 1, axis=0)
    for b in range(NB):
        prev = jnp.where(row == b * T, carry_sc[b:b + 1, :], prev)
        carry_sc[b:b + 1, :] = u[(b + 1) * T - 1:(b + 1) * T, :]
    ones_bd = _head_ones(W, N)
    lane = lax.broadcasted_iota(jnp.int32, (1, 128), 1)

    def token_stage(b):
        rows = slice(b * T, (b + 1) * T)
        x = u[rows] + (prev[rows] - u[rows]) * mu_ref[...]
        r = x[:, 0:W]
        k = x[:, W:2 * W]
        v = x[:, 2 * W:3 * W]
        slab = x[:, 3 * W:3 * W + 128]
        gd = x[:, 3 * W + 128:]
        slab = jnp.where(lane < 64, jnp.tanh(slab), slab)
        wa = _dot(slab, wwa_ref[...])
        logw = -math.exp(-0.5) * _sigmoid(w0_ref[...] + wa[:, :W])
        a = _sigmoid(a0_ref[...] + wa[:, W:])
        g = _dot(_sigmoid(gd), gup_ref[...])
        if has_vres:
            mix = _sigmoid(v0_ref[...] + _dot(_dot(v, vd_ref[...]), vu_ref[...]))
            v = v + (vf_ref[b] - v) * mix
        else:
            vout_ref[b] = v
        kkr = k * kk_ref[...]
        kk = kkr / jnp.maximum(jnp.sqrt(_head_sum(kkr * kkr, ones_bd)), 1e-12)
        k2 = k * (1.0 + (a - 1.0) * ka_ref[...])
        return dict(r=r, v=v, kk=kk, k2=k2, bhat=kk * a, logw=logw, g=g)

    ri = lax.broadcasted_iota(jnp.int32, (W, W), 0)
    ci = lax.broadcasted_iota(jnp.int32, (W, W), 1)
    same_head = (ri // C) == (ci // C)
    tw = lax.broadcasted_iota(jnp.int32, (C, W), 0)
    sw = lax.broadcasted_iota(jnp.int32, (C, W), 1) % C
    tri_s = lax.broadcasted_iota(jnp.int32, (C, 4 * C), 1)
    tri = jnp.where((tri_s % C <= lax.broadcasted_iota(jnp.int32, (C, 4 * C), 0)) & (tri_s < 3 * C),
                    1.0, 0.0).astype(BF16)

    def stack(t):
        return jnp.concatenate([t.astype(BF16)] * A_HEADS, axis=0) * ones_bd

    def chunk_inputs(tok, i):
        sl = slice(i * C, (i + 1) * C)
        lw = tok["logw"][sl]
        bhat, k2 = tok["bhat"][sl], tok["k2"][sl]
        cl = _const_dot(tri, lw)
        cl_last = cl[C - 1:C, :]
        e_neg = jnp.exp(-cl)
        e_last = jnp.exp(cl_last - cl)
        at = -tok["kk"][sl] * jnp.exp(cl - lw)
        rt = tok["r"][sl] * jnp.exp(cl)
        v_s = stack(tok["v"][sl])
        aa = _dot_nt(jnp.concatenate([at, rt], axis=0),
                     jnp.concatenate([stack(bhat * e_neg), stack(k2 * e_neg)], axis=0))
        a_ab = stack(jnp.where(sw < tw, aa[:C, :W], 0.0))
        a_ak = stack(jnp.where(sw < tw, aa[:C, W:], 0.0))
        a_rb = jnp.where(sw <= tw, aa[C:, :W], 0.0).astype(BF16)
        a_rk = jnp.where(sw <= tw, aa[C:, W:], 0.0).astype(BF16)
        rhs = (stack(at), _mm(a_ak, v_s).astype(BF16))
        btil = stack(bhat * e_last)
        ktil = stack(k2 * e_last)
        return a_ab, rhs, a_rb, a_rk, v_s, btil, ktil, rt, jnp.exp(cl_last)

    def chunk_terms(inputs, tinv):
        _, rhs, a_rb, a_rk, v_s, btil, ktil, rt, gam = inputs
        w_t = _mm(tinv, rhs[0]).astype(BF16)
        u_t = _mm(tinv, rhs[1]).astype(BF16)
        g_col = jnp.sum(jnp.where(ri == ci, gam, 0.0), axis=1, keepdims=True)
        m_mat = (g_col, _mm_tn(btil, w_t).astype(BF16))
        n_mat = _mm_tn(btil, u_t) + _mm_tn(ktil, v_s)
        q_mat = rt + _mm(a_rb, w_t)
        p_mat = _mm(a_rb, u_t) + _mm(a_rk, v_s)
        return m_mat, n_mat, q_mat, p_mat

    def finish(tok, outs):
        o = jnp.concatenate(outs, axis=0)
        mean = _head_sum(o, ones_bd) * (1.0 / N)
        d = o - mean
        var = _head_sum(d * d, ones_bd) * (1.0 / N)
        o = d * lax.rsqrt(var + A_GN_EPS) * lnw_ref[...] + lnb_ref[...]
        bonus = _head_sum(tok["r"] * tok["k2"] * rk_ref[...], ones_bd) * tok["v"]
        y_ref[...] = ((o + bonus) * tok["g"]).reshape(NB, T, W)

    per = T // C

    def finish_previous_tile():
        box = {}
        yield from _unit_lower_inverses([aab_sc[rslot, i] for i in range(NCH)], ri, ci, box)
        terms = []
        for i in range(NCH):
            held = (None, (rhs_sc[rslot, i, 0], rhs_sc[rslot, i, 1]), arb_sc[rslot, i, 0], arb_sc[rslot, i, 1],
                    bkv_sc[rslot, i, 2], bkv_sc[rslot, i, 0], bkv_sc[rslot, i, 1],
                    rt_sc[rslot, i], gam_sc[rslot, i, 0:1])
            terms.append(chunk_terms(held, box["tinv"][i]))
            yield
        sts = [state_sc[b] for b in range(NB)]
        outs = [None] * NCH
        for c in range(per):
            for b in range(NB):
                m_mat, n_mat, q_mat, p_mat = terms[b * per + c]
                sb = sts[b].astype(BF16)
                outs[b * per + c] = _mm(q_mat.astype(BF16), sb) + p_mat
                sts[b] = m_mat[0] * sts[b] + _mm(m_mat[1], sb) + n_mat
            yield
        for b in range(NB):
            state_sc[b] = sts[b]
        finish(dict(r=tok_sc[rslot, 0], k2=tok_sc[rslot, 1], v=tok_sc[rslot, 2],
                    g=tok_sc[rslot, 3]), outs)

    def prepare_this_tile():
        for b in range(NB):
            tok = token_stage(b)
            for n, name in enumerate(("r", "k2", "v", "g")):
                tok_sc[wslot, n, b * T:(b + 1) * T] = tok[name]
            yield
            for c in range(per):
                a_ab, rhs, a_rb, a_rk, v_s, btil, ktil, rt, gam = chunk_inputs(tok, c)
                i = b * per + c
                aab_sc[wslot, i] = a_ab
                rhs_sc[wslot, i, 0] = rhs[0]
                rhs_sc[wslot, i, 1] = rhs[1]
                arb_sc[wslot, i, 0] = a_rb
                arb_sc[wslot, i, 1] = a_rk
                bkv_sc[wslot, i, 0] = btil
                bkv_sc[wslot, i, 1] = ktil
                bkv_sc[wslot, i, 2] = v_s
                rt_sc[wslot, i] = rt
                gam_sc[wslot, i, 0:1] = gam
                yield

    _interleave(finish_previous_tile(), prepare_this_tile())

    @pl.when(j == 0)
    def _():
        state_sc[...] = jnp.zeros_like(state_sc)


def _rwkv(proj3, v_first, p, *, has_vres):
    bsz, lp, _ = proj3.shape
    T = SEQ_TILE
    NB = A_BATCH
    assert bsz % NB == 0
    nt = lp // T
    NCH = NB * T // A_CHUNK
    W = A_WIDTH
    in_spec = lambda w, col: pl.BlockSpec((NB, T, w), lambda b, j: (b, jnp.minimum(j, nt - 1), col))
    y_spec = pl.BlockSpec((NB, T, W), lambda b, j: (b, jnp.maximum(j - 1, 0), 0))
    v_spec = pl.BlockSpec((NB, T, W), lambda b, j: (b, j, 0))
    full = lambda arr: pl.BlockSpec(arr.shape, lambda b, j: (0,) * arr.ndim)
    args = [proj3]
    specs = [in_spec(A_COLS, OFF_A // A_COLS)]
    if has_vres:
        args.append(v_first)
        specs.append(in_spec(W, 0))
    names = ["mu", "wwa", "w0", "a0", "gup", "kk", "ka", "rk", "lnw", "lnb"]
    if has_vres:
        names += ["vd", "vu", "v0"]
    for n in names:
        args.append(p[n])
        specs.append(full(p[n]))
    y_shape = jax.ShapeDtypeStruct((bsz, lp, W), F32)
    v_shape = jax.ShapeDtypeStruct((bsz, lp + T, W), F32)
    out_shape = y_shape if has_vres else (y_shape, v_shape)
    out_specs = y_spec if has_vres else (y_spec, v_spec)
    return pl.pallas_call(
        functools.partial(_rwkv_kernel, has_vres=has_vres),
        grid=(bsz // NB, nt + 1),
        in_specs=specs,
        out_specs=out_specs,
        out_shape=out_shape,
        scratch_shapes=[pltpu.VMEM((NB, W, W), F32),
                        pltpu.VMEM((8, A_COLS), F32),
                        pltpu.VMEM((2, 4, NB * T, W), F32),
                        pltpu.VMEM((2, NCH, W, W), BF16),
                        pltpu.VMEM((2, NCH, 2, W, W), BF16),
                        pltpu.VMEM((2, NCH, 3, W, W), BF16),
                        pltpu.VMEM((2, NCH, 2, A_CHUNK, W), BF16),
                        pltpu.VMEM((2, NCH, A_CHUNK, W), F32),
                        pltpu.VMEM((2, NCH, 8, W), F32)],
        compiler_params=pltpu.CompilerParams(dimension_semantics=("arbitrary", "arbitrary"),
                                             vmem_limit_bytes=VMEM_LIMIT),
        name="rwkv7",
    )(*args)


def _pool_tile(u, buf, tile_in_seq, mix, scale, *, pad):
    rows = u.shape[0]

    @pl.when(tile_in_seq == 0)
    def _():
        buf[0:B_HALO, :] = jnp.zeros((B_HALO, B_WIDTH), F32)

    buf[B_HALO:B_HALO + rows, :] = u
    lane_group = lax.broadcasted_iota(jnp.int32, (1, B_WIDTH), 1) // B_GROUP_DIM
    acc = u
    win = jnp.zeros_like(u)
    for s in range(1, max(B_WINDOWS)):
        acc = acc + buf[B_HALO - s:B_HALO - s + rows, :]
        if s + 1 in B_WINDOWS:
            win = jnp.where(lane_group == B_WINDOWS.index(s + 1), acc, win)
    wlane = jnp.zeros((1, B_WIDTH), jnp.int32)
    for gi, w in enumerate(B_WINDOWS):
        wlane = jnp.where(lane_group == gi, w, wlane)
    t_real = tile_in_seq * rows + lax.broadcasted_iota(jnp.int32, (rows, 1), 0) - pad
    cnt = jnp.minimum(jnp.maximum(t_real + 1, 1), wlane).astype(F32)
    pooled = win / cnt - u
    buf[0:B_HALO, :] = u[rows - B_HALO:, :]
    return _dot(pooled, mix) * scale


def _swa_constants():
    T = SEQ_TILE
    dist = T + np.arange(T)[:, None] - np.arange(2 * T)[None, :]
    band = (dist >= 0) & (dist < T)
    bias = np.empty((C_KV_HEADS, C_GROUP * T, 2 * T), np.float32)
    for hq in range(C_HEADS):
        slope = np.float32(2.0 ** (-8.0 * (hq + 1) / C_HEADS))
        rows = slice((hq % C_GROUP) * T, (hq % C_GROUP + 1) * T)
        bias[hq // C_GROUP, rows] = np.where(band, -slope * dist.astype(np.float32), MASK_VALUE)
    return jnp.asarray(bias)


def _swa_tile(y_ref, blk, kv_prev, sinks_ref, bias_ref, j, *, pad):
    T = SEQ_TILE
    G = C_GROUP
    kv = blk[:, C_WIDTH:]
    kw = jnp.concatenate([kv_prev[:, :C_KV_WIDTH], kv[:, :C_KV_WIDTH]], axis=0)
    vw = jnp.concatenate([kv_prev[:, C_KV_WIDTH:], kv[:, C_KV_WIDTH:]], axis=0)
    lane = lax.broadcasted_iota(jnp.int32, (1, 2 * C_HEAD_DIM), 1)
    low = lane < C_HEAD_DIM
    kw_sw = pltpu.roll(kw, C_HEAD_DIM, axis=1)
    vw_sw = pltpu.roll(vw, C_HEAD_DIM, axis=1)
    head = lax.broadcasted_iota(jnp.int32, (G * T, 1), 0) // T
    key_real = (j - 1) * T + lax.broadcasted_iota(jnp.int32, (1, 2 * T), 1) >= pad
    tiles = []
    for hk in range(C_KV_HEADS):
        k2 = jnp.where(low == (hk == 0), kw, kw_sw).astype(BF16)
        v2 = jnp.where(low == (hk == 0), vw, vw_sw).astype(BF16)
        q_rows = []
        sink = jnp.zeros((G * T, 1), F32)
        for gq in range(G):
            hq = hk * G + gq
            pair = blk[:, (hq // 2) * 128:(hq // 2 + 1) * 128]
            q_rows.append(jnp.where(low == (hq % 2 == 0), pair * (C_HEAD_DIM ** -0.5), 0.0))
            sink = jnp.where(head == gq, sinks_ref[hq], sink)
        s = _dot_nt(jnp.concatenate(q_rows, axis=0), k2)
        logits = jnp.where(key_real, s + bias_ref[hk], MASK_VALUE)
        m = jnp.maximum(jnp.max(logits, axis=-1, keepdims=True), sink)
        p = jnp.exp(logits - m)
        denom = jnp.sum(p, axis=-1, keepdims=True) + jnp.exp(sink - m)
        yield
        o = _dot(p, v2) / denom
        for pr in range(G // 2):
            tiles.append(jnp.where(low, o[2 * pr * T:(2 * pr + 1) * T], o[(2 * pr + 1) * T:(2 * pr + 2) * T]))
        yield
    y_ref[...] = jnp.concatenate(tiles, axis=1)


def _hgrn_tile(y_ref, u, state_sc, lb_ref, ng_ref, sums_ref, level_ref, j, *, pad):
    T = SEQ_TILE
    N = D_KEY_DIM
    lb = lb_ref[...]
    q = _silu(u[:, :D_WIDTH])
    fpre = u[:, D_WIDTH:2 * D_WIDTH]
    vin = u[:, 2 * D_WIDTH:3 * D_WIDTH]
    gate = u[:, 3 * D_WIDTH:]
    sig = _sigmoid(fpre)
    f = lb + (1.0 - lb) * sig
    t_real = j * T + lax.broadcasted_iota(jnp.int32, (T, 1), 0) - pad
    logf = jnp.where(t_real >= 0, jnp.log(jnp.maximum(f, 1e-30)), 0.0)
    kx = (1.0 - lb) * (1.0 - sig)

    sums = _mm(sums_ref[...], jnp.concatenate(_split2(logf), axis=0))
    same_head = (lax.broadcasted_iota(jnp.int32, (D_WIDTH, D_WIDTH), 0) // N
                 == lax.broadcasted_iota(jnp.int32, (D_WIDTH, D_WIDTH), 1) // N)
    head_rows = (lax.broadcasted_iota(jnp.int32, (D_HEADS * T, D_WIDTH), 0) // T
                 == lax.broadcasted_iota(jnp.int32, (D_HEADS * T, D_WIDTH), 1) // N)
    stack = lambda t: jnp.where(head_rows, jnp.concatenate([t] * D_HEADS, axis=0), 0.0)
    k_st = stack(kx).astype(BF16)
    v_st = stack(vin).astype(BF16)
    lv = level_ref[...]
    att = jnp.where(lv == 0, _dot_nt(q, k_st), 0.0)
    for l in range(D_LEVELS):
        if l == 0:
            q_l, k_l = q * jnp.exp(logf), k_st
        else:
            p_h = sums[(2 * l - 2) * T:(2 * l - 1) * T]
            x_h = sums[(2 * l - 1) * T:2 * l * T]
            q_l = q * jnp.exp(p_h)
            k_l = k_st * jnp.concatenate([jnp.exp(x_h).astype(BF16)] * D_HEADS, axis=0)
        att = jnp.where(lv == l + 1, _dot_nt(q_l, k_l), att)
        yield
    b = sums[(2 * D_LEVELS - 2) * T:(2 * D_LEVELS - 1) * T]
    b_rev = sums[(2 * D_LEVELS - 1) * T:]
    st = state_sc[...]
    o = _dot(att, v_st) + _dot_nt(q * jnp.exp(b), st)
    upd = jnp.where(same_head, _dot_tn(vin, kx * jnp.exp(b_rev)), 0.0)
    state_sc[...] = st * jnp.exp(b[T - 1:T, :]) + upd

    ones_bd = _head_ones(D_WIDTH, N)
    ms = _head_sum(o * o, ones_bd) * (1.0 / N)
    y_ref[...] = o * lax.rsqrt(ms + NORM_EPS) * ng_ref[...] * _silu(gate)


def _swa_hgrn_kernel(sinks_ref, uc_ref, ud_ref, bias_ref, lb_ref, ng_ref, sums_ref, level_ref, yc_ref, yd_ref,
                     kv_sc, state_sc, *, pad):
    j = pl.program_id(1)

    @pl.when(j == 0)
    def _():
        kv_sc[...] = jnp.zeros_like(kv_sc)
        state_sc[...] = jnp.zeros_like(state_sc)

    swa, hgrn = [], []
    for b in range(CD_BATCH):
        blk = uc_ref[b]
        swa.append(_swa_tile(yc_ref.at[b], blk, kv_sc[b], sinks_ref, bias_ref, j, pad=pad))
        kv_sc[b] = blk[:, C_WIDTH:]
        hgrn.append(_hgrn_tile(yd_ref.at[b], ud_ref[b], state_sc.at[b], lb_ref, ng_ref, sums_ref,
                               level_ref, j, pad=pad))
    for b in range(CD_BATCH):
        _interleave(swa[b], hgrn[(b + 1) % CD_BATCH])


def _hgrn_constants():
    T = SEQ_TILE
    t = np.arange(T)[:, None]
    i = np.arange(T)[None, :]
    mats = []
    for l in list(range(1, D_LEVELS)) + [D_LEVELS]:
        h = 2 ** l
        same = (t // h) == (i // h)
        mats += [same & (i <= t), same & (i > t)]
    sums = np.tile(np.concatenate(mats, axis=0).astype(np.float32), (1, 2))
    x = t ^ i
    level = np.where(i == t, 0, np.where(i < t, np.floor(np.log2(np.maximum(x, 1))).astype(np.int64) + 1, -1))
    return jnp.asarray(sums, BF16), jnp.asarray(np.tile(level, (1, D_HEADS)), jnp.int32)


def _swa_hgrn(proj3, sinks, lb, norm_g, *, pad):
    bsz, lp, _ = proj3.shape
    T = SEQ_TILE
    NB = CD_BATCH
    assert bsz % NB == 0
    sums, level = _hgrn_constants()
    bias = _swa_constants()
    const = lambda arr: pl.BlockSpec(arr.shape, lambda b, j: (0, 0))
    seq = lambda w, col: pl.BlockSpec((NB, T, w), lambda b, j: (b, j, col))
    return pl.pallas_call(
        functools.partial(_swa_hgrn_kernel, pad=pad),
        grid=(bsz // NB, lp // T),
        in_specs=[pl.BlockSpec(memory_space=pltpu.SMEM),
                  seq(C_COLS, OFF_C // C_COLS), seq(D_COLS, OFF_D // D_COLS),
                  pl.BlockSpec(bias.shape, lambda b, j: (0, 0, 0)),
                  const(lb), const(norm_g), const(sums), const(level)],
        out_specs=(seq(C_WIDTH, 0), seq(D_WIDTH, 0)),
        out_shape=(jax.ShapeDtypeStruct((bsz, lp, C_WIDTH), F32),
                   jax.ShapeDtypeStruct((bsz, lp, D_WIDTH), F32)),
        scratch_shapes=[pltpu.VMEM((NB, T, 2 * C_KV_WIDTH), F32),
                        pltpu.VMEM((NB, D_WIDTH, D_WIDTH), F32)],
        compiler_params=pltpu.CompilerParams(dimension_semantics=("arbitrary", "arbitrary"),
                                             vmem_limit_bytes=VMEM_LIMIT),
        name="swa_hgrn2",
    )(sinks, proj3, proj3, bias, lb, norm_g, sums, level)


def _merge_ffn_kernel(h_ref, gates_ref, ya_ref, ub_ref, yc_ref, yd_ref, mix_ref, scale_ref, wb_ref,
                      wo_ref, gf_ref, wu_ref, wd_ref, *rest, ff_chunk, pad, tiles_per_seq):
    fg_ref, o_ref, pool_buf = rest if len(rest) == 3 else (None,) + rest
    y_b = _pool_tile(ub_ref[...], pool_buf, pl.program_id(0) % tiles_per_seq, mix_ref[...],
                     scale_ref[...], pad=pad)
    merged = None
    row = 0
    for bi, y in enumerate((ya_ref[...], y_b, yc_ref[...], yd_ref[...])):
        w = y.shape[-1]
        part = _sigmoid(gates_ref[:, bi * D_MODEL:(bi + 1) * D_MODEL]) * jnp.dot(
            y.astype(BF16), wb_ref[row:row + w, :], preferred_element_type=F32)
        merged = part if merged is None else merged + part
        row += w
    h = h_ref[...] + jnp.dot(merged.astype(BF16), wo_ref[...], preferred_element_type=F32)
    zb = _rms(h, gf_ref[...]).astype(BF16)
    acc = h
    for c in range(D_FF // ff_chunk):
        cs = slice(c * ff_chunk, (c + 1) * ff_chunk)
        gu = jnp.dot(zb, wu_ref[:, cs], preferred_element_type=F32)
        up = jnp.dot(zb, wu_ref[:, D_FF + c * ff_chunk:D_FF + (c + 1) * ff_chunk],
                     preferred_element_type=F32)
        acc = acc + jnp.dot((_silu(gu) * up).astype(BF16), wd_ref[cs, :],
                            preferred_element_type=F32)
    o_ref[...] = acc if fg_ref is None else _rms(acc, fg_ref[...])


def _merge_ffn(h2, proj2, ya, yc, yd, mix_bd, scale, wb, wo, gf, wu, wd, final_g=None, *, pad, lp):
    m = h2.shape[0]
    tm = ROW_TILE
    assert lp % tm == 0
    rows = lambda w, col=0: pl.BlockSpec((tm, w), lambda i: (i, col))
    const = lambda arr: pl.BlockSpec(arr.shape, lambda i: (0, 0), pipeline_mode=pl.Buffered(1))
    last = () if final_g is None else (final_g,)
    return pl.pallas_call(
        functools.partial(_merge_ffn_kernel, ff_chunk=256, pad=pad, tiles_per_seq=lp // tm),
        grid=(m // tm,),
        in_specs=[rows(D_MODEL), rows(GATE_COLS), rows(A_WIDTH), rows(B_WIDTH, OFF_B // B_WIDTH),
                  rows(C_WIDTH), rows(D_WIDTH), const(mix_bd), const(scale), const(wb), const(wo),
                  const(gf), const(wu), const(wd)] + [const(g) for g in last],
        out_specs=rows(D_MODEL),
        out_shape=jax.ShapeDtypeStruct((m, D_MODEL), F32),
        scratch_shapes=[pltpu.VMEM((B_HALO + tm, B_WIDTH), F32)],
        compiler_params=pltpu.CompilerParams(dimension_semantics=("arbitrary",),
                                             vmem_limit_bytes=VMEM_LIMIT),
        name="merge_ffn",
    )(h2, proj2, ya, proj2, yc, yd, mix_bd, scale, wb, wo, gf, wu, wd, *last)


def _block_diag(blocks):
    n, r, c = blocks.shape
    out = jnp.zeros((n * r, n * c), blocks.dtype)
    for i in range(n):
        out = out.at[i * r:(i + 1) * r, i * c:(i + 1) * c].set(blocks[i])
    return out


def kernel(x, meta, norm_mix, norm_ffn, norm_final, w_in, w_branch, w_out, a_mu, a_w_up, a_w0, a_a_up, a_a0, a_g_up, a_kk, a_ka, a_rk, a_ln_w, a_ln_b, a_vres_down, a_vres_up, a_vres0, b_mix, b_scale, c_sinks, d_lower_bounds, d_norm, w_ffn_up, w_ffn_down):
    bsz, seq, _ = x.shape
    depth = w_in.shape[0]
    T = SEQ_TILE
    L = N_META + seq
    pad = (-L) % T
    lp = L + pad
    assert (pad + N_META) % T == 0 and seq % T == 0
    h = jnp.concatenate([jnp.zeros((bsz, pad, D_MODEL), F32),
                         jnp.broadcast_to(meta.astype(F32)[None], (bsz, N_META, D_MODEL)),
                         x.astype(F32)], axis=1).reshape(bsz * lp, D_MODEL)
    lb_w = jax.nn.softmax(d_lower_bounds.astype(F32), axis=0)
    lb_table = jnp.cumsum(lb_w, axis=0) - lb_w[0]
    row2 = lambda t: t.reshape(1, -1).astype(F32)
    v_first = None
    for l in range(depth):
        proj2 = _in_proj(h, row2(norm_mix[l]), w_in[l].astype(BF16), pad=pad, lp=lp)
        proj3 = proj2.reshape(bsz, lp, IN_COLS)
        wwa = jnp.zeros((128, 2 * A_WIDTH), F32)
        wwa = wwa.at[:64, :A_WIDTH].set(a_w_up[l]).at[64:, A_WIDTH:].set(a_a_up[l])
        pa = dict(mu=row2(a_mu[l]), wwa=wwa.astype(BF16), w0=row2(a_w0[l]), a0=row2(a_a0[l]),
                  gup=a_g_up[l].astype(BF16), kk=row2(a_kk[l]), ka=row2(a_ka[l]), rk=row2(a_rk[l]),
                  lnw=row2(a_ln_w[l]), lnb=row2(a_ln_b[l]))
        if l == 0:
            y_a, v_first = _rwkv(proj3, None, pa, has_vres=False)
        else:
            pa.update(vd=a_vres_down[l - 1].astype(BF16), vu=a_vres_up[l - 1].astype(BF16),
                      v0=row2(a_vres0[l - 1]))
            y_a = _rwkv(proj3, v_first, pa, has_vres=True)
        y_c, y_d = _swa_hgrn(proj3, c_sinks[l].astype(F32), row2(lb_table[l]), row2(d_norm[l]),
                             pad=pad)
        flat = lambda t: t.reshape(bsz * lp, t.shape[-1])
        h = _merge_ffn(h, proj2, flat(y_a), flat(y_c), flat(y_d),
                       _block_diag(b_mix[l]).astype(BF16), row2(b_scale[l]),
                       w_branch[l].astype(BF16), w_out[l].astype(BF16), row2(norm_ffn[l]),
                       w_ffn_up[l].astype(BF16), w_ffn_down[l].astype(BF16),
                       row2(norm_final) if l == depth - 1 else None, pad=pad, lp=lp)
    return h.reshape(bsz, lp, D_MODEL)[:, pad + N_META:]
```

```python
import functools
import math

import jax
import jax.numpy as jnp
import numpy as np
from jax import lax
from jax.experimental import pallas as pl
from jax.experimental.pallas import tpu as pltpu

F32 = jnp.float32
BF16 = jnp.bfloat16

D_MODEL = 1024
N_META = 16
NORM_EPS = 1e-6
MASK_VALUE = -1e30

A_HEADS = 4
A_HEAD_DIM = 64
A_WIDTH = 256
A_GN_EPS = 64e-5
A_COLS = 1024
A_CHUNK = 64
A_BATCH = 4
A_GROUP = 2

B_WIDTH = 256
B_GROUP_DIM = 64
B_WINDOWS = (2, 4, 8, 16)
B_HALO = 16

C_HEADS = 8
C_KV_HEADS = 2
C_GROUP = 4
C_HEAD_DIM = 64
C_WIDTH = 512
C_KV_WIDTH = 128
C_COLS = 768

D_HEADS = 4
D_KEY_DIM = 64
D_WIDTH = 256
D_COLS = 1024
CD_BATCH = 4
D_LEVELS = 7

D_FF = 2816
GATE_COLS = 4 * D_MODEL
OFF_A = GATE_COLS
OFF_B = OFF_A + A_COLS
OFF_C = OFF_B + B_WIDTH
OFF_D = OFF_C + C_COLS
IN_COLS = OFF_D + D_COLS
MIX_WIDTH = 1280

SEQ_TILE = 128
ROW_TILE = 384
VMEM_LIMIT = 56 * 1024 * 1024


def _dot(a, b):
    return jnp.dot(a.astype(BF16), b.astype(BF16), preferred_element_type=F32)


def _dot_nt(a, b):
    return lax.dot_general(a.astype(BF16), b.astype(BF16), (((1,), (1,)), ((), ())),
                           preferred_element_type=F32)


def _dot_tn(a, b):
    return lax.dot_general(a.astype(BF16), b.astype(BF16), (((0,), (0,)), ((), ())),
                           preferred_element_type=F32)


def _interleave(*gens):
    gens = list(gens)
    while gens:
        for gen in list(gens):
            if next(gen, "done") == "done":
                gens.remove(gen)


def _split2(x):
    hi = x.astype(BF16)
    lo = (x - hi.astype(F32)).astype(BF16)
    return hi, lo


def _split3(x):
    hi = x.astype(BF16)
    r = x - hi.astype(F32)
    mid = r.astype(BF16)
    lo = (r - mid.astype(F32)).astype(BF16)
    return hi, mid, lo


def _const_dot(c4, x):
    hi, mid, lo = _split3(x)
    return jnp.dot(c4, jnp.concatenate([hi, mid, lo, jnp.zeros_like(hi)], axis=0),
                   preferred_element_type=F32)


def _mm(a, b):
    return jnp.dot(a, b, preferred_element_type=F32)


def _mm_tn(a, b):
    return lax.dot_general(a, b, (((0,), (0,)), ((), ())), preferred_element_type=F32)


def _d3(a, b):
    return _mm(a[0], b[0]) + _mm(a[0], b[1]) + _mm(a[1], b[0])


def _unit_lower_inverses(mats, ri, ci, box):
    in16 = (ri // 16) == (ci // 16)
    in32 = (ri // 32) == (ci // 32)
    diag = ri == ci
    as_mask = lambda cond: jnp.where(cond, 1.0, 0.0).astype(BF16)
    plus_eye = lambda m: jnp.where(diag, 1.0, m).astype(BF16)
    m16, eye = as_mask(in16), as_mask(diag)
    xb = [a * m16 for a in mats]
    t = [xi + eye for xi in xb]
    yield
    for _ in range(3):
        x = [_mm(xi, xi) for xi in xb]
        yield
        xb = [xi.astype(BF16) for xi in x]
        t = [_mm(ti, plus_eye(xi)).astype(BF16) for ti, xi in zip(t, x)]
        yield
    n = mats[0].shape[0]
    for half, sel in ((16, as_mask(in32 & ~in16)), (32, as_mask(~in32))):
        starts = range(0, n, 2 * half)
        low = lambda m: jnp.concatenate([m[r + half:r + 2 * half] for r in starts], axis=0)
        put = lambda rows, base: jnp.concatenate(
            [p for k, r in enumerate(starts) for p in (base[r:r + half], rows[k * half:(k + 1) * half])],
            axis=0)
        lt = [_mm(low(a * sel), ti).astype(BF16) for a, ti in zip(mats, t)]
        yield
        zero = jnp.zeros_like(t[0])
        t = [put(_mm(low(ti), put(li, zero) + eye).astype(BF16), ti) for ti, li in zip(t, lt)]
        yield
    box["tinv"] = t


def _head_sum(x, ones_bd):
    return jnp.dot(x.astype(BF16), ones_bd, preferred_element_type=F32)


def _head_ones(width, seg):
    r = lax.broadcasted_iota(jnp.int32, (width, width), 0) // seg
    c = lax.broadcasted_iota(jnp.int32, (width, width), 1) // seg
    return jnp.where(r == c, 1.0, 0.0).astype(BF16)


def _sigmoid(x):
    return 1.0 / (1.0 + jnp.exp(-x))


def _silu(x):
    return x * _sigmoid(x)


def _rms(x, g):
    return x * lax.rsqrt(jnp.mean(x * x, axis=-1, keepdims=True) + NORM_EPS) * g


def _in_proj_kernel(x_ref, g_ref, w_ref, mix_ref, scale_ref, o_ref, pool_buf, *, pad, tiles_per_seq,
                    tm, tn):
    tile_in_seq = pl.program_id(0) % tiles_per_seq
    z = _rms(x_ref[...], g_ref[...])
    row = tile_in_seq * tm + lax.broadcasted_iota(jnp.int32, (tm, 1), 0)
    zb = jnp.where(row >= pad, z, 0.0).astype(BF16)
    jb = OFF_B // tn
    assert jb * tn <= OFF_B and OFF_C <= (jb + 1) * tn
    pooled = None
    for j in [jb] + [j for j in range(IN_COLS // tn) if j != jb]:
        cols = jnp.dot(zb, w_ref[:, j * tn:(j + 1) * tn], preferred_element_type=F32)
        o_ref[:, j * tn:(j + 1) * tn] = _sigmoid(cols) if (j + 1) * tn <= GATE_COLS else cols
        if j == jb:
            pooled = _pool_windows(cols[:, OFF_B - j * tn:OFF_C - j * tn], pool_buf, tile_in_seq, pad=pad)
    o_ref[:, OFF_B:OFF_C] = _dot(pooled, mix_ref[...]) * scale_ref[...]


def _in_proj(h2, g, w, mix_bd, scale, *, pad, lp):
    m = h2.shape[0]
    tm = ROW_TILE
    assert lp % tm == 0
    const = lambda arr: pl.BlockSpec(arr.shape, lambda i: (0, 0))
    return pl.pallas_call(
        functools.partial(_in_proj_kernel, pad=pad, tiles_per_seq=lp // tm, tm=tm, tn=1024),
        grid=(m // tm,),
        in_specs=[pl.BlockSpec((tm, D_MODEL), lambda i: (i, 0)),
                  const(g),
                  pl.BlockSpec((D_MODEL, IN_COLS), lambda i: (0, 0), pipeline_mode=pl.Buffered(1)),
                  const(mix_bd), const(scale)],
        out_specs=pl.BlockSpec((tm, IN_COLS), lambda i: (i, 0)),
        out_shape=jax.ShapeDtypeStruct((m, IN_COLS), F32),
        scratch_shapes=[pltpu.VMEM((B_HALO + tm, B_WIDTH), F32)],
        compiler_params=pltpu.CompilerParams(dimension_semantics=("arbitrary",),
                                             vmem_limit_bytes=VMEM_LIMIT),
        name="in_proj",
    )(h2, g, w, mix_bd, scale)


def _rwkv_kernel(*refs, has_vres):
    for parity in (0, 1):
        pl.when(pl.program_id(1) % 2 == parity)(
            functools.partial(_rwkv_step, refs, has_vres=has_vres, wslot=parity))


def _rwkv_step(refs, *, has_vres, wslot):
    handoff = refs[-7:]
    tok_sc, aab_sc, rhs_sc, bkv_sc, arb_sc, rt_sc, gam_sc = handoff
    refs = refs[:-7]
    if has_vres:
        (u_ref, vf_ref, mu_ref, wwa_ref, w0_ref, a0_ref, gup_ref, kk_ref, ka_ref, rk_ref,
         lnw_ref, lnb_ref, vd_ref, vu_ref, v0_ref, y_ref, state_sc, carry_sc) = refs
    else:
        (u_ref, mu_ref, wwa_ref, w0_ref, a0_ref, gup_ref, kk_ref, ka_ref, rk_ref,
         lnw_ref, lnb_ref, y_ref, vout_ref, state_sc, carry_sc) = refs
    T = SEQ_TILE
    C = A_CHUNK
    N = A_HEAD_DIM
    W = A_WIDTH
    NB = A_BATCH
    R = NB * T
    NCH = R // C
    j = pl.program_id(1)

    @pl.when((pl.program_id(0) == 0) & (j == 0))
    def _():
        state_sc[...] = jnp.zeros_like(state_sc)
        for ref in handoff:
            ref[...] = jnp.zeros_like(ref)

    @pl.when(j == 0)
    def _():
        carry_sc[...] = jnp.zeros_like(carry_sc)

    rslot = 1 - wslot

    u = u_ref[...].reshape(R, A_COLS)
    row = lax.broadcasted_iota(jnp.int32, (R, 1), 0)
    prev = pltpu.roll(u, 1, axis=0)
    for b in range(NB):
        prev = jnp.where(row == b * T, carry_sc[b:b + 1, :], prev)
        carry_sc[b:b + 1, :] = u[(b + 1) * T - 1:(b + 1) * T, :]
    ones_bd = _head_ones(W, N)
    lane = lax.broadcasted_iota(jnp.int32, (1, 128), 1)

    def token_stage(b):
        rows = slice(b * T, (b + 1) * T)
        x = u[rows] + (prev[rows] - u[rows]) * mu_ref[...]
        r = x[:, 0:W]
        k = x[:, W:2 * W]
        v = x[:, 2 * W:3 * W]
        slab = x[:, 3 * W:3 * W + 128]
        gd = x[:, 3 * W + 128:]
        slab = jnp.where(lane < 64, jnp.tanh(slab), slab)
        wa = _dot(slab, wwa_ref[...])
        logw = -math.exp(-0.5) * _sigmoid(w0_ref[...] + wa[:, :W])
        a = _sigmoid(a0_ref[...] + wa[:, W:])
        g = _dot(_sigmoid(gd), gup_ref[...])
        if has_vres:
            mix = _sigmoid(v0_ref[...] + _dot(_dot(v, vd_ref[...]), vu_ref[...]))
            v = v + (vf_ref[b] - v) * mix
        else:
            vout_ref[b] = v
        kkr = k * kk_ref[...]
        kk = kkr / jnp.maximum(jnp.sqrt(_head_sum(kkr * kkr, ones_bd)), 1e-12)
        k2 = k * (1.0 + (a - 1.0) * ka_ref[...])
        return dict(r=r, v=v, kk=kk, k2=k2, bhat=kk * a, logw=logw, g=g)

    ri = lax.broadcasted_iota(jnp.int32, (W, W), 0)
    ci = lax.broadcasted_iota(jnp.int32, (W, W), 1)
    same_head = (ri // C) == (ci // C)
    tw = lax.broadcasted_iota(jnp.int32, (C, W), 0)
    sw = lax.broadcasted_iota(jnp.int32, (C, W), 1) % C
    tri_s = lax.broadcasted_iota(jnp.int32, (C, 4 * C), 1)
    tri = jnp.where((tri_s % C <= lax.broadcasted_iota(jnp.int32, (C, 4 * C), 0)) & (tri_s < 3 * C),
                    1.0, 0.0).astype(BF16)

    def stack(t):
        return jnp.concatenate([t.astype(BF16)] * A_HEADS, axis=0) * ones_bd

    def chunk_inputs(tok, i):
        sl = slice(i * C, (i + 1) * C)
        lw = tok["logw"][sl]
        bhat, k2 = tok["bhat"][sl], tok["k2"][sl]
        cl = _const_dot(tri, lw)
        cl_last = cl[C - 1:C, :]
        e_neg = jnp.exp(-cl)
        e_last = jnp.exp(cl_last - cl)
        at = -tok["kk"][sl] * jnp.exp(cl - lw)
        rt = tok["r"][sl] * jnp.exp(cl)
        v_s = stack(tok["v"][sl])
        aa = _dot_nt(jnp.concatenate([at, rt], axis=0),
                     jnp.concatenate([stack(bhat * e_neg), stack(k2 * e_neg)], axis=0))
        a_ab = stack(jnp.where(sw < tw, aa[:C, :W], 0.0))
        a_ak = stack(jnp.where(sw < tw, aa[:C, W:], 0.0))
        a_rb = jnp.where(sw <= tw, aa[C:, :W], 0.0).astype(BF16)
        a_rk = jnp.where(sw <= tw, aa[C:, W:], 0.0).astype(BF16)
        rhs = (stack(at), _mm(a_ak, v_s).astype(BF16))
        btil = stack(bhat * e_last)
        ktil = stack(k2 * e_last)
        return a_ab, rhs, a_rb, a_rk, v_s, btil, ktil, rt, jnp.exp(cl_last)

    def chunk_terms(inputs, tinv):
        _, rhs, a_rb, a_rk, v_s, btil, ktil, rt, gam = inputs
        w_t = _mm(tinv, rhs[0]).astype(BF16)
        u_t = _mm(tinv, rhs[1]).astype(BF16)
        g_col = jnp.sum(jnp.where(ri == ci, gam, 0.0), axis=1, keepdims=True)
        m_mat = (g_col, _mm_tn(btil, w_t).astype(BF16))
        n_mat = _mm_tn(btil, u_t) + _mm_tn(ktil, v_s)
        q_mat = rt + _mm(a_rb, w_t)
        p_mat = _mm(a_rb, u_t) + _mm(a_rk, v_s)
        return m_mat, n_mat, q_mat, p_mat

    def finish(tok, outs):
        o = jnp.concatenate(outs, axis=0)
        mean = _head_sum(o, ones_bd) * (1.0 / N)
        d = o - mean
        var = _head_sum(d * d, ones_bd) * (1.0 / N)
        o = d * lax.rsqrt(var + A_GN_EPS) * lnw_ref[...] + lnb_ref[...]
        bonus = _head_sum(tok["r"] * tok["k2"] * rk_ref[...], ones_bd) * tok["v"]
        y_ref[...] = ((o + bonus) * tok["g"]).reshape(NB, T, W)

    per = T // C

    def finish_previous_tile():
        box = {}
        yield from _unit_lower_inverses([aab_sc[rslot, i] for i in range(NCH)], ri, ci, box)
        terms = []
        for i in range(NCH):
            held = (None, (rhs_sc[rslot, i, 0], rhs_sc[rslot, i, 1]), arb_sc[rslot, i, 0], arb_sc[rslot, i, 1],
                    bkv_sc[rslot, i, 2], bkv_sc[rslot, i, 0], bkv_sc[rslot, i, 1],
                    rt_sc[rslot, i], gam_sc[rslot, i, 0:1])
            terms.append(chunk_terms(held, box["tinv"][i]))
            yield
        sts = [state_sc[b] for b in range(NB)]
        outs = [None] * NCH
        for c in range(per):
            for b in range(NB):
                m_mat, n_mat, q_mat, p_mat = terms[b * per + c]
                sb = sts[b].astype(BF16)
                outs[b * per + c] = _mm(q_mat.astype(BF16), sb) + p_mat
                sts[b] = m_mat[0] * sts[b] + _mm(m_mat[1], sb) + n_mat
            yield
        for b in range(NB):
            state_sc[b] = sts[b]
        finish(dict(r=tok_sc[rslot, 0], k2=tok_sc[rslot, 1], v=tok_sc[rslot, 2],
                    g=tok_sc[rslot, 3]), outs)

    def prepare_this_tile():
        for b in range(NB):
            tok = token_stage(b)
            for n, name in enumerate(("r", "k2", "v", "g")):
                tok_sc[wslot, n, b * T:(b + 1) * T] = tok[name]
            yield
            for c in range(per):
                a_ab, rhs, a_rb, a_rk, v_s, btil, ktil, rt, gam = chunk_inputs(tok, c)
                i = b * per + c
                aab_sc[wslot, i] = a_ab
                rhs_sc[wslot, i, 0] = rhs[0]
                rhs_sc[wslot, i, 1] = rhs[1]
                arb_sc[wslot, i, 0] = a_rb
                arb_sc[wslot, i, 1] = a_rk
                bkv_sc[wslot, i, 0] = btil
                bkv_sc[wslot, i, 1] = ktil
                bkv_sc[wslot, i, 2] = v_s
                rt_sc[wslot, i] = rt
                gam_sc[wslot, i, 0:1] = gam
                yield

    _interleave(finish_previous_tile(), prepare_this_tile())

    @pl.when(j == 0)
    def _():
        state_sc[...] = jnp.zeros_like(state_sc)


def _rwkv(proj3, v_first, p, *, has_vres):
    bsz, lp, _ = proj3.shape
    T = SEQ_TILE
    NB = A_BATCH
    assert bsz % NB == 0
    nt = lp // T
    NCH = NB * T // A_CHUNK
    W = A_WIDTH
    in_spec = lambda w, col: pl.BlockSpec((NB, T, w), lambda b, j: (b, jnp.minimum(j, nt - 1), col))
    y_spec = pl.BlockSpec((NB, T, W), lambda b, j: (b, jnp.maximum(j - 1, 0), 0))
    v_spec = pl.BlockSpec((NB, T, W), lambda b, j: (b, j, 0))
    full = lambda arr: pl.BlockSpec(arr.shape, lambda b, j: (0,) * arr.ndim)
    args = [proj3]
    specs = [in_spec(A_COLS, OFF_A // A_COLS)]
    if has_vres:
        args.append(v_first)
        specs.append(in_spec(W, 0))
    names = ["mu", "wwa", "w0", "a0", "gup", "kk", "ka", "rk", "lnw", "lnb"]
    if has_vres:
        names += ["vd", "vu", "v0"]
    for n in names:
        args.append(p[n])
        specs.append(full(p[n]))
    y_shape = jax.ShapeDtypeStruct((bsz, lp, W), F32)
    v_shape = jax.ShapeDtypeStruct((bsz, lp + T, W), F32)
    out_shape = y_shape if has_vres else (y_shape, v_shape)
    out_specs = y_spec if has_vres else (y_spec, v_spec)
    return pl.pallas_call(
        functools.partial(_rwkv_kernel, has_vres=has_vres),
        grid=(bsz // NB, nt + 1),
        in_specs=specs,
        out_specs=out_specs,
        out_shape=out_shape,
        scratch_shapes=[pltpu.VMEM((NB, W, W), F32),
                        pltpu.VMEM((8, A_COLS), F32),
                        pltpu.VMEM((2, 4, NB * T, W), F32),
                        pltpu.VMEM((2, NCH, W, W), BF16),
                        pltpu.VMEM((2, NCH, 2, W, W), BF16),
                        pltpu.VMEM((2, NCH, 3, W, W), BF16),
                        pltpu.VMEM((2, NCH, 2, A_CHUNK, W), BF16),
                        pltpu.VMEM((2, NCH, A_CHUNK, W), F32),
                        pltpu.VMEM((2, NCH, 8, W), F32)],
        compiler_params=pltpu.CompilerParams(dimension_semantics=("arbitrary", "arbitrary"),
                                             vmem_limit_bytes=VMEM_LIMIT),
        name="rwkv7",
    )(*args)


def _pool_windows(u, buf, tile_in_seq, *, pad):
    rows = u.shape[0]

    @pl.when(tile_in_seq == 0)
    def _():
        buf[0:B_HALO, :] = jnp.zeros((B_HALO, B_WIDTH), F32)

    buf[B_HALO:B_HALO + rows, :] = u
    lane_group = lax.broadcasted_iota(jnp.int32, (1, B_WIDTH), 1) // B_GROUP_DIM
    acc = u
    win = jnp.zeros_like(u)
    for s in range(1, max(B_WINDOWS)):
        acc = acc + buf[B_HALO - s:B_HALO - s + rows, :]
        if s + 1 in B_WINDOWS:
            win = jnp.where(lane_group == B_WINDOWS.index(s + 1), acc, win)
    wlane = jnp.zeros((1, B_WIDTH), jnp.int32)
    for gi, w in enumerate(B_WINDOWS):
        wlane = jnp.where(lane_group == gi, w, wlane)
    t_real = tile_in_seq * rows + lax.broadcasted_iota(jnp.int32, (rows, 1), 0) - pad
    cnt = jnp.minimum(jnp.maximum(t_real + 1, 1), wlane).astype(F32)
    buf[0:B_HALO, :] = u[rows - B_HALO:, :]
    return win / cnt - u


def _swa_constants():
    T = SEQ_TILE
    dist = T + np.arange(T)[:, None] - np.arange(2 * T)[None, :]
    band = (dist >= 0) & (dist < T)
    bias = np.empty((C_KV_HEADS, C_GROUP * T, 2 * T), np.float32)
    for hq in range(C_HEADS):
        slope = np.float32(2.0 ** (-8.0 * (hq + 1) / C_HEADS))
        rows = slice((hq % C_GROUP) * T, (hq % C_GROUP + 1) * T)
        bias[hq // C_GROUP, rows] = np.where(band, -slope * dist.astype(np.float32), MASK_VALUE)
    return jnp.asarray(bias)


def _swa_tile(y_ref, blk, kv_prev, sinks_ref, bias_ref, j, *, pad):
    T = SEQ_TILE
    G = C_GROUP
    kv = blk[:, C_WIDTH:]
    kw = jnp.concatenate([kv_prev[:, :C_KV_WIDTH], kv[:, :C_KV_WIDTH]], axis=0)
    vw = jnp.concatenate([kv_prev[:, C_KV_WIDTH:], kv[:, C_KV_WIDTH:]], axis=0)
    lane = lax.broadcasted_iota(jnp.int32, (1, 2 * C_HEAD_DIM), 1)
    low = lane < C_HEAD_DIM
    kw_sw = pltpu.roll(kw, C_HEAD_DIM, axis=1)
    vw_sw = pltpu.roll(vw, C_HEAD_DIM, axis=1)
    head = lax.broadcasted_iota(jnp.int32, (G * T, 1), 0) // T
    key_real = (j - 1) * T + lax.broadcasted_iota(jnp.int32, (1, 2 * T), 1) >= pad
    tiles = []
    for hk in range(C_KV_HEADS):
        k2 = jnp.where(low == (hk == 0), kw, kw_sw).astype(BF16)
        v2 = jnp.where(low == (hk == 0), vw, vw_sw).astype(BF16)
        q_rows = []
        sink = jnp.zeros((G * T, 1), F32)
        for gq in range(G):
            hq = hk * G + gq
            pair = blk[:, (hq // 2) * 128:(hq // 2 + 1) * 128]
            q_rows.append(jnp.where(low == (hq % 2 == 0), pair * (C_HEAD_DIM ** -0.5), 0.0))
            sink = jnp.where(head == gq, sinks_ref[hq], sink)
        s = _dot_nt(jnp.concatenate(q_rows, axis=0), k2)
        logits = jnp.where(key_real, s + bias_ref[hk], MASK_VALUE)
        m = jnp.maximum(jnp.max(logits, axis=-1, keepdims=True), sink)
        p = jnp.exp(logits - m)
        denom = jnp.sum(p, axis=-1, keepdims=True) + jnp.exp(sink - m)
        yield
        o = _dot(p, v2) / denom
        for pr in range(G // 2):
            tiles.append(jnp.where(low, o[2 * pr * T:(2 * pr + 1) * T], o[(2 * pr + 1) * T:(2 * pr + 2) * T]))
        yield
    y_ref[...] = jnp.concatenate(tiles, axis=1)


def _hgrn_tile(y_ref, u, state_sc, lb_ref, ng_ref, sums_ref, level_ref, j, *, pad):
    T = SEQ_TILE
    N = D_KEY_DIM
    lb = lb_ref[...]
    q = _silu(u[:, :D_WIDTH])
    fpre = u[:, D_WIDTH:2 * D_WIDTH]
    vin = u[:, 2 * D_WIDTH:3 * D_WIDTH]
    gate = u[:, 3 * D_WIDTH:]
    sig = _sigmoid(fpre)
    f = lb + (1.0 - lb) * sig
    t_real = j * T + lax.broadcasted_iota(jnp.int32, (T, 1), 0) - pad
    logf = jnp.where(t_real >= 0, jnp.log(jnp.maximum(f, 1e-30)), 0.0)
    kx = (1.0 - lb) * (1.0 - sig)

    sums = _mm(sums_ref[...], jnp.concatenate(_split2(logf), axis=0))
    same_head = (lax.broadcasted_iota(jnp.int32, (D_WIDTH, D_WIDTH), 0) // N
                 == lax.broadcasted_iota(jnp.int32, (D_WIDTH, D_WIDTH), 1) // N)
    head_rows = (lax.broadcasted_iota(jnp.int32, (D_HEADS * T, D_WIDTH), 0) // T
                 == lax.broadcasted_iota(jnp.int32, (D_HEADS * T, D_WIDTH), 1) // N)
    stack = lambda t: jnp.where(head_rows, jnp.concatenate([t] * D_HEADS, axis=0), 0.0)
    k_st = stack(kx).astype(BF16)
    v_st = stack(vin).astype(BF16)
    lv = level_ref[...]
    att = jnp.where(lv == 0, _dot_nt(q, k_st), 0.0)
    for l in range(D_LEVELS):
        if l == 0:
            q_l, k_l = q * jnp.exp(logf), k_st
        else:
            p_h = sums[(2 * l - 2) * T:(2 * l - 1) * T]
            x_h = sums[(2 * l - 1) * T:2 * l * T]
            q_l = q * jnp.exp(p_h)
            k_l = k_st * jnp.concatenate([jnp.exp(x_h).astype(BF16)] * D_HEADS, axis=0)
        att = jnp.where(lv == l + 1, _dot_nt(q_l, k_l), att)
        yield
    b = sums[(2 * D_LEVELS - 2) * T:(2 * D_LEVELS - 1) * T]
    b_rev = sums[(2 * D_LEVELS - 1) * T:]
    st = state_sc[...]
    o = _dot(att, v_st) + _dot_nt(q * jnp.exp(b), st)
    upd = jnp.where(same_head, _dot_tn(vin, kx * jnp.exp(b_rev)), 0.0)
    state_sc[...] = st * jnp.exp(b[T - 1:T, :]) + upd

    ones_bd = _head_ones(D_WIDTH, N)
    ms = _head_sum(o * o, ones_bd) * (1.0 / N)
    y_ref[...] = o * lax.rsqrt(ms + NORM_EPS) * ng_ref[...] * _silu(gate)


def _swa_hgrn_kernel(sinks_ref, uc_ref, ud_ref, bias_ref, lb_ref, ng_ref, sums_ref, level_ref, yc_ref, yd_ref,
                     kv_sc, state_sc, *, pad):
    j = pl.program_id(1)

    @pl.when(j == 0)
    def _():
        kv_sc[...] = jnp.zeros_like(kv_sc)
        state_sc[...] = jnp.zeros_like(state_sc)

    swa, hgrn = [], []
    for b in range(CD_BATCH):
        blk = uc_ref[b]
        swa.append(_swa_tile(yc_ref.at[b], blk, kv_sc[b], sinks_ref, bias_ref, j, pad=pad))
        kv_sc[b] = blk[:, C_WIDTH:]
        hgrn.append(_hgrn_tile(yd_ref.at[b], ud_ref[b], state_sc.at[b], lb_ref, ng_ref, sums_ref,
                               level_ref, j, pad=pad))
    for b in range(CD_BATCH):
        _interleave(swa[b], hgrn[(b + 1) % CD_BATCH])


def _hgrn_constants():
    T = SEQ_TILE
    t = np.arange(T)[:, None]
    i = np.arange(T)[None, :]
    mats = []
    for l in list(range(1, D_LEVELS)) + [D_LEVELS]:
        h = 2 ** l
        same = (t // h) == (i // h)
        mats += [same & (i <= t), same & (i > t)]
    sums = np.tile(np.concatenate(mats, axis=0).astype(np.float32), (1, 2))
    x = t ^ i
    level = np.where(i == t, 0, np.where(i < t, np.floor(np.log2(np.maximum(x, 1))).astype(np.int64) + 1, -1))
    return jnp.asarray(sums, BF16), jnp.asarray(np.tile(level, (1, D_HEADS)), jnp.int32)


def _swa_hgrn(proj3, sinks, lb, norm_g, *, pad):
    bsz, lp, _ = proj3.shape
    T = SEQ_TILE
    NB = CD_BATCH
    assert bsz % NB == 0
    sums, level = _hgrn_constants()
    bias = _swa_constants()
    const = lambda arr: pl.BlockSpec(arr.shape, lambda b, j: (0, 0))
    seq = lambda w, col: pl.BlockSpec((NB, T, w), lambda b, j: (b, j, col))
    return pl.pallas_call(
        functools.partial(_swa_hgrn_kernel, pad=pad),
        grid=(bsz // NB, lp // T),
        in_specs=[pl.BlockSpec(memory_space=pltpu.SMEM),
                  seq(C_COLS, OFF_C // C_COLS), seq(D_COLS, OFF_D // D_COLS),
                  pl.BlockSpec(bias.shape, lambda b, j: (0, 0, 0)),
                  const(lb), const(norm_g), const(sums), const(level)],
        out_specs=(seq(C_WIDTH, 0), seq(D_WIDTH, 0)),
        out_shape=(jax.ShapeDtypeStruct((bsz, lp, C_WIDTH), F32),
                   jax.ShapeDtypeStruct((bsz, lp, D_WIDTH), F32)),
        scratch_shapes=[pltpu.VMEM((NB, T, 2 * C_KV_WIDTH), F32),
                        pltpu.VMEM((NB, D_WIDTH, D_WIDTH), F32)],
        compiler_params=pltpu.CompilerParams(dimension_semantics=("arbitrary", "arbitrary"),
                                             vmem_limit_bytes=VMEM_LIMIT),
        name="swa_hgrn2",
    )(sinks, proj3, proj3, bias, lb, norm_g, sums, level)


def _merge_ffn_kernel(h_ref, gates_ref, ya_ref, yb_ref, yc_ref, yd_ref, wb_ref, wo_ref, gf_ref, wu_ref,
                      wd_ref, *rest, ff_chunk):
    fg_ref, o_ref = rest if len(rest) == 2 else (None,) + rest
    merged = None
    row = 0
    for bi, y_ref in enumerate((ya_ref, yb_ref, yc_ref, yd_ref)):
        w = y_ref.shape[-1]
        part = gates_ref[:, bi * D_MODEL:(bi + 1) * D_MODEL] * jnp.dot(
            y_ref[...].astype(BF16), wb_ref[row:row + w, :], preferred_element_type=F32)
        merged = part if merged is None else merged + part
        row += w
    h = h_ref[...] + jnp.dot(merged.astype(BF16), wo_ref[...], preferred_element_type=F32)
    zb = _rms(h, gf_ref[...]).astype(BF16)

    def up_proj(c):
        cols = lambda base: wu_ref[:, base + c * ff_chunk:base + (c + 1) * ff_chunk]
        return (jnp.dot(zb, cols(0), preferred_element_type=F32),
                jnp.dot(zb, cols(D_FF), preferred_element_type=F32))

    acc = h
    n_chunks = D_FF // ff_chunk
    gu, up = up_proj(0)
    for c in range(n_chunks):
        nxt = up_proj(c + 1) if c + 1 < n_chunks else None
        acc = acc + jnp.dot((_silu(gu) * up).astype(BF16), wd_ref[c * ff_chunk:(c + 1) * ff_chunk, :],
                            preferred_element_type=F32)
        if nxt is not None:
            gu, up = nxt
    o_ref[...] = acc if fg_ref is None else _rms(acc, fg_ref[...])


def _merge_ffn(h2, proj2, ya, yc, yd, wb, wo, gf, wu, wd, final_g=None):
    m = h2.shape[0]
    tm = ROW_TILE
    rows = lambda w, col=0: pl.BlockSpec((tm, w), lambda i: (i, col))
    const = lambda arr: pl.BlockSpec(arr.shape, lambda i: (0, 0), pipeline_mode=pl.Buffered(1))
    last = () if final_g is None else (final_g,)
    return pl.pallas_call(
        functools.partial(_merge_ffn_kernel, ff_chunk=256),
        grid=(m // tm,),
        in_specs=[rows(D_MODEL), rows(GATE_COLS), rows(A_WIDTH), rows(B_WIDTH, OFF_B // B_WIDTH),
                  rows(C_WIDTH), rows(D_WIDTH), const(wb), const(wo), const(gf), const(wu),
                  const(wd)] + [const(g) for g in last],
        out_specs=rows(D_MODEL),
        out_shape=jax.ShapeDtypeStruct((m, D_MODEL), F32),
        compiler_params=pltpu.CompilerParams(dimension_semantics=("arbitrary",),
                                             vmem_limit_bytes=VMEM_LIMIT),
        name="merge_ffn",
    )(h2, proj2, ya, proj2, yc, yd, wb, wo, gf, wu, wd, *last)


def _block_diag(blocks):
    n, r, c = blocks.shape
    out = jnp.zeros((n * r, n * c), blocks.dtype)
    for i in range(n):
        out = out.at[i * r:(i + 1) * r, i * c:(i + 1) * c].set(blocks[i])
    return out


def kernel(x, meta, norm_mix, norm_ffn, norm_final, w_in, w_branch, w_out, a_mu, a_w_up, a_w0, a_a_up, a_a0, a_g_up, a_kk, a_ka, a_rk, a_ln_w, a_ln_b, a_vres_down, a_vres_up, a_vres0, b_mix, b_scale, c_sinks, d_lower_bounds, d_norm, w_ffn_up, w_ffn_down):
    bsz, seq, _ = x.shape
    depth = w_in.shape[0]
    T = SEQ_TILE
    L = N_META + seq
    pad = (-L) % T
    lp = L + pad
    assert (pad + N_META) % T == 0 and seq % T == 0
    h = jnp.concatenate([jnp.zeros((bsz, pad, D_MODEL), F32),
                         jnp.broadcast_to(meta.astype(F32)[None], (bsz, N_META, D_MODEL)),
                         x.astype(F32)], axis=1).reshape(bsz * lp, D_MODEL)
    lb_w = jax.nn.softmax(d_lower_bounds.astype(F32), axis=0)
    lb_table = jnp.cumsum(lb_w, axis=0) - lb_w[0]
    row2 = lambda t: t.reshape(1, -1).astype(F32)
    v_first = None
    for l in range(depth):
        proj2 = _in_proj(h, row2(norm_mix[l]), w_in[l].astype(BF16),
                         _block_diag(b_mix[l]).astype(BF16), row2(b_scale[l]), pad=pad, lp=lp)
        proj3 = proj2.reshape(bsz, lp, IN_COLS)
        wwa = jnp.zeros((128, 2 * A_WIDTH), F32)
        wwa = wwa.at[:64, :A_WIDTH].set(a_w_up[l]).at[64:, A_WIDTH:].set(a_a_up[l])
        pa = dict(mu=row2(a_mu[l]), wwa=wwa.astype(BF16), w0=row2(a_w0[l]), a0=row2(a_a0[l]),
                  gup=a_g_up[l].astype(BF16), kk=row2(a_kk[l]), ka=row2(a_ka[l]), rk=row2(a_rk[l]),
                  lnw=row2(a_ln_w[l]), lnb=row2(a_ln_b[l]))
        if l == 0:
            y_a, v_first = _rwkv(proj3, None, pa, has_vres=False)
        else:
            pa.update(vd=a_vres_down[l - 1].astype(BF16), vu=a_vres_up[l - 1].astype(BF16),
                      v0=row2(a_vres0[l - 1]))
            y_a = _rwkv(proj3, v_first, pa, has_vres=True)
        y_c, y_d = _swa_hgrn(proj3, c_sinks[l].astype(F32), row2(lb_table[l]), row2(d_norm[l]),
                             pad=pad)
        flat = lambda t: t.reshape(bsz * lp, t.shape[-1])
        h = _merge_ffn(h, proj2, flat(y_a), flat(y_c), flat(y_d),
                       w_branch[l].astype(BF16), w_out[l].astype(BF16), row2(norm_ffn[l]),
                       w_ffn_up[l].astype(BF16), w_ffn_down[l].astype(BF16),
                       row2(norm_final) if l == depth - 1 else None)
    return h.reshape(bsz, lp, D_MODEL)[:, pad + N_META:]
```

```python
import functools
import math

import jax
import jax.numpy as jnp
import numpy as np
from jax import lax
from jax.experimental import pallas as pl
from jax.experimental.pallas import tpu as pltpu

F32 = jnp.float32
BF16 = jnp.bfloat16

D_MODEL = 1024
N_META = 16
NORM_EPS = 1e-6
MASK_VALUE = -1e30

A_HEADS = 4
A_HEAD_DIM = 64
A_WIDTH = 256
A_GN_EPS = 64e-5
A_COLS = 1024
A_CHUNK = 64
A_BATCH = 4
A_GROUP = 2

B_WIDTH = 256
B_GROUP_DIM = 64
B_WINDOWS = (2, 4, 8, 16)
B_HALO = 16

C_HEADS = 8
C_KV_HEADS = 2
C_GROUP = 4
C_HEAD_DIM = 64
C_WIDTH = 512
C_KV_WIDTH = 128
C_COLS = 768

D_HEADS = 4
D_KEY_DIM = 64
D_WIDTH = 256
D_COLS = 1024
CD_BATCH = 4
D_LEVELS = 7

D_FF = 2816
GATE_COLS = 4 * D_MODEL
OFF_A = GATE_COLS
OFF_B = OFF_A + A_COLS
OFF_C = OFF_B + B_WIDTH
OFF_D = OFF_C + C_COLS
IN_COLS = OFF_D + D_COLS
MIX_WIDTH = 1280

SEQ_TILE = 128
ROW_TILE = 384
VMEM_LIMIT = 56 * 1024 * 1024


def _dot(a, b):
    return jnp.dot(a.astype(BF16), b.astype(BF16), preferred_element_type=F32)


def _dot_nt(a, b):
    return lax.dot_general(a.astype(BF16), b.astype(BF16), (((1,), (1,)), ((), ())),
                           preferred_element_type=F32)


def _dot_tn(a, b):
    return lax.dot_general(a.astype(BF16), b.astype(BF16), (((0,), (0,)), ((), ())),
                           preferred_element_type=F32)


def _interleave(*gens):
    gens = list(gens)
    while gens:
        for gen in list(gens):
            if next(gen, "done") == "done":
                gens.remove(gen)


def _split2(x):
    hi = x.astype(BF16)
    lo = (x - hi.astype(F32)).astype(BF16)
    return hi, lo


def _split3(x):
    hi = x.astype(BF16)
    r = x - hi.astype(F32)
    mid = r.astype(BF16)
    lo = (r - mid.astype(F32)).astype(BF16)
    return hi, mid, lo


def _const_dot(c4, x):
    hi, mid, lo = _split3(x)
    return jnp.dot(c4, jnp.concatenate([hi, mid, lo, jnp.zeros_like(hi)], axis=0),
                   preferred_element_type=F32)


def _mm(a, b):
    return jnp.dot(a, b, preferred_element_type=F32)


def _mm_tn(a, b):
    return lax.dot_general(a, b, (((0,), (0,)), ((), ())), preferred_element_type=F32)


def _d3(a, b):
    return _mm(a[0], b[0]) + _mm(a[0], b[1]) + _mm(a[1], b[0])


def _unit_lower_inverses(mats, ri, ci, box):
    in16 = (ri // 16) == (ci // 16)
    in32 = (ri // 32) == (ci // 32)
    diag = ri == ci
    as_mask = lambda cond: jnp.where(cond, 1.0, 0.0).astype(BF16)
    plus_eye = lambda m: jnp.where(diag, 1.0, m).astype(BF16)
    m16, eye = as_mask(in16), as_mask(diag)
    xb = [a * m16 for a in mats]
    t = [xi + eye for xi in xb]
    yield
    for _ in range(3):
        x = [_mm(xi, xi) for xi in xb]
        yield
        xb = [xi.astype(BF16) for xi in x]
        t = [_mm(ti, plus_eye(xi)).astype(BF16) for ti, xi in zip(t, x)]
        yield
    n = mats[0].shape[0]
    for half, sel in ((16, as_mask(in32 & ~in16)), (32, as_mask(~in32))):
        starts = range(0, n, 2 * half)
        low = lambda m: jnp.concatenate([m[r + half:r + 2 * half] for r in starts], axis=0)
        put = lambda rows, base: jnp.concatenate(
            [p for k, r in enumerate(starts) for p in (base[r:r + half], rows[k * half:(k + 1) * half])],
            axis=0)
        lt = [_mm(low(a * sel), ti).astype(BF16) for a, ti in zip(mats, t)]
        yield
        zero = jnp.zeros_like(t[0])
        t = [put(_mm(low(ti), put(li, zero) + eye).astype(BF16), ti) for ti, li in zip(t, lt)]
        yield
    box["tinv"] = t


def _head_sum(x, ones_bd):
    return jnp.dot(x.astype(BF16), ones_bd, preferred_element_type=F32)


def _head_ones(width, seg):
    r = lax.broadcasted_iota(jnp.int32, (width, width), 0) // seg
    c = lax.broadcasted_iota(jnp.int32, (width, width), 1) // seg
    return jnp.where(r == c, 1.0, 0.0).astype(BF16)


def _sigmoid(x):
    return 1.0 / (1.0 + jnp.exp(-x))


def _silu(x):
    return x * _sigmoid(x)


def _rms(x, g):
    return x * lax.rsqrt(jnp.mean(x * x, axis=-1, keepdims=True) + NORM_EPS) * g


def _in_proj_kernel(x_ref, g_ref, w_ref, mix_ref, scale_ref, o_ref, pool_buf, *, pad, tiles_per_seq,
                    tm, tn):
    tile_in_seq = pl.program_id(0) % tiles_per_seq
    z = _rms(x_ref[...], g_ref[...])
    row = tile_in_seq * tm + lax.broadcasted_iota(jnp.int32, (tm, 1), 0)
    zb = jnp.where(row >= pad, z, 0.0).astype(BF16)
    @pl.when(tile_in_seq == 0)
    def _():
        pool_buf[0:B_HALO, :] = jnp.zeros((B_HALO, B_WIDTH), F32)

    jb = OFF_B // tn
    assert jb * tn <= OFF_B and OFF_C <= (jb + 1) * tn
    piece = SEQ_TILE
    u_b = None
    for n, j in enumerate([jb] + [j for j in range(IN_COLS // tn) if j != jb]):
        cols = jnp.dot(zb, w_ref[:, j * tn:(j + 1) * tn], preferred_element_type=F32)
        o_ref[:, j * tn:(j + 1) * tn] = _sigmoid(cols) if (j + 1) * tn <= GATE_COLS else cols
        if j == jb:
            u_b = cols[:, OFF_B - j * tn:OFF_C - j * tn]
        if 1 <= n <= tm // piece:
            rows = slice((n - 1) * piece, n * piece)
            pooled = _pool_windows(u_b[rows], pool_buf, tile_in_seq * tm + rows.start, pad=pad)
            o_ref[rows, OFF_B:OFF_C] = _dot(pooled, mix_ref[...]) * scale_ref[...]


def _in_proj(h2, g, w, mix_bd, scale, *, pad, lp):
    m = h2.shape[0]
    tm = ROW_TILE
    assert lp % tm == 0
    const = lambda arr: pl.BlockSpec(arr.shape, lambda i: (0, 0))
    return pl.pallas_call(
        functools.partial(_in_proj_kernel, pad=pad, tiles_per_seq=lp // tm, tm=tm, tn=1024),
        grid=(m // tm,),
        in_specs=[pl.BlockSpec((tm, D_MODEL), lambda i: (i, 0)),
                  const(g),
                  pl.BlockSpec((D_MODEL, IN_COLS), lambda i: (0, 0), pipeline_mode=pl.Buffered(1)),
                  const(mix_bd), const(scale)],
        out_specs=pl.BlockSpec((tm, IN_COLS), lambda i: (i, 0)),
        out_shape=jax.ShapeDtypeStruct((m, IN_COLS), F32),
        scratch_shapes=[pltpu.VMEM((B_HALO + tm, B_WIDTH), F32)],
        compiler_params=pltpu.CompilerParams(dimension_semantics=("arbitrary",),
                                             vmem_limit_bytes=VMEM_LIMIT),
        name="in_proj",
    )(h2, g, w, mix_bd, scale)


def _rwkv_kernel(*refs, has_vres):
    for parity in (0, 1):
        pl.when(pl.program_id(1) % 2 == parity)(
            functools.partial(_rwkv_step, refs, has_vres=has_vres, wslot=parity))


def _rwkv_step(refs, *, has_vres, wslot):
    handoff = refs[-7:]
    tok_sc, aab_sc, rhs_sc, bkv_sc, arb_sc, rt_sc, gam_sc = handoff
    refs = refs[:-7]
    if has_vres:
        (u_ref, vf_ref, mu_ref, wwa_ref, w0_ref, a0_ref, gup_ref, kk_ref, ka_ref, rk_ref,
         lnw_ref, lnb_ref, vd_ref, vu_ref, v0_ref, y_ref, state_sc, carry_sc) = refs
    else:
        (u_ref, mu_ref, wwa_ref, w0_ref, a0_ref, gup_ref, kk_ref, ka_ref, rk_ref,
         lnw_ref, lnb_ref, y_ref, vout_ref, state_sc, carry_sc) = refs
    T = SEQ_TILE
    C = A_CHUNK
    N = A_HEAD_DIM
    W = A_WIDTH
    NB = A_BATCH
    R = NB * T
    NCH = R // C
    j = pl.program_id(1)

    @pl.when((pl.program_id(0) == 0) & (j == 0))
    def _():
        state_sc[...] = jnp.zeros_like(state_sc)
        for ref in handoff:
            ref[...] = jnp.zeros_like(ref)

    @pl.when(j == 0)
    def _():
        carry_sc[...] = jnp.zeros_like(carry_sc)

    rslot = 1 - wslot

    u = u_ref[...].reshape(R, A_COLS)
    row = lax.broadcasted_iota(jnp.int32, (R, 1), 0)
    prev = pltpu.roll(u, 1, axis=0)
    for b in range(NB):
        prev = jnp.where(row == b * T, carry_sc[b:b + 1, :], prev)
        carry_sc[b:b + 1, :] = u[(b + 1) * T - 1:(b + 1) * T, :]
    ones_bd = _head_ones(W, N)
    lane = lax.broadcasted_iota(jnp.int32, (1, 128), 1)

    def token_stage(b):
        rows = slice(b * T, (b + 1) * T)
        x = u[rows] + (prev[rows] - u[rows]) * mu_ref[...]
        r = x[:, 0:W]
        k = x[:, W:2 * W]
        v = x[:, 2 * W:3 * W]
        slab = x[:, 3 * W:3 * W + 128]
        gd = x[:, 3 * W + 128:]
        slab = jnp.where(lane < 64, jnp.tanh(slab), slab)
        wa = _dot(slab, wwa_ref[...])
        logw = -math.exp(-0.5) * _sigmoid(w0_ref[...] + wa[:, :W])
        a = _sigmoid(a0_ref[...] + wa[:, W:])
        g = _dot(_sigmoid(gd), gup_ref[...])
        if has_vres:
            mix = _sigmoid(v0_ref[...] + _dot(_dot(v, vd_ref[...]), vu_ref[...]))
            v = v + (vf_ref[b] - v) * mix
        else:
            vout_ref[b] = v
        kkr = k * kk_ref[...]
        kk = kkr / jnp.maximum(jnp.sqrt(_head_sum(kkr * kkr, ones_bd)), 1e-12)
        k2 = k * (1.0 + (a - 1.0) * ka_ref[...])
        return dict(r=r, v=v, kk=kk, k2=k2, bhat=kk * a, logw=logw, g=g)

    ri = lax.broadcasted_iota(jnp.int32, (W, W), 0)
    ci = lax.broadcasted_iota(jnp.int32, (W, W), 1)
    same_head = (ri // C) == (ci // C)
    tw = lax.broadcasted_iota(jnp.int32, (C, W), 0)
    sw = lax.broadcasted_iota(jnp.int32, (C, W), 1) % C
    tri_s = lax.broadcasted_iota(jnp.int32, (C, 4 * C), 1)
    tri = jnp.where((tri_s % C <= lax.broadcasted_iota(jnp.int32, (C, 4 * C), 0)) & (tri_s < 3 * C),
                    1.0, 0.0).astype(BF16)

    def stack(t):
        return jnp.concatenate([t.astype(BF16)] * A_HEADS, axis=0) * ones_bd

    def chunk_inputs(tok, i):
        sl = slice(i * C, (i + 1) * C)
        lw = tok["logw"][sl]
        bhat, k2 = tok["bhat"][sl], tok["k2"][sl]
        cl = _const_dot(tri, lw)
        cl_last = cl[C - 1:C, :]
        e_neg = jnp.exp(-cl)
        e_last = jnp.exp(cl_last - cl)
        at = -tok["kk"][sl] * jnp.exp(cl - lw)
        rt = tok["r"][sl] * jnp.exp(cl)
        v_s = stack(tok["v"][sl])
        aa = _dot_nt(jnp.concatenate([at, rt], axis=0),
                     jnp.concatenate([stack(bhat * e_neg), stack(k2 * e_neg)], axis=0))
        a_ab = stack(jnp.where(sw < tw, aa[:C, :W], 0.0))
        a_ak = stack(jnp.where(sw < tw, aa[:C, W:], 0.0))
        a_rb = jnp.where(sw <= tw, aa[C:, :W], 0.0).astype(BF16)
        a_rk = jnp.where(sw <= tw, aa[C:, W:], 0.0).astype(BF16)
        rhs = (stack(at), _mm(a_ak, v_s).astype(BF16))
        btil = stack(bhat * e_last)
        ktil = stack(k2 * e_last)
        return a_ab, rhs, a_rb, a_rk, v_s, btil, ktil, rt, jnp.exp(cl_last)

    def chunk_terms(inputs, tinv):
        _, rhs, a_rb, a_rk, v_s, btil, ktil, rt, gam = inputs
        w_t = _mm(tinv, rhs[0]).astype(BF16)
        u_t = _mm(tinv, rhs[1]).astype(BF16)
        g_col = jnp.sum(jnp.where(ri == ci, gam, 0.0), axis=1, keepdims=True)
        m_mat = (g_col, _mm_tn(btil, w_t).astype(BF16))
        n_mat = _mm_tn(btil, u_t) + _mm_tn(ktil, v_s)
        q_mat = rt + _mm(a_rb, w_t)
        p_mat = _mm(a_rb, u_t) + _mm(a_rk, v_s)
        return m_mat, n_mat, q_mat, p_mat

    def finish(tok, outs):
        o = jnp.concatenate(outs, axis=0)
        mean = _head_sum(o, ones_bd) * (1.0 / N)
        d = o - mean
        var = _head_sum(d * d, ones_bd) * (1.0 / N)
        o = d * lax.rsqrt(var + A_GN_EPS) * lnw_ref[...] + lnb_ref[...]
        bonus = _head_sum(tok["r"] * tok["k2"] * rk_ref[...], ones_bd) * tok["v"]
        y_ref[...] = ((o + bonus) * tok["g"]).reshape(NB, T, W)

    per = T // C

    def finish_previous_tile():
        box = {}
        yield from _unit_lower_inverses([aab_sc[rslot, i] for i in range(NCH)], ri, ci, box)
        terms = []
        for i in range(NCH):
            held = (None, (rhs_sc[rslot, i, 0], rhs_sc[rslot, i, 1]), arb_sc[rslot, i, 0], arb_sc[rslot, i, 1],
                    bkv_sc[rslot, i, 2], bkv_sc[rslot, i, 0], bkv_sc[rslot, i, 1],
                    rt_sc[rslot, i], gam_sc[rslot, i, 0:1])
            terms.append(chunk_terms(held, box["tinv"][i]))
            yield
        sts = [state_sc[b] for b in range(NB)]
        outs = [None] * NCH
        for c in range(per):
            for b in range(NB):
                m_mat, n_mat, q_mat, p_mat = terms[b * per + c]
                sb = sts[b].astype(BF16)
                outs[b * per + c] = _mm(q_mat.astype(BF16), sb) + p_mat
                sts[b] = m_mat[0] * sts[b] + _mm(m_mat[1], sb) + n_mat
            yield
        for b in range(NB):
            state_sc[b] = sts[b]
        finish(dict(r=tok_sc[rslot, 0], k2=tok_sc[rslot, 1], v=tok_sc[rslot, 2],
                    g=tok_sc[rslot, 3]), outs)

    def prepare_this_tile():
        for b in range(NB):
            tok = token_stage(b)
            for n, name in enumerate(("r", "k2", "v", "g")):
                tok_sc[wslot, n, b * T:(b + 1) * T] = tok[name]
            yield
            for c in range(per):
                a_ab, rhs, a_rb, a_rk, v_s, btil, ktil, rt, gam = chunk_inputs(tok, c)
                i = b * per + c
                aab_sc[wslot, i] = a_ab
                rhs_sc[wslot, i, 0] = rhs[0]
                rhs_sc[wslot, i, 1] = rhs[1]
                arb_sc[wslot, i, 0] = a_rb
                arb_sc[wslot, i, 1] = a_rk
                bkv_sc[wslot, i, 0] = btil
                bkv_sc[wslot, i, 1] = ktil
                bkv_sc[wslot, i, 2] = v_s
                rt_sc[wslot, i] = rt
                gam_sc[wslot, i, 0:1] = gam
                yield

    _interleave(finish_previous_tile(), prepare_this_tile())

    @pl.when(j == 0)
    def _():
        state_sc[...] = jnp.zeros_like(state_sc)


def _rwkv(proj3, v_first, p, *, has_vres):
    bsz, lp, _ = proj3.shape
    T = SEQ_TILE
    NB = A_BATCH
    assert bsz % NB == 0
    nt = lp // T
    NCH = NB * T // A_CHUNK
    W = A_WIDTH
    in_spec = lambda w, col: pl.BlockSpec((NB, T, w), lambda b, j: (b, jnp.minimum(j, nt - 1), col))
    y_spec = pl.BlockSpec((NB, T, W), lambda b, j: (b, jnp.maximum(j - 1, 0), 0))
    v_spec = pl.BlockSpec((NB, T, W), lambda b, j: (b, j, 0))
    full = lambda arr: pl.BlockSpec(arr.shape, lambda b, j: (0,) * arr.ndim)
    args = [proj3]
    specs = [in_spec(A_COLS, OFF_A // A_COLS)]
    if has_vres:
        args.append(v_first)
        specs.append(in_spec(W, 0))
    names = ["mu", "wwa", "w0", "a0", "gup", "kk", "ka", "rk", "lnw", "lnb"]
    if has_vres:
        names += ["vd", "vu", "v0"]
    for n in names:
        args.append(p[n])
        specs.append(full(p[n]))
    y_shape = jax.ShapeDtypeStruct((bsz, lp, W), F32)
    v_shape = jax.ShapeDtypeStruct((bsz, lp + T, W), F32)
    out_shape = y_shape if has_vres else (y_shape, v_shape)
    out_specs = y_spec if has_vres else (y_spec, v_spec)
    return pl.pallas_call(
        functools.partial(_rwkv_kernel, has_vres=has_vres),
        grid=(bsz // NB, nt + 1),
        in_specs=specs,
        out_specs=out_specs,
        out_shape=out_shape,
        scratch_shapes=[pltpu.VMEM((NB, W, W), F32),
                        pltpu.VMEM((8, A_COLS), F32),
                        pltpu.VMEM((2, 4, NB * T, W), F32),
                        pltpu.VMEM((2, NCH, W, W), BF16),
                        pltpu.VMEM((2, NCH, 2, W, W), BF16),
                        pltpu.VMEM((2, NCH, 3, W, W), BF16),
                        pltpu.VMEM((2, NCH, 2, A_CHUNK, W), BF16),
                        pltpu.VMEM((2, NCH, A_CHUNK, W), F32),
                        pltpu.VMEM((2, NCH, 8, W), F32)],
        compiler_params=pltpu.CompilerParams(dimension_semantics=("arbitrary", "arbitrary"),
                                             vmem_limit_bytes=VMEM_LIMIT),
        name="rwkv7",
    )(*args)


def _pool_windows(u, buf, row0, *, pad):
    rows = u.shape[0]
    buf[B_HALO:B_HALO + rows, :] = u
    lane_group = lax.broadcasted_iota(jnp.int32, (1, B_WIDTH), 1) // B_GROUP_DIM
    acc = u
    win = jnp.zeros_like(u)
    for s in range(1, max(B_WINDOWS)):
        acc = acc + buf[B_HALO - s:B_HALO - s + rows, :]
        if s + 1 in B_WINDOWS:
            win = jnp.where(lane_group == B_WINDOWS.index(s + 1), acc, win)
    wlane = jnp.zeros((1, B_WIDTH), jnp.int32)
    for gi, w in enumerate(B_WINDOWS):
        wlane = jnp.where(lane_group == gi, w, wlane)
    t_real = row0 + lax.broadcasted_iota(jnp.int32, (rows, 1), 0) - pad
    cnt = jnp.minimum(jnp.maximum(t_real + 1, 1), wlane).astype(F32)
    buf[0:B_HALO, :] = u[rows - B_HALO:, :]
    return win / cnt - u


def _swa_constants(pad):
    T = SEQ_TILE
    dist = T + np.arange(T)[:, None] - np.arange(2 * T)[None, :]
    band = (dist >= 0) & (dist < T)
    n = -(-pad // T) + 2
    bias = np.empty((n, C_KV_HEADS, C_GROUP * T, 2 * T), np.float32)
    for j in range(n):
        ok = band & ((j - 1) * T + np.arange(2 * T)[None, :] >= pad)
        for hq in range(C_HEADS):
            slope = np.float32(2.0 ** (-8.0 * (hq + 1) / C_HEADS))
            rows = slice((hq % C_GROUP) * T, (hq % C_GROUP + 1) * T)
            bias[j, hq // C_GROUP, rows] = np.where(ok, -slope * dist.astype(np.float32), MASK_VALUE)
    return jnp.asarray(bias)


def _swa_tile(y_ref, blk, kv_prev, sinks_ref, bias_ref, j, *, pad):
    T = SEQ_TILE
    G = C_GROUP
    kv = blk[:, C_WIDTH:]
    kw = jnp.concatenate([kv_prev[:, :C_KV_WIDTH], kv[:, :C_KV_WIDTH]], axis=0)
    vw = jnp.concatenate([kv_prev[:, C_KV_WIDTH:], kv[:, C_KV_WIDTH:]], axis=0)
    lane = lax.broadcasted_iota(jnp.int32, (1, 2 * C_HEAD_DIM), 1)
    low = lane < C_HEAD_DIM
    kw_sw = pltpu.roll(kw, C_HEAD_DIM, axis=1)
    vw_sw = pltpu.roll(vw, C_HEAD_DIM, axis=1)
    head = lax.broadcasted_iota(jnp.int32, (G * T, 1), 0) // T
    variant = jnp.minimum(j, bias_ref.shape[0] - 1)
    tiles = []
    for hk in range(C_KV_HEADS):
        k2 = jnp.where(low == (hk == 0), kw, kw_sw).astype(BF16)
        v2 = jnp.where(low == (hk == 0), vw, vw_sw).astype(BF16)
        q_rows = []
        sink = jnp.zeros((G * T, 1), F32)
        for gq in range(G):
            hq = hk * G + gq
            pair = blk[:, (hq // 2) * 128:(hq // 2 + 1) * 128]
            q_rows.append(jnp.where(low == (hq % 2 == 0), pair * (C_HEAD_DIM ** -0.5), 0.0))
            sink = jnp.where(head == gq, sinks_ref[hq], sink)
        s = _dot_nt(jnp.concatenate(q_rows, axis=0), k2)
        logits = s + bias_ref[variant, hk]
        m = jnp.maximum(jnp.max(logits, axis=-1, keepdims=True), sink)
        p = jnp.exp(logits - m)
        denom = jnp.sum(p, axis=-1, keepdims=True) + jnp.exp(sink - m)
        yield
        o = _dot(p, v2) / denom
        for pr in range(G // 2):
            tiles.append(jnp.where(low, o[2 * pr * T:(2 * pr + 1) * T], o[(2 * pr + 1) * T:(2 * pr + 2) * T]))
        yield
    y_ref[...] = jnp.concatenate(tiles, axis=1)


def _hgrn_tile(y_ref, u, state_sc, lb_ref, ng_ref, sums_ref, level_ref, j, *, pad):
    T = SEQ_TILE
    N = D_KEY_DIM
    lb = lb_ref[...]
    q = _silu(u[:, :D_WIDTH])
    fpre = u[:, D_WIDTH:2 * D_WIDTH]
    vin = u[:, 2 * D_WIDTH:3 * D_WIDTH]
    gate = u[:, 3 * D_WIDTH:]
    sig = _sigmoid(fpre)
    f = lb + (1.0 - lb) * sig
    t_real = j * T + lax.broadcasted_iota(jnp.int32, (T, 1), 0) - pad
    logf = jnp.where(t_real >= 0, jnp.log(jnp.maximum(f, 1e-30)), 0.0)
    kx = (1.0 - lb) * (1.0 - sig)

    all_sums = _mm(sums_ref[...], jnp.concatenate(_split2(logf), axis=0))
    sums = lambda i: all_sums[i * T:(i + 1) * T]
    same_head = (lax.broadcasted_iota(jnp.int32, (D_WIDTH, D_WIDTH), 0) // N
                 == lax.broadcasted_iota(jnp.int32, (D_WIDTH, D_WIDTH), 1) // N)
    head_rows = jnp.where(lax.broadcasted_iota(jnp.int32, (D_HEADS * T, D_WIDTH), 0) // T
                          == lax.broadcasted_iota(jnp.int32, (D_HEADS * T, D_WIDTH), 1) // N,
                          1.0, 0.0).astype(BF16)
    stack = lambda t: jnp.concatenate([t.astype(BF16)] * D_HEADS, axis=0) * head_rows
    k_st = stack(kx)
    v_st = stack(vin)
    lv = level_ref[...]
    att = jnp.where(lv == 0, _dot_nt(q, k_st), 0.0)
    for l in range(D_LEVELS):
        if l == 0:
            q_l, k_l = q * jnp.exp(logf), k_st
        else:
            q_l = q * jnp.exp(sums(2 * l - 2))
            k_l = k_st * jnp.concatenate([jnp.exp(sums(2 * l - 1)).astype(BF16)] * D_HEADS, axis=0)
        att = jnp.where(lv == l + 1, _dot_nt(q_l, k_l), att)
        yield
    b = sums(2 * D_LEVELS - 2)
    b_rev = sums(2 * D_LEVELS - 1)
    st = state_sc[...]
    o = _dot(att, v_st) + _dot_nt(q * jnp.exp(b), st)
    upd = jnp.where(same_head, _dot_tn(vin, kx * jnp.exp(b_rev)), 0.0)
    state_sc[...] = st * jnp.exp(b[T - 1:T, :]) + upd

    ones_bd = _head_ones(D_WIDTH, N)
    ms = _head_sum(o * o, ones_bd) * (1.0 / N)
    y_ref[...] = o * lax.rsqrt(ms + NORM_EPS) * ng_ref[...] * _silu(gate)


def _swa_hgrn_kernel(sinks_ref, uc_ref, ud_ref, bias_ref, lb_ref, ng_ref, sums_ref, level_ref, yc_ref, yd_ref,
                     kv_sc, state_sc, *, pad):
    j = pl.program_id(1)

    @pl.when(j == 0)
    def _():
        kv_sc[...] = jnp.zeros_like(kv_sc)
        state_sc[...] = jnp.zeros_like(state_sc)

    swa, hgrn = [], []
    for b in range(CD_BATCH):
        blk = uc_ref[b]
        swa.append(_swa_tile(yc_ref.at[b], blk, kv_sc[b], sinks_ref, bias_ref, j, pad=pad))
        kv_sc[b] = blk[:, C_WIDTH:]
        hgrn.append(_hgrn_tile(yd_ref.at[b], ud_ref[b], state_sc.at[b], lb_ref, ng_ref, sums_ref,
                               level_ref, j, pad=pad))
    for b in range(CD_BATCH):
        _interleave(swa[b], hgrn[(b + 1) % CD_BATCH])


def _hgrn_constants():
    T = SEQ_TILE
    t = np.arange(T)[:, None]
    i = np.arange(T)[None, :]
    mats = []
    for l in list(range(1, D_LEVELS)) + [D_LEVELS]:
        h = 2 ** l
        same = (t // h) == (i // h)
        mats += [same & (i <= t), same & (i > t)]
    sums = np.tile(np.concatenate(mats, axis=0).astype(np.float32), (1, 2))
    x = t ^ i
    level = np.where(i == t, 0, np.where(i < t, np.floor(np.log2(np.maximum(x, 1))).astype(np.int64) + 1, -1))
    return jnp.asarray(sums, BF16), jnp.asarray(np.tile(level, (1, D_HEADS)), jnp.int32)


def _swa_hgrn(proj3, sinks, lb, norm_g, *, pad):
    bsz, lp, _ = proj3.shape
    T = SEQ_TILE
    NB = CD_BATCH
    assert bsz % NB == 0
    sums, level = _hgrn_constants()
    bias = _swa_constants(pad)
    const = lambda arr: pl.BlockSpec(arr.shape, lambda b, j: (0, 0))
    seq = lambda w, col: pl.BlockSpec((NB, T, w), lambda b, j: (b, j, col))
    return pl.pallas_call(
        functools.partial(_swa_hgrn_kernel, pad=pad),
        grid=(bsz // NB, lp // T),
        in_specs=[pl.BlockSpec(memory_space=pltpu.SMEM),
                  seq(C_COLS, OFF_C // C_COLS), seq(D_COLS, OFF_D // D_COLS),
                  pl.BlockSpec(bias.shape, lambda b, j: (0, 0, 0, 0)),
                  const(lb), const(norm_g), const(sums), const(level)],
        out_specs=(seq(C_WIDTH, 0), seq(D_WIDTH, 0)),
        out_shape=(jax.ShapeDtypeStruct((bsz, lp, C_WIDTH), F32),
                   jax.ShapeDtypeStruct((bsz, lp, D_WIDTH), F32)),
        scratch_shapes=[pltpu.VMEM((NB, T, 2 * C_KV_WIDTH), F32),
                        pltpu.VMEM((NB, D_WIDTH, D_WIDTH), F32)],
        compiler_params=pltpu.CompilerParams(dimension_semantics=("arbitrary", "arbitrary"),
                                             vmem_limit_bytes=VMEM_LIMIT),
        name="swa_hgrn2",
    )(sinks, proj3, proj3, bias, lb, norm_g, sums, level)


def _merge_ffn_kernel(h_ref, gates_ref, ya_ref, yb_ref, yc_ref, yd_ref, wb_ref, wo_ref, gf_ref, wu_ref,
                      wd_ref, *rest, ff_chunk):
    fg_ref, o_ref = rest if len(rest) == 2 else (None,) + rest
    merged = None
    row = 0
    for bi, y_ref in enumerate((ya_ref, yb_ref, yc_ref, yd_ref)):
        w = y_ref.shape[-1]
        part = gates_ref[:, bi * D_MODEL:(bi + 1) * D_MODEL] * jnp.dot(
            y_ref[...].astype(BF16), wb_ref[row:row + w, :], preferred_element_type=F32)
        merged = part if merged is None else merged + part
        row += w
    h = h_ref[...] + jnp.dot(merged.astype(BF16), wo_ref[...], preferred_element_type=F32)
    zb = _rms(h, gf_ref[...]).astype(BF16)

    def up_proj(c):
        cols = lambda base: wu_ref[:, base + c * ff_chunk:base + (c + 1) * ff_chunk]
        return (jnp.dot(zb, cols(0), preferred_element_type=F32),
                jnp.dot(zb, cols(D_FF), preferred_element_type=F32))

    acc = h
    n_chunks = D_FF // ff_chunk
    gu, up = up_proj(0)
    for c in range(n_chunks):
        nxt = up_proj(c + 1) if c + 1 < n_chunks else None
        acc = acc + jnp.dot((_silu(gu) * up).astype(BF16), wd_ref[c * ff_chunk:(c + 1) * ff_chunk, :],
                            preferred_element_type=F32)
        if nxt is not None:
            gu, up = nxt
    o_ref[...] = acc if fg_ref is None else _rms(acc, fg_ref[...])


def _merge_ffn(h2, proj2, ya, yc, yd, wb, wo, gf, wu, wd, final_g=None):
    m = h2.shape[0]
    tm = ROW_TILE
    rows = lambda w, col=0: pl.BlockSpec((tm, w), lambda i: (i, col))
    const = lambda arr: pl.BlockSpec(arr.shape, lambda i: (0, 0), pipeline_mode=pl.Buffered(1))
    last = () if final_g is None else (final_g,)
    return pl.pallas_call(
        functools.partial(_merge_ffn_kernel, ff_chunk=256),
        grid=(m // tm,),
        in_specs=[rows(D_MODEL), rows(GATE_COLS), rows(A_WIDTH), rows(B_WIDTH, OFF_B // B_WIDTH),
                  rows(C_WIDTH), rows(D_WIDTH), const(wb), const(wo), const(gf), const(wu),
                  const(wd)] + [const(g) for g in last],
        out_specs=rows(D_MODEL),
        out_shape=jax.ShapeDtypeStruct((m, D_MODEL), F32),
        compiler_params=pltpu.CompilerParams(dimension_semantics=("arbitrary",),
                                             vmem_limit_bytes=VMEM_LIMIT),
        name="merge_ffn",
    )(h2, proj2, ya, proj2, yc, yd, wb, wo, gf, wu, wd, *last)


def _block_diag(blocks):
    n, r, c = blocks.shape
    out = jnp.zeros((n * r, n * c), blocks.dtype)
    for i in range(n):
        out = out.at[i * r:(i + 1) * r, i * c:(i + 1) * c].set(blocks[i])
    return out


def kernel(x, meta, norm_mix, norm_ffn, norm_final, w_in, w_branch, w_out, a_mu, a_w_up, a_w0, a_a_up, a_a0, a_g_up, a_kk, a_ka, a_rk, a_ln_w, a_ln_b, a_vres_down, a_vres_up, a_vres0, b_mix, b_scale, c_sinks, d_lower_bounds, d_norm, w_ffn_up, w_ffn_down):
    bsz, seq, _ = x.shape
    depth = w_in.shape[0]
    T = SEQ_TILE
    L = N_META + seq
    pad = (-L) % T
    lp = L + pad
    assert (pad + N_META) % T == 0 and seq % T == 0
    h = jnp.concatenate([jnp.zeros((bsz, pad, D_MODEL), F32),
                         jnp.broadcast_to(meta.astype(F32)[None], (bsz, N_META, D_MODEL)),
                         x.astype(F32)], axis=1).reshape(bsz * lp, D_MODEL)
    lb_w = jax.nn.softmax(d_lower_bounds.astype(F32), axis=0)
    lb_table = jnp.cumsum(lb_w, axis=0) - lb_w[0]
    row2 = lambda t: t.reshape(1, -1).astype(F32)
    v_first = None
    for l in range(depth):
        proj2 = _in_proj(h, row2(norm_mix[l]), w_in[l].astype(BF16),
                         _block_diag(b_mix[l]).astype(BF16), row2(b_scale[l]), pad=pad, lp=lp)
        proj3 = proj2.reshape(bsz, lp, IN_COLS)
        wwa = jnp.zeros((128, 2 * A_WIDTH), F32)
        wwa = wwa.at[:64, :A_WIDTH].set(a_w_up[l]).at[64:, A_WIDTH:].set(a_a_up[l])
        pa = dict(mu=row2(a_mu[l]), wwa=wwa.astype(BF16), w0=row2(a_w0[l]), a0=row2(a_a0[l]),
                  gup=a_g_up[l].astype(BF16), kk=row2(a_kk[l]), ka=row2(a_ka[l]), rk=row2(a_rk[l]),
                  lnw=row2(a_ln_w[l]), lnb=row2(a_ln_b[l]))
        if l == 0:
            y_a, v_first = _rwkv(proj3, None, pa, has_vres=False)
        else:
            pa.update(vd=a_vres_down[l - 1].astype(BF16), vu=a_vres_up[l - 1].astype(BF16),
                      v0=row2(a_vres0[l - 1]))
            y_a = _rwkv(proj3, v_first, pa, has_vres=True)
        y_c, y_d = _swa_hgrn(proj3, c_sinks[l].astype(F32), row2(lb_table[l]), row2(d_norm[l]),
                             pad=pad)
        flat = lambda t: t.reshape(bsz * lp, t.shape[-1])
        h = _merge_ffn(h, proj2, flat(y_a), flat(y_c), flat(y_d),
                       w_branch[l].astype(BF16), w_out[l].astype(BF16), row2(norm_ffn[l]),
                       w_ffn_up[l].astype(BF16), w_ffn_down[l].astype(BF16),
                       row2(norm_final) if l == depth - 1 else None)
    return h.reshape(bsz, lp, D_MODEL)[:, pad + N_META:]
```

```python
import functools
import math

import jax
import jax.numpy as jnp
import numpy as np
from jax import lax
from jax.experimental import pallas as pl
from jax.experimental.pallas import tpu as pltpu

F32 = jnp.float32
BF16 = jnp.bfloat16

D_MODEL = 1024
N_META = 16
NORM_EPS = 1e-6
MASK_VALUE = -1e30

A_HEADS = 4
A_HEAD_DIM = 64
A_WIDTH = 256
A_GN_EPS = 64e-5
A_COLS = 1024
A_CHUNK = 64
A_BATCH = 4
A_GROUP = 2

B_WIDTH = 256
B_GROUP_DIM = 64
B_WINDOWS = (2, 4, 8, 16)
B_HALO = 16

C_HEADS = 8
C_KV_HEADS = 2
C_GROUP = 4
C_HEAD_DIM = 64
C_WIDTH = 512
C_KV_WIDTH = 128
C_COLS = 768

D_HEADS = 4
D_KEY_DIM = 64
D_WIDTH = 256
D_COLS = 1024
CD_BATCH = 4
D_LEVELS = 7

D_FF = 2816
GATE_COLS = 4 * D_MODEL
OFF_A = GATE_COLS
OFF_B = OFF_A + A_COLS
OFF_C = OFF_B + B_WIDTH
OFF_D = OFF_C + C_COLS
IN_COLS = OFF_D + D_COLS
MIX_WIDTH = 1280

SEQ_TILE = 128
ROW_TILE = 384
VMEM_LIMIT = 56 * 1024 * 1024


def _dot(a, b):
    return jnp.dot(a.astype(BF16), b.astype(BF16), preferred_element_type=F32)


def _dot_nt(a, b):
    return lax.dot_general(a.astype(BF16), b.astype(BF16), (((1,), (1,)), ((), ())),
                           preferred_element_type=F32)


def _dot_tn(a, b):
    return lax.dot_general(a.astype(BF16), b.astype(BF16), (((0,), (0,)), ((), ())),
                           preferred_element_type=F32)


def _interleave(*gens):
    gens = list(gens)
    while gens:
        for gen in list(gens):
            if next(gen, "done") == "done":
                gens.remove(gen)


def _split2(x):
    hi = x.astype(BF16)
    lo = (x - hi.astype(F32)).astype(BF16)
    return hi, lo


def _split3(x):
    hi = x.astype(BF16)
    r = x - hi.astype(F32)
    mid = r.astype(BF16)
    lo = (r - mid.astype(F32)).astype(BF16)
    return hi, mid, lo


def _const_dot(c4, x):
    hi, mid, lo = _split3(x)
    return jnp.dot(c4, jnp.concatenate([hi, mid, lo, jnp.zeros_like(hi)], axis=0),
                   preferred_element_type=F32)


def _mm(a, b):
    return jnp.dot(a, b, preferred_element_type=F32)


def _mm_tn(a, b):
    return lax.dot_general(a, b, (((0,), (0,)), ((), ())), preferred_element_type=F32)


def _d3(a, b):
    return _mm(a[0], b[0]) + _mm(a[0], b[1]) + _mm(a[1], b[0])


def _unit_lower_inverses(mats, ri, ci, box):
    in16 = (ri // 16) == (ci // 16)
    in32 = (ri // 32) == (ci // 32)
    diag = ri == ci
    as_mask = lambda cond: jnp.where(cond, 1.0, 0.0).astype(BF16)
    plus_eye = lambda m: jnp.where(diag, 1.0, m).astype(BF16)
    m16, eye = as_mask(in16), as_mask(diag)
    xb = [a * m16 for a in mats]
    t = [xi + eye for xi in xb]
    yield
    for _ in range(3):
        x = [_mm(xi, xi) for xi in xb]
        yield
        xb = [xi.astype(BF16) for xi in x]
        t = [_mm(ti, plus_eye(xi)).astype(BF16) for ti, xi in zip(t, x)]
        yield
    n = mats[0].shape[0]
    for half, sel in ((16, as_mask(in32 & ~in16)), (32, as_mask(~in32))):
        starts = range(0, n, 2 * half)
        low = lambda m: jnp.concatenate([m[r + half:r + 2 * half] for r in starts], axis=0)
        put = lambda rows, base: jnp.concatenate(
            [p for k, r in enumerate(starts) for p in (base[r:r + half], rows[k * half:(k + 1) * half])],
            axis=0)
        lt = [_mm(low(a * sel), ti).astype(BF16) for a, ti in zip(mats, t)]
        yield
        zero = jnp.zeros_like(t[0])
        t = [put(_mm(low(ti), put(li, zero) + eye).astype(BF16), ti) for ti, li in zip(t, lt)]
        yield
    box["tinv"] = t


def _head_sum(x, ones_bd):
    return jnp.dot(x.astype(BF16), ones_bd, preferred_element_type=F32)


def _head_ones(width, seg):
    r = lax.broadcasted_iota(jnp.int32, (width, width), 0) // seg
    c = lax.broadcasted_iota(jnp.int32, (width, width), 1) // seg
    return jnp.where(r == c, 1.0, 0.0).astype(BF16)


def _sigmoid(x):
    return 1.0 / (1.0 + jnp.exp(-x))


def _silu(x):
    return x * _sigmoid(x)


def _rms(x, g):
    return x * lax.rsqrt(jnp.mean(x * x, axis=-1, keepdims=True) + NORM_EPS) * g


def _in_proj_kernel(x_ref, g_ref, w_ref, mix_ref, scale_ref, o_ref, pool_buf, *, pad, tiles_per_seq,
                    tm, tn):
    tile_in_seq = pl.program_id(0) % tiles_per_seq
    z = _rms(x_ref[...], g_ref[...])
    row = tile_in_seq * tm + lax.broadcasted_iota(jnp.int32, (tm, 1), 0)
    zb = jnp.where(row >= pad, z, 0.0).astype(BF16)
    @pl.when(tile_in_seq == 0)
    def _():
        pool_buf[0:B_HALO, :] = jnp.zeros((B_HALO, B_WIDTH), F32)

    jb = OFF_B // tn
    assert jb * tn <= OFF_B and OFF_C <= (jb + 1) * tn
    piece = SEQ_TILE
    u_b = None
    for n, j in enumerate([jb] + [j for j in range(IN_COLS // tn) if j != jb]):
        cols = jnp.dot(zb, w_ref[:, j * tn:(j + 1) * tn], preferred_element_type=F32)
        o_ref[:, j * tn:(j + 1) * tn] = _sigmoid(cols) if (j + 1) * tn <= GATE_COLS else cols
        if j == jb:
            u_b = cols[:, OFF_B - j * tn:OFF_C - j * tn]
        if 1 <= n <= tm // piece:
            rows = slice((n - 1) * piece, n * piece)
            pooled = _pool_windows(u_b[rows], pool_buf, tile_in_seq * tm + rows.start, pad=pad)
            o_ref[rows, OFF_B:OFF_C] = _dot(pooled, mix_ref[...]) * scale_ref[...]


def _in_proj(h2, g, w, mix_bd, scale, *, pad, lp):
    m = h2.shape[0]
    tm = ROW_TILE
    assert lp % tm == 0
    const = lambda arr: pl.BlockSpec(arr.shape, lambda i: (0, 0))
    return pl.pallas_call(
        functools.partial(_in_proj_kernel, pad=pad, tiles_per_seq=lp // tm, tm=tm, tn=1024),
        grid=(m // tm,),
        in_specs=[pl.BlockSpec((tm, D_MODEL), lambda i: (i, 0)),
                  const(g),
                  pl.BlockSpec((D_MODEL, IN_COLS), lambda i: (0, 0), pipeline_mode=pl.Buffered(1)),
                  const(mix_bd), const(scale)],
        out_specs=pl.BlockSpec((tm, IN_COLS), lambda i: (i, 0)),
        out_shape=jax.ShapeDtypeStruct((m, IN_COLS), F32),
        scratch_shapes=[pltpu.VMEM((B_HALO + tm, B_WIDTH), F32)],
        compiler_params=pltpu.CompilerParams(dimension_semantics=("arbitrary",),
                                             vmem_limit_bytes=VMEM_LIMIT),
        name="in_proj",
    )(h2, g, w, mix_bd, scale)


def _rwkv_kernel(*refs, has_vres):
    for parity in (0, 1):
        pl.when(pl.program_id(1) % 2 == parity)(
            functools.partial(_rwkv_step, refs, has_vres=has_vres, wslot=parity))


def _rwkv_step(refs, *, has_vres, wslot):
    handoff = refs[-7:]
    tok_sc, aab_sc, rhs_sc, bkv_sc, arb_sc, rt_sc, gam_sc = handoff
    refs = refs[:-7]
    if has_vres:
        (u_ref, vf_ref, mu_ref, wwa_ref, w0_ref, a0_ref, gup_ref, kk_ref, ka_ref, rk_ref,
         lnw_ref, lnb_ref, vd_ref, vu_ref, v0_ref, y_ref, state_sc, carry_sc) = refs
    else:
        (u_ref, mu_ref, wwa_ref, w0_ref, a0_ref, gup_ref, kk_ref, ka_ref, rk_ref,
         lnw_ref, lnb_ref, y_ref, vout_ref, state_sc, carry_sc) = refs
    T = SEQ_TILE
    C = A_CHUNK
    N = A_HEAD_DIM
    W = A_WIDTH
    NB = A_BATCH
    R = NB * T
    NCH = R // C
    j = pl.program_id(1)

    @pl.when((pl.program_id(0) == 0) & (j == 0))
    def _():
        state_sc[...] = jnp.zeros_like(state_sc)
        for ref in handoff:
            ref[...] = jnp.zeros_like(ref)

    @pl.when(j == 0)
    def _():
        carry_sc[...] = jnp.zeros_like(carry_sc)

    rslot = 1 - wslot

    u = u_ref[...].reshape(R, A_COLS)
    row = lax.broadcasted_iota(jnp.int32, (R, 1), 0)
    prev = pltpu.roll(u, 1, axis=0)
    for b in range(NB):
        prev = jnp.where(row == b * T, carry_sc[b:b + 1, :], prev)
        carry_sc[b:b + 1, :] = u[(b + 1) * T - 1:(b + 1) * T, :]
    ones_bd = _head_ones(W, N)
    lane = lax.broadcasted_iota(jnp.int32, (1, 128), 1)

    def token_stage(b):
        rows = slice(b * T, (b + 1) * T)
        x = u[rows] + (prev[rows] - u[rows]) * mu_ref[...]
        r = x[:, 0:W]
        k = x[:, W:2 * W]
        v = x[:, 2 * W:3 * W]
        slab = x[:, 3 * W:3 * W + 128]
        gd = x[:, 3 * W + 128:]
        slab = jnp.where(lane < 64, jnp.tanh(slab), slab)
        wa = _dot(slab, wwa_ref[...])
        logw = -math.exp(-0.5) * _sigmoid(w0_ref[...] + wa[:, :W])
        a = _sigmoid(a0_ref[...] + wa[:, W:])
        g = _dot(_sigmoid(gd), gup_ref[...])
        if has_vres:
            mix = _sigmoid(v0_ref[...] + _dot(_dot(v, vd_ref[...]), vu_ref[...]))
            v = v + (vf_ref[b] - v) * mix
        else:
            vout_ref[b] = v
        kkr = k * kk_ref[...]
        kk = kkr / jnp.maximum(jnp.sqrt(_head_sum(kkr * kkr, ones_bd)), 1e-12)
        k2 = k * (1.0 + (a - 1.0) * ka_ref[...])
        return dict(r=r, v=v, kk=kk, k2=k2, bhat=kk * a, logw=logw, g=g)

    ri = lax.broadcasted_iota(jnp.int32, (W, W), 0)
    ci = lax.broadcasted_iota(jnp.int32, (W, W), 1)
    same_head = (ri // C) == (ci // C)
    tw = lax.broadcasted_iota(jnp.int32, (C, W), 0)
    sw = lax.broadcasted_iota(jnp.int32, (C, W), 1) % C
    tri_s = lax.broadcasted_iota(jnp.int32, (C, 4 * C), 1)
    tri = jnp.where((tri_s % C <= lax.broadcasted_iota(jnp.int32, (C, 4 * C), 0)) & (tri_s < 3 * C),
                    1.0, 0.0).astype(BF16)

    def stack(t):
        return jnp.concatenate([t.astype(BF16)] * A_HEADS, axis=0) * ones_bd

    def chunk_inputs(tok, i):
        sl = slice(i * C, (i + 1) * C)
        lw = tok["logw"][sl]
        bhat, k2 = tok["bhat"][sl], tok["k2"][sl]
        cl = _const_dot(tri, lw)
        cl_last = cl[C - 1:C, :]
        e_neg = jnp.exp(-cl)
        e_last = jnp.exp(cl_last - cl)
        at = -tok["kk"][sl] * jnp.exp(cl - lw)
        rt = tok["r"][sl] * jnp.exp(cl)
        v_s = stack(tok["v"][sl])
        aa = _dot_nt(jnp.concatenate([at, rt], axis=0),
                     jnp.concatenate([stack(bhat * e_neg), stack(k2 * e_neg)], axis=0))
        a_ab = stack(jnp.where(sw < tw, aa[:C, :W], 0.0))
        a_ak = stack(jnp.where(sw < tw, aa[:C, W:], 0.0))
        a_rb = jnp.where(sw <= tw, aa[C:, :W], 0.0).astype(BF16)
        a_rk = jnp.where(sw <= tw, aa[C:, W:], 0.0).astype(BF16)
        rhs = (stack(at), _mm(a_ak, v_s).astype(BF16))
        btil = stack(bhat * e_last)
        ktil = stack(k2 * e_last)
        return a_ab, rhs, a_rb, a_rk, v_s, btil, ktil, rt, jnp.exp(cl_last)

    def finish(tok, outs):
        o = jnp.concatenate(outs, axis=0)
        mean = _head_sum(o, ones_bd) * (1.0 / N)
        d = o - mean
        var = _head_sum(d * d, ones_bd) * (1.0 / N)
        o = d * lax.rsqrt(var + A_GN_EPS) * lnw_ref[...] + lnb_ref[...]
        bonus = _head_sum(tok["r"] * tok["k2"] * rk_ref[...], ones_bd) * tok["v"]
        y_ref[...] = ((o + bonus) * tok["g"]).reshape(NB, T, W)

    per = T // C

    def finish_previous_tile():
        box = {}
        yield from _unit_lower_inverses([aab_sc[rslot, i] for i in range(NCH)], ri, ci, box)
        tinv, every = box["tinv"], range(NCH)
        w_t = [_mm(tinv[i], rhs_sc[rslot, i, 0]).astype(BF16) for i in every]
        yield
        u_t = [_mm(tinv[i], rhs_sc[rslot, i, 1]).astype(BF16) for i in every]
        yield
        m_e = [_mm_tn(bkv_sc[rslot, i, 0], w_t[i]).astype(BF16) for i in every]
        yield
        n_mat = [_mm_tn(bkv_sc[rslot, i, 0], u_t[i]) + _mm_tn(bkv_sc[rslot, i, 1], bkv_sc[rslot, i, 2])
                 for i in every]
        yield
        q_mat = [rt_sc[rslot, i] + _mm(arb_sc[rslot, i, 0], w_t[i]) for i in every]
        p_mat = [_mm(arb_sc[rslot, i, 0], u_t[i]) + _mm(arb_sc[rslot, i, 1], bkv_sc[rslot, i, 2])
                 for i in every]
        g_col = [jnp.sum(jnp.where(ri == ci, gam_sc[rslot, i, 0:1], 0.0), axis=1, keepdims=True)
                 for i in every]
        terms = [((g_col[i], m_e[i]), n_mat[i], q_mat[i], p_mat[i]) for i in every]
        yield
        sts = [state_sc[b] for b in range(NB)]
        outs = [None] * NCH
        for c in range(per):
            for b in range(NB):
                m_mat, n_mat, q_mat, p_mat = terms[b * per + c]
                sb = sts[b].astype(BF16)
                outs[b * per + c] = _mm(q_mat.astype(BF16), sb) + p_mat
                sts[b] = m_mat[0] * sts[b] + _mm(m_mat[1], sb) + n_mat
            yield
        for b in range(NB):
            state_sc[b] = sts[b]
        finish(dict(r=tok_sc[rslot, 0], k2=tok_sc[rslot, 1], v=tok_sc[rslot, 2],
                    g=tok_sc[rslot, 3]), outs)

    def prepare_this_tile():
        for b in range(NB):
            tok = token_stage(b)
            for n, name in enumerate(("r", "k2", "v", "g")):
                tok_sc[wslot, n, b * T:(b + 1) * T] = tok[name]
            yield
            for c in range(per):
                a_ab, rhs, a_rb, a_rk, v_s, btil, ktil, rt, gam = chunk_inputs(tok, c)
                i = b * per + c
                aab_sc[wslot, i] = a_ab
                rhs_sc[wslot, i, 0] = rhs[0]
                rhs_sc[wslot, i, 1] = rhs[1]
                arb_sc[wslot, i, 0] = a_rb
                arb_sc[wslot, i, 1] = a_rk
                bkv_sc[wslot, i, 0] = btil
                bkv_sc[wslot, i, 1] = ktil
                bkv_sc[wslot, i, 2] = v_s
                rt_sc[wslot, i] = rt
                gam_sc[wslot, i, 0:1] = gam
                yield

    _interleave(finish_previous_tile(), prepare_this_tile())

    @pl.when(j == 0)
    def _():
        state_sc[...] = jnp.zeros_like(state_sc)


def _rwkv(proj3, v_first, p, *, has_vres):
    bsz, lp, _ = proj3.shape
    T = SEQ_TILE
    NB = A_BATCH
    assert bsz % NB == 0
    nt = lp // T
    NCH = NB * T // A_CHUNK
    W = A_WIDTH
    in_spec = lambda w, col: pl.BlockSpec((NB, T, w), lambda b, j: (b, jnp.minimum(j, nt - 1), col))
    y_spec = pl.BlockSpec((NB, T, W), lambda b, j: (b, jnp.maximum(j - 1, 0), 0))
    v_spec = pl.BlockSpec((NB, T, W), lambda b, j: (b, j, 0))
    full = lambda arr: pl.BlockSpec(arr.shape, lambda b, j: (0,) * arr.ndim)
    args = [proj3]
    specs = [in_spec(A_COLS, OFF_A // A_COLS)]
    if has_vres:
        args.append(v_first)
        specs.append(in_spec(W, 0))
    names = ["mu", "wwa", "w0", "a0", "gup", "kk", "ka", "rk", "lnw", "lnb"]
    if has_vres:
        names += ["vd", "vu", "v0"]
    for n in names:
        args.append(p[n])
        specs.append(full(p[n]))
    y_shape = jax.ShapeDtypeStruct((bsz, lp, W), F32)
    v_shape = jax.ShapeDtypeStruct((bsz, lp + T, W), F32)
    out_shape = y_shape if has_vres else (y_shape, v_shape)
    out_specs = y_spec if has_vres else (y_spec, v_spec)
    return pl.pallas_call(
        functools.partial(_rwkv_kernel, has_vres=has_vres),
        grid=(bsz // NB, nt + 1),
        in_specs=specs,
        out_specs=out_specs,
        out_shape=out_shape,
        scratch_shapes=[pltpu.VMEM((NB, W, W), F32),
                        pltpu.VMEM((8, A_COLS), F32),
                        pltpu.VMEM((2, 4, NB * T, W), F32),
                        pltpu.VMEM((2, NCH, W, W), BF16),
                        pltpu.VMEM((2, NCH, 2, W, W), BF16),
                        pltpu.VMEM((2, NCH, 3, W, W), BF16),
                        pltpu.VMEM((2, NCH, 2, A_CHUNK, W), BF16),
                        pltpu.VMEM((2, NCH, A_CHUNK, W), F32),
                        pltpu.VMEM((2, NCH, 8, W), F32)],
        compiler_params=pltpu.CompilerParams(dimension_semantics=("arbitrary", "arbitrary"),
                                             vmem_limit_bytes=VMEM_LIMIT),
        name="rwkv7",
    )(*args)


def _pool_windows(u, buf, row0, *, pad):
    rows = u.shape[0]
    buf[B_HALO:B_HALO + rows, :] = u
    lane_group = lax.broadcasted_iota(jnp.int32, (1, B_WIDTH), 1) // B_GROUP_DIM
    acc = u
    win = jnp.zeros_like(u)
    for s in range(1, max(B_WINDOWS)):
        acc = acc + buf[B_HALO - s:B_HALO - s + rows, :]
        if s + 1 in B_WINDOWS:
            win = jnp.where(lane_group == B_WINDOWS.index(s + 1), acc, win)
    wlane = jnp.zeros((1, B_WIDTH), jnp.int32)
    for gi, w in enumerate(B_WINDOWS):
        wlane = jnp.where(lane_group == gi, w, wlane)
    t_real = row0 + lax.broadcasted_iota(jnp.int32, (rows, 1), 0) - pad
    cnt = jnp.minimum(jnp.maximum(t_real + 1, 1), wlane).astype(F32)
    buf[0:B_HALO, :] = u[rows - B_HALO:, :]
    return win / cnt - u


def _swa_constants(pad):
    T = SEQ_TILE
    dist = T + np.arange(T)[:, None] - np.arange(2 * T)[None, :]
    band = (dist >= 0) & (dist < T)
    n = -(-pad // T) + 2
    bias = np.empty((n, C_KV_HEADS, C_GROUP * T, 2 * T), np.float32)
    for j in range(n):
        ok = band & ((j - 1) * T + np.arange(2 * T)[None, :] >= pad)
        for hq in range(C_HEADS):
            slope = np.float32(2.0 ** (-8.0 * (hq + 1) / C_HEADS))
            rows = slice((hq % C_GROUP) * T, (hq % C_GROUP + 1) * T)
            bias[j, hq // C_GROUP, rows] = np.where(ok, -slope * dist.astype(np.float32), MASK_VALUE)
    return jnp.asarray(bias)


def _swa_tile(y_ref, blk, kv_prev, sinks_ref, bias_ref, j, *, pad):
    T = SEQ_TILE
    G = C_GROUP
    kv = blk[:, C_WIDTH:]
    kw = jnp.concatenate([kv_prev[:, :C_KV_WIDTH], kv[:, :C_KV_WIDTH]], axis=0)
    vw = jnp.concatenate([kv_prev[:, C_KV_WIDTH:], kv[:, C_KV_WIDTH:]], axis=0)
    lane = lax.broadcasted_iota(jnp.int32, (1, 2 * C_HEAD_DIM), 1)
    low = lane < C_HEAD_DIM
    kw_sw = pltpu.roll(kw, C_HEAD_DIM, axis=1)
    vw_sw = pltpu.roll(vw, C_HEAD_DIM, axis=1)
    head = lax.broadcasted_iota(jnp.int32, (G * T, 1), 0) // T
    variant = jnp.minimum(j, bias_ref.shape[0] - 1)
    tiles = []
    for hk in range(C_KV_HEADS):
        k2 = jnp.where(low == (hk == 0), kw, kw_sw).astype(BF16)
        v2 = jnp.where(low == (hk == 0), vw, vw_sw).astype(BF16)
        q_rows = []
        sink = jnp.zeros((G * T, 1), F32)
        for gq in range(G):
            hq = hk * G + gq
            pair = blk[:, (hq // 2) * 128:(hq // 2 + 1) * 128]
            q_rows.append(jnp.where(low == (hq % 2 == 0), pair * (C_HEAD_DIM ** -0.5), 0.0))
            sink = jnp.where(head == gq, sinks_ref[hq], sink)
        s = _dot_nt(jnp.concatenate(q_rows, axis=0), k2)
        logits = s + bias_ref[variant, hk]
        m = jnp.maximum(jnp.max(logits, axis=-1, keepdims=True), sink)
        p = jnp.exp(logits - m)
        denom = jnp.sum(p, axis=-1, keepdims=True) + jnp.exp(sink - m)
        yield
        o = _dot(p, v2) / denom
        for pr in range(G // 2):
            tiles.append(jnp.where(low, o[2 * pr * T:(2 * pr + 1) * T], o[(2 * pr + 1) * T:(2 * pr + 2) * T]))
        yield
    y_ref[...] = jnp.concatenate(tiles, axis=1)


def _hgrn_tile(y_ref, u, state_sc, lb_ref, ng_ref, sums_ref, level_ref, j, *, pad):
    T = SEQ_TILE
    N = D_KEY_DIM
    lb = lb_ref[...]
    q = _silu(u[:, :D_WIDTH])
    fpre = u[:, D_WIDTH:2 * D_WIDTH]
    vin = u[:, 2 * D_WIDTH:3 * D_WIDTH]
    gate = u[:, 3 * D_WIDTH:]
    sig = _sigmoid(fpre)
    f = lb + (1.0 - lb) * sig
    t_real = j * T + lax.broadcasted_iota(jnp.int32, (T, 1), 0) - pad
    logf = jnp.where(t_real >= 0, jnp.log(jnp.maximum(f, 1e-30)), 0.0)
    kx = (1.0 - lb) * (1.0 - sig)

    all_sums = _mm(sums_ref[...], jnp.concatenate(_split2(logf), axis=0))
    sums = lambda i: all_sums[i * T:(i + 1) * T]
    same_head = (lax.broadcasted_iota(jnp.int32, (D_WIDTH, D_WIDTH), 0) // N
                 == lax.broadcasted_iota(jnp.int32, (D_WIDTH, D_WIDTH), 1) // N)
    head_rows = jnp.where(lax.broadcasted_iota(jnp.int32, (D_HEADS * T, D_WIDTH), 0) // T
                          == lax.broadcasted_iota(jnp.int32, (D_HEADS * T, D_WIDTH), 1) // N,
                          1.0, 0.0).astype(BF16)
    stack = lambda t: jnp.concatenate([t.astype(BF16)] * D_HEADS, axis=0) * head_rows
    k_st = stack(kx)
    v_st = stack(vin)
    lv = level_ref[...]
    att = jnp.where(lv == 0, _dot_nt(q, k_st), 0.0)
    for l in range(D_LEVELS):
        if l == 0:
            q_l, k_l = q * jnp.exp(logf), k_st
        else:
            q_l = q * jnp.exp(sums(2 * l - 2))
            k_l = k_st * jnp.concatenate([jnp.exp(sums(2 * l - 1)).astype(BF16)] * D_HEADS, axis=0)
        att = jnp.where(lv == l + 1, _dot_nt(q_l, k_l), att)
        yield
    b = sums(2 * D_LEVELS - 2)
    b_rev = sums(2 * D_LEVELS - 1)
    st = state_sc[...]
    o = _dot(att, v_st) + _dot_nt(q * jnp.exp(b), st)
    upd = jnp.where(same_head, _dot_tn(vin, kx * jnp.exp(b_rev)), 0.0)
    state_sc[...] = st * jnp.exp(b[T - 1:T, :]) + upd

    ones_bd = _head_ones(D_WIDTH, N)
    ms = _head_sum(o * o, ones_bd) * (1.0 / N)
    y_ref[...] = o * lax.rsqrt(ms + NORM_EPS) * ng_ref[...] * _silu(gate)


def _swa_hgrn_kernel(sinks_ref, uc_ref, ud_ref, bias_ref, lb_ref, ng_ref, sums_ref, level_ref, yc_ref, yd_ref,
                     kv_sc, state_sc, *, pad):
    j = pl.program_id(1)

    @pl.when(j == 0)
    def _():
        kv_sc[...] = jnp.zeros_like(kv_sc)
        state_sc[...] = jnp.zeros_like(state_sc)

    swa, hgrn = [], []
    for b in range(CD_BATCH):
        blk = uc_ref[b]
        swa.append(_swa_tile(yc_ref.at[b], blk, kv_sc[b], sinks_ref, bias_ref, j, pad=pad))
        kv_sc[b] = blk[:, C_WIDTH:]
        hgrn.append(_hgrn_tile(yd_ref.at[b], ud_ref[b], state_sc.at[b], lb_ref, ng_ref, sums_ref,
                               level_ref, j, pad=pad))
    for b in range(CD_BATCH):
        _interleave(swa[b], hgrn[(b + 1) % CD_BATCH])


def _hgrn_constants():
    T = SEQ_TILE
    t = np.arange(T)[:, None]
    i = np.arange(T)[None, :]
    mats = []
    for l in list(range(1, D_LEVELS)) + [D_LEVELS]:
        h = 2 ** l
        same = (t // h) == (i // h)
        mats += [same & (i <= t), same & (i > t)]
    sums = np.tile(np.concatenate(mats, axis=0).astype(np.float32), (1, 2))
    x = t ^ i
    level = np.where(i == t, 0, np.where(i < t, np.floor(np.log2(np.maximum(x, 1))).astype(np.int64) + 1, -1))
    return jnp.asarray(sums, BF16), jnp.asarray(np.tile(level, (1, D_HEADS)), jnp.int32)


def _swa_hgrn(proj3, sinks, lb, norm_g, *, pad):
    bsz, lp, _ = proj3.shape
    T = SEQ_TILE
    NB = CD_BATCH
    assert bsz % NB == 0
    sums, level = _hgrn_constants()
    bias = _swa_constants(pad)
    const = lambda arr: pl.BlockSpec(arr.shape, lambda b, j: (0, 0))
    seq = lambda w, col: pl.BlockSpec((NB, T, w), lambda b, j: (b, j, col))
    return pl.pallas_call(
        functools.partial(_swa_hgrn_kernel, pad=pad),
        grid=(bsz // NB, lp // T),
        in_specs=[pl.BlockSpec(memory_space=pltpu.SMEM),
                  seq(C_COLS, OFF_C // C_COLS), seq(D_COLS, OFF_D // D_COLS),
                  pl.BlockSpec(bias.shape, lambda b, j: (0, 0, 0, 0)),
                  const(lb), const(norm_g), const(sums), const(level)],
        out_specs=(seq(C_WIDTH, 0), seq(D_WIDTH, 0)),
        out_shape=(jax.ShapeDtypeStruct((bsz, lp, C_WIDTH), F32),
                   jax.ShapeDtypeStruct((bsz, lp, D_WIDTH), F32)),
        scratch_shapes=[pltpu.VMEM((NB, T, 2 * C_KV_WIDTH), F32),
                        pltpu.VMEM((NB, D_WIDTH, D_WIDTH), F32)],
        compiler_params=pltpu.CompilerParams(dimension_semantics=("arbitrary", "arbitrary"),
                                             vmem_limit_bytes=VMEM_LIMIT),
        name="swa_hgrn2",
    )(sinks, proj3, proj3, bias, lb, norm_g, sums, level)


def _merge_ffn_kernel(h_ref, gates_ref, ya_ref, yb_ref, yc_ref, yd_ref, wb_ref, wo_ref, gf_ref, wu_ref,
                      wd_ref, *rest, ff_chunk):
    fg_ref, o_ref = rest if len(rest) == 2 else (None,) + rest
    merged = None
    row = 0
    for bi, y_ref in enumerate((ya_ref, yb_ref, yc_ref, yd_ref)):
        w = y_ref.shape[-1]
        part = gates_ref[:, bi * D_MODEL:(bi + 1) * D_MODEL] * jnp.dot(
            y_ref[...].astype(BF16), wb_ref[row:row + w, :], preferred_element_type=F32)
        merged = part if merged is None else merged + part
        row += w
    h = h_ref[...] + jnp.dot(merged.astype(BF16), wo_ref[...], preferred_element_type=F32)
    zb = _rms(h, gf_ref[...]).astype(BF16)

    def up_proj(c):
        cols = lambda base: wu_ref[:, base + c * ff_chunk:base + (c + 1) * ff_chunk]
        return (jnp.dot(zb, cols(0), preferred_element_type=F32),
                jnp.dot(zb, cols(D_FF), preferred_element_type=F32))

    acc = h
    n_chunks = D_FF // ff_chunk
    gu, up = up_proj(0)
    for c in range(n_chunks):
        nxt = up_proj(c + 1) if c + 1 < n_chunks else None
        acc = acc + jnp.dot((_silu(gu) * up).astype(BF16), wd_ref[c * ff_chunk:(c + 1) * ff_chunk, :],
                            preferred_element_type=F32)
        if nxt is not None:
            gu, up = nxt
    o_ref[...] = acc if fg_ref is None else _rms(acc, fg_ref[...])


def _merge_ffn(h2, proj2, ya, yc, yd, wb, wo, gf, wu, wd, final_g=None):
    m = h2.shape[0]
    tm = ROW_TILE
    rows = lambda w, col=0: pl.BlockSpec((tm, w), lambda i: (i, col))
    const = lambda arr: pl.BlockSpec(arr.shape, lambda i: (0, 0), pipeline_mode=pl.Buffered(1))
    last = () if final_g is None else (final_g,)
    return pl.pallas_call(
        functools.partial(_merge_ffn_kernel, ff_chunk=256),
        grid=(m // tm,),
        in_specs=[rows(D_MODEL), rows(GATE_COLS), rows(A_WIDTH), rows(B_WIDTH, OFF_B // B_WIDTH),
                  rows(C_WIDTH), rows(D_WIDTH), const(wb), const(wo), const(gf), const(wu),
                  const(wd)] + [const(g) for g in last],
        out_specs=rows(D_MODEL),
        out_shape=jax.ShapeDtypeStruct((m, D_MODEL), F32),
        compiler_params=pltpu.CompilerParams(dimension_semantics=("arbitrary",),
                                             vmem_limit_bytes=VMEM_LIMIT),
        name="merge_ffn",
    )(h2, proj2, ya, proj2, yc, yd, wb, wo, gf, wu, wd, *last)


def _block_diag(blocks):
    n, r, c = blocks.shape
    out = jnp.zeros((n * r, n * c), blocks.dtype)
    for i in range(n):
        out = out.at[i * r:(i + 1) * r, i * c:(i + 1) * c].set(blocks[i])
    return out


def kernel(x, meta, norm_mix, norm_ffn, norm_final, w_in, w_branch, w_out, a_mu, a_w_up, a_w0, a_a_up, a_a0, a_g_up, a_kk, a_ka, a_rk, a_ln_w, a_ln_b, a_vres_down, a_vres_up, a_vres0, b_mix, b_scale, c_sinks, d_lower_bounds, d_norm, w_ffn_up, w_ffn_down):
    bsz, seq, _ = x.shape
    depth = w_in.shape[0]
    T = SEQ_TILE
    L = N_META + seq
    pad = (-L) % T
    lp = L + pad
    assert (pad + N_META) % T == 0 and seq % T == 0
    h = jnp.concatenate([jnp.zeros((bsz, pad, D_MODEL), F32),
                         jnp.broadcast_to(meta.astype(F32)[None], (bsz, N_META, D_MODEL)),
                         x.astype(F32)], axis=1).reshape(bsz * lp, D_MODEL)
    lb_w = jax.nn.softmax(d_lower_bounds.astype(F32), axis=0)
    lb_table = jnp.cumsum(lb_w, axis=0) - lb_w[0]
    row2 = lambda t: t.reshape(1, -1).astype(F32)
    v_first = None
    for l in range(depth):
        proj2 = _in_proj(h, row2(norm_mix[l]), w_in[l].astype(BF16),
                         _block_diag(b_mix[l]).astype(BF16), row2(b_scale[l]), pad=pad, lp=lp)
        proj3 = proj2.reshape(bsz, lp, IN_COLS)
        wwa = jnp.zeros((128, 2 * A_WIDTH), F32)
        wwa = wwa.at[:64, :A_WIDTH].set(a_w_up[l]).at[64:, A_WIDTH:].set(a_a_up[l])
        pa = dict(mu=row2(a_mu[l]), wwa=wwa.astype(BF16), w0=row2(a_w0[l]), a0=row2(a_a0[l]),
                  gup=a_g_up[l].astype(BF16), kk=row2(a_kk[l]), ka=row2(a_ka[l]), rk=row2(a_rk[l]),
                  lnw=row2(a_ln_w[l]), lnb=row2(a_ln_b[l]))
        if l == 0:
            y_a, v_first = _rwkv(proj3, None, pa, has_vres=False)
        else:
            pa.update(vd=a_vres_down[l - 1].astype(BF16), vu=a_vres_up[l - 1].astype(BF16),
                      v0=row2(a_vres0[l - 1]))
            y_a = _rwkv(proj3, v_first, pa, has_vres=True)
        y_c, y_d = _swa_hgrn(proj3, c_sinks[l].astype(F32), row2(lb_table[l]), row2(d_norm[l]),
                             pad=pad)
        flat = lambda t: t.reshape(bsz * lp, t.shape[-1])
        h = _merge_ffn(h, proj2, flat(y_a), flat(y_c), flat(y_d),
                       w_branch[l].astype(BF16), w_out[l].astype(BF16), row2(norm_ffn[l]),
                       w_ffn_up[l].astype(BF16), w_ffn_down[l].astype(BF16),
                       row2(norm_final) if l == depth - 1 else None)
    return h.reshape(bsz, lp, D_MODEL)[:, pad + N_META:]
```

```python
import functools
import math

import jax
import jax.numpy as jnp
import numpy as np
from jax import lax
from jax.experimental import pallas as pl
from jax.experimental.pallas import tpu as pltpu

F32 = jnp.float32
BF16 = jnp.bfloat16

D_MODEL = 1024
N_META = 16
NORM_EPS = 1e-6
MASK_VALUE = -1e30

A_HEADS = 4
A_HEAD_DIM = 64
A_WIDTH = 256
A_GN_EPS = 64e-5
A_COLS = 1024
A_CHUNK = 64
A_BATCH = 4
A_GROUP = 2

B_WIDTH = 256
B_GROUP_DIM = 64
B_WINDOWS = (2, 4, 8, 16)
B_HALO = 16

C_HEADS = 8
C_KV_HEADS = 2
C_GROUP = 4
C_HEAD_DIM = 64
C_WIDTH = 512
C_KV_WIDTH = 128
C_COLS = 768

D_HEADS = 4
D_KEY_DIM = 64
D_WIDTH = 256
D_COLS = 1024
DENSE_STEP = 256
D_LEVELS = 7

D_FF = 2816
GATE_COLS = 4 * D_MODEL
OFF_A = GATE_COLS
OFF_B = OFF_A + A_COLS
OFF_C = OFF_B + B_WIDTH
OFF_D = OFF_C + C_COLS
IN_COLS = OFF_D + D_COLS
MIX_WIDTH = 1280

SEQ_TILE = 128
ROW_TILE = 384
VMEM_LIMIT = 56 * 1024 * 1024


def _dot(a, b):
    return jnp.dot(a.astype(BF16), b.astype(BF16), preferred_element_type=F32)


def _dot_nt(a, b):
    return lax.dot_general(a.astype(BF16), b.astype(BF16), (((1,), (1,)), ((), ())),
                           preferred_element_type=F32)


def _dot_tn(a, b):
    return lax.dot_general(a.astype(BF16), b.astype(BF16), (((0,), (0,)), ((), ())),
                           preferred_element_type=F32)


def _round_robin(*gens):
    gens = list(gens)
    while gens:
        for gen in list(gens):
            if next(gen, "done") == "done":
                gens.remove(gen)
        yield


def _interleave(*gens):
    for _ in _round_robin(*gens):
        pass


def _split2(x):
    hi = x.astype(BF16)
    lo = (x - hi.astype(F32)).astype(BF16)
    return hi, lo


def _split3(x):
    hi = x.astype(BF16)
    r = x - hi.astype(F32)
    mid = r.astype(BF16)
    lo = (r - mid.astype(F32)).astype(BF16)
    return hi, mid, lo


def _const_dot(c4, x):
    hi, mid, lo = _split3(x)
    return jnp.dot(c4, jnp.concatenate([hi, mid, lo, jnp.zeros_like(hi)], axis=0),
                   preferred_element_type=F32)


def _mm(a, b):
    return jnp.dot(a, b, preferred_element_type=F32)


def _mm_tn(a, b):
    return lax.dot_general(a, b, (((0,), (0,)), ((), ())), preferred_element_type=F32)


def _d3(a, b):
    return _mm(a[0], b[0]) + _mm(a[0], b[1]) + _mm(a[1], b[0])


def _unit_lower_inverses(mats, ri, ci, box):
    in16 = (ri // 16) == (ci // 16)
    in32 = (ri // 32) == (ci // 32)
    diag = ri == ci
    as_mask = lambda cond: jnp.where(cond, 1.0, 0.0).astype(BF16)
    plus_eye = lambda m: jnp.where(diag, 1.0, m).astype(BF16)
    m16, eye = as_mask(in16), as_mask(diag)
    xb = [a * m16 for a in mats]
    t = [xi + eye for xi in xb]
    yield
    for _ in range(3):
        x = [_mm(xi, xi) for xi in xb]
        yield
        xb = [xi.astype(BF16) for xi in x]
        t = [_mm(ti, plus_eye(xi)).astype(BF16) for ti, xi in zip(t, x)]
        yield
    n = mats[0].shape[0]
    for half, sel in ((16, as_mask(in32 & ~in16)), (32, as_mask(~in32))):
        starts = range(0, n, 2 * half)
        low = lambda m: jnp.concatenate([m[r + half:r + 2 * half] for r in starts], axis=0)
        put = lambda rows, base: jnp.concatenate(
            [p for k, r in enumerate(starts) for p in (base[r:r + half], rows[k * half:(k + 1) * half])],
            axis=0)
        lt = [_mm(low(a * sel), ti).astype(BF16) for a, ti in zip(mats, t)]
        yield
        zero = jnp.zeros_like(t[0])
        t = [put(_mm(low(ti), put(li, zero) + eye).astype(BF16), ti) for ti, li in zip(t, lt)]
        yield
    box["tinv"] = t


def _head_sum(x, ones_bd):
    return jnp.dot(x.astype(BF16), ones_bd, preferred_element_type=F32)


def _head_ones(width, seg):
    r = lax.broadcasted_iota(jnp.int32, (width, width), 0) // seg
    c = lax.broadcasted_iota(jnp.int32, (width, width), 1) // seg
    return jnp.where(r == c, 1.0, 0.0).astype(BF16)


def _sigmoid(x):
    return 1.0 / (1.0 + jnp.exp(-x))


def _silu(x):
    return x * _sigmoid(x)


def _rms(x, g):
    return x * lax.rsqrt(jnp.mean(x * x, axis=-1, keepdims=True) + NORM_EPS) * g


def _in_proj_kernel(sinks_ref, x_ref, g_ref, w_ref, mix_ref, scale_ref, bias_ref, lb_ref, ng_ref,
                    sums_ref, level_ref, o_ref, yc_ref, yd_ref, pool_buf, kv_sc, state_sc, *, pad,
                    tiles_per_seq, tm, tn):
    T = SEQ_TILE
    tile_in_seq = pl.program_id(0) % tiles_per_seq

    @pl.when(tile_in_seq == 0)
    def _():
        pool_buf[0:B_HALO, :] = jnp.zeros((B_HALO, B_WIDTH), F32)
        kv_sc[...] = jnp.zeros_like(kv_sc)
        state_sc[...] = jnp.zeros_like(state_sc)

    z = _rms(x_ref[...], g_ref[...])
    row = tile_in_seq * tm + lax.broadcasted_iota(jnp.int32, (tm, 1), 0)
    zb = jnp.where(row >= pad, z, 0.0).astype(BF16)
    cd = jnp.dot(zb, w_ref[:, OFF_C:], preferred_element_type=F32)
    u_b = jnp.dot(zb, w_ref[:, OFF_B:OFF_C], preferred_element_type=F32)

    def dense():
        for n, c0 in enumerate(range(0, OFF_B, tn)):
            for lo in range(c0, c0 + tn, DENSE_STEP):
                cols = jnp.dot(zb, w_ref[:, lo:lo + DENSE_STEP], preferred_element_type=F32)
                o_ref[:, lo:lo + DENSE_STEP] = _sigmoid(cols) if lo < GATE_COLS else cols
                yield
            if n < tm // T:
                rows = slice(n * T, (n + 1) * T)
                pooled = _pool_windows(u_b[rows], pool_buf, tile_in_seq * tm + rows.start, pad=pad)
                o_ref[rows, OFF_B:OFF_C] = _dot(pooled, mix_ref[...]) * scale_ref[...]
                yield

    def mixers():
        for k in range(tm // T):
            rows = slice(k * T, (k + 1) * T)
            kv_prev = kv_sc[...] if k == 0 else cd[(k - 1) * T:k * T, C_WIDTH:C_COLS]
            jt = tile_in_seq * (tm // T) + k
            yield from _round_robin(
                _swa_tile(yc_ref.at[pl.ds(k * T, T)], cd[rows, :C_COLS], kv_prev, sinks_ref, bias_ref, jt,
                          pad=pad),
                _hgrn_tile(yd_ref.at[pl.ds(k * T, T)], cd[rows, C_COLS:], state_sc, lb_ref, ng_ref,
                           sums_ref, level_ref, jt, pad=pad))
        kv_sc[...] = cd[tm - T:, C_WIDTH:C_COLS]

    _interleave(dense(), mixers())


def _in_proj(h2, g, w, mix_bd, scale, sinks, lb, norm_g, *, pad, lp):
    m = h2.shape[0]
    tm = ROW_TILE
    assert lp % tm == 0 and tm % SEQ_TILE == 0
    sums, level = _hgrn_constants()
    bias = _swa_constants(pad)
    const = lambda arr: pl.BlockSpec(arr.shape, lambda i: (0,) * arr.ndim)
    rows = lambda w: pl.BlockSpec((tm, w), lambda i: (i, 0))
    return pl.pallas_call(
        functools.partial(_in_proj_kernel, pad=pad, tiles_per_seq=lp // tm, tm=tm, tn=1024),
        grid=(m // tm,),
        in_specs=[pl.BlockSpec(memory_space=pltpu.SMEM),
                  rows(D_MODEL), const(g),
                  pl.BlockSpec((D_MODEL, IN_COLS), lambda i: (0, 0), pipeline_mode=pl.Buffered(1)),
                  const(mix_bd), const(scale), const(bias), const(lb), const(norm_g), const(sums),
                  const(level)],
        out_specs=(rows(OFF_C), rows(C_WIDTH), rows(D_WIDTH)),
        out_shape=(jax.ShapeDtypeStruct((m, OFF_C), F32), jax.ShapeDtypeStruct((m, C_WIDTH), F32),
                   jax.ShapeDtypeStruct((m, D_WIDTH), F32)),
        scratch_shapes=[pltpu.VMEM((B_HALO + tm, B_WIDTH), F32),
                        pltpu.VMEM((SEQ_TILE, 2 * C_KV_WIDTH), F32),
                        pltpu.VMEM((D_WIDTH, D_WIDTH), F32)],
        compiler_params=pltpu.CompilerParams(dimension_semantics=("arbitrary",),
                                             vmem_limit_bytes=VMEM_LIMIT),
        name="in_proj_bcd",
    )(sinks, h2, g, w, mix_bd, scale, bias, lb, norm_g, sums, level)


def _rwkv_kernel(*refs, has_vres):
    for parity in (0, 1):
        pl.when(pl.program_id(1) % 2 == parity)(
            functools.partial(_rwkv_step, refs, has_vres=has_vres, wslot=parity))


def _rwkv_step(refs, *, has_vres, wslot):
    handoff = refs[-7:]
    tok_sc, aab_sc, rhs_sc, bkv_sc, arb_sc, rt_sc, gam_sc = handoff
    refs = refs[:-7]
    if has_vres:
        (u_ref, vf_ref, mu_ref, wwa_ref, w0_ref, a0_ref, gup_ref, kk_ref, ka_ref, rk_ref,
         lnw_ref, lnb_ref, vd_ref, vu_ref, v0_ref, y_ref, state_sc, carry_sc) = refs
    else:
        (u_ref, mu_ref, wwa_ref, w0_ref, a0_ref, gup_ref, kk_ref, ka_ref, rk_ref,
         lnw_ref, lnb_ref, y_ref, vout_ref, state_sc, carry_sc) = refs
    T = SEQ_TILE
    C = A_CHUNK
    N = A_HEAD_DIM
    W = A_WIDTH
    NB = A_BATCH
    R = NB * T
    NCH = R // C
    j = pl.program_id(1)

    @pl.when((pl.program_id(0) == 0) & (j == 0))
    def _():
        state_sc[...] = jnp.zeros_like(state_sc)
        for ref in handoff:
            ref[...] = jnp.zeros_like(ref)

    @pl.when(j == 0)
    def _():
        carry_sc[...] = jnp.zeros_like(carry_sc)

    rslot = 1 - wslot

    u = u_ref[...].reshape(R, A_COLS)
    row = lax.broadcasted_iota(jnp.int32, (R, 1), 0)
    prev = pltpu.roll(u, 1, axis=0)
    for b in range(NB):
        prev = jnp.where(row == b * T, carry_sc[b:b + 1, :], prev)
        carry_sc[b:b + 1, :] = u[(b + 1) * T - 1:(b + 1) * T, :]
    ones_bd = _head_ones(W, N)
    lane = lax.broadcasted_iota(jnp.int32, (1, 128), 1)

    def token_stage(b):
        rows = slice(b * T, (b + 1) * T)
        x = u[rows] + (prev[rows] - u[rows]) * mu_ref[...]
        r = x[:, 0:W]
        k = x[:, W:2 * W]
        v = x[:, 2 * W:3 * W]
        slab = x[:, 3 * W:3 * W + 128]
        gd = x[:, 3 * W + 128:]
        slab = jnp.where(lane < 64, jnp.tanh(slab), slab)
        wa = _dot(slab, wwa_ref[...])
        logw = -math.exp(-0.5) * _sigmoid(w0_ref[...] + wa[:, :W])
        a = _sigmoid(a0_ref[...] + wa[:, W:])
        g = _dot(_sigmoid(gd), gup_ref[...])
        if has_vres:
            mix = _sigmoid(v0_ref[...] + _dot(_dot(v, vd_ref[...]), vu_ref[...]))
            v = v + (vf_ref[b] - v) * mix
        else:
            vout_ref[b] = v
        kkr = k * kk_ref[...]
        kk = kkr / jnp.maximum(jnp.sqrt(_head_sum(kkr * kkr, ones_bd)), 1e-12)
        k2 = k * (1.0 + (a - 1.0) * ka_ref[...])
        return dict(r=r, v=v, kk=kk, k2=k2, bhat=kk * a, logw=logw, g=g)

    ri = lax.broadcasted_iota(jnp.int32, (W, W), 0)
    ci = lax.broadcasted_iota(jnp.int32, (W, W), 1)
    same_head = (ri // C) == (ci // C)
    tw = lax.broadcasted_iota(jnp.int32, (C, W), 0)
    sw = lax.broadcasted_iota(jnp.int32, (C, W), 1) % C
    tri_s = lax.broadcasted_iota(jnp.int32, (C, 4 * C), 1)
    tri = jnp.where((tri_s % C <= lax.broadcasted_iota(jnp.int32, (C, 4 * C), 0)) & (tri_s < 3 * C),
                    1.0, 0.0).astype(BF16)

    def stack(t):
        return jnp.concatenate([t.astype(BF16)] * A_HEADS, axis=0) * ones_bd

    def chunk_inputs(tok, i):
        sl = slice(i * C, (i + 1) * C)
        lw = tok["logw"][sl]
        bhat, k2 = tok["bhat"][sl], tok["k2"][sl]
        cl = _const_dot(tri, lw)
        cl_last = cl[C - 1:C, :]
        e_neg = jnp.exp(-cl)
        e_last = jnp.exp(cl_last - cl)
        at = -tok["kk"][sl] * jnp.exp(cl - lw)
        rt = tok["r"][sl] * jnp.exp(cl)
        v_s = stack(tok["v"][sl])
        aa = _dot_nt(jnp.concatenate([at, rt], axis=0),
                     jnp.concatenate([stack(bhat * e_neg), stack(k2 * e_neg)], axis=0))
        a_ab = stack(jnp.where(sw < tw, aa[:C, :W], 0.0))
        a_ak = stack(jnp.where(sw < tw, aa[:C, W:], 0.0))
        a_rb = jnp.where(sw <= tw, aa[C:, :W], 0.0).astype(BF16)
        a_rk = jnp.where(sw <= tw, aa[C:, W:], 0.0).astype(BF16)
        rhs = (stack(at), _mm(a_ak, v_s).astype(BF16))
        btil = stack(bhat * e_last)
        ktil = stack(k2 * e_last)
        return a_ab, rhs, a_rb, a_rk, v_s, btil, ktil, rt, jnp.exp(cl_last)

    def finish(tok, outs):
        o = jnp.concatenate(outs, axis=0)
        mean = _head_sum(o, ones_bd) * (1.0 / N)
        d = o - mean
        var = _head_sum(d * d, ones_bd) * (1.0 / N)
        o = d * lax.rsqrt(var + A_GN_EPS) * lnw_ref[...] + lnb_ref[...]
        bonus = _head_sum(tok["r"] * tok["k2"] * rk_ref[...], ones_bd) * tok["v"]
        y_ref[...] = ((o + bonus) * tok["g"]).reshape(NB, T, W)

    per = T // C

    def finish_previous_tile():
        box = {}
        yield from _unit_lower_inverses([aab_sc[rslot, i] for i in range(NCH)], ri, ci, box)
        tinv, every = box["tinv"], range(NCH)
        w_t = [_mm(tinv[i], rhs_sc[rslot, i, 0]).astype(BF16) for i in every]
        yield
        u_t = [_mm(tinv[i], rhs_sc[rslot, i, 1]).astype(BF16) for i in every]
        yield
        m_e = [_mm_tn(bkv_sc[rslot, i, 0], w_t[i]).astype(BF16) for i in every]
        yield
        n_mat = [_mm_tn(bkv_sc[rslot, i, 0], u_t[i]) + _mm_tn(bkv_sc[rslot, i, 1], bkv_sc[rslot, i, 2])
                 for i in every]
        yield
        q_mat = [rt_sc[rslot, i] + _mm(arb_sc[rslot, i, 0], w_t[i]) for i in every]
        p_mat = [_mm(arb_sc[rslot, i, 0], u_t[i]) + _mm(arb_sc[rslot, i, 1], bkv_sc[rslot, i, 2])
                 for i in every]
        g_col = [jnp.sum(jnp.where(ri == ci, gam_sc[rslot, i, 0:1], 0.0), axis=1, keepdims=True)
                 for i in every]
        terms = [((g_col[i], m_e[i]), n_mat[i], q_mat[i], p_mat[i]) for i in every]
        yield
        sts = [state_sc[b] for b in range(NB)]
        outs = [None] * NCH
        for c in range(per):
            for b in range(NB):
                m_mat, n_mat, q_mat, p_mat = terms[b * per + c]
                sb = sts[b].astype(BF16)
                outs[b * per + c] = _mm(q_mat.astype(BF16), sb) + p_mat
                sts[b] = m_mat[0] * sts[b] + _mm(m_mat[1], sb) + n_mat
            yield
        for b in range(NB):
            state_sc[b] = sts[b]
        finish(dict(r=tok_sc[rslot, 0], k2=tok_sc[rslot, 1], v=tok_sc[rslot, 2],
                    g=tok_sc[rslot, 3]), outs)

    def prepare_this_tile():
        for b in range(NB):
            tok = token_stage(b)
            for n, name in enumerate(("r", "k2", "v", "g")):
                tok_sc[wslot, n, b * T:(b + 1) * T] = tok[name]
            yield
            for c in range(per):
                a_ab, rhs, a_rb, a_rk, v_s, btil, ktil, rt, gam = chunk_inputs(tok, c)
                i = b * per + c
                aab_sc[wslot, i] = a_ab
                rhs_sc[wslot, i, 0] = rhs[0]
                rhs_sc[wslot, i, 1] = rhs[1]
                arb_sc[wslot, i, 0] = a_rb
                arb_sc[wslot, i, 1] = a_rk
                bkv_sc[wslot, i, 0] = btil
                bkv_sc[wslot, i, 1] = ktil
                bkv_sc[wslot, i, 2] = v_s
                rt_sc[wslot, i] = rt
                gam_sc[wslot, i, 0:1] = gam
                yield

    _interleave(finish_previous_tile(), prepare_this_tile())

    @pl.when(j == 0)
    def _():
        state_sc[...] = jnp.zeros_like(state_sc)


def _rwkv(proj3, v_first, p, *, has_vres):
    bsz, lp, _ = proj3.shape
    T = SEQ_TILE
    NB = A_BATCH
    assert bsz % NB == 0
    nt = lp // T
    NCH = NB * T // A_CHUNK
    W = A_WIDTH
    in_spec = lambda w, col: pl.BlockSpec((NB, T, w), lambda b, j: (b, jnp.minimum(j, nt - 1), col))
    y_spec = pl.BlockSpec((NB, T, W), lambda b, j: (b, jnp.maximum(j - 1, 0), 0))
    v_spec = pl.BlockSpec((NB, T, W), lambda b, j: (b, j, 0))
    full = lambda arr: pl.BlockSpec(arr.shape, lambda b, j: (0,) * arr.ndim)
    args = [proj3]
    specs = [in_spec(A_COLS, OFF_A // A_COLS)]
    if has_vres:
        args.append(v_first)
        specs.append(in_spec(W, 0))
    names = ["mu", "wwa", "w0", "a0", "gup", "kk", "ka", "rk", "lnw", "lnb"]
    if has_vres:
        names += ["vd", "vu", "v0"]
    for n in names:
        args.append(p[n])
        specs.append(full(p[n]))
    y_shape = jax.ShapeDtypeStruct((bsz, lp, W), F32)
    v_shape = jax.ShapeDtypeStruct((bsz, lp + T, W), F32)
    out_shape = y_shape if has_vres else (y_shape, v_shape)
    out_specs = y_spec if has_vres else (y_spec, v_spec)
    return pl.pallas_call(
        functools.partial(_rwkv_kernel, has_vres=has_vres),
        grid=(bsz // NB, nt + 1),
        in_specs=specs,
        out_specs=out_specs,
        out_shape=out_shape,
        scratch_shapes=[pltpu.VMEM((NB, W, W), F32),
                        pltpu.VMEM((8, A_COLS), F32),
                        pltpu.VMEM((2, 4, NB * T, W), F32),
                        pltpu.VMEM((2, NCH, W, W), BF16),
                        pltpu.VMEM((2, NCH, 2, W, W), BF16),
                        pltpu.VMEM((2, NCH, 3, W, W), BF16),
                        pltpu.VMEM((2, NCH, 2, A_CHUNK, W), BF16),
                        pltpu.VMEM((2, NCH, A_CHUNK, W), F32),
                        pltpu.VMEM((2, NCH, 8, W), F32)],
        compiler_params=pltpu.CompilerParams(dimension_semantics=("arbitrary", "arbitrary"),
                                             vmem_limit_bytes=VMEM_LIMIT),
        name="rwkv7",
    )(*args)


def _pool_windows(u, buf, row0, *, pad):
    rows = u.shape[0]
    buf[B_HALO:B_HALO + rows, :] = u
    lane_group = lax.broadcasted_iota(jnp.int32, (1, B_WIDTH), 1) // B_GROUP_DIM
    acc = u
    win = jnp.zeros_like(u)
    for s in range(1, max(B_WINDOWS)):
        acc = acc + buf[B_HALO - s:B_HALO - s + rows, :]
        if s + 1 in B_WINDOWS:
            win = jnp.where(lane_group == B_WINDOWS.index(s + 1), acc, win)
    wlane = jnp.zeros((1, B_WIDTH), jnp.int32)
    for gi, w in enumerate(B_WINDOWS):
        wlane = jnp.where(lane_group == gi, w, wlane)
    t_real = row0 + lax.broadcasted_iota(jnp.int32, (rows, 1), 0) - pad
    cnt = jnp.minimum(jnp.maximum(t_real + 1, 1), wlane).astype(F32)
    buf[0:B_HALO, :] = u[rows - B_HALO:, :]
    return win / cnt - u


def _swa_constants(pad):
    T = SEQ_TILE
    dist = T + np.arange(T)[:, None] - np.arange(2 * T)[None, :]
    band = (dist >= 0) & (dist < T)
    n = -(-pad // T) + 2
    bias = np.empty((n, C_KV_HEADS, C_GROUP * T, 2 * T), np.float32)
    for j in range(n):
        ok = band & ((j - 1) * T + np.arange(2 * T)[None, :] >= pad)
        for hq in range(C_HEADS):
            slope = np.float32(2.0 ** (-8.0 * (hq + 1) / C_HEADS))
            rows = slice((hq % C_GROUP) * T, (hq % C_GROUP + 1) * T)
            bias[j, hq // C_GROUP, rows] = np.where(ok, -slope * dist.astype(np.float32), MASK_VALUE)
    return jnp.asarray(bias)


def _swa_tile(y_ref, blk, kv_prev, sinks_ref, bias_ref, j, *, pad):
    T = SEQ_TILE
    G = C_GROUP
    kv = blk[:, C_WIDTH:]
    kw = jnp.concatenate([kv_prev[:, :C_KV_WIDTH], kv[:, :C_KV_WIDTH]], axis=0)
    vw = jnp.concatenate([kv_prev[:, C_KV_WIDTH:], kv[:, C_KV_WIDTH:]], axis=0)
    lane = lax.broadcasted_iota(jnp.int32, (1, 2 * C_HEAD_DIM), 1)
    low = lane < C_HEAD_DIM
    kw_sw = pltpu.roll(kw, C_HEAD_DIM, axis=1)
    vw_sw = pltpu.roll(vw, C_HEAD_DIM, axis=1)
    head = lax.broadcasted_iota(jnp.int32, (G * T, 1), 0) // T
    variant = jnp.minimum(j, bias_ref.shape[0] - 1)
    tiles = []
    for hk in range(C_KV_HEADS):
        k2 = jnp.where(low == (hk == 0), kw, kw_sw).astype(BF16)
        v2 = jnp.where(low == (hk == 0), vw, vw_sw).astype(BF16)
        q_rows = []
        sink = jnp.zeros((G * T, 1), F32)
        for gq in range(G):
            hq = hk * G + gq
            pair = blk[:, (hq // 2) * 128:(hq // 2 + 1) * 128]
            q_rows.append(jnp.where(low == (hq % 2 == 0), pair * (C_HEAD_DIM ** -0.5), 0.0))
            sink = jnp.where(head == gq, sinks_ref[hq], sink)
        s = _dot_nt(jnp.concatenate(q_rows, axis=0), k2)
        logits = s + bias_ref[variant, hk]
        m = jnp.maximum(jnp.max(logits, axis=-1, keepdims=True), sink)
        p = jnp.exp(logits - m)
        denom = jnp.sum(p, axis=-1, keepdims=True) + jnp.exp(sink - m)
        yield
        o = _dot(p, v2) / denom
        for pr in range(G // 2):
            tiles.append(jnp.where(low, o[2 * pr * T:(2 * pr + 1) * T], o[(2 * pr + 1) * T:(2 * pr + 2) * T]))
        yield
    y_ref[...] = jnp.concatenate(tiles, axis=1)


def _hgrn_tile(y_ref, u, state_sc, lb_ref, ng_ref, sums_ref, level_ref, j, *, pad):
    T = SEQ_TILE
    N = D_KEY_DIM
    lb = lb_ref[...]
    q = _silu(u[:, :D_WIDTH])
    fpre = u[:, D_WIDTH:2 * D_WIDTH]
    vin = u[:, 2 * D_WIDTH:3 * D_WIDTH]
    gate = u[:, 3 * D_WIDTH:]
    sig = _sigmoid(fpre)
    f = lb + (1.0 - lb) * sig
    t_real = j * T + lax.broadcasted_iota(jnp.int32, (T, 1), 0) - pad
    logf = jnp.where(t_real >= 0, jnp.log(jnp.maximum(f, 1e-30)), 0.0)
    kx = (1.0 - lb) * (1.0 - sig)

    all_sums = _mm(sums_ref[...], jnp.concatenate(_split2(logf), axis=0))
    sums = lambda i: all_sums[i * T:(i + 1) * T]
    same_head = (lax.broadcasted_iota(jnp.int32, (D_WIDTH, D_WIDTH), 0) // N
                 == lax.broadcasted_iota(jnp.int32, (D_WIDTH, D_WIDTH), 1) // N)
    head_rows = jnp.where(lax.broadcasted_iota(jnp.int32, (D_HEADS * T, D_WIDTH), 0) // T
                          == lax.broadcasted_iota(jnp.int32, (D_HEADS * T, D_WIDTH), 1) // N,
                          1.0, 0.0).astype(BF16)
    stack = lambda t: jnp.concatenate([t.astype(BF16)] * D_HEADS, axis=0) * head_rows
    k_st = stack(kx)
    v_st = stack(vin)
    lv = level_ref[...]
    att = jnp.where(lv == 0, _dot_nt(q, k_st), 0.0)
    for l in range(D_LEVELS):
        if l == 0:
            q_l, k_l = q * jnp.exp(logf), k_st
        else:
            q_l = q * jnp.exp(sums(2 * l - 2))
            k_l = k_st * jnp.concatenate([jnp.exp(sums(2 * l - 1)).astype(BF16)] * D_HEADS, axis=0)
        att = jnp.where(lv == l + 1, _dot_nt(q_l, k_l), att)
        yield
    b = sums(2 * D_LEVELS - 2)
    b_rev = sums(2 * D_LEVELS - 1)
    st = state_sc[...]
    o = _dot(att, v_st) + _dot_nt(q * jnp.exp(b), st)
    upd = jnp.where(same_head, _dot_tn(vin, kx * jnp.exp(b_rev)), 0.0)
    state_sc[...] = st * jnp.exp(b[T - 1:T, :]) + upd

    ones_bd = _head_ones(D_WIDTH, N)
    ms = _head_sum(o * o, ones_bd) * (1.0 / N)
    y_ref[...] = o * lax.rsqrt(ms + NORM_EPS) * ng_ref[...] * _silu(gate)


def _hgrn_constants():
    T = SEQ_TILE
    t = np.arange(T)[:, None]
    i = np.arange(T)[None, :]
    mats = []
    for l in list(range(1, D_LEVELS)) + [D_LEVELS]:
        h = 2 ** l
        same = (t // h) == (i // h)
        mats += [same & (i <= t), same & (i > t)]
    sums = np.tile(np.concatenate(mats, axis=0).astype(np.float32), (1, 2))
    x = t ^ i
    level = np.where(i == t, 0, np.where(i < t, np.floor(np.log2(np.maximum(x, 1))).astype(np.int64) + 1, -1))
    return jnp.asarray(sums, BF16), jnp.asarray(np.tile(level, (1, D_HEADS)), jnp.int32)


def _merge_ffn_kernel(h_ref, gates_ref, ya_ref, yb_ref, yc_ref, yd_ref, wb_ref, wo_ref, gf_ref, wu_ref,
                      wd_ref, *rest, ff_chunk):
    fg_ref, o_ref = rest if len(rest) == 2 else (None,) + rest
    merged = None
    row = 0
    for bi, y_ref in enumerate((ya_ref, yb_ref, yc_ref, yd_ref)):
        w = y_ref.shape[-1]
        part = gates_ref[:, bi * D_MODEL:(bi + 1) * D_MODEL] * jnp.dot(
            y_ref[...].astype(BF16), wb_ref[row:row + w, :], preferred_element_type=F32)
        merged = part if merged is None else merged + part
        row += w
    h = h_ref[...] + jnp.dot(merged.astype(BF16), wo_ref[...], preferred_element_type=F32)
    zb = _rms(h, gf_ref[...]).astype(BF16)

    def up_proj(c):
        cols = lambda base: wu_ref[:, base + c * ff_chunk:base + (c + 1) * ff_chunk]
        return (jnp.dot(zb, cols(0), preferred_element_type=F32),
                jnp.dot(zb, cols(D_FF), preferred_element_type=F32))

    acc = h
    n_chunks = D_FF // ff_chunk
    gu, up = up_proj(0)
    for c in range(n_chunks):
        nxt = up_proj(c + 1) if c + 1 < n_chunks else None
        acc = acc + jnp.dot((_silu(gu) * up).astype(BF16), wd_ref[c * ff_chunk:(c + 1) * ff_chunk, :],
                            preferred_element_type=F32)
        if nxt is not None:
            gu, up = nxt
    o_ref[...] = acc if fg_ref is None else _rms(acc, fg_ref[...])


def _merge_ffn(h2, proj2, ya, yc, yd, wb, wo, gf, wu, wd, final_g=None):
    m = h2.shape[0]
    tm = ROW_TILE
    rows = lambda w, col=0: pl.BlockSpec((tm, w), lambda i: (i, col))
    const = lambda arr: pl.BlockSpec(arr.shape, lambda i: (0, 0), pipeline_mode=pl.Buffered(1))
    last = () if final_g is None else (final_g,)
    return pl.pallas_call(
        functools.partial(_merge_ffn_kernel, ff_chunk=256),
        grid=(m // tm,),
        in_specs=[rows(D_MODEL), rows(GATE_COLS), rows(A_WIDTH), rows(B_WIDTH, OFF_B // B_WIDTH),
                  rows(C_WIDTH), rows(D_WIDTH), const(wb), const(wo), const(gf), const(wu),
                  const(wd)] + [const(g) for g in last],
        out_specs=rows(D_MODEL),
        out_shape=jax.ShapeDtypeStruct((m, D_MODEL), F32),
        compiler_params=pltpu.CompilerParams(dimension_semantics=("arbitrary",),
                                             vmem_limit_bytes=VMEM_LIMIT),
        name="merge_ffn",
    )(h2, proj2, ya, proj2, yc, yd, wb, wo, gf, wu, wd, *last)


def _block_diag(blocks):
    n, r, c = blocks.shape
    out = jnp.zeros((n * r, n * c), blocks.dtype)
    for i in range(n):
        out = out.at[i * r:(i + 1) * r, i * c:(i + 1) * c].set(blocks[i])
    return out


def kernel(x, meta, norm_mix, norm_ffn, norm_final, w_in, w_branch, w_out, a_mu, a_w_up, a_w0, a_a_up, a_a0, a_g_up, a_kk, a_ka, a_rk, a_ln_w, a_ln_b, a_vres_down, a_vres_up, a_vres0, b_mix, b_scale, c_sinks, d_lower_bounds, d_norm, w_ffn_up, w_ffn_down):
    bsz, seq, _ = x.shape
    depth = w_in.shape[0]
    T = SEQ_TILE
    L = N_META + seq
    pad = (-L) % T
    lp = L + pad
    assert (pad + N_META) % T == 0 and seq % T == 0
    h = jnp.concatenate([jnp.zeros((bsz, pad, D_MODEL), F32),
                         jnp.broadcast_to(meta.astype(F32)[None], (bsz, N_META, D_MODEL)),
                         x.astype(F32)], axis=1).reshape(bsz * lp, D_MODEL)
    lb_w = jax.nn.softmax(d_lower_bounds.astype(F32), axis=0)
    lb_table = jnp.cumsum(lb_w, axis=0) - lb_w[0]
    row2 = lambda t: t.reshape(1, -1).astype(F32)
    v_first = None
    for l in range(depth):
        proj2, y_c, y_d = _in_proj(h, row2(norm_mix[l]), w_in[l].astype(BF16),
                                   _block_diag(b_mix[l]).astype(BF16), row2(b_scale[l]),
                                   c_sinks[l].astype(F32), row2(lb_table[l]), row2(d_norm[l]),
                                   pad=pad, lp=lp)
        proj3 = proj2.reshape(bsz, lp, OFF_C)
        wwa = jnp.zeros((128, 2 * A_WIDTH), F32)
        wwa = wwa.at[:64, :A_WIDTH].set(a_w_up[l]).at[64:, A_WIDTH:].set(a_a_up[l])
        pa = dict(mu=row2(a_mu[l]), wwa=wwa.astype(BF16), w0=row2(a_w0[l]), a0=row2(a_a0[l]),
                  gup=a_g_up[l].astype(BF16), kk=row2(a_kk[l]), ka=row2(a_ka[l]), rk=row2(a_rk[l]),
                  lnw=row2(a_ln_w[l]), lnb=row2(a_ln_b[l]))
        if l == 0:
            y_a, v_first = _rwkv(proj3, None, pa, has_vres=False)
        else:
            pa.update(vd=a_vres_down[l - 1].astype(BF16), vu=a_vres_up[l - 1].astype(BF16),
                      v0=row2(a_vres0[l - 1]))
            y_a = _rwkv(proj3, v_first, pa, has_vres=True)
        h = _merge_ffn(h, proj2, y_a.reshape(bsz * lp, A_WIDTH), y_c, y_d,
                       w_branch[l].astype(BF16), w_out[l].astype(BF16), row2(norm_ffn[l]),
                       w_ffn_up[l].astype(BF16), w_ffn_down[l].astype(BF16),
                       row2(norm_final) if l == depth - 1 else None)
    return h.reshape(bsz, lp, D_MODEL)[:, pad + N_META:]
```

```python
import functools
import math

import jax
import jax.numpy as jnp
import numpy as np
from jax import lax
from jax.experimental import pallas as pl
from jax.experimental.pallas import tpu as pltpu

F32 = jnp.float32
BF16 = jnp.bfloat16

D_MODEL = 1024
N_META = 16
NORM_EPS = 1e-6
MASK_VALUE = -1e30

A_HEADS = 4
A_HEAD_DIM = 64
A_WIDTH = 256
A_GN_EPS = 64e-5
A_COLS = 1024
A_CHUNK = 64
A_BATCH = 4
A_GROUP = 2

B_WIDTH = 256
B_GROUP_DIM = 64
B_WINDOWS = (2, 4, 8, 16)
B_HALO = 16

C_HEADS = 8
C_KV_HEADS = 2
C_GROUP = 4
C_HEAD_DIM = 64
C_WIDTH = 512
C_KV_WIDTH = 128
C_COLS = 768

D_HEADS = 4
D_KEY_DIM = 64
D_WIDTH = 256
D_COLS = 1024
DENSE_STEP = 256
D_LEVELS = 7

D_FF = 2816
GATE_COLS = 4 * D_MODEL
OFF_A = GATE_COLS
OFF_B = OFF_A + A_COLS
OFF_C = OFF_B + B_WIDTH
OFF_D = OFF_C + C_COLS
IN_COLS = OFF_D + D_COLS
MIX_WIDTH = 1280

SEQ_TILE = 128
ROW_TILE = 384
VMEM_LIMIT = 60 * 1024 * 1024


def _dot(a, b):
    return jnp.dot(a.astype(BF16), b.astype(BF16), preferred_element_type=F32)


def _dot_nt(a, b):
    return lax.dot_general(a.astype(BF16), b.astype(BF16), (((1,), (1,)), ((), ())),
                           preferred_element_type=F32)


def _dot_tn(a, b):
    return lax.dot_general(a.astype(BF16), b.astype(BF16), (((0,), (0,)), ((), ())),
                           preferred_element_type=F32)


def _round_robin(*gens):
    gens = list(gens)
    while gens:
        for gen in list(gens):
            if next(gen, "done") == "done":
                gens.remove(gen)
        yield


def _interleave(*gens):
    for _ in _round_robin(*gens):
        pass


def _split2(x):
    hi = x.astype(BF16)
    lo = (x - hi.astype(F32)).astype(BF16)
    return hi, lo


def _split3(x):
    hi = x.astype(BF16)
    r = x - hi.astype(F32)
    mid = r.astype(BF16)
    lo = (r - mid.astype(F32)).astype(BF16)
    return hi, mid, lo


def _const_dot(c4, x):
    hi, mid, lo = _split3(x)
    return jnp.dot(c4, jnp.concatenate([hi, mid, lo, jnp.zeros_like(hi)], axis=0),
                   preferred_element_type=F32)


def _mm(a, b):
    return jnp.dot(a, b, preferred_element_type=F32)


def _mm_tn(a, b):
    return lax.dot_general(a, b, (((0,), (0,)), ((), ())), preferred_element_type=F32)


def _d3(a, b):
    return _mm(a[0], b[0]) + _mm(a[0], b[1]) + _mm(a[1], b[0])


def _unit_lower_inverses(mats, ri, ci, box):
    in16 = (ri // 16) == (ci // 16)
    in32 = (ri // 32) == (ci // 32)
    diag = ri == ci
    as_mask = lambda cond: jnp.where(cond, 1.0, 0.0).astype(BF16)
    plus_eye = lambda m: jnp.where(diag, 1.0, m).astype(BF16)
    m16, eye = as_mask(in16), as_mask(diag)
    xb = [a * m16 for a in mats]
    t = [xi + eye for xi in xb]
    yield
    for _ in range(3):
        x = [_mm(xi, xi) for xi in xb]
        yield
        xb = [xi.astype(BF16) for xi in x]
        t = [_mm(ti, plus_eye(xi)).astype(BF16) for ti, xi in zip(t, x)]
        yield
    n = mats[0].shape[0]
    for half, sel in ((16, as_mask(in32 & ~in16)), (32, as_mask(~in32))):
        starts = range(0, n, 2 * half)
        low = lambda m: jnp.concatenate([m[r + half:r + 2 * half] for r in starts], axis=0)
        put = lambda rows, base: jnp.concatenate(
            [p for k, r in enumerate(starts) for p in (base[r:r + half], rows[k * half:(k + 1) * half])],
            axis=0)
        lt = [_mm(low(a * sel), ti).astype(BF16) for a, ti in zip(mats, t)]
        yield
        zero = jnp.zeros_like(t[0])
        t = [put(_mm(low(ti), put(li, zero) + eye).astype(BF16), ti) for ti, li in zip(t, lt)]
        yield
    box["tinv"] = t


def _head_sum(x, ones_bd):
    return jnp.dot(x.astype(BF16), ones_bd, preferred_element_type=F32)


def _head_ones(width, seg):
    r = lax.broadcasted_iota(jnp.int32, (width, width), 0) // seg
    c = lax.broadcasted_iota(jnp.int32, (width, width), 1) // seg
    return jnp.where(r == c, 1.0, 0.0).astype(BF16)


def _sigmoid(x):
    return 1.0 / (1.0 + jnp.exp(-x))


def _silu(x):
    return x * _sigmoid(x)


def _rms(x, g):
    return x * lax.rsqrt(jnp.mean(x * x, axis=-1, keepdims=True) + NORM_EPS) * g


A_PARAMS = ("mu", "wwa", "w0", "a0", "gup", "kk", "ka", "rk", "lnw", "lnb")
A_VRES_PARAMS = ("vd", "vu", "v0")


def _in_proj_kernel(*refs, has_vres, pad, tiles_per_seq, tm, tn):
    refs = list(refs)
    sinks_ref, x_ref = refs[:2]
    vf_ref = refs[2] if has_vres else None
    k0 = 3 if has_vres else 2
    g_ref, w_ref, mix_ref, scale_ref, bias_ref, lb_ref, ng_ref, sums_ref, level_ref = refs[k0:k0 + 9]
    names = A_PARAMS + (A_VRES_PARAMS if has_vres else ())
    prm = dict(zip(names, refs[k0 + 9:k0 + 9 + len(names)]))
    outs = refs[k0 + 9 + len(names):]
    o_ref, ya_ref, yc_ref, yd_ref = outs[:4]
    vout_ref = None if has_vres else outs[4]
    pool_buf, kv_sc, state_sc, a_state_sc, a_carry_sc = outs[-5:]
    T = SEQ_TILE
    tile_in_seq = pl.program_id(0) % tiles_per_seq

    @pl.when(tile_in_seq == 0)
    def _():
        pool_buf[0:B_HALO, :] = jnp.zeros((B_HALO, B_WIDTH), F32)
        for ref in (kv_sc, state_sc, a_state_sc, a_carry_sc):
            ref[...] = jnp.zeros_like(ref)

    z = _rms(x_ref[...], g_ref[...])
    row = tile_in_seq * tm + lax.broadcasted_iota(jnp.int32, (tm, 1), 0)
    zb = jnp.where(row >= pad, z, 0.0).astype(BF16)
    mixer_cols = jnp.dot(zb, w_ref[:, OFF_A:], preferred_element_type=F32)
    u_a = mixer_cols[:, :A_COLS]
    u_b = mixer_cols[:, OFF_B - OFF_A:OFF_C - OFF_A]
    cd = mixer_cols[:, OFF_C - OFF_A:]

    def dense():
        for n, c0 in enumerate(range(0, GATE_COLS, tn)):
            for lo in range(c0, c0 + tn, DENSE_STEP):
                cols = jnp.dot(zb, w_ref[:, lo:lo + DENSE_STEP], preferred_element_type=F32)
                o_ref[:, lo:lo + DENSE_STEP] = _sigmoid(cols)
                yield
            if n < tm // T:
                rows = slice(n * T, (n + 1) * T)
                pooled = _pool_windows(u_b[rows], pool_buf, tile_in_seq * tm + rows.start, pad=pad)
                o_ref[rows, GATE_COLS:] = _dot(pooled, mix_ref[...]) * scale_ref[...]
                yield

    def mixers():
        for k in range(tm // T):
            rows = slice(k * T, (k + 1) * T)
            kv_prev = kv_sc[...] if k == 0 else cd[(k - 1) * T:k * T, C_WIDTH:C_COLS]
            jt = tile_in_seq * (tm // T) + k
            yield from _round_robin(
                _swa_tile(yc_ref.at[pl.ds(k * T, T)], cd[rows, :C_COLS], kv_prev, sinks_ref, bias_ref, jt,
                          pad=pad),
                _hgrn_tile(yd_ref.at[pl.ds(k * T, T)], cd[rows, C_COLS:], state_sc, lb_ref, ng_ref,
                           sums_ref, level_ref, jt, pad=pad))
        kv_sc[...] = cd[tm - T:, C_WIDTH:C_COLS]

    rwkv = _rwkv_tile(ya_ref, vout_ref, u_a, None if vf_ref is None else vf_ref[...], prm, a_state_sc,
                      a_carry_sc)
    _interleave(dense(), rwkv, mixers())


def _in_proj(h2, v_first, g, w, mix_bd, scale, sinks, lb, norm_g, pa, *, pad, lp):
    m = h2.shape[0]
    tm = ROW_TILE
    assert lp % tm == 0 and tm % SEQ_TILE == 0
    has_vres = v_first is not None
    sums, level = _hgrn_constants()
    bias = _swa_constants(pad)
    const = lambda arr: pl.BlockSpec(arr.shape, lambda i: (0,) * arr.ndim, pipeline_mode=pl.Buffered(1))
    rows = lambda w: pl.BlockSpec((tm, w), lambda i: (i, 0))
    names = A_PARAMS + (A_VRES_PARAMS if has_vres else ())
    shape = lambda w: jax.ShapeDtypeStruct((m, w), F32)
    n_out = 4 if has_vres else 5
    widths = (GATE_COLS + B_WIDTH, A_WIDTH, C_WIDTH, D_WIDTH, A_WIDTH)[:n_out]
    return pl.pallas_call(
        functools.partial(_in_proj_kernel, has_vres=has_vres, pad=pad, tiles_per_seq=lp // tm, tm=tm,
                          tn=1024),
        grid=(m // tm,),
        in_specs=[pl.BlockSpec(memory_space=pltpu.SMEM), rows(D_MODEL)]
                 + ([rows(A_WIDTH)] if has_vres else [])
                 + [const(a) for a in (g, w, mix_bd, scale, bias, lb, norm_g, sums, level)]
                 + [const(pa[n]) for n in names],
        out_specs=tuple(rows(w) for w in widths),
        out_shape=tuple(shape(w) for w in widths),
        scratch_shapes=[pltpu.VMEM((B_HALO + tm, B_WIDTH), F32),
                        pltpu.VMEM((SEQ_TILE, 2 * C_KV_WIDTH), F32),
                        pltpu.VMEM((D_WIDTH, D_WIDTH), F32),
                        pltpu.VMEM((A_WIDTH, A_WIDTH), F32),
                        pltpu.VMEM((8, A_COLS), F32)],
        compiler_params=pltpu.CompilerParams(dimension_semantics=("arbitrary",),
                                             vmem_limit_bytes=VMEM_LIMIT),
        name="in_proj_mixers",
    )(sinks, h2, *([v_first] if has_vres else []), g, w, mix_bd, scale, bias, lb, norm_g, sums, level,
      *[pa[n] for n in names])


def _rwkv_tile(y_ref, vout_ref, u, vf, prm, state_sc, carry_sc):
    T = SEQ_TILE
    C = A_CHUNK
    N = A_HEAD_DIM
    W = A_WIDTH
    rows_total = u.shape[0]
    n_chunks = rows_total // C
    per = T // C

    row = lax.broadcasted_iota(jnp.int32, (rows_total, 1), 0)
    prev = jnp.where(row == 0, carry_sc[0:1, :], pltpu.roll(u, 1, axis=0))
    carry_sc[0:1, :] = u[rows_total - 1:rows_total, :]
    ones_bd = _head_ones(W, N)
    lane = lax.broadcasted_iota(jnp.int32, (1, 128), 1)

    def token_stage(b):
        rows = slice(b * T, (b + 1) * T)
        x = u[rows] + (prev[rows] - u[rows]) * prm["mu"][...]
        r = x[:, 0:W]
        k = x[:, W:2 * W]
        v = x[:, 2 * W:3 * W]
        slab = x[:, 3 * W:3 * W + 128]
        gd = x[:, 3 * W + 128:]
        slab = jnp.where(lane < 64, jnp.tanh(slab), slab)
        wa = _dot(slab, prm["wwa"][...])
        logw = -math.exp(-0.5) * _sigmoid(prm["w0"][...] + wa[:, :W])
        a = _sigmoid(prm["a0"][...] + wa[:, W:])
        g = _dot(_sigmoid(gd), prm["gup"][...])
        if vf is not None:
            mix = _sigmoid(prm["v0"][...] + _dot(_dot(v, prm["vd"][...]), prm["vu"][...]))
            v = v + (vf[rows] - v) * mix
        else:
            vout_ref[pl.ds(b * T, T), :] = v
        kkr = k * prm["kk"][...]
        kk = kkr / jnp.maximum(jnp.sqrt(_head_sum(kkr * kkr, ones_bd)), 1e-12)
        k2 = k * (1.0 + (a - 1.0) * prm["ka"][...])
        return dict(r=r, v=v, kk=kk, k2=k2, bhat=kk * a, logw=logw, g=g)

    ri = lax.broadcasted_iota(jnp.int32, (W, W), 0)
    ci = lax.broadcasted_iota(jnp.int32, (W, W), 1)
    tw = lax.broadcasted_iota(jnp.int32, (C, W), 0)
    sw = lax.broadcasted_iota(jnp.int32, (C, W), 1) % C
    tri_s = lax.broadcasted_iota(jnp.int32, (C, 4 * C), 1)
    tri = jnp.where((tri_s % C <= lax.broadcasted_iota(jnp.int32, (C, 4 * C), 0)) & (tri_s < 3 * C),
                    1.0, 0.0).astype(BF16)

    def stack(t):
        return jnp.concatenate([t.astype(BF16)] * A_HEADS, axis=0) * ones_bd

    def chunk_inputs(tok, i):
        sl = slice(i * C, (i + 1) * C)
        lw = tok["logw"][sl]
        bhat, k2 = tok["bhat"][sl], tok["k2"][sl]
        cl = _const_dot(tri, lw)
        cl_last = cl[C - 1:C, :]
        e_neg = jnp.exp(-cl)
        e_last = jnp.exp(cl_last - cl)
        at = -tok["kk"][sl] * jnp.exp(cl - lw)
        rt = tok["r"][sl] * jnp.exp(cl)
        v_s = stack(tok["v"][sl])
        aa = _dot_nt(jnp.concatenate([at, rt], axis=0),
                     jnp.concatenate([stack(bhat * e_neg), stack(k2 * e_neg)], axis=0))
        a_ab = stack(jnp.where(sw < tw, aa[:C, :W], 0.0))
        a_ak = stack(jnp.where(sw < tw, aa[:C, W:], 0.0))
        a_rb = jnp.where(sw <= tw, aa[C:, :W], 0.0).astype(BF16)
        a_rk = jnp.where(sw <= tw, aa[C:, W:], 0.0).astype(BF16)
        return dict(a_ab=a_ab, at_s=stack(at), aakv=_mm(a_ak, v_s).astype(BF16), a_rb=a_rb, a_rk=a_rk,
                    v_s=v_s, btil=stack(bhat * e_last), ktil=stack(k2 * e_last), rt=rt,
                    gam=jnp.exp(cl_last))

    toks, pre = [], []
    for b in range(rows_total // T):
        toks.append(token_stage(b))
        yield
        for c in range(per):
            pre.append(chunk_inputs(toks[-1], c))
            yield

    box = {}
    yield from _unit_lower_inverses([p["a_ab"] for p in pre], ri, ci, box)
    tinv = box["tinv"]
    w_t = [_mm(t, p["at_s"]).astype(BF16) for t, p in zip(tinv, pre)]
    yield
    u_t = [_mm(t, p["aakv"]).astype(BF16) for t, p in zip(tinv, pre)]
    yield
    m_e = [_mm_tn(p["btil"], w).astype(BF16) for p, w in zip(pre, w_t)]
    yield
    n_mat = [_mm_tn(p["btil"], x) + _mm_tn(p["ktil"], p["v_s"]) for p, x in zip(pre, u_t)]
    yield
    q_mat = [p["rt"] + _mm(p["a_rb"], w) for p, w in zip(pre, w_t)]
    p_mat = [_mm(p["a_rb"], x) + _mm(p["a_rk"], p["v_s"]) for p, x in zip(pre, u_t)]
    g_col = [jnp.sum(jnp.where(ri == ci, p["gam"], 0.0), axis=1, keepdims=True) for p in pre]
    yield
    st = state_sc[...]
    outs = []
    for i in range(n_chunks):
        sb = st.astype(BF16)
        outs.append(_mm(q_mat[i].astype(BF16), sb) + p_mat[i])
        st = g_col[i] * st + _mm(m_e[i], sb) + n_mat[i]
        yield
    state_sc[...] = st

    cat = lambda name: jnp.concatenate([t[name] for t in toks], axis=0)
    o = jnp.concatenate(outs, axis=0)
    mean = _head_sum(o, ones_bd) * (1.0 / N)
    d = o - mean
    var = _head_sum(d * d, ones_bd) * (1.0 / N)
    o = d * lax.rsqrt(var + A_GN_EPS) * prm["lnw"][...] + prm["lnb"][...]
    bonus = _head_sum(cat("r") * cat("k2") * prm["rk"][...], ones_bd) * cat("v")
    y_ref[...] = (o + bonus) * cat("g")


def _rwkv_kernel(*refs, has_vres):
    for parity in (0, 1):
        pl.when(pl.program_id(1) % 2 == parity)(
            functools.partial(_rwkv_step, refs, has_vres=has_vres, wslot=parity))


def _rwkv_step(refs, *, has_vres, wslot):
    handoff = refs[-7:]
    tok_sc, aab_sc, rhs_sc, bkv_sc, arb_sc, rt_sc, gam_sc = handoff
    refs = refs[:-7]
    if has_vres:
        (u_ref, vf_ref, mu_ref, wwa_ref, w0_ref, a0_ref, gup_ref, kk_ref, ka_ref, rk_ref,
         lnw_ref, lnb_ref, vd_ref, vu_ref, v0_ref, y_ref, state_sc, carry_sc) = refs
    else:
        (u_ref, mu_ref, wwa_ref, w0_ref, a0_ref, gup_ref, kk_ref, ka_ref, rk_ref,
         lnw_ref, lnb_ref, y_ref, vout_ref, state_sc, carry_sc) = refs
    T = SEQ_TILE
    C = A_CHUNK
    N = A_HEAD_DIM
    W = A_WIDTH
    NB = A_BATCH
    R = NB * T
    NCH = R // C
    j = pl.program_id(1)

    @pl.when((pl.program_id(0) == 0) & (j == 0))
    def _():
        state_sc[...] = jnp.zeros_like(state_sc)
        for ref in handoff:
            ref[...] = jnp.zeros_like(ref)

    @pl.when(j == 0)
    def _():
        carry_sc[...] = jnp.zeros_like(carry_sc)

    rslot = 1 - wslot

    u = u_ref[...].reshape(R, A_COLS)
    row = lax.broadcasted_iota(jnp.int32, (R, 1), 0)
    prev = pltpu.roll(u, 1, axis=0)
    for b in range(NB):
        prev = jnp.where(row == b * T, carry_sc[b:b + 1, :], prev)
        carry_sc[b:b + 1, :] = u[(b + 1) * T - 1:(b + 1) * T, :]
    ones_bd = _head_ones(W, N)
    lane = lax.broadcasted_iota(jnp.int32, (1, 128), 1)

    def token_stage(b):
        rows = slice(b * T, (b + 1) * T)
        x = u[rows] + (prev[rows] - u[rows]) * mu_ref[...]
        r = x[:, 0:W]
        k = x[:, W:2 * W]
        v = x[:, 2 * W:3 * W]
        slab = x[:, 3 * W:3 * W + 128]
        gd = x[:, 3 * W + 128:]
        slab = jnp.where(lane < 64, jnp.tanh(slab), slab)
        wa = _dot(slab, wwa_ref[...])
        logw = -math.exp(-0.5) * _sigmoid(w0_ref[...] + wa[:, :W])
        a = _sigmoid(a0_ref[...] + wa[:, W:])
        g = _dot(_sigmoid(gd), gup_ref[...])
        if has_vres:
            mix = _sigmoid(v0_ref[...] + _dot(_dot(v, vd_ref[...]), vu_ref[...]))
            v = v + (vf_ref[b] - v) * mix
        else:
            vout_ref[b] = v
        kkr = k * kk_ref[...]
        kk = kkr / jnp.maximum(jnp.sqrt(_head_sum(kkr * kkr, ones_bd)), 1e-12)
        k2 = k * (1.0 + (a - 1.0) * ka_ref[...])
        return dict(r=r, v=v, kk=kk, k2=k2, bhat=kk * a, logw=logw, g=g)

    ri = lax.broadcasted_iota(jnp.int32, (W, W), 0)
    ci = lax.broadcasted_iota(jnp.int32, (W, W), 1)
    same_head = (ri // C) == (ci // C)
    tw = lax.broadcasted_iota(jnp.int32, (C, W), 0)
    sw = lax.broadcasted_iota(jnp.int32, (C, W), 1) % C
    tri_s = lax.broadcasted_iota(jnp.int32, (C, 4 * C), 1)
    tri = jnp.where((tri_s % C <= lax.broadcasted_iota(jnp.int32, (C, 4 * C), 0)) & (tri_s < 3 * C),
                    1.0, 0.0).astype(BF16)

    def stack(t):
        return jnp.concatenate([t.astype(BF16)] * A_HEADS, axis=0) * ones_bd

    def chunk_inputs(tok, i):
        sl = slice(i * C, (i + 1) * C)
        lw = tok["logw"][sl]
        bhat, k2 = tok["bhat"][sl], tok["k2"][sl]
        cl = _const_dot(tri, lw)
        cl_last = cl[C - 1:C, :]
        e_neg = jnp.exp(-cl)
        e_last = jnp.exp(cl_last - cl)
        at = -tok["kk"][sl] * jnp.exp(cl - lw)
        rt = tok["r"][sl] * jnp.exp(cl)
        v_s = stack(tok["v"][sl])
        aa = _dot_nt(jnp.concatenate([at, rt], axis=0),
                     jnp.concatenate([stack(bhat * e_neg), stack(k2 * e_neg)], axis=0))
        a_ab = stack(jnp.where(sw < tw, aa[:C, :W], 0.0))
        a_ak = stack(jnp.where(sw < tw, aa[:C, W:], 0.0))
        a_rb = jnp.where(sw <= tw, aa[C:, :W], 0.0).astype(BF16)
        a_rk = jnp.where(sw <= tw, aa[C:, W:], 0.0).astype(BF16)
        rhs = (stack(at), _mm(a_ak, v_s).astype(BF16))
        btil = stack(bhat * e_last)
        ktil = stack(k2 * e_last)
        return a_ab, rhs, a_rb, a_rk, v_s, btil, ktil, rt, jnp.exp(cl_last)

    def finish(tok, outs):
        o = jnp.concatenate(outs, axis=0)
        mean = _head_sum(o, ones_bd) * (1.0 / N)
        d = o - mean
        var = _head_sum(d * d, ones_bd) * (1.0 / N)
        o = d * lax.rsqrt(var + A_GN_EPS) * lnw_ref[...] + lnb_ref[...]
        bonus = _head_sum(tok["r"] * tok["k2"] * rk_ref[...], ones_bd) * tok["v"]
        y_ref[...] = ((o + bonus) * tok["g"]).reshape(NB, T, W)

    per = T // C

    def finish_previous_tile():
        box = {}
        yield from _unit_lower_inverses([aab_sc[rslot, i] for i in range(NCH)], ri, ci, box)
        tinv, every = box["tinv"], range(NCH)
        w_t = [_mm(tinv[i], rhs_sc[rslot, i, 0]).astype(BF16) for i in every]
        yield
        u_t = [_mm(tinv[i], rhs_sc[rslot, i, 1]).astype(BF16) for i in every]
        yield
        m_e = [_mm_tn(bkv_sc[rslot, i, 0], w_t[i]).astype(BF16) for i in every]
        yield
        n_mat = [_mm_tn(bkv_sc[rslot, i, 0], u_t[i]) + _mm_tn(bkv_sc[rslot, i, 1], bkv_sc[rslot, i, 2])
                 for i in every]
        yield
        q_mat = [rt_sc[rslot, i] + _mm(arb_sc[rslot, i, 0], w_t[i]) for i in every]
        p_mat = [_mm(arb_sc[rslot, i, 0], u_t[i]) + _mm(arb_sc[rslot, i, 1], bkv_sc[rslot, i, 2])
                 for i in every]
        g_col = [jnp.sum(jnp.where(ri == ci, gam_sc[rslot, i, 0:1], 0.0), axis=1, keepdims=True)
                 for i in every]
        terms = [((g_col[i], m_e[i]), n_mat[i], q_mat[i], p_mat[i]) for i in every]
        yield
        sts = [state_sc[b] for b in range(NB)]
        outs = [None] * NCH
        for c in range(per):
            for b in range(NB):
                m_mat, n_mat, q_mat, p_mat = terms[b * per + c]
                sb = sts[b].astype(BF16)
                outs[b * per + c] = _mm(q_mat.astype(BF16), sb) + p_mat
                sts[b] = m_mat[0] * sts[b] + _mm(m_mat[1], sb) + n_mat
            yield
        for b in range(NB):
            state_sc[b] = sts[b]
        finish(dict(r=tok_sc[rslot, 0], k2=tok_sc[rslot, 1], v=tok_sc[rslot, 2],
                    g=tok_sc[rslot, 3]), outs)

    def prepare_this_tile():
        for b in range(NB):
            tok = token_stage(b)
            for n, name in enumerate(("r", "k2", "v", "g")):
                tok_sc[wslot, n, b * T:(b + 1) * T] = tok[name]
            yield
            for c in range(per):
                a_ab, rhs, a_rb, a_rk, v_s, btil, ktil, rt, gam = chunk_inputs(tok, c)
                i = b * per + c
                aab_sc[wslot, i] = a_ab
                rhs_sc[wslot, i, 0] = rhs[0]
                rhs_sc[wslot, i, 1] = rhs[1]
                arb_sc[wslot, i, 0] = a_rb
                arb_sc[wslot, i, 1] = a_rk
                bkv_sc[wslot, i, 0] = btil
                bkv_sc[wslot, i, 1] = ktil
                bkv_sc[wslot, i, 2] = v_s
                rt_sc[wslot, i] = rt
                gam_sc[wslot, i, 0:1] = gam
                yield

    _interleave(finish_previous_tile(), prepare_this_tile())

    @pl.when(j == 0)
    def _():
        state_sc[...] = jnp.zeros_like(state_sc)


def _rwkv(proj3, v_first, p, *, has_vres):
    bsz, lp, _ = proj3.shape
    T = SEQ_TILE
    NB = A_BATCH
    assert bsz % NB == 0
    nt = lp // T
    NCH = NB * T // A_CHUNK
    W = A_WIDTH
    in_spec = lambda w, col: pl.BlockSpec((NB, T, w), lambda b, j: (b, jnp.minimum(j, nt - 1), col))
    y_spec = pl.BlockSpec((NB, T, W), lambda b, j: (b, jnp.maximum(j - 1, 0), 0))
    v_spec = pl.BlockSpec((NB, T, W), lambda b, j: (b, j, 0))
    full = lambda arr: pl.BlockSpec(arr.shape, lambda b, j: (0,) * arr.ndim)
    args = [proj3]
    specs = [in_spec(A_COLS, OFF_A // A_COLS)]
    if has_vres:
        args.append(v_first)
        specs.append(in_spec(W, 0))
    names = ["mu", "wwa", "w0", "a0", "gup", "kk", "ka", "rk", "lnw", "lnb"]
    if has_vres:
        names += ["vd", "vu", "v0"]
    for n in names:
        args.append(p[n])
        specs.append(full(p[n]))
    y_shape = jax.ShapeDtypeStruct((bsz, lp, W), F32)
    v_shape = jax.ShapeDtypeStruct((bsz, lp + T, W), F32)
    out_shape = y_shape if has_vres else (y_shape, v_shape)
    out_specs = y_spec if has_vres else (y_spec, v_spec)
    return pl.pallas_call(
        functools.partial(_rwkv_kernel, has_vres=has_vres),
        grid=(bsz // NB, nt + 1),
        in_specs=specs,
        out_specs=out_specs,
        out_shape=out_shape,
        scratch_shapes=[pltpu.VMEM((NB, W, W), F32),
                        pltpu.VMEM((8, A_COLS), F32),
                        pltpu.VMEM((2, 4, NB * T, W), F32),
                        pltpu.VMEM((2, NCH, W, W), BF16),
                        pltpu.VMEM((2, NCH, 2, W, W), BF16),
                        pltpu.VMEM((2, NCH, 3, W, W), BF16),
                        pltpu.VMEM((2, NCH, 2, A_CHUNK, W), BF16),
                        pltpu.VMEM((2, NCH, A_CHUNK, W), F32),
                        pltpu.VMEM((2, NCH, 8, W), F32)],
        compiler_params=pltpu.CompilerParams(dimension_semantics=("arbitrary", "arbitrary"),
                                             vmem_limit_bytes=VMEM_LIMIT),
        name="rwkv7",
    )(*args)


def _pool_windows(u, buf, row0, *, pad):
    rows = u.shape[0]
    buf[B_HALO:B_HALO + rows, :] = u
    lane_group = lax.broadcasted_iota(jnp.int32, (1, B_WIDTH), 1) // B_GROUP_DIM
    acc = u
    win = jnp.zeros_like(u)
    for s in range(1, max(B_WINDOWS)):
        acc = acc + buf[B_HALO - s:B_HALO - s + rows, :]
        if s + 1 in B_WINDOWS:
            win = jnp.where(lane_group == B_WINDOWS.index(s + 1), acc, win)
    wlane = jnp.zeros((1, B_WIDTH), jnp.int32)
    for gi, w in enumerate(B_WINDOWS):
        wlane = jnp.where(lane_group == gi, w, wlane)
    t_real = row0 + lax.broadcasted_iota(jnp.int32, (rows, 1), 0) - pad
    cnt = jnp.minimum(jnp.maximum(t_real + 1, 1), wlane).astype(F32)
    buf[0:B_HALO, :] = u[rows - B_HALO:, :]
    return win / cnt - u


def _swa_constants(pad):
    T = SEQ_TILE
    dist = T + np.arange(T)[:, None] - np.arange(2 * T)[None, :]
    band = (dist >= 0) & (dist < T)
    n = -(-pad // T) + 2
    bias = np.empty((n, C_KV_HEADS, C_GROUP * T, 2 * T), np.float32)
    for j in range(n):
        ok = band & ((j - 1) * T + np.arange(2 * T)[None, :] >= pad)
        for hq in range(C_HEADS):
            slope = np.float32(2.0 ** (-8.0 * (hq + 1) / C_HEADS))
            rows = slice((hq % C_GROUP) * T, (hq % C_GROUP + 1) * T)
            bias[j, hq // C_GROUP, rows] = np.where(ok, -slope * dist.astype(np.float32), MASK_VALUE)
    return jnp.asarray(bias)


def _swa_tile(y_ref, blk, kv_prev, sinks_ref, bias_ref, j, *, pad):
    T = SEQ_TILE
    G = C_GROUP
    kv = blk[:, C_WIDTH:]
    kw = jnp.concatenate([kv_prev[:, :C_KV_WIDTH], kv[:, :C_KV_WIDTH]], axis=0)
    vw = jnp.concatenate([kv_prev[:, C_KV_WIDTH:], kv[:, C_KV_WIDTH:]], axis=0)
    lane = lax.broadcasted_iota(jnp.int32, (1, 2 * C_HEAD_DIM), 1)
    low = lane < C_HEAD_DIM
    kw_sw = pltpu.roll(kw, C_HEAD_DIM, axis=1)
    vw_sw = pltpu.roll(vw, C_HEAD_DIM, axis=1)
    head = lax.broadcasted_iota(jnp.int32, (G * T, 1), 0) // T
    variant = jnp.minimum(j, bias_ref.shape[0] - 1)
    tiles = []
    for hk in range(C_KV_HEADS):
        k2 = jnp.where(low == (hk == 0), kw, kw_sw).astype(BF16)
        v2 = jnp.where(low == (hk == 0), vw, vw_sw).astype(BF16)
        q_rows = []
        sink = jnp.zeros((G * T, 1), F32)
        for gq in range(G):
            hq = hk * G + gq
            pair = blk[:, (hq // 2) * 128:(hq // 2 + 1) * 128]
            q_rows.append(jnp.where(low == (hq % 2 == 0), pair * (C_HEAD_DIM ** -0.5), 0.0))
            sink = jnp.where(head == gq, sinks_ref[hq], sink)
        s = _dot_nt(jnp.concatenate(q_rows, axis=0), k2)
        logits = s + bias_ref[variant, hk]
        m = jnp.maximum(jnp.max(logits, axis=-1, keepdims=True), sink)
        p = jnp.exp(logits - m)
        denom = jnp.sum(p, axis=-1, keepdims=True) + jnp.exp(sink - m)
        yield
        o = _dot(p, v2) / denom
        for pr in range(G // 2):
            tiles.append(jnp.where(low, o[2 * pr * T:(2 * pr + 1) * T], o[(2 * pr + 1) * T:(2 * pr + 2) * T]))
        yield
    y_ref[...] = jnp.concatenate(tiles, axis=1)


def _hgrn_tile(y_ref, u, state_sc, lb_ref, ng_ref, sums_ref, level_ref, j, *, pad):
    T = SEQ_TILE
    N = D_KEY_DIM
    lb = lb_ref[...]
    q = _silu(u[:, :D_WIDTH])
    fpre = u[:, D_WIDTH:2 * D_WIDTH]
    vin = u[:, 2 * D_WIDTH:3 * D_WIDTH]
    gate = u[:, 3 * D_WIDTH:]
    sig = _sigmoid(fpre)
    f = lb + (1.0 - lb) * sig
    t_real = j * T + lax.broadcasted_iota(jnp.int32, (T, 1), 0) - pad
    logf = jnp.where(t_real >= 0, jnp.log(jnp.maximum(f, 1e-30)), 0.0)
    kx = (1.0 - lb) * (1.0 - sig)

    all_sums = _mm(sums_ref[...], jnp.concatenate(_split2(logf), axis=0))
    sums = lambda i: all_sums[i * T:(i + 1) * T]
    same_head = (lax.broadcasted_iota(jnp.int32, (D_WIDTH, D_WIDTH), 0) // N
                 == lax.broadcasted_iota(jnp.int32, (D_WIDTH, D_WIDTH), 1) // N)
    head_rows = jnp.where(lax.broadcasted_iota(jnp.int32, (D_HEADS * T, D_WIDTH), 0) // T
                          == lax.broadcasted_iota(jnp.int32, (D_HEADS * T, D_WIDTH), 1) // N,
                          1.0, 0.0).astype(BF16)
    stack = lambda t: jnp.concatenate([t.astype(BF16)] * D_HEADS, axis=0) * head_rows
    k_st = stack(kx)
    v_st = stack(vin)
    lv = level_ref[...]
    att = jnp.where(lv == 0, _dot_nt(q, k_st), 0.0)
    for l in range(D_LEVELS):
        if l == 0:
            q_l, k_l = q * jnp.exp(logf), k_st
        else:
            q_l = q * jnp.exp(sums(2 * l - 2))
            k_l = k_st * jnp.concatenate([jnp.exp(sums(2 * l - 1)).astype(BF16)] * D_HEADS, axis=0)
        att = jnp.where(lv == l + 1, _dot_nt(q_l, k_l), att)
        yield
    b = sums(2 * D_LEVELS - 2)
    b_rev = sums(2 * D_LEVELS - 1)
    st = state_sc[...]
    o = _dot(att, v_st) + _dot_nt(q * jnp.exp(b), st)
    upd = jnp.where(same_head, _dot_tn(vin, kx * jnp.exp(b_rev)), 0.0)
    state_sc[...] = st * jnp.exp(b[T - 1:T, :]) + upd

    ones_bd = _head_ones(D_WIDTH, N)
    ms = _head_sum(o * o, ones_bd) * (1.0 / N)
    y_ref[...] = o * lax.rsqrt(ms + NORM_EPS) * ng_ref[...] * _silu(gate)


def _hgrn_constants():
    T = SEQ_TILE
    t = np.arange(T)[:, None]
    i = np.arange(T)[None, :]
    mats = []
    for l in list(range(1, D_LEVELS)) + [D_LEVELS]:
        h = 2 ** l
        same = (t // h) == (i // h)
        mats += [same & (i <= t), same & (i > t)]
    sums = np.tile(np.concatenate(mats, axis=0).astype(np.float32), (1, 2))
    x = t ^ i
    level = np.where(i == t, 0, np.where(i < t, np.floor(np.log2(np.maximum(x, 1))).astype(np.int64) + 1, -1))
    return jnp.asarray(sums, BF16), jnp.asarray(np.tile(level, (1, D_HEADS)), jnp.int32)


def _merge_ffn_kernel(h_ref, gates_ref, ya_ref, yb_ref, yc_ref, yd_ref, wb_ref, wo_ref, gf_ref, wu_ref,
                      wd_ref, *rest, ff_chunk):
    fg_ref, o_ref = rest if len(rest) == 2 else (None,) + rest
    merged = None
    row = 0
    for bi, y_ref in enumerate((ya_ref, yb_ref, yc_ref, yd_ref)):
        w = y_ref.shape[-1]
        part = gates_ref[:, bi * D_MODEL:(bi + 1) * D_MODEL] * jnp.dot(
            y_ref[...].astype(BF16), wb_ref[row:row + w, :], preferred_element_type=F32)
        merged = part if merged is None else merged + part
        row += w
    h = h_ref[...] + jnp.dot(merged.astype(BF16), wo_ref[...], preferred_element_type=F32)
    zb = _rms(h, gf_ref[...]).astype(BF16)

    def up_proj(c):
        cols = lambda base: wu_ref[:, base + c * ff_chunk:base + (c + 1) * ff_chunk]
        return (jnp.dot(zb, cols(0), preferred_element_type=F32),
                jnp.dot(zb, cols(D_FF), preferred_element_type=F32))

    acc = h
    n_chunks = D_FF // ff_chunk
    gu, up = up_proj(0)
    for c in range(n_chunks):
        nxt = up_proj(c + 1) if c + 1 < n_chunks else None
        acc = acc + jnp.dot((_silu(gu) * up).astype(BF16), wd_ref[c * ff_chunk:(c + 1) * ff_chunk, :],
                            preferred_element_type=F32)
        if nxt is not None:
            gu, up = nxt
    o_ref[...] = acc if fg_ref is None else _rms(acc, fg_ref[...])


def _merge_ffn(h2, proj2, ya, yc, yd, wb, wo, gf, wu, wd, final_g=None):
    m = h2.shape[0]
    tm = ROW_TILE
    rows = lambda w, col=0: pl.BlockSpec((tm, w), lambda i: (i, col))
    const = lambda arr: pl.BlockSpec(arr.shape, lambda i: (0, 0), pipeline_mode=pl.Buffered(1))
    last = () if final_g is None else (final_g,)
    return pl.pallas_call(
        functools.partial(_merge_ffn_kernel, ff_chunk=256),
        grid=(m // tm,),
        in_specs=[rows(D_MODEL), rows(GATE_COLS), rows(A_WIDTH), rows(B_WIDTH, GATE_COLS // B_WIDTH),
                  rows(C_WIDTH), rows(D_WIDTH), const(wb), const(wo), const(gf), const(wu),
                  const(wd)] + [const(g) for g in last],
        out_specs=rows(D_MODEL),
        out_shape=jax.ShapeDtypeStruct((m, D_MODEL), F32),
        compiler_params=pltpu.CompilerParams(dimension_semantics=("arbitrary",),
                                             vmem_limit_bytes=VMEM_LIMIT),
        name="merge_ffn",
    )(h2, proj2, ya, proj2, yc, yd, wb, wo, gf, wu, wd, *last)


def _block_diag(blocks):
    n, r, c = blocks.shape
    out = jnp.zeros((n * r, n * c), blocks.dtype)
    for i in range(n):
        out = out.at[i * r:(i + 1) * r, i * c:(i + 1) * c].set(blocks[i])
    return out


def kernel(x, meta, norm_mix, norm_ffn, norm_final, w_in, w_branch, w_out, a_mu, a_w_up, a_w0, a_a_up, a_a0, a_g_up, a_kk, a_ka, a_rk, a_ln_w, a_ln_b, a_vres_down, a_vres_up, a_vres0, b_mix, b_scale, c_sinks, d_lower_bounds, d_norm, w_ffn_up, w_ffn_down):
    bsz, seq, _ = x.shape
    depth = w_in.shape[0]
    T = SEQ_TILE
    L = N_META + seq
    pad = (-L) % T
    lp = L + pad
    assert (pad + N_META) % T == 0 and seq % T == 0
    h = jnp.concatenate([jnp.zeros((bsz, pad, D_MODEL), F32),
                         jnp.broadcast_to(meta.astype(F32)[None], (bsz, N_META, D_MODEL)),
                         x.astype(F32)], axis=1).reshape(bsz * lp, D_MODEL)
    lb_w = jax.nn.softmax(d_lower_bounds.astype(F32), axis=0)
    lb_table = jnp.cumsum(lb_w, axis=0) - lb_w[0]
    row2 = lambda t: t.reshape(1, -1).astype(F32)
    v_first = None
    for l in range(depth):
        wwa = jnp.zeros((128, 2 * A_WIDTH), F32)
        wwa = wwa.at[:64, :A_WIDTH].set(a_w_up[l]).at[64:, A_WIDTH:].set(a_a_up[l])
        pa = dict(mu=row2(a_mu[l]), wwa=wwa.astype(BF16), w0=row2(a_w0[l]), a0=row2(a_a0[l]),
                  gup=a_g_up[l].astype(BF16), kk=row2(a_kk[l]), ka=row2(a_ka[l]), rk=row2(a_rk[l]),
                  lnw=row2(a_ln_w[l]), lnb=row2(a_ln_b[l]))
        if l > 0:
            pa.update(vd=a_vres_down[l - 1].astype(BF16), vu=a_vres_up[l - 1].astype(BF16),
                      v0=row2(a_vres0[l - 1]))
        outs = _in_proj(h, v_first, row2(norm_mix[l]), w_in[l].astype(BF16),
                        _block_diag(b_mix[l]).astype(BF16), row2(b_scale[l]), c_sinks[l].astype(F32),
                        row2(lb_table[l]), row2(d_norm[l]), pa, pad=pad, lp=lp)
        proj2, y_a, y_c, y_d = outs[:4]
        if l == 0:
            v_first = outs[4]
        h = _merge_ffn(h, proj2, y_a, y_c, y_d,
                       w_branch[l].astype(BF16), w_out[l].astype(BF16), row2(norm_ffn[l]),
                       w_ffn_up[l].astype(BF16), w_ffn_down[l].astype(BF16),
                       row2(norm_final) if l == depth - 1 else None)
    return h.reshape(bsz, lp, D_MODEL)[:, pad + N_META:]
```

```python
import functools
import math

import jax
import jax.numpy as jnp
import numpy as np
from jax import lax
from jax.experimental import pallas as pl
from jax.experimental.pallas import tpu as pltpu

F32 = jnp.float32
BF16 = jnp.bfloat16

D_MODEL = 1024
N_META = 16
NORM_EPS = 1e-6
MASK_VALUE = -1e30

A_HEADS = 4
A_HEAD_DIM = 64
A_WIDTH = 256
A_GN_EPS = 64e-5
A_COLS = 1024
A_CHUNK = 64

B_WIDTH = 256
B_GROUP_DIM = 64
B_WINDOWS = (2, 4, 8, 16)
B_HALO = 16

C_HEADS = 8
C_KV_HEADS = 2
C_GROUP = 4
C_HEAD_DIM = 64
C_WIDTH = 512
C_KV_WIDTH = 128
C_COLS = 768

D_HEADS = 4
D_KEY_DIM = 64
D_WIDTH = 256
D_COLS = 1024
DENSE_STEP = 256
D_LEVELS = 7

D_FF = 2816
GATE_COLS = 4 * D_MODEL
OFF_A = GATE_COLS
OFF_B = OFF_A + A_COLS
OFF_C = OFF_B + B_WIDTH
OFF_D = OFF_C + C_COLS
IN_COLS = OFF_D + D_COLS
MIX_WIDTH = 1280

SEQ_TILE = 128
ROW_TILE = 384
VMEM_LIMIT = 60 * 1024 * 1024


def _dot(a, b):
    return jnp.dot(a.astype(BF16), b.astype(BF16), preferred_element_type=F32)


def _dot_nt(a, b):
    return lax.dot_general(a.astype(BF16), b.astype(BF16), (((1,), (1,)), ((), ())),
                           preferred_element_type=F32)


def _dot_tn(a, b):
    return lax.dot_general(a.astype(BF16), b.astype(BF16), (((0,), (0,)), ((), ())),
                           preferred_element_type=F32)


def _round_robin(*gens, turns=None):
    gens = list(zip(gens, turns or [1] * len(gens)))
    while gens:
        for item in list(gens):
            for _ in range(item[1]):
                if next(item[0], "done") == "done":
                    gens.remove(item)
                    break
        yield


def _interleave(*gens, turns=None):
    for _ in _round_robin(*gens, turns=turns):
        pass


def _split2(x):
    hi = x.astype(BF16)
    lo = (x - hi.astype(F32)).astype(BF16)
    return hi, lo


def _split3(x):
    hi = x.astype(BF16)
    r = x - hi.astype(F32)
    mid = r.astype(BF16)
    lo = (r - mid.astype(F32)).astype(BF16)
    return hi, mid, lo


def _const_dot(c4, x):
    hi, mid, lo = _split3(x)
    return jnp.dot(c4, jnp.concatenate([hi, mid, lo, jnp.zeros_like(hi)], axis=0),
                   preferred_element_type=F32)


def _mm(a, b):
    return jnp.dot(a, b, preferred_element_type=F32)


def _mm_tn(a, b):
    return lax.dot_general(a, b, (((0,), (0,)), ((), ())), preferred_element_type=F32)


def _unit_lower_inverses(mats, ri, ci, box):
    in16 = (ri // 16) == (ci // 16)
    in32 = (ri // 32) == (ci // 32)
    diag = ri == ci
    as_mask = lambda cond: jnp.where(cond, 1.0, 0.0).astype(BF16)
    plus_eye = lambda m: jnp.where(diag, 1.0, m).astype(BF16)
    m16, eye = as_mask(in16), as_mask(diag)
    xb = [a * m16 for a in mats]
    t = [xi + eye for xi in xb]
    yield
    for _ in range(3):
        x = [_mm(xi, xi) for xi in xb]
        yield
        xb = [xi.astype(BF16) for xi in x]
        t = [_mm(ti, plus_eye(xi)).astype(BF16) for ti, xi in zip(t, x)]
        yield
    n = mats[0].shape[0]
    for half, sel in ((16, as_mask(in32 & ~in16)), (32, as_mask(~in32))):
        starts = range(0, n, 2 * half)
        low = lambda m: jnp.concatenate([m[r + half:r + 2 * half] for r in starts], axis=0)
        put = lambda rows, base: jnp.concatenate(
            [p for k, r in enumerate(starts) for p in (base[r:r + half], rows[k * half:(k + 1) * half])],
            axis=0)
        lt = [_mm(low(a * sel), ti).astype(BF16) for a, ti in zip(mats, t)]
        yield
        zero = jnp.zeros_like(t[0])
        t = [put(_mm(low(ti), put(li, zero) + eye).astype(BF16), ti) for ti, li in zip(t, lt)]
        yield
    box["tinv"] = t


def _head_sum(x, ones_bd):
    return jnp.dot(x.astype(BF16), ones_bd, preferred_element_type=F32)


def _head_ones(width, seg):
    r = lax.broadcasted_iota(jnp.int32, (width, width), 0) // seg
    c = lax.broadcasted_iota(jnp.int32, (width, width), 1) // seg
    return jnp.where(r == c, 1.0, 0.0).astype(BF16)


def _sigmoid(x):
    return 1.0 / (1.0 + jnp.exp(-x))


def _silu(x):
    return x * _sigmoid(x)


def _rms(x, g):
    return x * lax.rsqrt(jnp.mean(x * x, axis=-1, keepdims=True) + NORM_EPS) * g


A_PARAMS = ("mu", "wwa", "w0", "a0", "gup", "kk", "ka", "rk", "lnw", "lnb")
A_VRES_PARAMS = ("vd", "vu", "v0")


def _in_proj_kernel(*refs, has_vres, pad, tiles_per_seq, tm, tn):
    refs = list(refs)
    sinks_ref, x_ref = refs[:2]
    vf_ref = refs[2] if has_vres else None
    k0 = 3 if has_vres else 2
    g_ref, w_ref, mix_ref, scale_ref, bias_ref, lb_ref, ng_ref, sums_ref, level_ref = refs[k0:k0 + 9]
    names = A_PARAMS + (A_VRES_PARAMS if has_vres else ())
    prm = dict(zip(names, refs[k0 + 9:k0 + 9 + len(names)]))
    outs = refs[k0 + 9 + len(names):]
    o_ref, ya_ref, yc_ref, yd_ref = outs[:4]
    vout_ref = None if has_vres else outs[4]
    pool_buf, kv_sc, state_sc, a_state_sc, a_carry_sc = outs[-5:]
    T = SEQ_TILE
    tile_in_seq = pl.program_id(0) % tiles_per_seq

    @pl.when(tile_in_seq == 0)
    def _():
        pool_buf[0:B_HALO, :] = jnp.zeros((B_HALO, B_WIDTH), F32)
        for ref in (kv_sc, state_sc, a_state_sc, a_carry_sc):
            ref[...] = jnp.zeros_like(ref)

    z = _rms(x_ref[...], g_ref[...])
    row = tile_in_seq * tm + lax.broadcasted_iota(jnp.int32, (tm, 1), 0)
    zb = jnp.where(row >= pad, z, 0.0).astype(BF16)
    mixer_cols = jnp.dot(zb, w_ref[:, OFF_A:], preferred_element_type=F32)
    u_a = mixer_cols[:, :A_COLS]
    u_b = mixer_cols[:, OFF_B - OFF_A:OFF_C - OFF_A]
    cd = mixer_cols[:, OFF_C - OFF_A:]

    def dense():
        for n, c0 in enumerate(range(0, GATE_COLS, tn)):
            for lo in range(c0, c0 + tn, DENSE_STEP):
                cols = jnp.dot(zb, w_ref[:, lo:lo + DENSE_STEP], preferred_element_type=F32)
                o_ref[:, lo:lo + DENSE_STEP] = _sigmoid(cols)
                yield
            if n < tm // T:
                rows = slice(n * T, (n + 1) * T)
                pooled = _pool_windows(u_b[rows], pool_buf, tile_in_seq * tm + rows.start, pad=pad)
                o_ref[rows, GATE_COLS:] = _dot(pooled, mix_ref[...]) * scale_ref[...]
                yield

    def mixers():
        for k in range(tm // T):
            rows = slice(k * T, (k + 1) * T)
            kv_prev = kv_sc[...] if k == 0 else cd[(k - 1) * T:k * T, C_WIDTH:C_COLS]
            jt = tile_in_seq * (tm // T) + k
            yield from _round_robin(
                _swa_tile(yc_ref.at[pl.ds(k * T, T)], cd[rows, :C_COLS], kv_prev, sinks_ref, bias_ref, jt,
                          pad=pad),
                _hgrn_tile(yd_ref.at[pl.ds(k * T, T)], cd[rows, C_COLS:], state_sc, lb_ref, ng_ref,
                           sums_ref, level_ref, jt, pad=pad))
        kv_sc[...] = cd[tm - T:, C_WIDTH:C_COLS]

    rwkv = _rwkv_tile(ya_ref, vout_ref, u_a, None if vf_ref is None else vf_ref[...], prm, a_state_sc,
                      a_carry_sc)
    _interleave(dense(), rwkv, mixers())


def _in_proj(h2, v_first, g, w, mix_bd, scale, sinks, lb, norm_g, pa, *, pad, lp):
    m = h2.shape[0]
    tm = ROW_TILE
    assert lp % tm == 0 and tm % SEQ_TILE == 0
    has_vres = v_first is not None
    sums, level = _hgrn_constants()
    bias = _swa_constants(pad)
    const = lambda arr: pl.BlockSpec(arr.shape, lambda i: (0,) * arr.ndim, pipeline_mode=pl.Buffered(1))
    rows = lambda w: pl.BlockSpec((tm, w), lambda i: (i, 0))
    names = A_PARAMS + (A_VRES_PARAMS if has_vres else ())
    shape = lambda w: jax.ShapeDtypeStruct((m, w), F32)
    n_out = 4 if has_vres else 5
    widths = (GATE_COLS + B_WIDTH, A_WIDTH, C_WIDTH, D_WIDTH, A_WIDTH)[:n_out]
    return pl.pallas_call(
        functools.partial(_in_proj_kernel, has_vres=has_vres, pad=pad, tiles_per_seq=lp // tm, tm=tm,
                          tn=1024),
        grid=(m // tm,),
        in_specs=[pl.BlockSpec(memory_space=pltpu.SMEM), rows(D_MODEL)]
                 + ([rows(A_WIDTH)] if has_vres else [])
                 + [const(a) for a in (g, w, mix_bd, scale, bias, lb, norm_g, sums, level)]
                 + [const(pa[n]) for n in names],
        out_specs=tuple(rows(w) for w in widths),
        out_shape=tuple(shape(w) for w in widths),
        scratch_shapes=[pltpu.VMEM((B_HALO + tm, B_WIDTH), F32),
                        pltpu.VMEM((SEQ_TILE, 2 * C_KV_WIDTH), F32),
                        pltpu.VMEM((D_WIDTH, D_WIDTH), F32),
                        pltpu.VMEM((A_WIDTH, A_WIDTH), F32),
                        pltpu.VMEM((8, A_COLS), F32)],
        compiler_params=pltpu.CompilerParams(dimension_semantics=("arbitrary",),
                                             vmem_limit_bytes=VMEM_LIMIT),
        name="in_proj_mixers",
    )(sinks, h2, *([v_first] if has_vres else []), g, w, mix_bd, scale, bias, lb, norm_g, sums, level,
      *[pa[n] for n in names])


def _rwkv_tile(y_ref, vout_ref, u, vf, prm, state_sc, carry_sc):
    T = SEQ_TILE
    C = A_CHUNK
    N = A_HEAD_DIM
    W = A_WIDTH
    rows_total = u.shape[0]
    n_chunks = rows_total // C
    per = T // C

    row = lax.broadcasted_iota(jnp.int32, (rows_total, 1), 0)
    prev = jnp.where(row == 0, carry_sc[0:1, :], pltpu.roll(u, 1, axis=0))
    carry_sc[0:1, :] = u[rows_total - 1:rows_total, :]
    ones_bd = _head_ones(W, N)
    lane = lax.broadcasted_iota(jnp.int32, (1, 128), 1)

    def token_stage(b):
        rows = slice(b * T, (b + 1) * T)
        x = u[rows] + (prev[rows] - u[rows]) * prm["mu"][...]
        r = x[:, 0:W]
        k = x[:, W:2 * W]
        v = x[:, 2 * W:3 * W]
        slab = x[:, 3 * W:3 * W + 128]
        gd = x[:, 3 * W + 128:]
        slab = jnp.where(lane < 64, jnp.tanh(slab), slab)
        wa = _dot(slab, prm["wwa"][...])
        logw = -math.exp(-0.5) * _sigmoid(prm["w0"][...] + wa[:, :W])
        a = _sigmoid(prm["a0"][...] + wa[:, W:])
        g = _dot(_sigmoid(gd), prm["gup"][...])
        if vf is not None:
            mix = _sigmoid(prm["v0"][...] + _dot(_dot(v, prm["vd"][...]), prm["vu"][...]))
            v = v + (vf[rows] - v) * mix
        else:
            vout_ref[pl.ds(b * T, T), :] = v
        kkr = k * prm["kk"][...]
        kk = kkr / jnp.maximum(jnp.sqrt(_head_sum(kkr * kkr, ones_bd)), 1e-12)
        k2 = k * (1.0 + (a - 1.0) * prm["ka"][...])
        return dict(r=r, v=v, kk=kk, k2=k2, bhat=kk * a, logw=logw, g=g)

    ri = lax.broadcasted_iota(jnp.int32, (W, W), 0)
    ci = lax.broadcasted_iota(jnp.int32, (W, W), 1)
    tw = lax.broadcasted_iota(jnp.int32, (C, W), 0)
    sw = lax.broadcasted_iota(jnp.int32, (C, W), 1) % C
    tri_s = lax.broadcasted_iota(jnp.int32, (C, 4 * C), 1)
    tri = jnp.where((tri_s % C <= lax.broadcasted_iota(jnp.int32, (C, 4 * C), 0)) & (tri_s < 3 * C),
                    1.0, 0.0).astype(BF16)

    def stack(t):
        return jnp.concatenate([t.astype(BF16)] * A_HEADS, axis=0) * ones_bd

    def chunk_inputs(tok, i):
        sl = slice(i * C, (i + 1) * C)
        lw = tok["logw"][sl]
        bhat, k2 = tok["bhat"][sl], tok["k2"][sl]
        cl = _const_dot(tri, lw)
        cl_last = cl[C - 1:C, :]
        e_neg = jnp.exp(-cl)
        e_last = jnp.exp(cl_last - cl)
        at = -tok["kk"][sl] * jnp.exp(cl - lw)
        rt = tok["r"][sl] * jnp.exp(cl)
        v_s = stack(tok["v"][sl])
        aa = _dot_nt(jnp.concatenate([at, rt], axis=0),
                     jnp.concatenate([stack(bhat * e_neg), stack(k2 * e_neg)], axis=0))
        a_ab = stack(jnp.where(sw < tw, aa[:C, :W], 0.0))
        a_ak = stack(jnp.where(sw < tw, aa[:C, W:], 0.0))
        a_rb = jnp.where(sw <= tw, aa[C:, :W], 0.0).astype(BF16)
        a_rk = jnp.where(sw <= tw, aa[C:, W:], 0.0).astype(BF16)
        return dict(a_ab=a_ab, at_s=stack(at), aakv=_mm(a_ak, v_s).astype(BF16), a_rb=a_rb, a_rk=a_rk,
                    v_s=v_s, btil=stack(bhat * e_last), ktil=stack(k2 * e_last), rt=rt,
                    gam=jnp.exp(cl_last))

    toks, pre = [], []
    for b in range(rows_total // T):
        toks.append(token_stage(b))
        yield
        for c in range(per):
            pre.append(chunk_inputs(toks[-1], c))
            yield

    box = {}
    yield from _unit_lower_inverses([p["a_ab"] for p in pre], ri, ci, box)
    tinv = box["tinv"]
    w_t = [_mm(t, p["at_s"]).astype(BF16) for t, p in zip(tinv, pre)]
    yield
    u_t = [_mm(t, p["aakv"]).astype(BF16) for t, p in zip(tinv, pre)]
    yield
    m_e = [_mm_tn(p["btil"], w).astype(BF16) for p, w in zip(pre, w_t)]
    yield
    n_mat = [_mm_tn(p["btil"], x) + _mm_tn(p["ktil"], p["v_s"]) for p, x in zip(pre, u_t)]
    yield
    q_mat = [p["rt"] + _mm(p["a_rb"], w) for p, w in zip(pre, w_t)]
    p_mat = [_mm(p["a_rb"], x) + _mm(p["a_rk"], p["v_s"]) for p, x in zip(pre, u_t)]
    g_col = [jnp.sum(jnp.where(ri == ci, p["gam"], 0.0), axis=1, keepdims=True) for p in pre]
    yield
    st = state_sc[...]
    outs = []
    for i in range(n_chunks):
        sb = st.astype(BF16)
        outs.append(_mm(q_mat[i].astype(BF16), sb) + p_mat[i])
        st = g_col[i] * st + _mm(m_e[i], sb) + n_mat[i]
        yield
    state_sc[...] = st

    cat = lambda name: jnp.concatenate([t[name] for t in toks], axis=0)
    o = jnp.concatenate(outs, axis=0)
    mean = _head_sum(o, ones_bd) * (1.0 / N)
    d = o - mean
    var = _head_sum(d * d, ones_bd) * (1.0 / N)
    o = d * lax.rsqrt(var + A_GN_EPS) * prm["lnw"][...] + prm["lnb"][...]
    bonus = _head_sum(cat("r") * cat("k2") * prm["rk"][...], ones_bd) * cat("v")
    y_ref[...] = (o + bonus) * cat("g")


def _pool_windows(u, buf, row0, *, pad):
    rows = u.shape[0]
    buf[B_HALO:B_HALO + rows, :] = u
    lane_group = lax.broadcasted_iota(jnp.int32, (1, B_WIDTH), 1) // B_GROUP_DIM
    acc = u
    win = jnp.zeros_like(u)
    for s in range(1, max(B_WINDOWS)):
        acc = acc + buf[B_HALO - s:B_HALO - s + rows, :]
        if s + 1 in B_WINDOWS:
            win = jnp.where(lane_group == B_WINDOWS.index(s + 1), acc, win)
    wlane = jnp.zeros((1, B_WIDTH), jnp.int32)
    for gi, w in enumerate(B_WINDOWS):
        wlane = jnp.where(lane_group == gi, w, wlane)
    t_real = row0 + lax.broadcasted_iota(jnp.int32, (rows, 1), 0) - pad
    cnt = jnp.minimum(jnp.maximum(t_real + 1, 1), wlane).astype(F32)
    buf[0:B_HALO, :] = u[rows - B_HALO:, :]
    return win / cnt - u


def _swa_constants(pad):
    T = SEQ_TILE
    dist = T + np.arange(T)[:, None] - np.arange(2 * T)[None, :]
    band = (dist >= 0) & (dist < T)
    n = -(-pad // T) + 2
    bias = np.empty((n, C_KV_HEADS, C_GROUP * T, 2 * T), np.float32)
    for j in range(n):
        ok = band & ((j - 1) * T + np.arange(2 * T)[None, :] >= pad)
        for hq in range(C_HEADS):
            slope = np.float32(2.0 ** (-8.0 * (hq + 1) / C_HEADS))
            rows = slice((hq % C_GROUP) * T, (hq % C_GROUP + 1) * T)
            bias[j, hq // C_GROUP, rows] = np.where(ok, -slope * dist.astype(np.float32), MASK_VALUE)
    return jnp.asarray(bias)


def _swa_tile(y_ref, blk, kv_prev, sinks_ref, bias_ref, j, *, pad):
    T = SEQ_TILE
    G = C_GROUP
    kv = blk[:, C_WIDTH:]
    kw = jnp.concatenate([kv_prev[:, :C_KV_WIDTH], kv[:, :C_KV_WIDTH]], axis=0)
    vw = jnp.concatenate([kv_prev[:, C_KV_WIDTH:], kv[:, C_KV_WIDTH:]], axis=0)
    lane = lax.broadcasted_iota(jnp.int32, (1, 2 * C_HEAD_DIM), 1)
    low = lane < C_HEAD_DIM
    kw_sw = pltpu.roll(kw, C_HEAD_DIM, axis=1)
    vw_sw = pltpu.roll(vw, C_HEAD_DIM, axis=1)
    head = lax.broadcasted_iota(jnp.int32, (G * T, 1), 0) // T
    variant = jnp.minimum(j, bias_ref.shape[0] - 1)
    tiles = []
    for hk in range(C_KV_HEADS):
        k2 = jnp.where(low == (hk == 0), kw, kw_sw).astype(BF16)
        v2 = jnp.where(low == (hk == 0), vw, vw_sw).astype(BF16)
        q_rows = []
        sink = jnp.zeros((G * T, 1), F32)
        for gq in range(G):
            hq = hk * G + gq
            pair = blk[:, (hq // 2) * 128:(hq // 2 + 1) * 128]
            q_rows.append(jnp.where(low == (hq % 2 == 0), pair * (C_HEAD_DIM ** -0.5), 0.0))
            sink = jnp.where(head == gq, sinks_ref[hq], sink)
        s = _dot_nt(jnp.concatenate(q_rows, axis=0), k2)
        logits = s + bias_ref[variant, hk]
        m = jnp.maximum(jnp.max(logits, axis=-1, keepdims=True), sink)
        p = jnp.exp(logits - m)
        denom = jnp.sum(p, axis=-1, keepdims=True) + jnp.exp(sink - m)
        yield
        o = _dot(p, v2) / denom
        for pr in range(G // 2):
            tiles.append(jnp.where(low, o[2 * pr * T:(2 * pr + 1) * T], o[(2 * pr + 1) * T:(2 * pr + 2) * T]))
        yield
    y_ref[...] = jnp.concatenate(tiles, axis=1)


def _hgrn_tile(y_ref, u, state_sc, lb_ref, ng_ref, sums_ref, level_ref, j, *, pad):
    T = SEQ_TILE
    N = D_KEY_DIM
    lb = lb_ref[...]
    q = _silu(u[:, :D_WIDTH])
    fpre = u[:, D_WIDTH:2 * D_WIDTH]
    vin = u[:, 2 * D_WIDTH:3 * D_WIDTH]
    gate = u[:, 3 * D_WIDTH:]
    sig = _sigmoid(fpre)
    f = lb + (1.0 - lb) * sig
    t_real = j * T + lax.broadcasted_iota(jnp.int32, (T, 1), 0) - pad
    logf = jnp.where(t_real >= 0, jnp.log(jnp.maximum(f, 1e-30)), 0.0)
    kx = (1.0 - lb) * (1.0 - sig)

    all_sums = _mm(sums_ref[...], jnp.concatenate(_split2(logf), axis=0))
    sums = lambda i: all_sums[i * T:(i + 1) * T]
    same_head = (lax.broadcasted_iota(jnp.int32, (D_WIDTH, D_WIDTH), 0) // N
                 == lax.broadcasted_iota(jnp.int32, (D_WIDTH, D_WIDTH), 1) // N)
    head_rows = jnp.where(lax.broadcasted_iota(jnp.int32, (D_HEADS * T, D_WIDTH), 0) // T
                          == lax.broadcasted_iota(jnp.int32, (D_HEADS * T, D_WIDTH), 1) // N,
                          1.0, 0.0).astype(BF16)
    stack = lambda t: jnp.concatenate([t.astype(BF16)] * D_HEADS, axis=0) * head_rows
    k_st = stack(kx)
    v_st = stack(vin)
    lv = level_ref[...]
    att = jnp.where(lv == 0, _dot_nt(q, k_st), 0.0)
    for l in range(D_LEVELS):
        if l == 0:
            q_l, k_l = q * jnp.exp(logf), k_st
        else:
            q_l = q * jnp.exp(sums(2 * l - 2))
            k_l = k_st * jnp.concatenate([jnp.exp(sums(2 * l - 1)).astype(BF16)] * D_HEADS, axis=0)
        att = jnp.where(lv == l + 1, _dot_nt(q_l, k_l), att)
        yield
    b = sums(2 * D_LEVELS - 2)
    b_rev = sums(2 * D_LEVELS - 1)
    st = state_sc[...]
    o = _dot(att, v_st) + _dot_nt(q * jnp.exp(b), st)
    upd = jnp.where(same_head, _dot_tn(vin, kx * jnp.exp(b_rev)), 0.0)
    state_sc[...] = st * jnp.exp(b[T - 1:T, :]) + upd

    ones_bd = _head_ones(D_WIDTH, N)
    ms = _head_sum(o * o, ones_bd) * (1.0 / N)
    y_ref[...] = o * lax.rsqrt(ms + NORM_EPS) * ng_ref[...] * _silu(gate)


def _hgrn_constants():
    T = SEQ_TILE
    t = np.arange(T)[:, None]
    i = np.arange(T)[None, :]
    mats = []
    for l in list(range(1, D_LEVELS)) + [D_LEVELS]:
        h = 2 ** l
        same = (t // h) == (i // h)
        mats += [same & (i <= t), same & (i > t)]
    sums = np.tile(np.concatenate(mats, axis=0).astype(np.float32), (1, 2))
    x = t ^ i
    level = np.where(i == t, 0, np.where(i < t, np.floor(np.log2(np.maximum(x, 1))).astype(np.int64) + 1, -1))
    return jnp.asarray(sums, BF16), jnp.asarray(np.tile(level, (1, D_HEADS)), jnp.int32)


def _merge_ffn_kernel(h_ref, gates_ref, ya_ref, yb_ref, yc_ref, yd_ref, wb_ref, wo_ref, gf_ref, wu_ref,
                      wd_ref, *rest, ff_chunk):
    fg_ref, o_ref = rest if len(rest) == 2 else (None,) + rest
    merged = None
    row = 0
    for bi, y_ref in enumerate((ya_ref, yb_ref, yc_ref, yd_ref)):
        w = y_ref.shape[-1]
        part = gates_ref[:, bi * D_MODEL:(bi + 1) * D_MODEL] * jnp.dot(
            y_ref[...].astype(BF16), wb_ref[row:row + w, :], preferred_element_type=F32)
        merged = part if merged is None else merged + part
        row += w
    h = h_ref[...] + jnp.dot(merged.astype(BF16), wo_ref[...], preferred_element_type=F32)
    zb = _rms(h, gf_ref[...]).astype(BF16)

    def up_proj(c):
        cols = lambda base: wu_ref[:, base + c * ff_chunk:base + (c + 1) * ff_chunk]
        return (jnp.dot(zb, cols(0), preferred_element_type=F32),
                jnp.dot(zb, cols(D_FF), preferred_element_type=F32))

    acc = h
    n_chunks = D_FF // ff_chunk
    gu, up = up_proj(0)
    for c in range(n_chunks):
        nxt = up_proj(c + 1) if c + 1 < n_chunks else None
        acc = acc + jnp.dot((_silu(gu) * up).astype(BF16), wd_ref[c * ff_chunk:(c + 1) * ff_chunk, :],
                            preferred_element_type=F32)
        if nxt is not None:
            gu, up = nxt
    o_ref[...] = acc if fg_ref is None else _rms(acc, fg_ref[...])


def _merge_ffn(h2, proj2, ya, yc, yd, wb, wo, gf, wu, wd, final_g=None):
    m = h2.shape[0]
    tm = ROW_TILE
    rows = lambda w, col=0: pl.BlockSpec((tm, w), lambda i: (i, col))
    const = lambda arr: pl.BlockSpec(arr.shape, lambda i: (0, 0), pipeline_mode=pl.Buffered(1))
    last = () if final_g is None else (final_g,)
    return pl.pallas_call(
        functools.partial(_merge_ffn_kernel, ff_chunk=256),
        grid=(m // tm,),
        in_specs=[rows(D_MODEL), rows(GATE_COLS), rows(A_WIDTH), rows(B_WIDTH, GATE_COLS // B_WIDTH),
                  rows(C_WIDTH), rows(D_WIDTH), const(wb), const(wo), const(gf), const(wu),
                  const(wd)] + [const(g) for g in last],
        out_specs=rows(D_MODEL),
        out_shape=jax.ShapeDtypeStruct((m, D_MODEL), F32),
        compiler_params=pltpu.CompilerParams(dimension_semantics=("arbitrary",),
                                             vmem_limit_bytes=VMEM_LIMIT),
        name="merge_ffn",
    )(h2, proj2, ya, proj2, yc, yd, wb, wo, gf, wu, wd, *last)


def _block_diag(blocks):
    n, r, c = blocks.shape
    out = jnp.zeros((n * r, n * c), blocks.dtype)
    for i in range(n):
        out = out.at[i * r:(i + 1) * r, i * c:(i + 1) * c].set(blocks[i])
    return out


def kernel(x, meta, norm_mix, norm_ffn, norm_final, w_in, w_branch, w_out, a_mu, a_w_up, a_w0, a_a_up, a_a0, a_g_up, a_kk, a_ka, a_rk, a_ln_w, a_ln_b, a_vres_down, a_vres_up, a_vres0, b_mix, b_scale, c_sinks, d_lower_bounds, d_norm, w_ffn_up, w_ffn_down):
    bsz, seq, _ = x.shape
    depth = w_in.shape[0]
    T = SEQ_TILE
    L = N_META + seq
    pad = (-L) % T
    lp = L + pad
    assert (pad + N_META) % T == 0 and seq % T == 0
    h = jnp.concatenate([jnp.zeros((bsz, pad, D_MODEL), F32),
                         jnp.broadcast_to(meta.astype(F32)[None], (bsz, N_META, D_MODEL)),
                         x.astype(F32)], axis=1).reshape(bsz * lp, D_MODEL)
    lb_w = jax.nn.softmax(d_lower_bounds.astype(F32), axis=0)
    lb_table = jnp.cumsum(lb_w, axis=0) - lb_w[0]
    row2 = lambda t: t.reshape(1, -1).astype(F32)
    v_first = None
    for l in range(depth):
        wwa = jnp.zeros((128, 2 * A_WIDTH), F32)
        wwa = wwa.at[:64, :A_WIDTH].set(a_w_up[l]).at[64:, A_WIDTH:].set(a_a_up[l])
        pa = dict(mu=row2(a_mu[l]), wwa=wwa.astype(BF16), w0=row2(a_w0[l]), a0=row2(a_a0[l]),
                  gup=a_g_up[l].astype(BF16), kk=row2(a_kk[l]), ka=row2(a_ka[l]), rk=row2(a_rk[l]),
                  lnw=row2(a_ln_w[l]), lnb=row2(a_ln_b[l]))
        if l > 0:
            pa.update(vd=a_vres_down[l - 1].astype(BF16), vu=a_vres_up[l - 1].astype(BF16),
                      v0=row2(a_vres0[l - 1]))
        outs = _in_proj(h, v_first, row2(norm_mix[l]), w_in[l].astype(BF16),
                        _block_diag(b_mix[l]).astype(BF16), row2(b_scale[l]), c_sinks[l].astype(F32),
                        row2(lb_table[l]), row2(d_norm[l]), pa, pad=pad, lp=lp)
        proj2, y_a, y_c, y_d = outs[:4]
        if l == 0:
            v_first = outs[4]
        h = _merge_ffn(h, proj2, y_a, y_c, y_d,
                       w_branch[l].astype(BF16), w_out[l].astype(BF16), row2(norm_ffn[l]),
                       w_ffn_up[l].astype(BF16), w_ffn_down[l].astype(BF16),
                       row2(norm_final) if l == depth - 1 else None)
    return h.reshape(bsz, lp, D_MODEL)[:, pad + N_META:]
```

```python
import functools
import math

import jax
import jax.numpy as jnp
import numpy as np
from jax import lax
from jax.experimental import pallas as pl
from jax.experimental.pallas import tpu as pltpu

F32 = jnp.float32
BF16 = jnp.bfloat16

D_MODEL = 1024
N_META = 16
NORM_EPS = 1e-6
MASK_VALUE = -1e30

A_HEADS = 4
A_HEAD_DIM = 64
A_WIDTH = 256
A_GN_EPS = 64e-5
A_COLS = 1024
A_CHUNK = 64

B_WIDTH = 256
B_GROUP_DIM = 64
B_WINDOWS = (2, 4, 8, 16)
B_HALO = 16

C_HEADS = 8
C_KV_HEADS = 2
C_GROUP = 4
C_HEAD_DIM = 64
C_WIDTH = 512
C_KV_WIDTH = 128
C_COLS = 768

D_HEADS = 4
D_KEY_DIM = 64
D_WIDTH = 256
D_COLS = 1024
D_LEVELS = 7

D_FF = 2816
GATE_COLS = 4 * D_MODEL
OFF_A = GATE_COLS
OFF_B = OFF_A + A_COLS
OFF_C = OFF_B + B_WIDTH
IN_COLS = OFF_C + C_COLS + D_COLS

SEQ_TILE = 128
ROW_TILE = 384
DENSE_STEP = 256
V7X_VMEM_BYTES = 64 * 1024 * 1024
VMEM_LIMIT = V7X_VMEM_BYTES - 4 * 1024 * 1024


def _dot(a, b):
    return jnp.dot(a.astype(BF16), b.astype(BF16), preferred_element_type=F32)


def _dot_nt(a, b):
    return lax.dot_general(a.astype(BF16), b.astype(BF16), (((1,), (1,)), ((), ())),
                           preferred_element_type=F32)


def _dot_tn(a, b):
    return lax.dot_general(a.astype(BF16), b.astype(BF16), (((0,), (0,)), ((), ())),
                           preferred_element_type=F32)


def _round_robin(*gens):
    gens = list(gens)
    while gens:
        for gen in list(gens):
            if next(gen, "done") == "done":
                gens.remove(gen)
        yield


def _interleave(*gens):
    for _ in _round_robin(*gens):
        pass


def _split2(x):
    hi = x.astype(BF16)
    lo = (x - hi.astype(F32)).astype(BF16)
    return hi, lo


def _split3(x):
    hi = x.astype(BF16)
    r = x - hi.astype(F32)
    mid = r.astype(BF16)
    lo = (r - mid.astype(F32)).astype(BF16)
    return hi, mid, lo


def _const_dot(c4, x):
    hi, mid, lo = _split3(x)
    return jnp.dot(c4, jnp.concatenate([hi, mid, lo, jnp.zeros_like(hi)], axis=0),
                   preferred_element_type=F32)


def _mm(a, b):
    return jnp.dot(a, b, preferred_element_type=F32)


def _mm_tn(a, b):
    return lax.dot_general(a, b, (((0,), (0,)), ((), ())), preferred_element_type=F32)


def _unit_lower_inverses(mats, ri, ci, box):
    in16 = (ri // 16) == (ci // 16)
    in32 = (ri // 32) == (ci // 32)
    diag = ri == ci
    as_mask = lambda cond: jnp.where(cond, 1.0, 0.0).astype(BF16)
    plus_eye = lambda m: jnp.where(diag, 1.0, m).astype(BF16)
    m16, eye = as_mask(in16), as_mask(diag)
    xb = [a * m16 for a in mats]
    t = [xi + eye for xi in xb]
    yield
    for _ in range(3):
        x = [_mm(xi, xi) for xi in xb]
        yield
        xb = [xi.astype(BF16) for xi in x]
        t = [_mm(ti, plus_eye(xi)).astype(BF16) for ti, xi in zip(t, x)]
        yield
    n = mats[0].shape[0]
    for half, sel in ((16, as_mask(in32 & ~in16)), (32, as_mask(~in32))):
        starts = range(0, n, 2 * half)
        low = lambda m: jnp.concatenate([m[r + half:r + 2 * half] for r in starts], axis=0)
        put = lambda rows, base: jnp.concatenate(
            [p for k, r in enumerate(starts) for p in (base[r:r + half], rows[k * half:(k + 1) * half])],
            axis=0)
        lt = [_mm(low(a * sel), ti).astype(BF16) for a, ti in zip(mats, t)]
        yield
        zero = jnp.zeros_like(t[0])
        t = [put(_mm(low(ti), put(li, zero) + eye).astype(BF16), ti) for ti, li in zip(t, lt)]
        yield
    box["tinv"] = t


def _head_sum(x, ones_bd):
    return jnp.dot(x.astype(BF16), ones_bd, preferred_element_type=F32)


def _head_ones(width, seg):
    r = lax.broadcasted_iota(jnp.int32, (width, width), 0) // seg
    c = lax.broadcasted_iota(jnp.int32, (width, width), 1) // seg
    return jnp.where(r == c, 1.0, 0.0).astype(BF16)


def _sigmoid(x):
    return 1.0 / (1.0 + jnp.exp(-x))


def _silu(x):
    return x * _sigmoid(x)


def _rms(x, g):
    return x * lax.rsqrt(jnp.mean(x * x, axis=-1, keepdims=True) + NORM_EPS) * g


A_PARAMS = ("mu", "wwa", "w0", "a0", "gup", "kk", "ka", "rk", "lnw", "lnb")
A_VRES_PARAMS = ("vd", "vu", "v0")


def _in_proj_kernel(*refs, has_vres, pad, tiles_per_seq, tm, tn):
    refs = list(refs)
    sinks_ref, x_ref = refs[:2]
    vf_ref = refs[2] if has_vres else None
    k0 = 3 if has_vres else 2
    g_ref, w_ref, mix_ref, scale_ref, bias_ref, lb_ref, ng_ref, sums_ref, level_ref = refs[k0:k0 + 9]
    names = A_PARAMS + (A_VRES_PARAMS if has_vres else ())
    prm = dict(zip(names, refs[k0 + 9:k0 + 9 + len(names)]))
    outs = refs[k0 + 9 + len(names):]
    o_ref, ya_ref, yc_ref, yd_ref = outs[:4]
    vout_ref = None if has_vres else outs[4]
    pool_buf, kv_sc, state_sc, a_state_sc, a_carry_sc = outs[-5:]
    T = SEQ_TILE
    tile_in_seq = pl.program_id(0) % tiles_per_seq

    @pl.when(tile_in_seq == 0)
    def _():
        pool_buf[0:B_HALO, :] = jnp.zeros((B_HALO, B_WIDTH), F32)
        for ref in (kv_sc, state_sc, a_state_sc, a_carry_sc):
            ref[...] = jnp.zeros_like(ref)

    z = _rms(x_ref[...], g_ref[...])
    row = tile_in_seq * tm + lax.broadcasted_iota(jnp.int32, (tm, 1), 0)
    zb = jnp.where(row >= pad, z, 0.0).astype(BF16)
    mixer_cols = jnp.dot(zb, w_ref[:, OFF_A:], preferred_element_type=F32)
    u_a = mixer_cols[:, :A_COLS]
    u_b = mixer_cols[:, OFF_B - OFF_A:OFF_C - OFF_A]
    cd = mixer_cols[:, OFF_C - OFF_A:]

    def dense():
        for n, c0 in enumerate(range(0, GATE_COLS, tn)):
            for lo in range(c0, c0 + tn, DENSE_STEP):
                cols = jnp.dot(zb, w_ref[:, lo:lo + DENSE_STEP], preferred_element_type=F32)
                o_ref[:, lo:lo + DENSE_STEP] = _sigmoid(cols)
                yield
            if n < tm // T:
                rows = slice(n * T, (n + 1) * T)
                pooled = _pool_windows(u_b[rows], pool_buf, tile_in_seq * tm + rows.start, pad=pad)
                o_ref[rows, GATE_COLS:] = _dot(pooled, mix_ref[...]) * scale_ref[...]
                yield

    def mixers():
        for k in range(tm // T):
            rows = slice(k * T, (k + 1) * T)
            kv_prev = kv_sc[...] if k == 0 else cd[(k - 1) * T:k * T, C_WIDTH:C_COLS]
            jt = tile_in_seq * (tm // T) + k
            yield from _round_robin(
                _swa_tile(yc_ref.at[pl.ds(k * T, T)], cd[rows, :C_COLS], kv_prev, sinks_ref, bias_ref, jt,
                          pad=pad),
                _hgrn_tile(yd_ref.at[pl.ds(k * T, T)], cd[rows, C_COLS:], state_sc, lb_ref, ng_ref,
                           sums_ref, level_ref, jt, pad=pad))
        kv_sc[...] = cd[tm - T:, C_WIDTH:C_COLS]

    rwkv = _rwkv_tile(ya_ref, vout_ref, u_a, None if vf_ref is None else vf_ref[...], prm, a_state_sc,
                      a_carry_sc)
    _interleave(dense(), rwkv, mixers())


def _in_proj(h2, v_first, g, w, mix_bd, scale, sinks, lb, norm_g, pa, *, pad, lp):
    m = h2.shape[0]
    tm = ROW_TILE
    assert lp % tm == 0 and tm % SEQ_TILE == 0
    has_vres = v_first is not None
    sums, level = _hgrn_constants()
    bias = _swa_constants(pad)
    const = lambda arr: pl.BlockSpec(arr.shape, lambda i: (0,) * arr.ndim, pipeline_mode=pl.Buffered(1))
    rows = lambda w: pl.BlockSpec((tm, w), lambda i: (i, 0))
    names = A_PARAMS + (A_VRES_PARAMS if has_vres else ())
    shape = lambda w: jax.ShapeDtypeStruct((m, w), F32)
    n_out = 4 if has_vres else 5
    widths = (GATE_COLS + B_WIDTH, A_WIDTH, C_WIDTH, D_WIDTH, A_WIDTH)[:n_out]
    return pl.pallas_call(
        functools.partial(_in_proj_kernel, has_vres=has_vres, pad=pad, tiles_per_seq=lp // tm, tm=tm,
                          tn=1024),
        grid=(m // tm,),
        in_specs=[pl.BlockSpec(memory_space=pltpu.SMEM), rows(D_MODEL)]
                 + ([rows(A_WIDTH)] if has_vres else [])
                 + [const(a) for a in (g, w, mix_bd, scale, bias, lb, norm_g, sums, level)]
                 + [const(pa[n]) for n in names],
        out_specs=tuple(rows(w) for w in widths),
        out_shape=tuple(shape(w) for w in widths),
        scratch_shapes=[pltpu.VMEM((B_HALO + tm, B_WIDTH), F32),
                        pltpu.VMEM((SEQ_TILE, 2 * C_KV_WIDTH), F32),
                        pltpu.VMEM((D_WIDTH, D_WIDTH), F32),
                        pltpu.VMEM((A_WIDTH, A_WIDTH), F32),
                        pltpu.VMEM((8, A_COLS), F32)],
        compiler_params=pltpu.CompilerParams(dimension_semantics=("arbitrary",),
                                             vmem_limit_bytes=VMEM_LIMIT),
        name="in_proj_mixers",
    )(sinks, h2, *([v_first] if has_vres else []), g, w, mix_bd, scale, bias, lb, norm_g, sums, level,
      *[pa[n] for n in names])


def _rwkv_tile(y_ref, vout_ref, u, vf, prm, state_sc, carry_sc):
    T = SEQ_TILE
    C = A_CHUNK
    N = A_HEAD_DIM
    W = A_WIDTH
    rows_total = u.shape[0]
    n_chunks = rows_total // C
    per = T // C

    row = lax.broadcasted_iota(jnp.int32, (rows_total, 1), 0)
    prev = jnp.where(row == 0, carry_sc[0:1, :], pltpu.roll(u, 1, axis=0))
    carry_sc[0:1, :] = u[rows_total - 1:rows_total, :]
    ones_bd = _head_ones(W, N)
    lane = lax.broadcasted_iota(jnp.int32, (1, 128), 1)

    def token_stage(b):
        rows = slice(b * T, (b + 1) * T)
        x = u[rows] + (prev[rows] - u[rows]) * prm["mu"][...]
        r = x[:, 0:W]
        k = x[:, W:2 * W]
        v = x[:, 2 * W:3 * W]
        slab = x[:, 3 * W:3 * W + 128]
        gd = x[:, 3 * W + 128:]
        slab = jnp.where(lane < 64, jnp.tanh(slab), slab)
        wa = _dot(slab, prm["wwa"][...])
        logw = -math.exp(-0.5) * _sigmoid(prm["w0"][...] + wa[:, :W])
        a = _sigmoid(prm["a0"][...] + wa[:, W:])
        g = _dot(_sigmoid(gd), prm["gup"][...])
        if vf is not None:
            mix = _sigmoid(prm["v0"][...] + _dot(_dot(v, prm["vd"][...]), prm["vu"][...]))
            v = v + (vf[rows] - v) * mix
        else:
            vout_ref[pl.ds(b * T, T), :] = v
        kkr = k * prm["kk"][...]
        kk = kkr / jnp.maximum(jnp.sqrt(_head_sum(kkr * kkr, ones_bd)), 1e-12)
        k2 = k * (1.0 + (a - 1.0) * prm["ka"][...])
        return dict(r=r, v=v, kk=kk, k2=k2, bhat=kk * a, logw=logw, g=g)

    ri = lax.broadcasted_iota(jnp.int32, (W, W), 0)
    ci = lax.broadcasted_iota(jnp.int32, (W, W), 1)
    tw = lax.broadcasted_iota(jnp.int32, (C, W), 0)
    sw = lax.broadcasted_iota(jnp.int32, (C, W), 1) % C
    tri_s = lax.broadcasted_iota(jnp.int32, (C, 4 * C), 1)
    tri = jnp.where((tri_s % C <= lax.broadcasted_iota(jnp.int32, (C, 4 * C), 0)) & (tri_s < 3 * C),
                    1.0, 0.0).astype(BF16)

    def stack(t):
        return jnp.concatenate([t.astype(BF16)] * A_HEADS, axis=0) * ones_bd

    def chunk_inputs(tok, i):
        sl = slice(i * C, (i + 1) * C)
        lw = tok["logw"][sl]
        bhat, k2 = tok["bhat"][sl], tok["k2"][sl]
        cl = _const_dot(tri, lw)
        cl_last = cl[C - 1:C, :]
        e_neg = jnp.exp(-cl)
        e_last = jnp.exp(cl_last - cl)
        at = -tok["kk"][sl] * jnp.exp(cl - lw)
        rt = tok["r"][sl] * jnp.exp(cl)
        v_s = stack(tok["v"][sl])
        aa = _dot_nt(jnp.concatenate([at, rt], axis=0),
                     jnp.concatenate([stack(bhat * e_neg), stack(k2 * e_neg)], axis=0))
        a_ab = stack(jnp.where(sw < tw, aa[:C, :W], 0.0))
        a_ak = stack(jnp.where(sw < tw, aa[:C, W:], 0.0))
        a_rb = jnp.where(sw <= tw, aa[C:, :W], 0.0).astype(BF16)
        a_rk = jnp.where(sw <= tw, aa[C:, W:], 0.0).astype(BF16)
        return dict(a_ab=a_ab, at_s=stack(at), aakv=_mm(a_ak, v_s).astype(BF16), a_rb=a_rb, a_rk=a_rk,
                    v_s=v_s, btil=stack(bhat * e_last), ktil=stack(k2 * e_last), rt=rt,
                    gam=jnp.exp(cl_last))

    toks, pre = [], []
    for b in range(rows_total // T):
        toks.append(token_stage(b))
        yield
        for c in range(per):
            pre.append(chunk_inputs(toks[-1], c))
            yield

    box = {}
    yield from _unit_lower_inverses([p["a_ab"] for p in pre], ri, ci, box)
    tinv = box["tinv"]
    w_t = [_mm(t, p["at_s"]).astype(BF16) for t, p in zip(tinv, pre)]
    yield
    u_t = [_mm(t, p["aakv"]).astype(BF16) for t, p in zip(tinv, pre)]
    yield
    m_e = [_mm_tn(p["btil"], w).astype(BF16) for p, w in zip(pre, w_t)]
    yield
    n_mat = [_mm_tn(p["btil"], x) + _mm_tn(p["ktil"], p["v_s"]) for p, x in zip(pre, u_t)]
    yield
    q_mat = [p["rt"] + _mm(p["a_rb"], w) for p, w in zip(pre, w_t)]
    p_mat = [_mm(p["a_rb"], x) + _mm(p["a_rk"], p["v_s"]) for p, x in zip(pre, u_t)]
    g_col = [jnp.sum(jnp.where(ri == ci, p["gam"], 0.0), axis=1, keepdims=True) for p in pre]
    yield
    st = state_sc[...]
    outs = []
    for i in range(n_chunks):
        sb = st.astype(BF16)
        outs.append(_mm(q_mat[i].astype(BF16), sb) + p_mat[i])
        st = g_col[i] * st + _mm(m_e[i], sb) + n_mat[i]
        yield
    state_sc[...] = st

    cat = lambda name: jnp.concatenate([t[name] for t in toks], axis=0)
    o = jnp.concatenate(outs, axis=0)
    mean = _head_sum(o, ones_bd) * (1.0 / N)
    d = o - mean
    var = _head_sum(d * d, ones_bd) * (1.0 / N)
    o = d * lax.rsqrt(var + A_GN_EPS) * prm["lnw"][...] + prm["lnb"][...]
    bonus = _head_sum(cat("r") * cat("k2") * prm["rk"][...], ones_bd) * cat("v")
    y_ref[...] = (o + bonus) * cat("g")


def _pool_windows(u, buf, row0, *, pad):
    rows = u.shape[0]
    buf[B_HALO:B_HALO + rows, :] = u
    lane_group = lax.broadcasted_iota(jnp.int32, (1, B_WIDTH), 1) // B_GROUP_DIM
    acc = u
    win = jnp.zeros_like(u)
    for s in range(1, max(B_WINDOWS)):
        acc = acc + buf[B_HALO - s:B_HALO - s + rows, :]
        if s + 1 in B_WINDOWS:
            win = jnp.where(lane_group == B_WINDOWS.index(s + 1), acc, win)
    wlane = jnp.zeros((1, B_WIDTH), jnp.int32)
    for gi, w in enumerate(B_WINDOWS):
        wlane = jnp.where(lane_group == gi, w, wlane)
    t_real = row0 + lax.broadcasted_iota(jnp.int32, (rows, 1), 0) - pad
    cnt = jnp.minimum(jnp.maximum(t_real + 1, 1), wlane).astype(F32)
    buf[0:B_HALO, :] = u[rows - B_HALO:, :]
    return win / cnt - u


def _swa_constants(pad):
    T = SEQ_TILE
    dist = T + np.arange(T)[:, None] - np.arange(2 * T)[None, :]
    band = (dist >= 0) & (dist < T)
    n = -(-pad // T) + 2
    bias = np.empty((n, C_KV_HEADS, C_GROUP * T, 2 * T), np.float32)
    for j in range(n):
        ok = band & ((j - 1) * T + np.arange(2 * T)[None, :] >= pad)
        for hq in range(C_HEADS):
            slope = np.float32(2.0 ** (-8.0 * (hq + 1) / C_HEADS))
            rows = slice((hq % C_GROUP) * T, (hq % C_GROUP + 1) * T)
            bias[j, hq // C_GROUP, rows] = np.where(ok, -slope * dist.astype(np.float32), MASK_VALUE)
    return jnp.asarray(bias)


def _swa_tile(y_ref, blk, kv_prev, sinks_ref, bias_ref, j, *, pad):
    T = SEQ_TILE
    G = C_GROUP
    kv = blk[:, C_WIDTH:]
    kw = jnp.concatenate([kv_prev[:, :C_KV_WIDTH], kv[:, :C_KV_WIDTH]], axis=0)
    vw = jnp.concatenate([kv_prev[:, C_KV_WIDTH:], kv[:, C_KV_WIDTH:]], axis=0)
    lane = lax.broadcasted_iota(jnp.int32, (1, 2 * C_HEAD_DIM), 1)
    low = lane < C_HEAD_DIM
    kw_sw = pltpu.roll(kw, C_HEAD_DIM, axis=1)
    vw_sw = pltpu.roll(vw, C_HEAD_DIM, axis=1)
    head = lax.broadcasted_iota(jnp.int32, (G * T, 1), 0) // T
    variant = jnp.minimum(j, bias_ref.shape[0] - 1)
    tiles = []
    for hk in range(C_KV_HEADS):
        k2 = jnp.where(low == (hk == 0), kw, kw_sw).astype(BF16)
        v2 = jnp.where(low == (hk == 0), vw, vw_sw).astype(BF16)
        q_rows = []
        sink = jnp.zeros((G * T, 1), F32)
        for gq in range(G):
            hq = hk * G + gq
            pair = blk[:, (hq // 2) * 128:(hq // 2 + 1) * 128]
            q_rows.append(jnp.where(low == (hq % 2 == 0), pair * (C_HEAD_DIM ** -0.5), 0.0))
            sink = jnp.where(head == gq, sinks_ref[hq], sink)
        s = _dot_nt(jnp.concatenate(q_rows, axis=0), k2)
        logits = s + bias_ref[variant, hk]
        m = jnp.maximum(jnp.max(logits, axis=-1, keepdims=True), sink)
        p = jnp.exp(logits - m)
        denom = jnp.sum(p, axis=-1, keepdims=True) + jnp.exp(sink - m)
        yield
        o = _dot(p, v2) / denom
        for pr in range(G // 2):
            tiles.append(jnp.where(low, o[2 * pr * T:(2 * pr + 1) * T], o[(2 * pr + 1) * T:(2 * pr + 2) * T]))
        yield
    y_ref[...] = jnp.concatenate(tiles, axis=1)


def _hgrn_tile(y_ref, u, state_sc, lb_ref, ng_ref, sums_ref, level_ref, j, *, pad):
    T = SEQ_TILE
    N = D_KEY_DIM
    lb = lb_ref[...]
    q = _silu(u[:, :D_WIDTH])
    fpre = u[:, D_WIDTH:2 * D_WIDTH]
    vin = u[:, 2 * D_WIDTH:3 * D_WIDTH]
    gate = u[:, 3 * D_WIDTH:]
    sig = _sigmoid(fpre)
    f = lb + (1.0 - lb) * sig
    t_real = j * T + lax.broadcasted_iota(jnp.int32, (T, 1), 0) - pad
    logf = jnp.where(t_real >= 0, jnp.log(jnp.maximum(f, 1e-30)), 0.0)
    kx = (1.0 - lb) * (1.0 - sig)

    all_sums = _mm(sums_ref[...], jnp.concatenate(_split2(logf), axis=0))
    sums = lambda i: all_sums[i * T:(i + 1) * T]
    same_head = (lax.broadcasted_iota(jnp.int32, (D_WIDTH, D_WIDTH), 0) // N
                 == lax.broadcasted_iota(jnp.int32, (D_WIDTH, D_WIDTH), 1) // N)
    head_rows = jnp.where(lax.broadcasted_iota(jnp.int32, (D_HEADS * T, D_WIDTH), 0) // T
                          == lax.broadcasted_iota(jnp.int32, (D_HEADS * T, D_WIDTH), 1) // N,
                          1.0, 0.0).astype(BF16)
    stack = lambda t: jnp.concatenate([t.astype(BF16)] * D_HEADS, axis=0) * head_rows
    k_st = stack(kx)
    v_st = stack(vin)
    lv = level_ref[...]
    att = jnp.where(lv == 0, _dot_nt(q, k_st), 0.0)
    for l in range(D_LEVELS):
        if l == 0:
            q_l, k_l = q * jnp.exp(logf), k_st
        else:
            q_l = q * jnp.exp(sums(2 * l - 2))
            k_l = k_st * jnp.concatenate([jnp.exp(sums(2 * l - 1)).astype(BF16)] * D_HEADS, axis=0)
        att = jnp.where(lv == l + 1, _dot_nt(q_l, k_l), att)
        yield
    b = sums(2 * D_LEVELS - 2)
    b_rev = sums(2 * D_LEVELS - 1)
    st = state_sc[...]
    o = _dot(att, v_st) + _dot_nt(q * jnp.exp(b), st)
    upd = jnp.where(same_head, _dot_tn(vin, kx * jnp.exp(b_rev)), 0.0)
    state_sc[...] = st * jnp.exp(b[T - 1:T, :]) + upd

    ones_bd = _head_ones(D_WIDTH, N)
    ms = _head_sum(o * o, ones_bd) * (1.0 / N)
    y_ref[...] = o * lax.rsqrt(ms + NORM_EPS) * ng_ref[...] * _silu(gate)


def _hgrn_constants():
    T = SEQ_TILE
    t = np.arange(T)[:, None]
    i = np.arange(T)[None, :]
    mats = []
    for l in list(range(1, D_LEVELS)) + [D_LEVELS]:
        h = 2 ** l
        same = (t // h) == (i // h)
        mats += [same & (i <= t), same & (i > t)]
    sums = np.tile(np.concatenate(mats, axis=0).astype(np.float32), (1, 2))
    x = t ^ i
    level = np.where(i == t, 0, np.where(i < t, np.floor(np.log2(np.maximum(x, 1))).astype(np.int64) + 1, -1))
    return jnp.asarray(sums, BF16), jnp.asarray(np.tile(level, (1, D_HEADS)), jnp.int32)


def _merge_ffn_kernel(h_ref, gates_ref, ya_ref, yb_ref, yc_ref, yd_ref, wb_ref, wo_ref, gf_ref, wu_ref,
                      wd_ref, *rest, ff_chunk):
    fg_ref, o_ref = rest if len(rest) == 2 else (None,) + rest
    merged = None
    row = 0
    for bi, y_ref in enumerate((ya_ref, yb_ref, yc_ref, yd_ref)):
        w = y_ref.shape[-1]
        part = gates_ref[:, bi * D_MODEL:(bi + 1) * D_MODEL] * jnp.dot(
            y_ref[...].astype(BF16), wb_ref[row:row + w, :], preferred_element_type=F32)
        merged = part if merged is None else merged + part
        row += w
    h = h_ref[...] + jnp.dot(merged.astype(BF16), wo_ref[...], preferred_element_type=F32)
    zb = _rms(h, gf_ref[...]).astype(BF16)

    def up_proj(c):
        cols = lambda base: wu_ref[:, base + c * ff_chunk:base + (c + 1) * ff_chunk]
        return (jnp.dot(zb, cols(0), preferred_element_type=F32),
                jnp.dot(zb, cols(D_FF), preferred_element_type=F32))

    acc = h
    n_chunks = D_FF // ff_chunk
    gu, up = up_proj(0)
    for c in range(n_chunks):
        nxt = up_proj(c + 1) if c + 1 < n_chunks else None
        acc = acc + jnp.dot((_silu(gu) * up).astype(BF16), wd_ref[c * ff_chunk:(c + 1) * ff_chunk, :],
                            preferred_element_type=F32)
        if nxt is not None:
            gu, up = nxt
    o_ref[...] = acc if fg_ref is None else _rms(acc, fg_ref[...])


def _merge_ffn(h2, proj2, ya, yc, yd, wb, wo, gf, wu, wd, final_g=None):
    m = h2.shape[0]
    tm = ROW_TILE
    rows = lambda w, col=0: pl.BlockSpec((tm, w), lambda i: (i, col))
    const = lambda arr: pl.BlockSpec(arr.shape, lambda i: (0, 0), pipeline_mode=pl.Buffered(1))
    last = () if final_g is None else (final_g,)
    return pl.pallas_call(
        functools.partial(_merge_ffn_kernel, ff_chunk=256),
        grid=(m // tm,),
        in_specs=[rows(D_MODEL), rows(GATE_COLS), rows(A_WIDTH), rows(B_WIDTH, GATE_COLS // B_WIDTH),
                  rows(C_WIDTH), rows(D_WIDTH), const(wb), const(wo), const(gf), const(wu),
                  const(wd)] + [const(g) for g in last],
        out_specs=rows(D_MODEL),
        out_shape=jax.ShapeDtypeStruct((m, D_MODEL), F32),
        compiler_params=pltpu.CompilerParams(dimension_semantics=("arbitrary",),
                                             vmem_limit_bytes=VMEM_LIMIT),
        name="merge_ffn",
    )(h2, proj2, ya, proj2, yc, yd, wb, wo, gf, wu, wd, *last)


def _block_diag(blocks):
    n, r, c = blocks.shape
    out = jnp.zeros((n * r, n * c), blocks.dtype)
    for i in range(n):
        out = out.at[i * r:(i + 1) * r, i * c:(i + 1) * c].set(blocks[i])
    return out


def kernel(x, meta, norm_mix, norm_ffn, norm_final, w_in, w_branch, w_out, a_mu, a_w_up, a_w0, a_a_up, a_a0, a_g_up, a_kk, a_ka, a_rk, a_ln_w, a_ln_b, a_vres_down, a_vres_up, a_vres0, b_mix, b_scale, c_sinks, d_lower_bounds, d_norm, w_ffn_up, w_ffn_down):
    bsz, seq, _ = x.shape
    depth = w_in.shape[0]
    T = SEQ_TILE
    L = N_META + seq
    pad = (-L) % T
    lp = L + pad
    assert (pad + N_META) % T == 0 and seq % T == 0
    h = jnp.concatenate([jnp.zeros((bsz, pad, D_MODEL), F32),
                         jnp.broadcast_to(meta.astype(F32)[None], (bsz, N_META, D_MODEL)),
                         x.astype(F32)], axis=1).reshape(bsz * lp, D_MODEL)
    lb_w = jax.nn.softmax(d_lower_bounds.astype(F32), axis=0)
    lb_table = jnp.cumsum(lb_w, axis=0) - lb_w[0]
    row2 = lambda t: t.reshape(1, -1).astype(F32)
    v_first = None
    for l in range(depth):
        wwa = jnp.zeros((128, 2 * A_WIDTH), F32)
        wwa = wwa.at[:64, :A_WIDTH].set(a_w_up[l]).at[64:, A_WIDTH:].set(a_a_up[l])
        pa = dict(mu=row2(a_mu[l]), wwa=wwa.astype(BF16), w0=row2(a_w0[l]), a0=row2(a_a0[l]),
                  gup=a_g_up[l].astype(BF16), kk=row2(a_kk[l]), ka=row2(a_ka[l]), rk=row2(a_rk[l]),
                  lnw=row2(a_ln_w[l]), lnb=row2(a_ln_b[l]))
        if l > 0:
            pa.update(vd=a_vres_down[l - 1].astype(BF16), vu=a_vres_up[l - 1].astype(BF16),
                      v0=row2(a_vres0[l - 1]))
        outs = _in_proj(h, v_first, row2(norm_mix[l]), w_in[l].astype(BF16),
                        _block_diag(b_mix[l]).astype(BF16), row2(b_scale[l]), c_sinks[l].astype(F32),
                        row2(lb_table[l]), row2(d_norm[l]), pa, pad=pad, lp=lp)
        proj2, y_a, y_c, y_d = outs[:4]
        if l == 0:
            v_first = outs[4]
        h = _merge_ffn(h, proj2, y_a, y_c, y_d,
                       w_branch[l].astype(BF16), w_out[l].astype(BF16), row2(norm_ffn[l]),
                       w_ffn_up[l].astype(BF16), w_ffn_down[l].astype(BF16),
                       row2(norm_final) if l == depth - 1 else None)
    return h.reshape(bsz, lp, D_MODEL)[:, pad + N_META:]
```

```python
import functools
import math

import jax
import jax.numpy as jnp
import numpy as np
from jax import lax
from jax.experimental import pallas as pl
from jax.experimental.pallas import tpu as pltpu

F32 = jnp.float32
BF16 = jnp.bfloat16

D_MODEL = 1024
N_META = 16
NORM_EPS = 1e-6
MASK_VALUE = -1e30

A_HEADS = 4
A_HEAD_DIM = 64
A_WIDTH = 256
A_GN_EPS = 64e-5
A_COLS = 1024
A_CHUNK = 64

B_WIDTH = 256
B_GROUP_DIM = 64
B_WINDOWS = (2, 4, 8, 16)
B_HALO = 16

C_HEADS = 8
C_KV_HEADS = 2
C_GROUP = 4
C_HEAD_DIM = 64
C_WIDTH = 512
C_KV_WIDTH = 128
C_COLS = 768

D_HEADS = 4
D_KEY_DIM = 64
D_WIDTH = 256
D_COLS = 1024
D_LEVELS = 7

D_FF = 2816
GATE_COLS = 4 * D_MODEL
OFF_A = GATE_COLS
OFF_B = OFF_A + A_COLS
OFF_C = OFF_B + B_WIDTH
IN_COLS = OFF_C + C_COLS + D_COLS

SEQ_TILE = 128
ROW_TILE = 384
DENSE_STEP = 256
V7X_VMEM_BYTES = 64 * 1024 * 1024
VMEM_LIMIT = V7X_VMEM_BYTES - 4 * 1024 * 1024


def _dot(a, b):
    return jnp.dot(a.astype(BF16), b.astype(BF16), preferred_element_type=F32)


def _dot_nt(a, b):
    return lax.dot_general(a.astype(BF16), b.astype(BF16), (((1,), (1,)), ((), ())),
                           preferred_element_type=F32)


def _dot_tn(a, b):
    return lax.dot_general(a.astype(BF16), b.astype(BF16), (((0,), (0,)), ((), ())),
                           preferred_element_type=F32)


def _round_robin(*gens):
    gens = list(gens)
    while gens:
        for gen in list(gens):
            if next(gen, "done") == "done":
                gens.remove(gen)
        yield


def _interleave(*gens):
    for _ in _round_robin(*gens):
        pass


def _split2(x):
    hi = x.astype(BF16)
    lo = (x - hi.astype(F32)).astype(BF16)
    return hi, lo


def _split3(x):
    hi = x.astype(BF16)
    r = x - hi.astype(F32)
    mid = r.astype(BF16)
    lo = (r - mid.astype(F32)).astype(BF16)
    return hi, mid, lo


def _const_dot(c4, x):
    hi, mid, lo = _split3(x)
    return jnp.dot(c4, jnp.concatenate([hi, mid, lo, jnp.zeros_like(hi)], axis=0),
                   preferred_element_type=F32)


def _mm(a, b):
    return jnp.dot(a, b, preferred_element_type=F32)


def _mm_tn(a, b):
    return lax.dot_general(a, b, (((0,), (0,)), ((), ())), preferred_element_type=F32)


def _unit_lower_inverses(mats, ri, ci, box):
    in16 = (ri // 16) == (ci // 16)
    in32 = (ri // 32) == (ci // 32)
    diag = ri == ci
    as_mask = lambda cond: jnp.where(cond, 1.0, 0.0).astype(BF16)
    plus_eye = lambda m: jnp.where(diag, 1.0, m).astype(BF16)
    m16, eye = as_mask(in16), as_mask(diag)
    xb = [a * m16 for a in mats]
    t = [xi + eye for xi in xb]
    yield
    for _ in range(3):
        x = [_mm(xi, xi) for xi in xb]
        xb = [xi.astype(BF16) for xi in x]
        t = [_mm(ti, plus_eye(xi)).astype(BF16) for ti, xi in zip(t, x)]
        yield
    n = mats[0].shape[0]
    for half, sel in ((16, as_mask(in32 & ~in16)), (32, as_mask(~in32))):
        starts = range(0, n, 2 * half)
        low = lambda m: jnp.concatenate([m[r + half:r + 2 * half] for r in starts], axis=0)
        put = lambda rows, base: jnp.concatenate(
            [p for k, r in enumerate(starts) for p in (base[r:r + half], rows[k * half:(k + 1) * half])],
            axis=0)
        lt = [_mm(low(a * sel), ti).astype(BF16) for a, ti in zip(mats, t)]
        zero = jnp.zeros_like(t[0])
        t = [put(_mm(low(ti), put(li, zero) + eye).astype(BF16), ti) for ti, li in zip(t, lt)]
        yield
    box["tinv"] = t


def _head_sum(x, ones_bd):
    return jnp.dot(x.astype(BF16), ones_bd, preferred_element_type=F32)


def _head_ones(width, seg):
    r = lax.broadcasted_iota(jnp.int32, (width, width), 0) // seg
    c = lax.broadcasted_iota(jnp.int32, (width, width), 1) // seg
    return jnp.where(r == c, 1.0, 0.0).astype(BF16)


def _sigmoid(x):
    return 1.0 / (1.0 + jnp.exp(-x))


def _silu(x):
    return x * _sigmoid(x)


def _rms(x, g):
    return x * lax.rsqrt(jnp.mean(x * x, axis=-1, keepdims=True) + NORM_EPS) * g


A_PARAMS = ("mu", "wwa", "w0", "a0", "gup", "kk", "ka", "rk", "lnw", "lnb")
A_VRES_PARAMS = ("vd", "vu", "v0")


def _in_proj_kernel(*refs, has_vres, pad, tiles_per_seq, tm, tn):
    refs = list(refs)
    sinks_ref, x_ref = refs[:2]
    vf_ref = refs[2] if has_vres else None
    k0 = 3 if has_vres else 2
    g_ref, w_ref, mix_ref, scale_ref, bias_ref, lb_ref, ng_ref, sums_ref, level_ref = refs[k0:k0 + 9]
    names = A_PARAMS + (A_VRES_PARAMS if has_vres else ())
    prm = dict(zip(names, refs[k0 + 9:k0 + 9 + len(names)]))
    outs = refs[k0 + 9 + len(names):]
    o_ref, ya_ref, yc_ref, yd_ref = outs[:4]
    vout_ref = None if has_vres else outs[4]
    pool_buf, kv_sc, state_sc, a_state_sc, a_carry_sc = outs[-5:]
    T = SEQ_TILE
    tile_in_seq = pl.program_id(0) % tiles_per_seq

    @pl.when(tile_in_seq == 0)
    def _():
        pool_buf[0:B_HALO, :] = jnp.zeros((B_HALO, B_WIDTH), F32)
        for ref in (kv_sc, state_sc, a_state_sc, a_carry_sc):
            ref[...] = jnp.zeros_like(ref)

    z = _rms(x_ref[...], g_ref[...])
    row = tile_in_seq * tm + lax.broadcasted_iota(jnp.int32, (tm, 1), 0)
    zb = jnp.where(row >= pad, z, 0.0).astype(BF16)
    mixer_cols = jnp.dot(zb, w_ref[:, OFF_A:], preferred_element_type=F32)
    u_a = mixer_cols[:, :A_COLS]
    u_b = mixer_cols[:, OFF_B - OFF_A:OFF_C - OFF_A]
    cd = mixer_cols[:, OFF_C - OFF_A:]

    def dense():
        for n, c0 in enumerate(range(0, GATE_COLS, tn)):
            for lo in range(c0, c0 + tn, DENSE_STEP):
                cols = jnp.dot(zb, w_ref[:, lo:lo + DENSE_STEP], preferred_element_type=F32)
                o_ref[:, lo:lo + DENSE_STEP] = _sigmoid(cols)
                yield
            if n < tm // T:
                rows = slice(n * T, (n + 1) * T)
                pooled = _pool_windows(u_b[rows], pool_buf, tile_in_seq * tm + rows.start, pad=pad)
                o_ref[rows, GATE_COLS:] = _dot(pooled, mix_ref[...]) * scale_ref[...]
                yield

    def mixers():
        for k in range(tm // T):
            rows = slice(k * T, (k + 1) * T)
            kv_prev = kv_sc[...] if k == 0 else cd[(k - 1) * T:k * T, C_WIDTH:C_COLS]
            jt = tile_in_seq * (tm // T) + k
            yield from _round_robin(
                _swa_tile(yc_ref.at[pl.ds(k * T, T)], cd[rows, :C_COLS], kv_prev, sinks_ref, bias_ref, jt,
                          pad=pad),
                _hgrn_tile(yd_ref.at[pl.ds(k * T, T)], cd[rows, C_COLS:], state_sc, lb_ref, ng_ref,
                           sums_ref, level_ref, jt, pad=pad))
        kv_sc[...] = cd[tm - T:, C_WIDTH:C_COLS]

    rwkv = _rwkv_tile(ya_ref, vout_ref, u_a, None if vf_ref is None else vf_ref[...], prm, a_state_sc,
                      a_carry_sc)
    _interleave(dense(), rwkv, mixers())


def _in_proj(h2, v_first, g, w, mix_bd, scale, sinks, lb, norm_g, pa, *, pad, lp):
    m = h2.shape[0]
    tm = ROW_TILE
    assert lp % tm == 0 and tm % SEQ_TILE == 0
    has_vres = v_first is not None
    sums, level = _hgrn_constants()
    bias = _swa_constants(pad)
    const = lambda arr: pl.BlockSpec(arr.shape, lambda i: (0,) * arr.ndim, pipeline_mode=pl.Buffered(1))
    rows = lambda w: pl.BlockSpec((tm, w), lambda i: (i, 0))
    names = A_PARAMS + (A_VRES_PARAMS if has_vres else ())
    shape = lambda w: jax.ShapeDtypeStruct((m, w), F32)
    n_out = 4 if has_vres else 5
    widths = (GATE_COLS + B_WIDTH, A_WIDTH, C_WIDTH, D_WIDTH, A_WIDTH)[:n_out]
    return pl.pallas_call(
        functools.partial(_in_proj_kernel, has_vres=has_vres, pad=pad, tiles_per_seq=lp // tm, tm=tm,
                          tn=1024),
        grid=(m // tm,),
        in_specs=[pl.BlockSpec(memory_space=pltpu.SMEM), rows(D_MODEL)]
                 + ([rows(A_WIDTH)] if has_vres else [])
                 + [const(a) for a in (g, w, mix_bd, scale, bias, lb, norm_g, sums, level)]
                 + [const(pa[n]) for n in names],
        out_specs=tuple(rows(w) for w in widths),
        out_shape=tuple(shape(w) for w in widths),
        scratch_shapes=[pltpu.VMEM((B_HALO + tm, B_WIDTH), F32),
                        pltpu.VMEM((SEQ_TILE, 2 * C_KV_WIDTH), F32),
                        pltpu.VMEM((D_WIDTH, D_WIDTH), F32),
                        pltpu.VMEM((A_WIDTH, A_WIDTH), F32),
                        pltpu.VMEM((8, A_COLS), F32)],
        compiler_params=pltpu.CompilerParams(dimension_semantics=("arbitrary",),
                                             vmem_limit_bytes=VMEM_LIMIT),
        name="in_proj_mixers",
    )(sinks, h2, *([v_first] if has_vres else []), g, w, mix_bd, scale, bias, lb, norm_g, sums, level,
      *[pa[n] for n in names])


def _rwkv_tile(y_ref, vout_ref, u, vf, prm, state_sc, carry_sc):
    C = A_CHUNK
    N = A_HEAD_DIM
    W = A_WIDTH
    rows_total = u.shape[0]
    n_chunks = rows_total // C

    row = lax.broadcasted_iota(jnp.int32, (rows_total, 1), 0)
    prev = jnp.where(row == 0, carry_sc[0:1, :], pltpu.roll(u, 1, axis=0))
    carry_sc[0:1, :] = u[rows_total - 1:rows_total, :]
    ones_bd = _head_ones(W, N)
    lane = lax.broadcasted_iota(jnp.int32, (1, 128), 1)

    x = u + (prev - u) * prm["mu"][...]
    r = x[:, 0:W]
    k = x[:, W:2 * W]
    v = x[:, 2 * W:3 * W]
    slab = x[:, 3 * W:3 * W + 128]
    gd = x[:, 3 * W + 128:]
    slab = jnp.where(lane < 64, jnp.tanh(slab), slab)
    wa = _dot(slab, prm["wwa"][...])
    yield
    logw = -math.exp(-0.5) * _sigmoid(prm["w0"][...] + wa[:, :W])
    a = _sigmoid(prm["a0"][...] + wa[:, W:])
    g = _dot(_sigmoid(gd), prm["gup"][...])
    yield
    if vf is not None:
        low_rank = _dot(v, prm["vd"][...])
        yield
        v = v + (vf - v) * _sigmoid(prm["v0"][...] + _dot(low_rank, prm["vu"][...]))
    else:
        vout_ref[...] = v
    kkr = k * prm["kk"][...]
    kk = kkr / jnp.maximum(jnp.sqrt(_head_sum(kkr * kkr, ones_bd)), 1e-12)
    k2 = k * (1.0 + (a - 1.0) * prm["ka"][...])
    bhat = kk * a
    yield

    ri = lax.broadcasted_iota(jnp.int32, (W, W), 0)
    ci = lax.broadcasted_iota(jnp.int32, (W, W), 1)
    tw = lax.broadcasted_iota(jnp.int32, (C, W), 0)
    sw = lax.broadcasted_iota(jnp.int32, (C, W), 1) % C
    tri_s = lax.broadcasted_iota(jnp.int32, (C, 4 * C), 1)
    tri = jnp.where((tri_s % C <= lax.broadcasted_iota(jnp.int32, (C, 4 * C), 0)) & (tri_s < 3 * C),
                    1.0, 0.0).astype(BF16)

    def stack(t):
        return jnp.concatenate([t.astype(BF16)] * A_HEADS, axis=0) * ones_bd

    chunks = [slice(i * C, (i + 1) * C) for i in range(n_chunks)]
    cl = [_const_dot(tri, logw[sl]) for sl in chunks]
    yield
    pre, aa_ops = [], []
    for sl, c in zip(chunks, cl):
        cl_last = c[C - 1:C, :]
        e_neg = jnp.exp(-c)
        e_last = jnp.exp(cl_last - c)
        at = -kk[sl] * jnp.exp(c - logw[sl])
        rt = r[sl] * jnp.exp(c)
        aa_ops.append((jnp.concatenate([at, rt], axis=0),
                       jnp.concatenate([stack(bhat[sl] * e_neg), stack(k2[sl] * e_neg)], axis=0)))
        pre.append(dict(at_s=stack(at), v_s=stack(v[sl]), btil=stack(bhat[sl] * e_last),
                        ktil=stack(k2[sl] * e_last), rt=rt, gam=jnp.exp(cl_last)))
        yield
    aa = [_dot_nt(lhs, rhs) for lhs, rhs in aa_ops]
    yield
    for p, m in zip(pre, aa):
        p["a_ab"] = stack(jnp.where(sw < tw, m[:C, :W], 0.0))
        p["a_ak"] = stack(jnp.where(sw < tw, m[:C, W:], 0.0))
        p["a_rb"] = jnp.where(sw <= tw, m[C:, :W], 0.0).astype(BF16)
        p["a_rk"] = jnp.where(sw <= tw, m[C:, W:], 0.0).astype(BF16)
    yield
    for p in pre:
        p["aakv"] = _mm(p["a_ak"], p["v_s"]).astype(BF16)
    yield

    box = {}
    yield from _unit_lower_inverses([p["a_ab"] for p in pre], ri, ci, box)
    tinv = box["tinv"]
    w_t = [_mm(t, p["at_s"]).astype(BF16) for t, p in zip(tinv, pre)]
    yield
    u_t = [_mm(t, p["aakv"]).astype(BF16) for t, p in zip(tinv, pre)]
    yield
    m_e = [_mm_tn(p["btil"], w).astype(BF16) for p, w in zip(pre, w_t)]
    yield
    n_mat = [_mm_tn(p["btil"], x) + _mm_tn(p["ktil"], p["v_s"]) for p, x in zip(pre, u_t)]
    yield
    q_mat = [p["rt"] + _mm(p["a_rb"], w) for p, w in zip(pre, w_t)]
    p_mat = [_mm(p["a_rb"], x) + _mm(p["a_rk"], p["v_s"]) for p, x in zip(pre, u_t)]
    g_col = [jnp.sum(jnp.where(ri == ci, p["gam"], 0.0), axis=1, keepdims=True) for p in pre]
    yield
    st = state_sc[...]
    outs = []
    for i in range(n_chunks):
        sb = st.astype(BF16)
        outs.append(_mm(q_mat[i].astype(BF16), sb) + p_mat[i])
        st = g_col[i] * st + _mm(m_e[i], sb) + n_mat[i]
        yield
    state_sc[...] = st

    o = jnp.concatenate(outs, axis=0)
    mean = _head_sum(o, ones_bd) * (1.0 / N)
    d = o - mean
    var = _head_sum(d * d, ones_bd) * (1.0 / N)
    o = d * lax.rsqrt(var + A_GN_EPS) * prm["lnw"][...] + prm["lnb"][...]
    bonus = _head_sum(r * k2 * prm["rk"][...], ones_bd) * v
    y_ref[...] = (o + bonus) * g


def _pool_windows(u, buf, row0, *, pad):
    rows = u.shape[0]
    buf[B_HALO:B_HALO + rows, :] = u
    lane_group = lax.broadcasted_iota(jnp.int32, (1, B_WIDTH), 1) // B_GROUP_DIM
    acc = u
    win = jnp.zeros_like(u)
    for s in range(1, max(B_WINDOWS)):
        acc = acc + buf[B_HALO - s:B_HALO - s + rows, :]
        if s + 1 in B_WINDOWS:
            win = jnp.where(lane_group == B_WINDOWS.index(s + 1), acc, win)
    wlane = jnp.zeros((1, B_WIDTH), jnp.int32)
    for gi, w in enumerate(B_WINDOWS):
        wlane = jnp.where(lane_group == gi, w, wlane)
    t_real = row0 + lax.broadcasted_iota(jnp.int32, (rows, 1), 0) - pad
    cnt = jnp.minimum(jnp.maximum(t_real + 1, 1), wlane).astype(F32)
    buf[0:B_HALO, :] = u[rows - B_HALO:, :]
    return win / cnt - u


def _swa_constants(pad):
    T = SEQ_TILE
    dist = T + np.arange(T)[:, None] - np.arange(2 * T)[None, :]
    band = (dist >= 0) & (dist < T)
    n = -(-pad // T) + 2
    bias = np.empty((n, C_KV_HEADS, C_GROUP * T, 2 * T), np.float32)
    for j in range(n):
        ok = band & ((j - 1) * T + np.arange(2 * T)[None, :] >= pad)
        for hq in range(C_HEADS):
            slope = np.float32(2.0 ** (-8.0 * (hq + 1) / C_HEADS))
            rows = slice((hq % C_GROUP) * T, (hq % C_GROUP + 1) * T)
            bias[j, hq // C_GROUP, rows] = np.where(ok, -slope * dist.astype(np.float32), MASK_VALUE)
    return jnp.asarray(bias)


def _swa_tile(y_ref, blk, kv_prev, sinks_ref, bias_ref, j, *, pad):
    T = SEQ_TILE
    G = C_GROUP
    kv = blk[:, C_WIDTH:]
    kw = jnp.concatenate([kv_prev[:, :C_KV_WIDTH], kv[:, :C_KV_WIDTH]], axis=0)
    vw = jnp.concatenate([kv_prev[:, C_KV_WIDTH:], kv[:, C_KV_WIDTH:]], axis=0)
    lane = lax.broadcasted_iota(jnp.int32, (1, 2 * C_HEAD_DIM), 1)
    low = lane < C_HEAD_DIM
    kw_sw = pltpu.roll(kw, C_HEAD_DIM, axis=1)
    vw_sw = pltpu.roll(vw, C_HEAD_DIM, axis=1)
    head = lax.broadcasted_iota(jnp.int32, (G * T, 1), 0) // T
    variant = jnp.minimum(j, bias_ref.shape[0] - 1)
    tiles = []
    for hk in range(C_KV_HEADS):
        k2 = jnp.where(low == (hk == 0), kw, kw_sw).astype(BF16)
        v2 = jnp.where(low == (hk == 0), vw, vw_sw).astype(BF16)
        q_rows = []
        sink = jnp.zeros((G * T, 1), F32)
        for gq in range(G):
            hq = hk * G + gq
            pair = blk[:, (hq // 2) * 128:(hq // 2 + 1) * 128]
            q_rows.append(jnp.where(low == (hq % 2 == 0), pair * (C_HEAD_DIM ** -0.5), 0.0))
            sink = jnp.where(head == gq, sinks_ref[hq], sink)
        s = _dot_nt(jnp.concatenate(q_rows, axis=0), k2)
        logits = s + bias_ref[variant, hk]
        m = jnp.maximum(jnp.max(logits, axis=-1, keepdims=True), sink)
        p = jnp.exp(logits - m)
        denom = jnp.sum(p, axis=-1, keepdims=True) + jnp.exp(sink - m)
        yield
        o = _dot(p, v2) / denom
        for pr in range(G // 2):
            tiles.append(jnp.where(low, o[2 * pr * T:(2 * pr + 1) * T], o[(2 * pr + 1) * T:(2 * pr + 2) * T]))
        yield
    y_ref[...] = jnp.concatenate(tiles, axis=1)


def _hgrn_tile(y_ref, u, state_sc, lb_ref, ng_ref, sums_ref, level_ref, j, *, pad):
    T = SEQ_TILE
    N = D_KEY_DIM
    lb = lb_ref[...]
    q = _silu(u[:, :D_WIDTH])
    fpre = u[:, D_WIDTH:2 * D_WIDTH]
    vin = u[:, 2 * D_WIDTH:3 * D_WIDTH]
    gate = u[:, 3 * D_WIDTH:]
    sig = _sigmoid(fpre)
    f = lb + (1.0 - lb) * sig
    t_real = j * T + lax.broadcasted_iota(jnp.int32, (T, 1), 0) - pad
    logf = jnp.where(t_real >= 0, jnp.log(jnp.maximum(f, 1e-30)), 0.0)
    kx = (1.0 - lb) * (1.0 - sig)

    all_sums = _mm(sums_ref[...], jnp.concatenate(_split2(logf), axis=0))
    sums = lambda i: all_sums[i * T:(i + 1) * T]
    same_head = (lax.broadcasted_iota(jnp.int32, (D_WIDTH, D_WIDTH), 0) // N
                 == lax.broadcasted_iota(jnp.int32, (D_WIDTH, D_WIDTH), 1) // N)
    head_rows = jnp.where(lax.broadcasted_iota(jnp.int32, (D_HEADS * T, D_WIDTH), 0) // T
                          == lax.broadcasted_iota(jnp.int32, (D_HEADS * T, D_WIDTH), 1) // N,
                          1.0, 0.0).astype(BF16)
    stack = lambda t: jnp.concatenate([t.astype(BF16)] * D_HEADS, axis=0) * head_rows
    k_st = stack(kx)
    v_st = stack(vin)
    lv = level_ref[...]
    att = jnp.where(lv == 0, _dot_nt(q, k_st), 0.0)
    for l in range(D_LEVELS):
        if l == 0:
            q_l, k_l = q * jnp.exp(logf), k_st
        else:
            q_l = q * jnp.exp(sums(2 * l - 2))
            k_l = k_st * jnp.concatenate([jnp.exp(sums(2 * l - 1)).astype(BF16)] * D_HEADS, axis=0)
        att = jnp.where(lv == l + 1, _dot_nt(q_l, k_l), att)
        yield
    b = sums(2 * D_LEVELS - 2)
    b_rev = sums(2 * D_LEVELS - 1)
    st = state_sc[...]
    o = _dot(att, v_st) + _dot_nt(q * jnp.exp(b), st)
    upd = jnp.where(same_head, _dot_tn(vin, kx * jnp.exp(b_rev)), 0.0)
    state_sc[...] = st * jnp.exp(b[T - 1:T, :]) + upd

    ones_bd = _head_ones(D_WIDTH, N)
    ms = _head_sum(o * o, ones_bd) * (1.0 / N)
    y_ref[...] = o * lax.rsqrt(ms + NORM_EPS) * ng_ref[...] * _silu(gate)


def _hgrn_constants():
    T = SEQ_TILE
    t = np.arange(T)[:, None]
    i = np.arange(T)[None, :]
    mats = []
    for l in list(range(1, D_LEVELS)) + [D_LEVELS]:
        h = 2 ** l
        same = (t // h) == (i // h)
        mats += [same & (i <= t), same & (i > t)]
    sums = np.tile(np.concatenate(mats, axis=0).astype(np.float32), (1, 2))
    x = t ^ i
    level = np.where(i == t, 0, np.where(i < t, np.floor(np.log2(np.maximum(x, 1))).astype(np.int64) + 1, -1))
    return jnp.asarray(sums, BF16), jnp.asarray(np.tile(level, (1, D_HEADS)), jnp.int32)


def _merge_ffn_kernel(h_ref, gates_ref, ya_ref, yb_ref, yc_ref, yd_ref, wb_ref, wo_ref, gf_ref, wu_ref,
                      wd_ref, *rest, ff_chunk):
    fg_ref, o_ref = rest if len(rest) == 2 else (None,) + rest
    merged = None
    row = 0
    for bi, y_ref in enumerate((ya_ref, yb_ref, yc_ref, yd_ref)):
        w = y_ref.shape[-1]
        part = gates_ref[:, bi * D_MODEL:(bi + 1) * D_MODEL] * jnp.dot(
            y_ref[...].astype(BF16), wb_ref[row:row + w, :], preferred_element_type=F32)
        merged = part if merged is None else merged + part
        row += w
    h = h_ref[...] + jnp.dot(merged.astype(BF16), wo_ref[...], preferred_element_type=F32)
    zb = _rms(h, gf_ref[...]).astype(BF16)

    def up_proj(c):
        cols = lambda base: wu_ref[:, base + c * ff_chunk:base + (c + 1) * ff_chunk]
        return (jnp.dot(zb, cols(0), preferred_element_type=F32),
                jnp.dot(zb, cols(D_FF), preferred_element_type=F32))

    acc = h
    n_chunks = D_FF // ff_chunk
    gu, up = up_proj(0)
    for c in range(n_chunks):
        nxt = up_proj(c + 1) if c + 1 < n_chunks else None
        acc = acc + jnp.dot((_silu(gu) * up).astype(BF16), wd_ref[c * ff_chunk:(c + 1) * ff_chunk, :],
                            preferred_element_type=F32)
        if nxt is not None:
            gu, up = nxt
    o_ref[...] = acc if fg_ref is None else _rms(acc, fg_ref[...])


def _merge_ffn(h2, proj2, ya, yc, yd, wb, wo, gf, wu, wd, final_g=None):
    m = h2.shape[0]
    tm = ROW_TILE
    rows = lambda w, col=0: pl.BlockSpec((tm, w), lambda i: (i, col))
    const = lambda arr: pl.BlockSpec(arr.shape, lambda i: (0, 0), pipeline_mode=pl.Buffered(1))
    last = () if final_g is None else (final_g,)
    return pl.pallas_call(
        functools.partial(_merge_ffn_kernel, ff_chunk=256),
        grid=(m // tm,),
        in_specs=[rows(D_MODEL), rows(GATE_COLS), rows(A_WIDTH), rows(B_WIDTH, GATE_COLS // B_WIDTH),
                  rows(C_WIDTH), rows(D_WIDTH), const(wb), const(wo), const(gf), const(wu),
                  const(wd)] + [const(g) for g in last],
        out_specs=rows(D_MODEL),
        out_shape=jax.ShapeDtypeStruct((m, D_MODEL), F32),
        compiler_params=pltpu.CompilerParams(dimension_semantics=("arbitrary",),
                                             vmem_limit_bytes=VMEM_LIMIT),
        name="merge_ffn",
    )(h2, proj2, ya, proj2, yc, yd, wb, wo, gf, wu, wd, *last)


def _block_diag(blocks):
    n, r, c = blocks.shape
    out = jnp.zeros((n * r, n * c), blocks.dtype)
    for i in range(n):
        out = out.at[i * r:(i + 1) * r, i * c:(i + 1) * c].set(blocks[i])
    return out


def kernel(x, meta, norm_mix, norm_ffn, norm_final, w_in, w_branch, w_out, a_mu, a_w_up, a_w0, a_a_up, a_a0, a_g_up, a_kk, a_ka, a_rk, a_ln_w, a_ln_b, a_vres_down, a_vres_up, a_vres0, b_mix, b_scale, c_sinks, d_lower_bounds, d_norm, w_ffn_up, w_ffn_down):
    bsz, seq, _ = x.shape
    depth = w_in.shape[0]
    T = SEQ_TILE
    L = N_META + seq
    pad = (-L) % T
    lp = L + pad
    assert (pad + N_META) % T == 0 and seq % T == 0
    h = jnp.concatenate([jnp.zeros((bsz, pad, D_MODEL), F32),
                         jnp.broadcast_to(meta.astype(F32)[None], (bsz, N_META, D_MODEL)),
                         x.astype(F32)], axis=1).reshape(bsz * lp, D_MODEL)
    lb_w = jax.nn.softmax(d_lower_bounds.astype(F32), axis=0)
    lb_table = jnp.cumsum(lb_w, axis=0) - lb_w[0]
    row2 = lambda t: t.reshape(1, -1).astype(F32)
    v_first = None
    for l in range(depth):
        wwa = jnp.zeros((128, 2 * A_WIDTH), F32)
        wwa = wwa.at[:64, :A_WIDTH].set(a_w_up[l]).at[64:, A_WIDTH:].set(a_a_up[l])
        pa = dict(mu=row2(a_mu[l]), wwa=wwa.astype(BF16), w0=row2(a_w0[l]), a0=row2(a_a0[l]),
                  gup=a_g_up[l].astype(BF16), kk=row2(a_kk[l]), ka=row2(a_ka[l]), rk=row2(a_rk[l]),
                  lnw=row2(a_ln_w[l]), lnb=row2(a_ln_b[l]))
        if l > 0:
            pa.update(vd=a_vres_down[l - 1].astype(BF16), vu=a_vres_up[l - 1].astype(BF16),
                      v0=row2(a_vres0[l - 1]))
        outs = _in_proj(h, v_first, row2(norm_mix[l]), w_in[l].astype(BF16),
                        _block_diag(b_mix[l]).astype(BF16), row2(b_scale[l]), c_sinks[l].astype(F32),
                        row2(lb_table[l]), row2(d_norm[l]), pa, pad=pad, lp=lp)
        proj2, y_a, y_c, y_d = outs[:4]
        if l == 0:
            v_first = outs[4]
        h = _merge_ffn(h, proj2, y_a, y_c, y_d,
                       w_branch[l].astype(BF16), w_out[l].astype(BF16), row2(norm_ffn[l]),
                       w_ffn_up[l].astype(BF16), w_ffn_down[l].astype(BF16),
                       row2(norm_final) if l == depth - 1 else None)
    return h.reshape(bsz, lp, D_MODEL)[:, pad + N_META:]
```

```python
import functools
import math

import jax
import jax.numpy as jnp
import numpy as np
from jax import lax
from jax.experimental import pallas as pl
from jax.experimental.pallas import tpu as pltpu

F32 = jnp.float32
BF16 = jnp.bfloat16

D_MODEL = 1024
N_META = 16
NORM_EPS = 1e-6
MASK_VALUE = -1e30

A_HEADS = 4
A_HEAD_DIM = 64
A_WIDTH = 256
A_GN_EPS = 64e-5
A_COLS = 1024
A_CHUNK = 64

B_WIDTH = 256
B_GROUP_DIM = 64
B_WINDOWS = (2, 4, 8, 16)
B_HALO = 16

C_HEADS = 8
C_KV_HEADS = 2
C_GROUP = 4
C_HEAD_DIM = 64
C_WIDTH = 512
C_KV_WIDTH = 128
C_COLS = 768

D_HEADS = 4
D_KEY_DIM = 64
D_WIDTH = 256
D_COLS = 1024
D_LEVELS = 7

D_FF = 2816
GATE_COLS = 4 * D_MODEL
OFF_A = GATE_COLS
OFF_B = OFF_A + A_COLS
OFF_C = OFF_B + B_WIDTH
IN_COLS = OFF_C + C_COLS + D_COLS

SEQ_TILE = 128
ROW_TILE = 384
DENSE_STEP = 256
MIXER_DELAY = 7
V7X_VMEM_BYTES = 64 * 1024 * 1024
VMEM_LIMIT = V7X_VMEM_BYTES - 4 * 1024 * 1024


def _dot(a, b):
    return jnp.dot(a.astype(BF16), b.astype(BF16), preferred_element_type=F32)


def _dot_nt(a, b):
    return lax.dot_general(a.astype(BF16), b.astype(BF16), (((1,), (1,)), ((), ())),
                           preferred_element_type=F32)


def _dot_tn(a, b):
    return lax.dot_general(a.astype(BF16), b.astype(BF16), (((0,), (0,)), ((), ())),
                           preferred_element_type=F32)


def _round_robin(*gens):
    gens = list(gens)
    while gens:
        for gen in list(gens):
            if next(gen, "done") == "done":
                gens.remove(gen)
        yield


def _interleave(*gens):
    for _ in _round_robin(*gens):
        pass


def _split2(x):
    hi = x.astype(BF16)
    lo = (x - hi.astype(F32)).astype(BF16)
    return hi, lo


def _split3(x):
    hi = x.astype(BF16)
    r = x - hi.astype(F32)
    mid = r.astype(BF16)
    lo = (r - mid.astype(F32)).astype(BF16)
    return hi, mid, lo


def _const_dot(c4, x):
    hi, mid, lo = _split3(x)
    return jnp.dot(c4, jnp.concatenate([hi, mid, lo, jnp.zeros_like(hi)], axis=0),
                   preferred_element_type=F32)


def _mm(a, b):
    return jnp.dot(a, b, preferred_element_type=F32)


def _mm_tn(a, b):
    return lax.dot_general(a, b, (((0,), (0,)), ((), ())), preferred_element_type=F32)


def _unit_lower_inverses(mats, ri, ci, box):
    in16 = (ri // 16) == (ci // 16)
    in32 = (ri // 32) == (ci // 32)
    diag = ri == ci
    as_mask = lambda cond: jnp.where(cond, 1.0, 0.0).astype(BF16)
    plus_eye = lambda m: jnp.where(diag, 1.0, m).astype(BF16)
    m16, eye = as_mask(in16), as_mask(diag)
    xb = [a * m16 for a in mats]
    t = [xi + eye for xi in xb]
    yield
    for _ in range(3):
        x = [_mm(xi, xi) for xi in xb]
        xb = [xi.astype(BF16) for xi in x]
        t = [_mm(ti, plus_eye(xi)).astype(BF16) for ti, xi in zip(t, x)]
        yield
    n = mats[0].shape[0]
    for half, sel in ((16, as_mask(in32 & ~in16)), (32, as_mask(~in32))):
        starts = range(0, n, 2 * half)
        low = lambda m: jnp.concatenate([m[r + half:r + 2 * half] for r in starts], axis=0)
        put = lambda rows, base: jnp.concatenate(
            [p for k, r in enumerate(starts) for p in (base[r:r + half], rows[k * half:(k + 1) * half])],
            axis=0)
        lt = [_mm(low(a * sel), ti).astype(BF16) for a, ti in zip(mats, t)]
        zero = jnp.zeros_like(t[0])
        t = [put(_mm(low(ti), put(li, zero) + eye).astype(BF16), ti) for ti, li in zip(t, lt)]
        yield
    box["tinv"] = t


def _head_sum(x, ones_bd):
    return jnp.dot(x.astype(BF16), ones_bd, preferred_element_type=F32)


def _head_ones(width, seg):
    r = lax.broadcasted_iota(jnp.int32, (width, width), 0) // seg
    c = lax.broadcasted_iota(jnp.int32, (width, width), 1) // seg
    return jnp.where(r == c, 1.0, 0.0).astype(BF16)


def _sigmoid(x):
    return 1.0 / (1.0 + jnp.exp(-x))


def _silu(x):
    return x * _sigmoid(x)


def _rms(x, g):
    return x * lax.rsqrt(jnp.mean(x * x, axis=-1, keepdims=True) + NORM_EPS) * g


A_PARAMS = ("mu", "wwa", "w0", "a0", "gup", "kk", "ka", "rk", "lnw", "lnb")
A_VRES_PARAMS = ("vd", "vu", "v0")


def _in_proj_kernel(*refs, has_vres, pad, tiles_per_seq, tm, tn):
    refs = list(refs)
    sinks_ref, x_ref = refs[:2]
    vf_ref = refs[2] if has_vres else None
    k0 = 3 if has_vres else 2
    g_ref, w_ref, mix_ref, scale_ref, bias_ref, lb_ref, ng_ref, sums_ref, level_ref = refs[k0:k0 + 9]
    names = A_PARAMS + (A_VRES_PARAMS if has_vres else ())
    prm = dict(zip(names, refs[k0 + 9:k0 + 9 + len(names)]))
    outs = refs[k0 + 9 + len(names):]
    o_ref, ya_ref, yc_ref, yd_ref = outs[:4]
    vout_ref = None if has_vres else outs[4]
    pool_buf, kv_sc, state_sc, a_state_sc, a_carry_sc = outs[-5:]
    T = SEQ_TILE
    tile_in_seq = pl.program_id(0) % tiles_per_seq

    @pl.when(tile_in_seq == 0)
    def _():
        pool_buf[0:B_HALO, :] = jnp.zeros((B_HALO, B_WIDTH), F32)
        for ref in (kv_sc, state_sc, a_state_sc, a_carry_sc):
            ref[...] = jnp.zeros_like(ref)

    z = _rms(x_ref[...], g_ref[...])
    row = tile_in_seq * tm + lax.broadcasted_iota(jnp.int32, (tm, 1), 0)
    zb = jnp.where(row >= pad, z, 0.0).astype(BF16)
    project = lambda lo, hi: jnp.dot(zb, w_ref[:, lo:hi], preferred_element_type=F32)
    mixer_cols = project(OFF_A, IN_COLS)
    u_a = mixer_cols[:, :A_COLS]
    u_b = mixer_cols[:, OFF_B - OFF_A:OFF_C - OFF_A]
    u_c = mixer_cols[:, OFF_C - OFF_A:OFF_C - OFF_A + C_COLS]
    u_d = mixer_cols[:, OFF_C - OFF_A + C_COLS:]

    def dense():
        for n, c0 in enumerate(range(0, GATE_COLS, tn)):
            for lo in range(c0, c0 + tn, DENSE_STEP):
                o_ref[:, lo:lo + DENSE_STEP] = _sigmoid(project(lo, lo + DENSE_STEP))
                yield
            if n < tm // T:
                rows = slice(n * T, (n + 1) * T)
                pooled = _pool_windows(u_b[rows], pool_buf, tile_in_seq * tm + rows.start, pad=pad)
                o_ref[rows, GATE_COLS:] = _dot(pooled, mix_ref[...]) * scale_ref[...]
                yield

    def mixers():
        for _ in range(MIXER_DELAY):
            yield
        for k in range(tm // T):
            rows = slice(k * T, (k + 1) * T)
            kv_prev = kv_sc[...] if k == 0 else u_c[(k - 1) * T:k * T, C_WIDTH:]
            jt = tile_in_seq * (tm // T) + k
            yield from _round_robin(
                _swa_tile(yc_ref.at[pl.ds(k * T, T)], u_c[rows], kv_prev, sinks_ref, bias_ref, jt, pad=pad),
                _hgrn_tile(yd_ref.at[pl.ds(k * T, T)], u_d[rows], state_sc, lb_ref, ng_ref, sums_ref,
                           level_ref, jt, pad=pad))
        kv_sc[...] = u_c[tm - T:, C_WIDTH:]

    rwkv = _rwkv_tile(ya_ref, vout_ref, u_a, None if vf_ref is None else vf_ref[...], prm, a_state_sc,
                      a_carry_sc)
    _interleave(dense(), rwkv, mixers())


def _in_proj(h2, v_first, g, w, mix_bd, scale, sinks, lb, norm_g, pa, *, pad, lp):
    m = h2.shape[0]
    tm = ROW_TILE
    assert lp % tm == 0 and tm % SEQ_TILE == 0
    has_vres = v_first is not None
    sums, level = _hgrn_constants()
    bias = _swa_constants(pad)
    const = lambda arr: pl.BlockSpec(arr.shape, lambda i: (0,) * arr.ndim, pipeline_mode=pl.Buffered(1))
    rows = lambda w: pl.BlockSpec((tm, w), lambda i: (i, 0))
    names = A_PARAMS + (A_VRES_PARAMS if has_vres else ())
    shape = lambda w: jax.ShapeDtypeStruct((m, w), F32)
    n_out = 4 if has_vres else 5
    widths = (GATE_COLS + B_WIDTH, A_WIDTH, C_WIDTH, D_WIDTH, A_WIDTH)[:n_out]
    return pl.pallas_call(
        functools.partial(_in_proj_kernel, has_vres=has_vres, pad=pad, tiles_per_seq=lp // tm, tm=tm,
                          tn=1024),
        grid=(m // tm,),
        in_specs=[pl.BlockSpec(memory_space=pltpu.SMEM), rows(D_MODEL)]
                 + ([rows(A_WIDTH)] if has_vres else [])
                 + [const(a) for a in (g, w, mix_bd, scale, bias, lb, norm_g, sums, level)]
                 + [const(pa[n]) for n in names],
        out_specs=tuple(rows(w) for w in widths),
        out_shape=tuple(shape(w) for w in widths),
        scratch_shapes=[pltpu.VMEM((B_HALO + tm, B_WIDTH), F32),
                        pltpu.VMEM((SEQ_TILE, 2 * C_KV_WIDTH), F32),
                        pltpu.VMEM((D_WIDTH, D_WIDTH), F32),
                        pltpu.VMEM((A_WIDTH, A_WIDTH), F32),
                        pltpu.VMEM((8, A_COLS), F32)],
        compiler_params=pltpu.CompilerParams(dimension_semantics=("arbitrary",),
                                             vmem_limit_bytes=VMEM_LIMIT),
        name="in_proj_mixers",
    )(sinks, h2, *([v_first] if has_vres else []), g, w, mix_bd, scale, bias, lb, norm_g, sums, level,
      *[pa[n] for n in names])


def _rwkv_tile(y_ref, vout_ref, u, vf, prm, state_sc, carry_sc):
    C = A_CHUNK
    N = A_HEAD_DIM
    W = A_WIDTH
    rows_total = u.shape[0]
    n_chunks = rows_total // C

    row = lax.broadcasted_iota(jnp.int32, (rows_total, 1), 0)
    prev = jnp.where(row == 0, carry_sc[0:1, :], pltpu.roll(u, 1, axis=0))
    carry_sc[0:1, :] = u[rows_total - 1:rows_total, :]
    ones_bd = _head_ones(W, N)
    lane = lax.broadcasted_iota(jnp.int32, (1, 128), 1)

    x = u + (prev - u) * prm["mu"][...]
    r = x[:, 0:W]
    k = x[:, W:2 * W]
    v = x[:, 2 * W:3 * W]
    slab = x[:, 3 * W:3 * W + 128]
    gd = x[:, 3 * W + 128:]
    slab = jnp.where(lane < 64, jnp.tanh(slab), slab)
    wa = _dot(slab, prm["wwa"][...])
    yield
    logw = -math.exp(-0.5) * _sigmoid(prm["w0"][...] + wa[:, :W])
    a = _sigmoid(prm["a0"][...] + wa[:, W:])
    g = _dot(_sigmoid(gd), prm["gup"][...])
    yield
    if vf is not None:
        low_rank = _dot(v, prm["vd"][...])
        yield
        v = v + (vf - v) * _sigmoid(prm["v0"][...] + _dot(low_rank, prm["vu"][...]))
    else:
        vout_ref[...] = v
    kkr = k * prm["kk"][...]
    kk = kkr / jnp.maximum(jnp.sqrt(_head_sum(kkr * kkr, ones_bd)), 1e-12)
    k2 = k * (1.0 + (a - 1.0) * prm["ka"][...])
    bhat = kk * a
    yield

    ri = lax.broadcasted_iota(jnp.int32, (W, W), 0)
    ci = lax.broadcasted_iota(jnp.int32, (W, W), 1)
    tw = lax.broadcasted_iota(jnp.int32, (C, W), 0)
    sw = lax.broadcasted_iota(jnp.int32, (C, W), 1) % C
    tri_s = lax.broadcasted_iota(jnp.int32, (C, 4 * C), 1)
    tri = jnp.where((tri_s % C <= lax.broadcasted_iota(jnp.int32, (C, 4 * C), 0)) & (tri_s < 3 * C),
                    1.0, 0.0).astype(BF16)

    def stack(t):
        return jnp.concatenate([t.astype(BF16)] * A_HEADS, axis=0) * ones_bd

    chunks = [slice(i * C, (i + 1) * C) for i in range(n_chunks)]
    cl = [_const_dot(tri, logw[sl]) for sl in chunks]
    yield
    pre, aa_ops = [], []
    for sl, c in zip(chunks, cl):
        cl_last = c[C - 1:C, :]
        e_neg = jnp.exp(-c)
        e_last = jnp.exp(cl_last - c)
        at = -kk[sl] * jnp.exp(c - logw[sl])
        rt = r[sl] * jnp.exp(c)
        aa_ops.append((jnp.concatenate([at, rt], axis=0),
                       jnp.concatenate([stack(bhat[sl] * e_neg), stack(k2[sl] * e_neg)], axis=0)))
        pre.append(dict(at_s=stack(at), v_s=stack(v[sl]), btil=stack(bhat[sl] * e_last),
                        ktil=stack(k2[sl] * e_last), rt=rt, gam=jnp.exp(cl_last)))
        yield
    aa = [_dot_nt(lhs, rhs) for lhs, rhs in aa_ops]
    yield
    for p, m in zip(pre, aa):
        p["a_ab"] = stack(jnp.where(sw < tw, m[:C, :W], 0.0))
        p["a_ak"] = stack(jnp.where(sw < tw, m[:C, W:], 0.0))
        p["a_rb"] = jnp.where(sw <= tw, m[C:, :W], 0.0).astype(BF16)
        p["a_rk"] = jnp.where(sw <= tw, m[C:, W:], 0.0).astype(BF16)
    yield
    for p in pre:
        p["aakv"] = _mm(p["a_ak"], p["v_s"]).astype(BF16)
    yield

    box = {}
    yield from _unit_lower_inverses([p["a_ab"] for p in pre], ri, ci, box)
    tinv = box["tinv"]
    w_t = [_mm(t, p["at_s"]).astype(BF16) for t, p in zip(tinv, pre)]
    yield
    u_t = [_mm(t, p["aakv"]).astype(BF16) for t, p in zip(tinv, pre)]
    yield
    m_e = [_mm_tn(p["btil"], w).astype(BF16) for p, w in zip(pre, w_t)]
    yield
    n_mat = [_mm_tn(p["btil"], x) + _mm_tn(p["ktil"], p["v_s"]) for p, x in zip(pre, u_t)]
    yield
    q_mat = [p["rt"] + _mm(p["a_rb"], w) for p, w in zip(pre, w_t)]
    p_mat = [_mm(p["a_rb"], x) + _mm(p["a_rk"], p["v_s"]) for p, x in zip(pre, u_t)]
    g_col = [jnp.sum(jnp.where(ri == ci, p["gam"], 0.0), axis=1, keepdims=True) for p in pre]
    yield
    st = state_sc[...]
    outs = []
    for i in range(n_chunks):
        sb = st.astype(BF16)
        outs.append(_mm(q_mat[i].astype(BF16), sb) + p_mat[i])
        st = g_col[i] * st + _mm(m_e[i], sb) + n_mat[i]
        yield
    state_sc[...] = st

    o = jnp.concatenate(outs, axis=0)
    mean = _head_sum(o, ones_bd) * (1.0 / N)
    d = o - mean
    var = _head_sum(d * d, ones_bd) * (1.0 / N)
    o = d * lax.rsqrt(var + A_GN_EPS) * prm["lnw"][...] + prm["lnb"][...]
    bonus = _head_sum(r * k2 * prm["rk"][...], ones_bd) * v
    y_ref[...] = (o + bonus) * g


def _pool_windows(u, buf, row0, *, pad):
    rows = u.shape[0]
    buf[B_HALO:B_HALO + rows, :] = u
    lane_group = lax.broadcasted_iota(jnp.int32, (1, B_WIDTH), 1) // B_GROUP_DIM
    acc = u
    win = jnp.zeros_like(u)
    for s in range(1, max(B_WINDOWS)):
        acc = acc + buf[B_HALO - s:B_HALO - s + rows, :]
        if s + 1 in B_WINDOWS:
            win = jnp.where(lane_group == B_WINDOWS.index(s + 1), acc, win)
    wlane = jnp.zeros((1, B_WIDTH), jnp.int32)
    for gi, w in enumerate(B_WINDOWS):
        wlane = jnp.where(lane_group == gi, w, wlane)
    t_real = row0 + lax.broadcasted_iota(jnp.int32, (rows, 1), 0) - pad
    cnt = jnp.minimum(jnp.maximum(t_real + 1, 1), wlane).astype(F32)
    buf[0:B_HALO, :] = u[rows - B_HALO:, :]
    return win / cnt - u


def _swa_constants(pad):
    T = SEQ_TILE
    dist = T + np.arange(T)[:, None] - np.arange(2 * T)[None, :]
    band = (dist >= 0) & (dist < T)
    n = -(-pad // T) + 2
    bias = np.empty((n, C_KV_HEADS, C_GROUP * T, 2 * T), np.float32)
    for j in range(n):
        ok = band & ((j - 1) * T + np.arange(2 * T)[None, :] >= pad)
        for hq in range(C_HEADS):
            slope = np.float32(2.0 ** (-8.0 * (hq + 1) / C_HEADS))
            rows = slice((hq % C_GROUP) * T, (hq % C_GROUP + 1) * T)
            bias[j, hq // C_GROUP, rows] = np.where(ok, -slope * dist.astype(np.float32), MASK_VALUE)
    return jnp.asarray(bias)


def _swa_tile(y_ref, blk, kv_prev, sinks_ref, bias_ref, j, *, pad):
    T = SEQ_TILE
    G = C_GROUP
    kv = blk[:, C_WIDTH:]
    kw = jnp.concatenate([kv_prev[:, :C_KV_WIDTH], kv[:, :C_KV_WIDTH]], axis=0)
    vw = jnp.concatenate([kv_prev[:, C_KV_WIDTH:], kv[:, C_KV_WIDTH:]], axis=0)
    lane = lax.broadcasted_iota(jnp.int32, (1, 2 * C_HEAD_DIM), 1)
    low = lane < C_HEAD_DIM
    kw_sw = pltpu.roll(kw, C_HEAD_DIM, axis=1)
    vw_sw = pltpu.roll(vw, C_HEAD_DIM, axis=1)
    head = lax.broadcasted_iota(jnp.int32, (G * T, 1), 0) // T
    variant = jnp.minimum(j, bias_ref.shape[0] - 1)
    tiles = []
    for hk in range(C_KV_HEADS):
        k2 = jnp.where(low == (hk == 0), kw, kw_sw).astype(BF16)
        v2 = jnp.where(low == (hk == 0), vw, vw_sw).astype(BF16)
        q_rows = []
        sink = jnp.zeros((G * T, 1), F32)
        for gq in range(G):
            hq = hk * G + gq
            pair = blk[:, (hq // 2) * 128:(hq // 2 + 1) * 128]
            q_rows.append(jnp.where(low == (hq % 2 == 0), pair * (C_HEAD_DIM ** -0.5), 0.0))
            sink = jnp.where(head == gq, sinks_ref[hq], sink)
        s = _dot_nt(jnp.concatenate(q_rows, axis=0), k2)
        logits = s + bias_ref[variant, hk]
        m = jnp.maximum(jnp.max(logits, axis=-1, keepdims=True), sink)
        p = jnp.exp(logits - m)
        denom = jnp.sum(p, axis=-1, keepdims=True) + jnp.exp(sink - m)
        yield
        o = _dot(p, v2) / denom
        for pr in range(G // 2):
            tiles.append(jnp.where(low, o[2 * pr * T:(2 * pr + 1) * T], o[(2 * pr + 1) * T:(2 * pr + 2) * T]))
        yield
    y_ref[...] = jnp.concatenate(tiles, axis=1)


def _hgrn_tile(y_ref, u, state_sc, lb_ref, ng_ref, sums_ref, level_ref, j, *, pad):
    T = SEQ_TILE
    N = D_KEY_DIM
    lb = lb_ref[...]
    q = _silu(u[:, :D_WIDTH])
    fpre = u[:, D_WIDTH:2 * D_WIDTH]
    vin = u[:, 2 * D_WIDTH:3 * D_WIDTH]
    gate = u[:, 3 * D_WIDTH:]
    sig = _sigmoid(fpre)
    f = lb + (1.0 - lb) * sig
    t_real = j * T + lax.broadcasted_iota(jnp.int32, (T, 1), 0) - pad
    logf = jnp.where(t_real >= 0, jnp.log(jnp.maximum(f, 1e-30)), 0.0)
    kx = (1.0 - lb) * (1.0 - sig)

    all_sums = _mm(sums_ref[...], jnp.concatenate(_split2(logf), axis=0))
    sums = lambda i: all_sums[i * T:(i + 1) * T]
    same_head = (lax.broadcasted_iota(jnp.int32, (D_WIDTH, D_WIDTH), 0) // N
                 == lax.broadcasted_iota(jnp.int32, (D_WIDTH, D_WIDTH), 1) // N)
    head_rows = jnp.where(lax.broadcasted_iota(jnp.int32, (D_HEADS * T, D_WIDTH), 0) // T
                          == lax.broadcasted_iota(jnp.int32, (D_HEADS * T, D_WIDTH), 1) // N,
                          1.0, 0.0).astype(BF16)
    stack = lambda t: jnp.concatenate([t.astype(BF16)] * D_HEADS, axis=0) * head_rows
    k_st = stack(kx)
    v_st = stack(vin)
    lv = level_ref[...]
    att = jnp.where(lv == 0, _dot_nt(q, k_st), 0.0)
    for l in range(D_LEVELS):
        if l == 0:
            q_l, k_l = q * jnp.exp(logf), k_st
        else:
            q_l = q * jnp.exp(sums(2 * l - 2))
            k_l = k_st * jnp.concatenate([jnp.exp(sums(2 * l - 1)).astype(BF16)] * D_HEADS, axis=0)
        att = jnp.where(lv == l + 1, _dot_nt(q_l, k_l), att)
        yield
    b = sums(2 * D_LEVELS - 2)
    b_rev = sums(2 * D_LEVELS - 1)
    st = state_sc[...]
    o = _dot(att, v_st) + _dot_nt(q * jnp.exp(b), st)
    upd = jnp.where(same_head, _dot_tn(vin, kx * jnp.exp(b_rev)), 0.0)
    state_sc[...] = st * jnp.exp(b[T - 1:T, :]) + upd

    ones_bd = _head_ones(D_WIDTH, N)
    ms = _head_sum(o * o, ones_bd) * (1.0 / N)
    y_ref[...] = o * lax.rsqrt(ms + NORM_EPS) * ng_ref[...] * _silu(gate)


def _hgrn_constants():
    T = SEQ_TILE
    t = np.arange(T)[:, None]
    i = np.arange(T)[None, :]
    mats = []
    for l in list(range(1, D_LEVELS)) + [D_LEVELS]:
        h = 2 ** l
        same = (t // h) == (i // h)
        mats += [same & (i <= t), same & (i > t)]
    sums = np.tile(np.concatenate(mats, axis=0).astype(np.float32), (1, 2))
    x = t ^ i
    level = np.where(i == t, 0, np.where(i < t, np.floor(np.log2(np.maximum(x, 1))).astype(np.int64) + 1, -1))
    return jnp.asarray(sums, BF16), jnp.asarray(np.tile(level, (1, D_HEADS)), jnp.int32)


def _merge_ffn_kernel(h_ref, gates_ref, ya_ref, yb_ref, yc_ref, yd_ref, wb_ref, wo_ref, gf_ref, wu_ref,
                      wd_ref, *rest, ff_chunk):
    fg_ref, o_ref = rest if len(rest) == 2 else (None,) + rest
    merged = None
    row = 0
    for bi, y_ref in enumerate((ya_ref, yb_ref, yc_ref, yd_ref)):
        w = y_ref.shape[-1]
        part = gates_ref[:, bi * D_MODEL:(bi + 1) * D_MODEL] * jnp.dot(
            y_ref[...].astype(BF16), wb_ref[row:row + w, :], preferred_element_type=F32)
        merged = part if merged is None else merged + part
        row += w
    h = h_ref[...] + jnp.dot(merged.astype(BF16), wo_ref[...], preferred_element_type=F32)
    zb = _rms(h, gf_ref[...]).astype(BF16)

    def up_proj(c):
        cols = lambda base: wu_ref[:, base + c * ff_chunk:base + (c + 1) * ff_chunk]
        return (jnp.dot(zb, cols(0), preferred_element_type=F32),
                jnp.dot(zb, cols(D_FF), preferred_element_type=F32))

    acc = h
    n_chunks = D_FF // ff_chunk
    gu, up = up_proj(0)
    for c in range(n_chunks):
        nxt = up_proj(c + 1) if c + 1 < n_chunks else None
        acc = acc + jnp.dot((_silu(gu) * up).astype(BF16), wd_ref[c * ff_chunk:(c + 1) * ff_chunk, :],
                            preferred_element_type=F32)
        if nxt is not None:
            gu, up = nxt
    o_ref[...] = acc if fg_ref is None else _rms(acc, fg_ref[...])


def _merge_ffn(h2, proj2, ya, yc, yd, wb, wo, gf, wu, wd, final_g=None):
    m = h2.shape[0]
    tm = ROW_TILE
    rows = lambda w, col=0: pl.BlockSpec((tm, w), lambda i: (i, col))
    const = lambda arr: pl.BlockSpec(arr.shape, lambda i: (0, 0), pipeline_mode=pl.Buffered(1))
    last = () if final_g is None else (final_g,)
    return pl.pallas_call(
        functools.partial(_merge_ffn_kernel, ff_chunk=256),
        grid=(m // tm,),
        in_specs=[rows(D_MODEL), rows(GATE_COLS), rows(A_WIDTH), rows(B_WIDTH, GATE_COLS // B_WIDTH),
                  rows(C_WIDTH), rows(D_WIDTH), const(wb), const(wo), const(gf), const(wu),
                  const(wd)] + [const(g) for g in last],
        out_specs=rows(D_MODEL),
        out_shape=jax.ShapeDtypeStruct((m, D_MODEL), F32),
        compiler_params=pltpu.CompilerParams(dimension_semantics=("arbitrary",),
                                             vmem_limit_bytes=VMEM_LIMIT),
        name="merge_ffn",
    )(h2, proj2, ya, proj2, yc, yd, wb, wo, gf, wu, wd, *last)


def _block_diag(blocks):
    n, r, c = blocks.shape
    out = jnp.zeros((n * r, n * c), blocks.dtype)
    for i in range(n):
        out = out.at[i * r:(i + 1) * r, i * c:(i + 1) * c].set(blocks[i])
    return out


def kernel(x, meta, norm_mix, norm_ffn, norm_final, w_in, w_branch, w_out, a_mu, a_w_up, a_w0, a_a_up, a_a0, a_g_up, a_kk, a_ka, a_rk, a_ln_w, a_ln_b, a_vres_down, a_vres_up, a_vres0, b_mix, b_scale, c_sinks, d_lower_bounds, d_norm, w_ffn_up, w_ffn_down):
    bsz, seq, _ = x.shape
    depth = w_in.shape[0]
    T = SEQ_TILE
    L = N_META + seq
    pad = (-L) % T
    lp = L + pad
    assert (pad + N_META) % T == 0 and seq % T == 0
    h = jnp.concatenate([jnp.zeros((bsz, pad, D_MODEL), F32),
                         jnp.broadcast_to(meta.astype(F32)[None], (bsz, N_META, D_MODEL)),
                         x.astype(F32)], axis=1).reshape(bsz * lp, D_MODEL)
    lb_w = jax.nn.softmax(d_lower_bounds.astype(F32), axis=0)
    lb_table = jnp.cumsum(lb_w, axis=0) - lb_w[0]
    row2 = lambda t: t.reshape(1, -1).astype(F32)
    v_first = None
    for l in range(depth):
        wwa = jnp.zeros((128, 2 * A_WIDTH), F32)
        wwa = wwa.at[:64, :A_WIDTH].set(a_w_up[l]).at[64:, A_WIDTH:].set(a_a_up[l])
        pa = dict(mu=row2(a_mu[l]), wwa=wwa.astype(BF16), w0=row2(a_w0[l]), a0=row2(a_a0[l]),
                  gup=a_g_up[l].astype(BF16), kk=row2(a_kk[l]), ka=row2(a_ka[l]), rk=row2(a_rk[l]),
                  lnw=row2(a_ln_w[l]), lnb=row2(a_ln_b[l]))
        if l > 0:
            pa.update(vd=a_vres_down[l - 1].astype(BF16), vu=a_vres_up[l - 1].astype(BF16),
                      v0=row2(a_vres0[l - 1]))
        outs = _in_proj(h, v_first, row2(norm_mix[l]), w_in[l].astype(BF16),
                        _block_diag(b_mix[l]).astype(BF16), row2(b_scale[l]), c_sinks[l].astype(F32),
                        row2(lb_table[l]), row2(d_norm[l]), pa, pad=pad, lp=lp)
        proj2, y_a, y_c, y_d = outs[:4]
        if l == 0:
            v_first = outs[4]
        h = _merge_ffn(h, proj2, y_a, y_c, y_d,
                       w_branch[l].astype(BF16), w_out[l].astype(BF16), row2(norm_ffn[l]),
                       w_ffn_up[l].astype(BF16), w_ffn_down[l].astype(BF16),
                       row2(norm_final) if l == depth - 1 else None)
    return h.reshape(bsz, lp, D_MODEL)[:, pad + N_META:]
```

```python
import functools
import math

import jax
import jax.numpy as jnp
import numpy as np
from jax import lax
from jax.experimental import pallas as pl
from jax.experimental.pallas import tpu as pltpu

F32 = jnp.float32
BF16 = jnp.bfloat16

D_MODEL = 1024
N_META = 16
NORM_EPS = 1e-6
MASK_VALUE = -1e30

A_HEADS = 4
A_HEAD_DIM = 64
A_WIDTH = 256
A_GN_EPS = 64e-5
A_COLS = 1024
A_CHUNK = 64

B_WIDTH = 256
B_GROUP_DIM = 64
B_WINDOWS = (2, 4, 8, 16)
B_HALO = 16

C_HEADS = 8
C_KV_HEADS = 2
C_GROUP = 4
C_HEAD_DIM = 64
C_WIDTH = 512
C_KV_WIDTH = 128
C_COLS = 768

D_HEADS = 4
D_KEY_DIM = 64
D_WIDTH = 256
D_COLS = 1024
D_LEVELS = 7

D_FF = 2816
GATE_COLS = 4 * D_MODEL
OFF_A = GATE_COLS
OFF_B = OFF_A + A_COLS
OFF_C = OFF_B + B_WIDTH
IN_COLS = OFF_C + C_COLS + D_COLS

SEQ_TILE = 128
ROW_TILE = 384
DENSE_STEP = 256
MIXER_DELAY = 7
V7X_VMEM_BYTES = 64 * 1024 * 1024
VMEM_LIMIT = V7X_VMEM_BYTES - 4 * 1024 * 1024


def _dot(a, b):
    return jnp.dot(a.astype(BF16), b.astype(BF16), preferred_element_type=F32)


def _dot_nt(a, b):
    return lax.dot_general(a.astype(BF16), b.astype(BF16), (((1,), (1,)), ((), ())),
                           preferred_element_type=F32)


def _dot_tn(a, b):
    return lax.dot_general(a.astype(BF16), b.astype(BF16), (((0,), (0,)), ((), ())),
                           preferred_element_type=F32)


def _round_robin(*gens):
    gens = list(gens)
    while gens:
        for gen in list(gens):
            if next(gen, "done") == "done":
                gens.remove(gen)
        yield


def _interleave(*gens):
    for _ in _round_robin(*gens):
        pass


def _split2(x):
    hi = x.astype(BF16)
    lo = (x - hi.astype(F32)).astype(BF16)
    return hi, lo


def _split3(x):
    hi = x.astype(BF16)
    r = x - hi.astype(F32)
    mid = r.astype(BF16)
    lo = (r - mid.astype(F32)).astype(BF16)
    return hi, mid, lo


def _const_dot(c4, x):
    hi, mid, lo = _split3(x)
    return jnp.dot(c4, jnp.concatenate([hi, mid, lo, jnp.zeros_like(hi)], axis=0),
                   preferred_element_type=F32)


def _mm(a, b):
    return jnp.dot(a, b, preferred_element_type=F32)


def _mm_tn(a, b):
    return lax.dot_general(a, b, (((0,), (0,)), ((), ())), preferred_element_type=F32)


def _unit_lower_inverses(mats, ri, ci, box):
    in16 = (ri // 16) == (ci // 16)
    in32 = (ri // 32) == (ci // 32)
    diag = ri == ci
    as_mask = lambda cond: jnp.where(cond, 1.0, 0.0).astype(BF16)
    plus_eye = lambda m: jnp.where(diag, 1.0, m).astype(BF16)
    m16, eye = as_mask(in16), as_mask(diag)
    xb = [a * m16 for a in mats]
    t = [xi + eye for xi in xb]
    yield
    for _ in range(3):
        x = [_mm(xi, xi) for xi in xb]
        xb = [xi.astype(BF16) for xi in x]
        t = [_mm(ti, plus_eye(xi)).astype(BF16) for ti, xi in zip(t, x)]
        yield
    n = mats[0].shape[0]
    for half, sel in ((16, as_mask(in32 & ~in16)), (32, as_mask(~in32))):
        starts = range(0, n, 2 * half)
        low = lambda m: jnp.concatenate([m[r + half:r + 2 * half] for r in starts], axis=0)
        put = lambda rows, base: jnp.concatenate(
            [p for k, r in enumerate(starts) for p in (base[r:r + half], rows[k * half:(k + 1) * half])],
            axis=0)
        lt = [_mm(low(a * sel), ti).astype(BF16) for a, ti in zip(mats, t)]
        zero = jnp.zeros_like(t[0])
        t = [put(_mm(low(ti), put(li, zero) + eye).astype(BF16), ti) for ti, li in zip(t, lt)]
        yield
    box["tinv"] = t


def _head_sum(x, ones_bd):
    return jnp.dot(x.astype(BF16), ones_bd, preferred_element_type=F32)


def _head_ones(width, seg):
    r = lax.broadcasted_iota(jnp.int32, (width, width), 0) // seg
    c = lax.broadcasted_iota(jnp.int32, (width, width), 1) // seg
    return jnp.where(r == c, 1.0, 0.0).astype(BF16)


def _sigmoid(x):
    return 1.0 / (1.0 + jnp.exp(-x))


def _silu(x):
    return x * _sigmoid(x)


def _rms(x, g):
    return x * lax.rsqrt(jnp.mean(x * x, axis=-1, keepdims=True) + NORM_EPS) * g


A_PARAMS = ("mu", "wwa", "w0", "a0", "gup", "kk", "ka", "rk", "lnw", "lnb")
A_VRES_PARAMS = ("vd", "vu", "v0")


def _in_proj_kernel(*refs, has_vres, pad, tiles_per_seq, tm, tn):
    refs = list(refs)
    sinks_ref, x_ref = refs[:2]
    vf_ref = refs[2] if has_vres else None
    k0 = 3 if has_vres else 2
    g_ref, w_ref, mix_ref, scale_ref, bias_ref, lb_ref, ng_ref, sums_ref, level_ref = refs[k0:k0 + 9]
    names = A_PARAMS + (A_VRES_PARAMS if has_vres else ())
    prm = dict(zip(names, refs[k0 + 9:k0 + 9 + len(names)]))
    outs = refs[k0 + 9 + len(names):]
    o_ref, ya_ref, yc_ref, yd_ref = outs[:4]
    vout_ref = None if has_vres else outs[4]
    pool_buf, kv_sc, state_sc, a_state_sc, a_carry_sc = outs[-5:]
    T = SEQ_TILE
    tile_in_seq = pl.program_id(0) % tiles_per_seq

    @pl.when(tile_in_seq == 0)
    def _():
        pool_buf[0:B_HALO, :] = jnp.zeros((B_HALO, B_WIDTH), F32)
        for ref in (kv_sc, state_sc, a_state_sc, a_carry_sc):
            ref[...] = jnp.zeros_like(ref)

    z = _rms(x_ref[...], g_ref[...])
    row = tile_in_seq * tm + lax.broadcasted_iota(jnp.int32, (tm, 1), 0)
    zb = jnp.where(row >= pad, z, 0.0).astype(BF16)
    project = lambda lo, hi: jnp.dot(zb, w_ref[:, lo:hi], preferred_element_type=F32)
    mixer_cols = project(OFF_A, IN_COLS)
    u_a = mixer_cols[:, :A_COLS]
    u_b = mixer_cols[:, OFF_B - OFF_A:OFF_C - OFF_A]
    u_c = mixer_cols[:, OFF_C - OFF_A:OFF_C - OFF_A + C_COLS]
    u_d = mixer_cols[:, OFF_C - OFF_A + C_COLS:]

    def dense():
        for n, c0 in enumerate(range(0, GATE_COLS, tn)):
            for lo in range(c0, c0 + tn, DENSE_STEP):
                o_ref[:, lo:lo + DENSE_STEP] = _sigmoid(project(lo, lo + DENSE_STEP))
                yield
            if n < tm // T:
                rows = slice(n * T, (n + 1) * T)
                pooled = _pool_windows(u_b[rows], pool_buf, tile_in_seq * tm + rows.start, pad=pad)
                o_ref[rows, GATE_COLS:] = _dot(pooled, mix_ref[...]) * scale_ref[...]
                yield

    def mixers():
        for _ in range(MIXER_DELAY):
            yield
        for k in range(tm // T):
            rows = slice(k * T, (k + 1) * T)
            kv_prev = kv_sc[...] if k == 0 else u_c[(k - 1) * T:k * T, C_WIDTH:]
            jt = tile_in_seq * (tm // T) + k
            yield from _round_robin(
                _swa_tile(yc_ref.at[pl.ds(k * T, T)], u_c[rows], kv_prev, sinks_ref, bias_ref, jt, pad=pad),
                _hgrn_tile(yd_ref.at[pl.ds(k * T, T)], u_d[rows], state_sc, lb_ref, ng_ref, sums_ref,
                           level_ref, jt, pad=pad))
        kv_sc[...] = u_c[tm - T:, C_WIDTH:]

    rwkv = _rwkv_tile(ya_ref, vout_ref, u_a, None if vf_ref is None else vf_ref[...], prm, a_state_sc,
                      a_carry_sc)
    _interleave(dense(), rwkv, mixers())


def _in_proj(h2, v_first, g, w, mix_bd, scale, sinks, lb, norm_g, pa, *, pad, lp):
    m = h2.shape[0]
    tm = ROW_TILE
    assert lp % tm == 0 and tm % SEQ_TILE == 0
    has_vres = v_first is not None
    sums, level = _hgrn_constants()
    bias = _swa_constants(pad)
    const = lambda arr: pl.BlockSpec(arr.shape, lambda i: (0,) * arr.ndim, pipeline_mode=pl.Buffered(1))
    rows = lambda w: pl.BlockSpec((tm, w), lambda i: (i, 0))
    names = A_PARAMS + (A_VRES_PARAMS if has_vres else ())
    shape = lambda w: jax.ShapeDtypeStruct((m, w), F32)
    n_out = 4 if has_vres else 5
    widths = (GATE_COLS + B_WIDTH, A_WIDTH, C_WIDTH, D_WIDTH, A_WIDTH)[:n_out]
    return pl.pallas_call(
        functools.partial(_in_proj_kernel, has_vres=has_vres, pad=pad, tiles_per_seq=lp // tm, tm=tm,
                          tn=1024),
        grid=(m // tm,),
        in_specs=[pl.BlockSpec(memory_space=pltpu.SMEM), rows(D_MODEL)]
                 + ([rows(A_WIDTH)] if has_vres else [])
                 + [const(a) for a in (g, w, mix_bd, scale, bias, lb, norm_g, sums, level)]
                 + [const(pa[n]) for n in names],
        out_specs=tuple(rows(w) for w in widths),
        out_shape=tuple(shape(w) for w in widths),
        scratch_shapes=[pltpu.VMEM((B_HALO + tm, B_WIDTH), F32),
                        pltpu.VMEM((SEQ_TILE, 2 * C_KV_WIDTH), F32),
                        pltpu.VMEM((D_WIDTH, D_WIDTH), F32),
                        pltpu.VMEM((A_WIDTH, A_WIDTH), F32),
                        pltpu.VMEM((8, A_COLS), F32)],
        compiler_params=pltpu.CompilerParams(dimension_semantics=("arbitrary",),
                                             vmem_limit_bytes=VMEM_LIMIT),
        name="in_proj_mixers",
    )(sinks, h2, *([v_first] if has_vres else []), g, w, mix_bd, scale, bias, lb, norm_g, sums, level,
      *[pa[n] for n in names])


def _rwkv_tile(y_ref, vout_ref, u, vf, prm, state_sc, carry_sc):
    C = A_CHUNK
    N = A_HEAD_DIM
    W = A_WIDTH
    rows_total = u.shape[0]
    n_chunks = rows_total // C

    row = lax.broadcasted_iota(jnp.int32, (rows_total, 1), 0)
    prev = jnp.where(row == 0, carry_sc[0:1, :], pltpu.roll(u, 1, axis=0))
    carry_sc[0:1, :] = u[rows_total - 1:rows_total, :]
    ones_bd = _head_ones(W, N)
    lane = lax.broadcasted_iota(jnp.int32, (1, 128), 1)

    x = u + (prev - u) * prm["mu"][...]
    r = x[:, 0:W]
    k = x[:, W:2 * W]
    v = x[:, 2 * W:3 * W]
    slab = x[:, 3 * W:3 * W + 128]
    gd = x[:, 3 * W + 128:]
    slab = jnp.where(lane < 64, jnp.tanh(slab), slab)
    wa = _dot(slab, prm["wwa"][...])
    yield
    logw = -math.exp(-0.5) * _sigmoid(prm["w0"][...] + wa[:, :W])
    a = _sigmoid(prm["a0"][...] + wa[:, W:])
    g = _dot(_sigmoid(gd), prm["gup"][...])
    yield
    if vf is not None:
        low_rank = _dot(v, prm["vd"][...])
        yield
        v = v + (vf - v) * _sigmoid(prm["v0"][...] + _dot(low_rank, prm["vu"][...]))
    else:
        vout_ref[...] = v
    kkr = k * prm["kk"][...]
    kk = kkr / jnp.maximum(jnp.sqrt(_head_sum(kkr * kkr, ones_bd)), 1e-12)
    k2 = k * (1.0 + (a - 1.0) * prm["ka"][...])
    bhat = kk * a
    yield

    ri = lax.broadcasted_iota(jnp.int32, (W, W), 0)
    ci = lax.broadcasted_iota(jnp.int32, (W, W), 1)
    tw = lax.broadcasted_iota(jnp.int32, (C, W), 0)
    sw = lax.broadcasted_iota(jnp.int32, (C, W), 1) % C
    tri_s = lax.broadcasted_iota(jnp.int32, (C, 4 * C), 1)
    tri = jnp.where((tri_s % C <= lax.broadcasted_iota(jnp.int32, (C, 4 * C), 0)) & (tri_s < 3 * C),
                    1.0, 0.0).astype(BF16)

    def stack(t):
        return jnp.concatenate([t.astype(BF16)] * A_HEADS, axis=0) * ones_bd

    chunks = [slice(i * C, (i + 1) * C) for i in range(n_chunks)]
    cl = [_const_dot(tri, logw[sl]) for sl in chunks]
    yield
    pre, aa_ops = [], []
    for sl, c in zip(chunks, cl):
        cl_last = c[C - 1:C, :]
        e_neg = jnp.exp(-c)
        e_last = jnp.exp(cl_last - c)
        at = -kk[sl] * jnp.exp(c - logw[sl])
        rt = r[sl] * jnp.exp(c)
        aa_ops.append((jnp.concatenate([at, rt], axis=0),
                       jnp.concatenate([stack(bhat[sl] * e_neg), stack(k2[sl] * e_neg)], axis=0)))
        pre.append(dict(at_s=stack(at), v_s=stack(v[sl]), btil=stack(bhat[sl] * e_last),
                        ktil=stack(k2[sl] * e_last), rt=rt, gam=jnp.exp(cl_last)))
        yield
    aa = [_dot_nt(lhs, rhs) for lhs, rhs in aa_ops]
    yield
    for p, m in zip(pre, aa):
        p["a_ab"] = stack(jnp.where(sw < tw, m[:C, :W], 0.0))
        p["a_ak"] = stack(jnp.where(sw < tw, m[:C, W:], 0.0))
        p["a_rb"] = jnp.where(sw <= tw, m[C:, :W], 0.0).astype(BF16)
        p["a_rk"] = jnp.where(sw <= tw, m[C:, W:], 0.0).astype(BF16)
    yield
    for p in pre:
        p["aakv"] = _mm(p["a_ak"], p["v_s"]).astype(BF16)
    yield

    box = {}
    yield from _unit_lower_inverses([p["a_ab"] for p in pre], ri, ci, box)
    tinv = box["tinv"]
    w_t = [_mm(t, p["at_s"]).astype(BF16) for t, p in zip(tinv, pre)]
    yield
    u_t = [_mm(t, p["aakv"]).astype(BF16) for t, p in zip(tinv, pre)]
    yield
    m_e = [_mm_tn(p["btil"], w).astype(BF16) for p, w in zip(pre, w_t)]
    yield
    n_mat = [_mm_tn(p["btil"], x) + _mm_tn(p["ktil"], p["v_s"]) for p, x in zip(pre, u_t)]
    yield
    g_col = [jnp.sum(jnp.where(ri == ci, p["gam"], 0.0), axis=1, keepdims=True) for p in pre]
    yield
    st = state_sc[...]
    outs = []
    def out_terms(i):
        p = pre[i]
        return ((p["rt"] + _mm(p["a_rb"], w_t[i])).astype(BF16),
                _mm(p["a_rb"], u_t[i]) + _mm(p["a_rk"], p["v_s"]))

    q_mat, p_mat = out_terms(0)
    for i in range(n_chunks):
        sb = st.astype(BF16)
        moved = _mm(m_e[i], sb)
        outs.append(_mm(q_mat, sb) + p_mat)
        if i + 1 < n_chunks:
            q_mat, p_mat = out_terms(i + 1)
        st = g_col[i] * st + moved + n_mat[i]
        yield
    state_sc[...] = st

    o = jnp.concatenate(outs, axis=0)
    mean = _head_sum(o, ones_bd) * (1.0 / N)
    d = o - mean
    var = _head_sum(d * d, ones_bd) * (1.0 / N)
    o = d * lax.rsqrt(var + A_GN_EPS) * prm["lnw"][...] + prm["lnb"][...]
    bonus = _head_sum(r * k2 * prm["rk"][...], ones_bd) * v
    y_ref[...] = (o + bonus) * g


def _pool_windows(u, buf, row0, *, pad):
    rows = u.shape[0]
    buf[B_HALO:B_HALO + rows, :] = u
    lane_group = lax.broadcasted_iota(jnp.int32, (1, B_WIDTH), 1) // B_GROUP_DIM
    acc = u
    win = jnp.zeros_like(u)
    for s in range(1, max(B_WINDOWS)):
        acc = acc + buf[B_HALO - s:B_HALO - s + rows, :]
        if s + 1 in B_WINDOWS:
            win = jnp.where(lane_group == B_WINDOWS.index(s + 1), acc, win)
    wlane = jnp.zeros((1, B_WIDTH), jnp.int32)
    for gi, w in enumerate(B_WINDOWS):
        wlane = jnp.where(lane_group == gi, w, wlane)
    t_real = row0 + lax.broadcasted_iota(jnp.int32, (rows, 1), 0) - pad
    cnt = jnp.minimum(jnp.maximum(t_real + 1, 1), wlane).astype(F32)
    buf[0:B_HALO, :] = u[rows - B_HALO:, :]
    return win / cnt - u


def _swa_constants(pad):
    T = SEQ_TILE
    dist = T + np.arange(T)[:, None] - np.arange(2 * T)[None, :]
    band = (dist >= 0) & (dist < T)
    n = -(-pad // T) + 2
    bias = np.empty((n, C_KV_HEADS, C_GROUP * T, 2 * T), np.float32)
    for j in range(n):
        ok = band & ((j - 1) * T + np.arange(2 * T)[None, :] >= pad)
        for hq in range(C_HEADS):
            slope = np.float32(2.0 ** (-8.0 * (hq + 1) / C_HEADS))
            rows = slice((hq % C_GROUP) * T, (hq % C_GROUP + 1) * T)
            bias[j, hq // C_GROUP, rows] = np.where(ok, -slope * dist.astype(np.float32), MASK_VALUE)
    return jnp.asarray(bias)


def _swa_tile(y_ref, blk, kv_prev, sinks_ref, bias_ref, j, *, pad):
    T = SEQ_TILE
    G = C_GROUP
    kv = blk[:, C_WIDTH:]
    kw = jnp.concatenate([kv_prev[:, :C_KV_WIDTH], kv[:, :C_KV_WIDTH]], axis=0)
    vw = jnp.concatenate([kv_prev[:, C_KV_WIDTH:], kv[:, C_KV_WIDTH:]], axis=0)
    lane = lax.broadcasted_iota(jnp.int32, (1, 2 * C_HEAD_DIM), 1)
    low = lane < C_HEAD_DIM
    kw_sw = pltpu.roll(kw, C_HEAD_DIM, axis=1)
    vw_sw = pltpu.roll(vw, C_HEAD_DIM, axis=1)
    head = lax.broadcasted_iota(jnp.int32, (G * T, 1), 0) // T
    variant = jnp.minimum(j, bias_ref.shape[0] - 1)
    tiles = []
    for hk in range(C_KV_HEADS):
        k2 = jnp.where(low == (hk == 0), kw, kw_sw).astype(BF16)
        v2 = jnp.where(low == (hk == 0), vw, vw_sw).astype(BF16)
        q_rows = []
        sink = jnp.zeros((G * T, 1), F32)
        for gq in range(G):
            hq = hk * G + gq
            pair = blk[:, (hq // 2) * 128:(hq // 2 + 1) * 128]
            q_rows.append(jnp.where(low == (hq % 2 == 0), pair * (C_HEAD_DIM ** -0.5), 0.0))
            sink = jnp.where(head == gq, sinks_ref[hq], sink)
        s = _dot_nt(jnp.concatenate(q_rows, axis=0), k2)
        logits = s + bias_ref[variant, hk]
        m = jnp.maximum(jnp.max(logits, axis=-1, keepdims=True), sink)
        p = jnp.exp(logits - m)
        denom = jnp.sum(p, axis=-1, keepdims=True) + jnp.exp(sink - m)
        yield
        o = _dot(p, v2) / denom
        for pr in range(G // 2):
            tiles.append(jnp.where(low, o[2 * pr * T:(2 * pr + 1) * T], o[(2 * pr + 1) * T:(2 * pr + 2) * T]))
        yield
    y_ref[...] = jnp.concatenate(tiles, axis=1)


def _hgrn_tile(y_ref, u, state_sc, lb_ref, ng_ref, sums_ref, level_ref, j, *, pad):
    T = SEQ_TILE
    N = D_KEY_DIM
    lb = lb_ref[...]
    q = _silu(u[:, :D_WIDTH])
    fpre = u[:, D_WIDTH:2 * D_WIDTH]
    vin = u[:, 2 * D_WIDTH:3 * D_WIDTH]
    gate = u[:, 3 * D_WIDTH:]
    sig = _sigmoid(fpre)
    f = lb + (1.0 - lb) * sig
    t_real = j * T + lax.broadcasted_iota(jnp.int32, (T, 1), 0) - pad
    logf = jnp.where(t_real >= 0, jnp.log(jnp.maximum(f, 1e-30)), 0.0)
    kx = (1.0 - lb) * (1.0 - sig)

    all_sums = _mm(sums_ref[...], jnp.concatenate(_split2(logf), axis=0))
    sums = lambda i: all_sums[i * T:(i + 1) * T]
    same_head = (lax.broadcasted_iota(jnp.int32, (D_WIDTH, D_WIDTH), 0) // N
                 == lax.broadcasted_iota(jnp.int32, (D_WIDTH, D_WIDTH), 1) // N)
    head_rows = jnp.where(lax.broadcasted_iota(jnp.int32, (D_HEADS * T, D_WIDTH), 0) // T
                          == lax.broadcasted_iota(jnp.int32, (D_HEADS * T, D_WIDTH), 1) // N,
                          1.0, 0.0).astype(BF16)
    stack = lambda t: jnp.concatenate([t.astype(BF16)] * D_HEADS, axis=0) * head_rows
    k_st = stack(kx)
    v_st = stack(vin)
    lv = level_ref[...]
    att = jnp.where(lv == 0, _dot_nt(q, k_st), 0.0)
    for l in range(D_LEVELS):
        if l == 0:
            q_l, k_l = q * jnp.exp(logf), k_st
        else:
            q_l = q * jnp.exp(sums(2 * l - 2))
            k_l = k_st * jnp.concatenate([jnp.exp(sums(2 * l - 1)).astype(BF16)] * D_HEADS, axis=0)
        att = jnp.where(lv == l + 1, _dot_nt(q_l, k_l), att)
        yield
    b = sums(2 * D_LEVELS - 2)
    b_rev = sums(2 * D_LEVELS - 1)
    st = state_sc[...]
    o = _dot(att, v_st) + _dot_nt(q * jnp.exp(b), st)
    upd = jnp.where(same_head, _dot_tn(vin, kx * jnp.exp(b_rev)), 0.0)
    state_sc[...] = st * jnp.exp(b[T - 1:T, :]) + upd

    ones_bd = _head_ones(D_WIDTH, N)
    ms = _head_sum(o * o, ones_bd) * (1.0 / N)
    y_ref[...] = o * lax.rsqrt(ms + NORM_EPS) * ng_ref[...] * _silu(gate)


def _hgrn_constants():
    T = SEQ_TILE
    t = np.arange(T)[:, None]
    i = np.arange(T)[None, :]
    mats = []
    for l in list(range(1, D_LEVELS)) + [D_LEVELS]:
        h = 2 ** l
        same = (t // h) == (i // h)
        mats += [same & (i <= t), same & (i > t)]
    sums = np.tile(np.concatenate(mats, axis=0).astype(np.float32), (1, 2))
    x = t ^ i
    level = np.where(i == t, 0, np.where(i < t, np.floor(np.log2(np.maximum(x, 1))).astype(np.int64) + 1, -1))
    return jnp.asarray(sums, BF16), jnp.asarray(np.tile(level, (1, D_HEADS)), jnp.int32)


def _merge_ffn_kernel(h_ref, gates_ref, ya_ref, yb_ref, yc_ref, yd_ref, wb_ref, wo_ref, gf_ref, wu_ref,
                      wd_ref, *rest, ff_chunk):
    fg_ref, o_ref = rest if len(rest) == 2 else (None,) + rest
    merged = None
    row = 0
    for bi, y_ref in enumerate((ya_ref, yb_ref, yc_ref, yd_ref)):
        w = y_ref.shape[-1]
        part = gates_ref[:, bi * D_MODEL:(bi + 1) * D_MODEL] * jnp.dot(
            y_ref[...].astype(BF16), wb_ref[row:row + w, :], preferred_element_type=F32)
        merged = part if merged is None else merged + part
        row += w
    h = h_ref[...] + jnp.dot(merged.astype(BF16), wo_ref[...], preferred_element_type=F32)
    zb = _rms(h, gf_ref[...]).astype(BF16)

    def up_proj(c):
        cols = lambda base: wu_ref[:, base + c * ff_chunk:base + (c + 1) * ff_chunk]
        return (jnp.dot(zb, cols(0), preferred_element_type=F32),
                jnp.dot(zb, cols(D_FF), preferred_element_type=F32))

    acc = h
    n_chunks = D_FF // ff_chunk
    gu, up = up_proj(0)
    for c in range(n_chunks):
        nxt = up_proj(c + 1) if c + 1 < n_chunks else None
        acc = acc + jnp.dot((_silu(gu) * up).astype(BF16), wd_ref[c * ff_chunk:(c + 1) * ff_chunk, :],
                            preferred_element_type=F32)
        if nxt is not None:
            gu, up = nxt
    o_ref[...] = acc if fg_ref is None else _rms(acc, fg_ref[...])


def _merge_ffn(h2, proj2, ya, yc, yd, wb, wo, gf, wu, wd, final_g=None):
    m = h2.shape[0]
    tm = ROW_TILE
    rows = lambda w, col=0: pl.BlockSpec((tm, w), lambda i: (i, col))
    const = lambda arr: pl.BlockSpec(arr.shape, lambda i: (0, 0), pipeline_mode=pl.Buffered(1))
    last = () if final_g is None else (final_g,)
    return pl.pallas_call(
        functools.partial(_merge_ffn_kernel, ff_chunk=256),
        grid=(m // tm,),
        in_specs=[rows(D_MODEL), rows(GATE_COLS), rows(A_WIDTH), rows(B_WIDTH, GATE_COLS // B_WIDTH),
                  rows(C_WIDTH), rows(D_WIDTH), const(wb), const(wo), const(gf), const(wu),
                  const(wd)] + [const(g) for g in last],
        out_specs=rows(D_MODEL),
        out_shape=jax.ShapeDtypeStruct((m, D_MODEL), F32),
        compiler_params=pltpu.CompilerParams(dimension_semantics=("arbitrary",),
                                             vmem_limit_bytes=VMEM_LIMIT),
        name="merge_ffn",
    )(h2, proj2, ya, proj2, yc, yd, wb, wo, gf, wu, wd, *last)


def _block_diag(blocks):
    n, r, c = blocks.shape
    out = jnp.zeros((n * r, n * c), blocks.dtype)
    for i in range(n):
        out = out.at[i * r:(i + 1) * r, i * c:(i + 1) * c].set(blocks[i])
    return out


def kernel(x, meta, norm_mix, norm_ffn, norm_final, w_in, w_branch, w_out, a_mu, a_w_up, a_w0, a_a_up, a_a0, a_g_up, a_kk, a_ka, a_rk, a_ln_w, a_ln_b, a_vres_down, a_vres_up, a_vres0, b_mix, b_scale, c_sinks, d_lower_bounds, d_norm, w_ffn_up, w_ffn_down):
    bsz, seq, _ = x.shape
    depth = w_in.shape[0]
    T = SEQ_TILE
    L = N_META + seq
    pad = (-L) % T
    lp = L + pad
    assert (pad + N_META) % T == 0 and seq % T == 0
    h = jnp.concatenate([jnp.zeros((bsz, pad, D_MODEL), F32),
                         jnp.broadcast_to(meta.astype(F32)[None], (bsz, N_META, D_MODEL)),
                         x.astype(F32)], axis=1).reshape(bsz * lp, D_MODEL)
    lb_w = jax.nn.softmax(d_lower_bounds.astype(F32), axis=0)
    lb_table = jnp.cumsum(lb_w, axis=0) - lb_w[0]
    row2 = lambda t: t.reshape(1, -1).astype(F32)
    v_first = None
    for l in range(depth):
        wwa = jnp.zeros((128, 2 * A_WIDTH), F32)
        wwa = wwa.at[:64, :A_WIDTH].set(a_w_up[l]).at[64:, A_WIDTH:].set(a_a_up[l])
        pa = dict(mu=row2(a_mu[l]), wwa=wwa.astype(BF16), w0=row2(a_w0[l]), a0=row2(a_a0[l]),
                  gup=a_g_up[l].astype(BF16), kk=row2(a_kk[l]), ka=row2(a_ka[l]), rk=row2(a_rk[l]),
                  lnw=row2(a_ln_w[l]), lnb=row2(a_ln_b[l]))
        if l > 0:
            pa.update(vd=a_vres_down[l - 1].astype(BF16), vu=a_vres_up[l - 1].astype(BF16),
                      v0=row2(a_vres0[l - 1]))
        outs = _in_proj(h, v_first, row2(norm_mix[l]), w_in[l].astype(BF16),
                        _block_diag(b_mix[l]).astype(BF16), row2(b_scale[l]), c_sinks[l].astype(F32),
                        row2(lb_table[l]), row2(d_norm[l]), pa, pad=pad, lp=lp)
        proj2, y_a, y_c, y_d = outs[:4]
        if l == 0:
            v_first = outs[4]
        h = _merge_ffn(h, proj2, y_a, y_c, y_d,
                       w_branch[l].astype(BF16), w_out[l].astype(BF16), row2(norm_ffn[l]),
                       w_ffn_up[l].astype(BF16), w_ffn_down[l].astype(BF16),
                       row2(norm_final) if l == depth - 1 else None)
    return h.reshape(bsz, lp, D_MODEL)[:, pad + N_META:]
```

```python
import functools
import math

import jax
import jax.numpy as jnp
import numpy as np
from jax import lax
from jax.experimental import pallas as pl
from jax.experimental.pallas import tpu as pltpu

F32 = jnp.float32
BF16 = jnp.bfloat16

D_MODEL = 1024
N_META = 16
NORM_EPS = 1e-6
MASK_VALUE = -1e30

A_HEADS = 4
A_HEAD_DIM = 64
A_WIDTH = 256
A_GN_EPS = 64e-5
A_COLS = 1024
A_CHUNK = 64

B_WIDTH = 256
B_GROUP_DIM = 64
B_WINDOWS = (2, 4, 8, 16)
B_HALO = 16

C_HEADS = 8
C_KV_HEADS = 2
C_GROUP = 4
C_HEAD_DIM = 64
C_WIDTH = 512
C_KV_WIDTH = 128
C_COLS = 768

D_HEADS = 4
D_KEY_DIM = 64
D_WIDTH = 256
D_COLS = 1024
D_LEVELS = 7

D_FF = 2816
GATE_COLS = 4 * D_MODEL
OFF_A = GATE_COLS
OFF_B = OFF_A + A_COLS
OFF_C = OFF_B + B_WIDTH
IN_COLS = OFF_C + C_COLS + D_COLS

SEQ_TILE = 128
ROW_TILE = 384
DENSE_STEP = 256
MIXER_DELAY = 7
V7X_VMEM_BYTES = 64 * 1024 * 1024
VMEM_LIMIT = V7X_VMEM_BYTES - 4 * 1024 * 1024


def _dot(a, b):
    return jnp.dot(a.astype(BF16), b.astype(BF16), preferred_element_type=F32)


def _dot_nt(a, b):
    return lax.dot_general(a.astype(BF16), b.astype(BF16), (((1,), (1,)), ((), ())),
                           preferred_element_type=F32)


def _dot_tn(a, b):
    return lax.dot_general(a.astype(BF16), b.astype(BF16), (((0,), (0,)), ((), ())),
                           preferred_element_type=F32)


def _round_robin(*gens):
    gens = list(gens)
    while gens:
        for gen in list(gens):
            if next(gen, "done") == "done":
                gens.remove(gen)
        yield


def _interleave(*gens):
    for _ in _round_robin(*gens):
        pass


def _split2(x):
    hi = x.astype(BF16)
    lo = (x - hi.astype(F32)).astype(BF16)
    return hi, lo


def _split3(x):
    hi = x.astype(BF16)
    r = x - hi.astype(F32)
    mid = r.astype(BF16)
    lo = (r - mid.astype(F32)).astype(BF16)
    return hi, mid, lo


def _const_dot(c4, x):
    hi, mid, lo = _split3(x)
    return jnp.dot(c4, jnp.concatenate([hi, mid, lo, jnp.zeros_like(hi)], axis=0),
                   preferred_element_type=F32)


def _mm(a, b):
    return jnp.dot(a, b, preferred_element_type=F32)


def _mm_tn(a, b):
    return lax.dot_general(a, b, (((0,), (0,)), ((), ())), preferred_element_type=F32)


def _unit_lower_inverses(mats, ri, ci, box):
    in16 = (ri // 16) == (ci // 16)
    in32 = (ri // 32) == (ci // 32)
    diag = ri == ci
    as_mask = lambda cond: jnp.where(cond, 1.0, 0.0).astype(BF16)
    plus_eye = lambda m: jnp.where(diag, 1.0, m).astype(BF16)
    m16, eye = as_mask(in16), as_mask(diag)
    xb = [a * m16 for a in mats]
    t = [xi + eye for xi in xb]
    yield
    for _ in range(3):
        x = [_mm(xi, xi) for xi in xb]
        xb = [xi.astype(BF16) for xi in x]
        t = [_mm(ti, plus_eye(xi)).astype(BF16) for ti, xi in zip(t, x)]
        yield
    n = mats[0].shape[0]
    for half, sel in ((16, as_mask(in32 & ~in16)), (32, as_mask(~in32))):
        starts = range(0, n, 2 * half)
        low = lambda m: jnp.concatenate([m[r + half:r + 2 * half] for r in starts], axis=0)
        put = lambda rows, base: jnp.concatenate(
            [p for k, r in enumerate(starts) for p in (base[r:r + half], rows[k * half:(k + 1) * half])],
            axis=0)
        lt = [_mm(low(a * sel), ti).astype(BF16) for a, ti in zip(mats, t)]
        zero = jnp.zeros_like(t[0])
        t = [put(_mm(low(ti), put(li, zero) + eye).astype(BF16), ti) for ti, li in zip(t, lt)]
        yield
    box["tinv"] = t


def _head_sum(x, ones_bd):
    return jnp.dot(x.astype(BF16), ones_bd, preferred_element_type=F32)


def _head_ones(width, seg):
    r = lax.broadcasted_iota(jnp.int32, (width, width), 0) // seg
    c = lax.broadcasted_iota(jnp.int32, (width, width), 1) // seg
    return jnp.where(r == c, 1.0, 0.0).astype(BF16)


def _sigmoid(x):
    return 1.0 / (1.0 + jnp.exp(-x))


def _silu(x):
    return x * _sigmoid(x)


def _rms(x, g):
    return x * lax.rsqrt(jnp.mean(x * x, axis=-1, keepdims=True) + NORM_EPS) * g


A_PARAMS = ("mu", "wwa", "w0", "a0", "gup", "kk", "ka", "rk", "lnw", "lnb")
A_VRES_PARAMS = ("vd", "vu", "v0")


def _in_proj_kernel(*refs, has_vres, pad, tiles_per_seq, tm, tn):
    refs = list(refs)
    sinks_ref, x_ref = refs[:2]
    vf_ref = refs[2] if has_vres else None
    k0 = 3 if has_vres else 2
    g_ref, w_ref, mix_ref, scale_ref, bias_ref, lb_ref, ng_ref, sums_ref, level_ref = refs[k0:k0 + 9]
    names = A_PARAMS + (A_VRES_PARAMS if has_vres else ())
    prm = dict(zip(names, refs[k0 + 9:k0 + 9 + len(names)]))
    outs = refs[k0 + 9 + len(names):]
    o_ref, ya_ref, yc_ref, yd_ref = outs[:4]
    vout_ref = None if has_vres else outs[4]
    pool_buf, kv_sc, state_sc, a_state_sc, a_carry_sc = outs[-5:]
    T = SEQ_TILE
    tile_in_seq = pl.program_id(0) % tiles_per_seq

    @pl.when(tile_in_seq == 0)
    def _():
        pool_buf[0:B_HALO, :] = jnp.zeros((B_HALO, B_WIDTH), F32)
        for ref in (kv_sc, state_sc, a_state_sc, a_carry_sc):
            ref[...] = jnp.zeros_like(ref)

    z = _rms(x_ref[...], g_ref[...])
    row = tile_in_seq * tm + lax.broadcasted_iota(jnp.int32, (tm, 1), 0)
    zb = jnp.where(row >= pad, z, 0.0).astype(BF16)
    project = lambda lo, hi: jnp.dot(zb, w_ref[:, lo:hi], preferred_element_type=F32)
    mixer_cols = project(OFF_A, IN_COLS)
    u_a = mixer_cols[:, :A_COLS]
    u_b = mixer_cols[:, OFF_B - OFF_A:OFF_C - OFF_A]
    u_c = mixer_cols[:, OFF_C - OFF_A:OFF_C - OFF_A + C_COLS]
    u_d = mixer_cols[:, OFF_C - OFF_A + C_COLS:]

    def dense():
        for n, c0 in enumerate(range(0, GATE_COLS, tn)):
            for lo in range(c0, c0 + tn, DENSE_STEP):
                o_ref[:, lo:lo + DENSE_STEP] = _sigmoid(project(lo, lo + DENSE_STEP))
                yield
            if n < tm // T:
                rows = slice(n * T, (n + 1) * T)
                pooled = _pool_windows(u_b[rows], pool_buf, tile_in_seq * tm + rows.start, pad=pad)
                o_ref[rows, GATE_COLS:] = _dot(pooled, mix_ref[...]) * scale_ref[...]
                yield

    def mixers():
        for _ in range(MIXER_DELAY):
            yield
        for k in range(tm // T):
            rows = slice(k * T, (k + 1) * T)
            kv_prev = kv_sc[...] if k == 0 else u_c[(k - 1) * T:k * T, C_WIDTH:]
            jt = tile_in_seq * (tm // T) + k
            yield from _round_robin(
                _swa_tile(yc_ref.at[pl.ds(k * T, T)], u_c[rows], kv_prev, sinks_ref, bias_ref, jt, pad=pad),
                _hgrn_tile(yd_ref.at[pl.ds(k * T, T)], u_d[rows], state_sc, lb_ref, ng_ref, sums_ref,
                           level_ref, jt, pad=pad))
        kv_sc[...] = u_c[tm - T:, C_WIDTH:]

    rwkv = _rwkv_tile(ya_ref, vout_ref, u_a, None if vf_ref is None else vf_ref[...], prm, a_state_sc,
                      a_carry_sc)
    _interleave(dense(), rwkv, mixers())


def _in_proj(h2, v_first, g, w, mix_bd, scale, sinks, lb, norm_g, pa, *, pad, lp):
    m = h2.shape[0]
    tm = ROW_TILE
    assert lp % tm == 0 and tm % SEQ_TILE == 0
    has_vres = v_first is not None
    sums, level = _hgrn_constants()
    bias = _swa_constants(pad)
    const = lambda arr: pl.BlockSpec(arr.shape, lambda i: (0,) * arr.ndim, pipeline_mode=pl.Buffered(1))
    rows = lambda w: pl.BlockSpec((tm, w), lambda i: (i, 0))
    names = A_PARAMS + (A_VRES_PARAMS if has_vres else ())
    n_out = 4 if has_vres else 5
    widths = (GATE_COLS + B_WIDTH, A_WIDTH, C_WIDTH, D_WIDTH, A_WIDTH)[:n_out]
    dtypes = (F32, BF16, BF16, BF16, F32)[:n_out]
    return pl.pallas_call(
        functools.partial(_in_proj_kernel, has_vres=has_vres, pad=pad, tiles_per_seq=lp // tm, tm=tm,
                          tn=1024),
        grid=(m // tm,),
        in_specs=[pl.BlockSpec(memory_space=pltpu.SMEM), rows(D_MODEL)]
                 + ([rows(A_WIDTH)] if has_vres else [])
                 + [const(a) for a in (g, w, mix_bd, scale, bias, lb, norm_g, sums, level)]
                 + [const(pa[n]) for n in names],
        out_specs=tuple(rows(w) for w in widths),
        out_shape=tuple(jax.ShapeDtypeStruct((m, w), dt) for w, dt in zip(widths, dtypes)),
        scratch_shapes=[pltpu.VMEM((B_HALO + tm, B_WIDTH), F32),
                        pltpu.VMEM((SEQ_TILE, 2 * C_KV_WIDTH), F32),
                        pltpu.VMEM((D_WIDTH, D_WIDTH), F32),
                        pltpu.VMEM((A_WIDTH, A_WIDTH), F32),
                        pltpu.VMEM((8, A_COLS), F32)],
        compiler_params=pltpu.CompilerParams(dimension_semantics=("arbitrary",),
                                             vmem_limit_bytes=VMEM_LIMIT),
        name="in_proj_mixers",
    )(sinks, h2, *([v_first] if has_vres else []), g, w, mix_bd, scale, bias, lb, norm_g, sums, level,
      *[pa[n] for n in names])


def _rwkv_tile(y_ref, vout_ref, u, vf, prm, state_sc, carry_sc):
    C = A_CHUNK
    N = A_HEAD_DIM
    W = A_WIDTH
    rows_total = u.shape[0]
    n_chunks = rows_total // C

    row = lax.broadcasted_iota(jnp.int32, (rows_total, 1), 0)
    prev = jnp.where(row == 0, carry_sc[0:1, :], pltpu.roll(u, 1, axis=0))
    carry_sc[0:1, :] = u[rows_total - 1:rows_total, :]
    ones_bd = _head_ones(W, N)
    lane = lax.broadcasted_iota(jnp.int32, (1, 128), 1)

    x = u + (prev - u) * prm["mu"][...]
    r = x[:, 0:W]
    k = x[:, W:2 * W]
    v = x[:, 2 * W:3 * W]
    slab = x[:, 3 * W:3 * W + 128]
    gd = x[:, 3 * W + 128:]
    slab = jnp.where(lane < 64, jnp.tanh(slab), slab)
    wa = _dot(slab, prm["wwa"][...])
    yield
    logw = -math.exp(-0.5) * _sigmoid(prm["w0"][...] + wa[:, :W])
    a = _sigmoid(prm["a0"][...] + wa[:, W:])
    g = _dot(_sigmoid(gd), prm["gup"][...])
    yield
    if vf is not None:
        low_rank = _dot(v, prm["vd"][...])
        yield
        v = v + (vf - v) * _sigmoid(prm["v0"][...] + _dot(low_rank, prm["vu"][...]))
    else:
        vout_ref[...] = v
    kkr = k * prm["kk"][...]
    kk = kkr / jnp.maximum(jnp.sqrt(_head_sum(kkr * kkr, ones_bd)), 1e-12)
    k2 = k * (1.0 + (a - 1.0) * prm["ka"][...])
    bhat = kk * a
    yield

    ri = lax.broadcasted_iota(jnp.int32, (W, W), 0)
    ci = lax.broadcasted_iota(jnp.int32, (W, W), 1)
    tw = lax.broadcasted_iota(jnp.int32, (C, W), 0)
    sw = lax.broadcasted_iota(jnp.int32, (C, W), 1) % C
    tri_s = lax.broadcasted_iota(jnp.int32, (C, 4 * C), 1)
    tri = jnp.where((tri_s % C <= lax.broadcasted_iota(jnp.int32, (C, 4 * C), 0)) & (tri_s < 3 * C),
                    1.0, 0.0).astype(BF16)

    def stack(t):
        return jnp.concatenate([t.astype(BF16)] * A_HEADS, axis=0) * ones_bd

    chunks = [slice(i * C, (i + 1) * C) for i in range(n_chunks)]
    cl = [_const_dot(tri, logw[sl]) for sl in chunks]
    yield
    pre, aa_ops = [], []
    for sl, c in zip(chunks, cl):
        cl_last = c[C - 1:C, :]
        e_neg = jnp.exp(-c)
        e_last = jnp.exp(cl_last - c)
        at = -kk[sl] * jnp.exp(c - logw[sl])
        rt = r[sl] * jnp.exp(c)
        aa_ops.append((jnp.concatenate([at, rt], axis=0),
                       jnp.concatenate([stack(bhat[sl] * e_neg), stack(k2[sl] * e_neg)], axis=0)))
        pre.append(dict(at_s=stack(at), v_s=stack(v[sl]), btil=stack(bhat[sl] * e_last),
                        ktil=stack(k2[sl] * e_last), rt=rt, gam=jnp.exp(cl_last)))
        yield
    aa = [_dot_nt(lhs, rhs) for lhs, rhs in aa_ops]
    yield
    for p, m in zip(pre, aa):
        p["a_ab"] = stack(jnp.where(sw < tw, m[:C, :W], 0.0))
        p["a_ak"] = stack(jnp.where(sw < tw, m[:C, W:], 0.0))
        p["a_rb"] = jnp.where(sw <= tw, m[C:, :W], 0.0).astype(BF16)
        p["a_rk"] = jnp.where(sw <= tw, m[C:, W:], 0.0).astype(BF16)
    yield
    for p in pre:
        p["aakv"] = _mm(p["a_ak"], p["v_s"]).astype(BF16)
    yield

    box = {}
    yield from _unit_lower_inverses([p["a_ab"] for p in pre], ri, ci, box)
    tinv = box["tinv"]
    w_t = [_mm(t, p["at_s"]).astype(BF16) for t, p in zip(tinv, pre)]
    yield
    u_t = [_mm(t, p["aakv"]).astype(BF16) for t, p in zip(tinv, pre)]
    yield
    m_e = [_mm_tn(p["btil"], w).astype(BF16) for p, w in zip(pre, w_t)]
    yield
    n_mat = [_mm_tn(p["btil"], x) + _mm_tn(p["ktil"], p["v_s"]) for p, x in zip(pre, u_t)]
    yield
    g_col = [jnp.sum(jnp.where(ri == ci, p["gam"], 0.0), axis=1, keepdims=True) for p in pre]
    yield
    st = state_sc[...]
    outs = []
    def out_terms(i):
        p = pre[i]
        return ((p["rt"] + _mm(p["a_rb"], w_t[i])).astype(BF16),
                _mm(p["a_rb"], u_t[i]) + _mm(p["a_rk"], p["v_s"]))

    q_mat, p_mat = out_terms(0)
    for i in range(n_chunks):
        sb = st.astype(BF16)
        moved = _mm(m_e[i], sb)
        outs.append(_mm(q_mat, sb) + p_mat)
        if i + 1 < n_chunks:
            q_mat, p_mat = out_terms(i + 1)
        st = g_col[i] * st + moved + n_mat[i]
        yield
    state_sc[...] = st

    o = jnp.concatenate(outs, axis=0)
    mean = _head_sum(o, ones_bd) * (1.0 / N)
    d = o - mean
    var = _head_sum(d * d, ones_bd) * (1.0 / N)
    o = d * lax.rsqrt(var + A_GN_EPS) * prm["lnw"][...] + prm["lnb"][...]
    bonus = _head_sum(r * k2 * prm["rk"][...], ones_bd) * v
    y_ref[...] = ((o + bonus) * g).astype(y_ref.dtype)


def _pool_windows(u, buf, row0, *, pad):
    rows = u.shape[0]
    buf[B_HALO:B_HALO + rows, :] = u
    lane_group = lax.broadcasted_iota(jnp.int32, (1, B_WIDTH), 1) // B_GROUP_DIM
    acc = u
    win = jnp.zeros_like(u)
    for s in range(1, max(B_WINDOWS)):
        acc = acc + buf[B_HALO - s:B_HALO - s + rows, :]
        if s + 1 in B_WINDOWS:
            win = jnp.where(lane_group == B_WINDOWS.index(s + 1), acc, win)
    wlane = jnp.zeros((1, B_WIDTH), jnp.int32)
    for gi, w in enumerate(B_WINDOWS):
        wlane = jnp.where(lane_group == gi, w, wlane)
    t_real = row0 + lax.broadcasted_iota(jnp.int32, (rows, 1), 0) - pad
    cnt = jnp.minimum(jnp.maximum(t_real + 1, 1), wlane).astype(F32)
    buf[0:B_HALO, :] = u[rows - B_HALO:, :]
    return win / cnt - u


def _swa_constants(pad):
    T = SEQ_TILE
    dist = T + np.arange(T)[:, None] - np.arange(2 * T)[None, :]
    band = (dist >= 0) & (dist < T)
    n = -(-pad // T) + 2
    bias = np.empty((n, C_KV_HEADS, C_GROUP * T, 2 * T), np.float32)
    for j in range(n):
        ok = band & ((j - 1) * T + np.arange(2 * T)[None, :] >= pad)
        for hq in range(C_HEADS):
            slope = np.float32(2.0 ** (-8.0 * (hq + 1) / C_HEADS))
            rows = slice((hq % C_GROUP) * T, (hq % C_GROUP + 1) * T)
            bias[j, hq // C_GROUP, rows] = np.where(ok, -slope * dist.astype(np.float32), MASK_VALUE)
    return jnp.asarray(bias)


def _swa_tile(y_ref, blk, kv_prev, sinks_ref, bias_ref, j, *, pad):
    T = SEQ_TILE
    G = C_GROUP
    kv = blk[:, C_WIDTH:]
    kw = jnp.concatenate([kv_prev[:, :C_KV_WIDTH], kv[:, :C_KV_WIDTH]], axis=0)
    vw = jnp.concatenate([kv_prev[:, C_KV_WIDTH:], kv[:, C_KV_WIDTH:]], axis=0)
    lane = lax.broadcasted_iota(jnp.int32, (1, 2 * C_HEAD_DIM), 1)
    low = lane < C_HEAD_DIM
    kw_sw = pltpu.roll(kw, C_HEAD_DIM, axis=1)
    vw_sw = pltpu.roll(vw, C_HEAD_DIM, axis=1)
    head = lax.broadcasted_iota(jnp.int32, (G * T, 1), 0) // T
    variant = jnp.minimum(j, bias_ref.shape[0] - 1)
    tiles = []
    for hk in range(C_KV_HEADS):
        k2 = jnp.where(low == (hk == 0), kw, kw_sw).astype(BF16)
        v2 = jnp.where(low == (hk == 0), vw, vw_sw).astype(BF16)
        q_rows = []
        sink = jnp.zeros((G * T, 1), F32)
        for gq in range(G):
            hq = hk * G + gq
            pair = blk[:, (hq // 2) * 128:(hq // 2 + 1) * 128]
            q_rows.append(jnp.where(low == (hq % 2 == 0), pair * (C_HEAD_DIM ** -0.5), 0.0))
            sink = jnp.where(head == gq, sinks_ref[hq], sink)
        s = _dot_nt(jnp.concatenate(q_rows, axis=0), k2)
        logits = s + bias_ref[variant, hk]
        m = jnp.maximum(jnp.max(logits, axis=-1, keepdims=True), sink)
        p = jnp.exp(logits - m)
        denom = jnp.sum(p, axis=-1, keepdims=True) + jnp.exp(sink - m)
        yield
        o = _dot(p, v2) / denom
        for pr in range(G // 2):
            tiles.append(jnp.where(low, o[2 * pr * T:(2 * pr + 1) * T], o[(2 * pr + 1) * T:(2 * pr + 2) * T]))
        yield
    y_ref[...] = jnp.concatenate(tiles, axis=1).astype(y_ref.dtype)


def _hgrn_tile(y_ref, u, state_sc, lb_ref, ng_ref, sums_ref, level_ref, j, *, pad):
    T = SEQ_TILE
    N = D_KEY_DIM
    lb = lb_ref[...]
    q = _silu(u[:, :D_WIDTH])
    fpre = u[:, D_WIDTH:2 * D_WIDTH]
    vin = u[:, 2 * D_WIDTH:3 * D_WIDTH]
    gate = u[:, 3 * D_WIDTH:]
    sig = _sigmoid(fpre)
    f = lb + (1.0 - lb) * sig
    t_real = j * T + lax.broadcasted_iota(jnp.int32, (T, 1), 0) - pad
    logf = jnp.where(t_real >= 0, jnp.log(jnp.maximum(f, 1e-30)), 0.0)
    kx = (1.0 - lb) * (1.0 - sig)

    all_sums = _mm(sums_ref[...], jnp.concatenate(_split2(logf), axis=0))
    sums = lambda i: all_sums[i * T:(i + 1) * T]
    same_head = (lax.broadcasted_iota(jnp.int32, (D_WIDTH, D_WIDTH), 0) // N
                 == lax.broadcasted_iota(jnp.int32, (D_WIDTH, D_WIDTH), 1) // N)
    head_rows = jnp.where(lax.broadcasted_iota(jnp.int32, (D_HEADS * T, D_WIDTH), 0) // T
                          == lax.broadcasted_iota(jnp.int32, (D_HEADS * T, D_WIDTH), 1) // N,
                          1.0, 0.0).astype(BF16)
    stack = lambda t: jnp.concatenate([t.astype(BF16)] * D_HEADS, axis=0) * head_rows
    k_st = stack(kx)
    v_st = stack(vin)
    lv = level_ref[...]
    att = jnp.where(lv == 0, _dot_nt(q, k_st), 0.0)
    for l in range(D_LEVELS):
        if l == 0:
            q_l, k_l = q * jnp.exp(logf), k_st
        else:
            q_l = q * jnp.exp(sums(2 * l - 2))
            k_l = k_st * jnp.concatenate([jnp.exp(sums(2 * l - 1)).astype(BF16)] * D_HEADS, axis=0)
        att = jnp.where(lv == l + 1, _dot_nt(q_l, k_l), att)
        yield
    b = sums(2 * D_LEVELS - 2)
    b_rev = sums(2 * D_LEVELS - 1)
    st = state_sc[...]
    o = _dot(att, v_st) + _dot_nt(q * jnp.exp(b), st)
    upd = jnp.where(same_head, _dot_tn(vin, kx * jnp.exp(b_rev)), 0.0)
    state_sc[...] = st * jnp.exp(b[T - 1:T, :]) + upd

    ones_bd = _head_ones(D_WIDTH, N)
    ms = _head_sum(o * o, ones_bd) * (1.0 / N)
    y_ref[...] = (o * lax.rsqrt(ms + NORM_EPS) * ng_ref[...] * _silu(gate)).astype(y_ref.dtype)


def _hgrn_constants():
    T = SEQ_TILE
    t = np.arange(T)[:, None]
    i = np.arange(T)[None, :]
    mats = []
    for l in list(range(1, D_LEVELS)) + [D_LEVELS]:
        h = 2 ** l
        same = (t // h) == (i // h)
        mats += [same & (i <= t), same & (i > t)]
    sums = np.tile(np.concatenate(mats, axis=0).astype(np.float32), (1, 2))
    x = t ^ i
    level = np.where(i == t, 0, np.where(i < t, np.floor(np.log2(np.maximum(x, 1))).astype(np.int64) + 1, -1))
    return jnp.asarray(sums, BF16), jnp.asarray(np.tile(level, (1, D_HEADS)), jnp.int32)


def _merge_ffn_kernel(h_ref, gates_ref, ya_ref, yb_ref, yc_ref, yd_ref, wb_ref, wo_ref, gf_ref, wu_ref,
                      wd_ref, *rest, ff_chunk):
    fg_ref, o_ref = rest if len(rest) == 2 else (None,) + rest
    merged = None
    row = 0
    for bi, y_ref in enumerate((ya_ref, yb_ref, yc_ref, yd_ref)):
        w = y_ref.shape[-1]
        part = gates_ref[:, bi * D_MODEL:(bi + 1) * D_MODEL] * jnp.dot(
            y_ref[...].astype(BF16), wb_ref[row:row + w, :], preferred_element_type=F32)
        merged = part if merged is None else merged + part
        row += w
    h = h_ref[...] + jnp.dot(merged.astype(BF16), wo_ref[...], preferred_element_type=F32)
    zb = _rms(h, gf_ref[...]).astype(BF16)

    def up_proj(c):
        cols = lambda base: wu_ref[:, base + c * ff_chunk:base + (c + 1) * ff_chunk]
        return (jnp.dot(zb, cols(0), preferred_element_type=F32),
                jnp.dot(zb, cols(D_FF), preferred_element_type=F32))

    acc = h
    n_chunks = D_FF // ff_chunk
    gu, up = up_proj(0)
    for c in range(n_chunks):
        nxt = up_proj(c + 1) if c + 1 < n_chunks else None
        acc = acc + jnp.dot((_silu(gu) * up).astype(BF16), wd_ref[c * ff_chunk:(c + 1) * ff_chunk, :],
                            preferred_element_type=F32)
        if nxt is not None:
            gu, up = nxt
    o_ref[...] = acc if fg_ref is None else _rms(acc, fg_ref[...])


def _merge_ffn(h2, proj2, ya, yc, yd, wb, wo, gf, wu, wd, final_g=None):
    m = h2.shape[0]
    tm = ROW_TILE
    rows = lambda w, col=0: pl.BlockSpec((tm, w), lambda i: (i, col))
    const = lambda arr: pl.BlockSpec(arr.shape, lambda i: (0, 0), pipeline_mode=pl.Buffered(1))
    last = () if final_g is None else (final_g,)
    return pl.pallas_call(
        functools.partial(_merge_ffn_kernel, ff_chunk=256),
        grid=(m // tm,),
        in_specs=[rows(D_MODEL), rows(GATE_COLS), rows(A_WIDTH), rows(B_WIDTH, GATE_COLS // B_WIDTH),
                  rows(C_WIDTH), rows(D_WIDTH), const(wb), const(wo), const(gf), const(wu),
                  const(wd)] + [const(g) for g in last],
        out_specs=rows(D_MODEL),
        out_shape=jax.ShapeDtypeStruct((m, D_MODEL), F32),
        compiler_params=pltpu.CompilerParams(dimension_semantics=("arbitrary",),
                                             vmem_limit_bytes=VMEM_LIMIT),
        name="merge_ffn",
    )(h2, proj2, ya, proj2, yc, yd, wb, wo, gf, wu, wd, *last)


def _block_diag(blocks):
    n, r, c = blocks.shape
    out = jnp.zeros((n * r, n * c), blocks.dtype)
    for i in range(n):
        out = out.at[i * r:(i + 1) * r, i * c:(i + 1) * c].set(blocks[i])
    return out


def kernel(x, meta, norm_mix, norm_ffn, norm_final, w_in, w_branch, w_out, a_mu, a_w_up, a_w0, a_a_up, a_a0, a_g_up, a_kk, a_ka, a_rk, a_ln_w, a_ln_b, a_vres_down, a_vres_up, a_vres0, b_mix, b_scale, c_sinks, d_lower_bounds, d_norm, w_ffn_up, w_ffn_down):
    bsz, seq, _ = x.shape
    depth = w_in.shape[0]
    T = SEQ_TILE
    L = N_META + seq
    pad = (-L) % T
    lp = L + pad
    assert (pad + N_META) % T == 0 and seq % T == 0
    h = jnp.concatenate([jnp.zeros((bsz, pad, D_MODEL), F32),
                         jnp.broadcast_to(meta.astype(F32)[None], (bsz, N_META, D_MODEL)),
                         x.astype(F32)], axis=1).reshape(bsz * lp, D_MODEL)
    lb_w = jax.nn.softmax(d_lower_bounds.astype(F32), axis=0)
    lb_table = jnp.cumsum(lb_w, axis=0) - lb_w[0]
    row2 = lambda t: t.reshape(1, -1).astype(F32)
    v_first = None
    for l in range(depth):
        wwa = jnp.zeros((128, 2 * A_WIDTH), F32)
        wwa = wwa.at[:64, :A_WIDTH].set(a_w_up[l]).at[64:, A_WIDTH:].set(a_a_up[l])
        pa = dict(mu=row2(a_mu[l]), wwa=wwa.astype(BF16), w0=row2(a_w0[l]), a0=row2(a_a0[l]),
                  gup=a_g_up[l].astype(BF16), kk=row2(a_kk[l]), ka=row2(a_ka[l]), rk=row2(a_rk[l]),
                  lnw=row2(a_ln_w[l]), lnb=row2(a_ln_b[l]))
        if l > 0:
            pa.update(vd=a_vres_down[l - 1].astype(BF16), vu=a_vres_up[l - 1].astype(BF16),
                      v0=row2(a_vres0[l - 1]))
        outs = _in_proj(h, v_first, row2(norm_mix[l]), w_in[l].astype(BF16),
                        _block_diag(b_mix[l]).astype(BF16), row2(b_scale[l]), c_sinks[l].astype(F32),
                        row2(lb_table[l]), row2(d_norm[l]), pa, pad=pad, lp=lp)
        proj2, y_a, y_c, y_d = outs[:4]
        if l == 0:
            v_first = outs[4]
        h = _merge_ffn(h, proj2, y_a, y_c, y_d,
                       w_branch[l].astype(BF16), w_out[l].astype(BF16), row2(norm_ffn[l]),
                       w_ffn_up[l].astype(BF16), w_ffn_down[l].astype(BF16),
                       row2(norm_final) if l == depth - 1 else None)
    return h.reshape(bsz, lp, D_MODEL)[:, pad + N_META:]
```
